```python
import math
import jax
import jax.numpy as jnp
from jax import lax
import numpy as np


D_MODEL = 2048
BATCH = 4
SEQ = 2048
DEPTH = 4

RMS_EPS = 1e-6
ROPE_THETA = 10000.0
Q_BLOCK = 128
NEG_INF = -1e30

MLA_HEADS = 8
MLA_Q_LORA = 768
MLA_KV_LORA = 512
MLA_NOPE = 128
MLA_ROPE = 64
MLA_V = 128

LRU_WIDTH = D_MODEL // 2
LRU_BLOCKS = 8
LRU_BLOCK_W = LRU_WIDTH // LRU_BLOCKS
LRU_C = 8.0
CONV_WIDTH = 4

EVEN_IN = MLA_Q_LORA + MLA_KV_LORA + MLA_ROPE + 2 * LRU_WIDTH
EVEN_MIX = MLA_HEADS * MLA_V + LRU_WIDTH

NSA_HEADS = 16
NSA_GROUPS = 4
NSA_HPG = NSA_HEADS // NSA_GROUPS
NSA_DH = D_MODEL // NSA_HEADS
CMP_BLOCK = 32
CMP_STRIDE = 16
SEL_BLOCK = 64
SEL_COUNT = 16
SEL_Q_BLOCK = 32
SEL_FORCE = 1e4
WINDOW = 512
NSA_Q_DIM = NSA_HEADS * NSA_DH
NSA_KV_DIM = NSA_GROUPS * NSA_DH
ODD_IN = NSA_Q_DIM + 6 * NSA_KV_DIM + 3 * NSA_HEADS

DENSE_FF = 5632
N_EXPERTS = 8
TOP_K = 2
EXPERT_FF = 7168
EXPERT_ROW_BLOCK = 128

kernel_name = 'hybrid_mla_rglru_nsa_moe_trunk'

F32 = jnp.float32


def rmsnorm(x, g):
    xf = x.astype(F32)
    y = xf * lax.rsqrt(jnp.mean(xf * xf, axis=-1, keepdims=True) + RMS_EPS)
    return (y * g.astype(F32)).astype(x.dtype)


def rope_tables(seq, dim):
    pos = jnp.arange(seq, dtype=F32)
    inv = ROPE_THETA ** (-jnp.arange(0, dim, 2, dtype=F32) / dim)
    ang = pos[:, None] * inv[None, :]
    return jnp.cos(ang), jnp.sin(ang)


def apply_rope(x, cos, sin):
    half = x.shape[-1] // 2
    xf = x.astype(F32)
    x1, x2 = xf[..., :half], xf[..., half:]
    c = cos[None, :, None, :]
    s = sin[None, :, None, :]
    return jnp.concatenate([x1 * c - x2 * s, x2 * c + x1 * s], axis=-1).astype(x.dtype)


def causal_block_attention(q, k, v, scale):
    S = q.shape[1]
    outs = []
    for start in range(0, S, Q_BLOCK):
        end = start + Q_BLOCK
        s = jnp.einsum('bqhd,bkhd->bhqk', q[:, start:end], k[:, :end]).astype(F32) * scale
        mask = jnp.arange(end)[None, :] <= (start + jnp.arange(Q_BLOCK))[:, None]
        p = jax.nn.softmax(jnp.where(mask, s, NEG_INF), axis=-1)
        outs.append(jnp.einsum('bhqk,bkhd->bqhd', p.astype(v.dtype), v[:, :end]))
    return jnp.concatenate(outs, axis=1)


def causal_dwconv(x, w, b):
    c = x.shape[-1]
    y = lax.conv_general_dilated(x, w[:, None, :], window_strides=(1,),
                                 padding=[(w.shape[0] - 1, 0)],
                                 dimension_numbers=('NWC', 'WIO', 'NWC'),
                                 feature_group_count=c)
    return y + b


def _linear_scan_combine(left, right):
    a_l, b_l = left
    a_r, b_r = right
    return a_l * a_r, a_r * b_l + b_r


def rg_lru(x, ga_w, ga_b, gx_w, gx_b, lam):
    B, S, W = x.shape
    xb = x.reshape(B, S, LRU_BLOCKS, LRU_BLOCK_W)
    r = jax.nn.sigmoid((jnp.einsum('bsnc,ncd->bsnd', xb, ga_w) + ga_b).astype(F32)).reshape(B, S, W)
    i = jax.nn.sigmoid((jnp.einsum('bsnc,ncd->bsnd', xb, gx_w) + gx_b).astype(F32)).reshape(B, S, W)
    log_a = -LRU_C * r * jax.nn.softplus(-lam.astype(F32))
    a = jnp.exp(log_a)
    mult = jnp.sqrt(-jnp.expm1(2.0 * log_a))
    mult = jnp.where((jnp.arange(S) == 0)[None, :, None], 1.0, mult)
    u = mult * i * x.astype(F32)
    _, h = lax.associative_scan(_linear_scan_combine, (a, u), axis=1)
    return h.astype(x.dtype)


def mla_rglru_mixer(h, w_in, q_norm, w_q_up, kv_norm, w_kv_up, conv_w, conv_b,
                    ga_w, ga_b, gx_w, gx_b, lam, w_out):
    B, S, _ = h.shape
    u = h @ w_in
    o1 = MLA_Q_LORA
    o2 = o1 + MLA_KV_LORA
    o3 = o2 + MLA_ROPE
    o4 = o3 + LRU_WIDTH
    c_q, c_kv, k_pe, x_rec, y_gate = jnp.split(u, [o1, o2, o3, o4], axis=-1)
    cos, sin = rope_tables(S, MLA_ROPE)
    q = (rmsnorm(c_q, q_norm) @ w_q_up).reshape(B, S, MLA_HEADS, MLA_NOPE + MLA_ROPE)
    q = jnp.concatenate([q[..., :MLA_NOPE], apply_rope(q[..., MLA_NOPE:], cos, sin)], axis=-1)
    kv = (rmsnorm(c_kv, kv_norm) @ w_kv_up).reshape(B, S, MLA_HEADS, MLA_NOPE + MLA_V)
    k_pe = jnp.broadcast_to(apply_rope(k_pe[:, :, None, :], cos, sin), (B, S, MLA_HEADS, MLA_ROPE))
    k = jnp.concatenate([kv[..., :MLA_NOPE], k_pe], axis=-1)
    o_mla = causal_block_attention(q, k, kv[..., MLA_NOPE:], (MLA_NOPE + MLA_ROPE) ** -0.5)
    o_mla = o_mla.reshape(B, S, MLA_HEADS * MLA_V)
    x_rec = causal_dwconv(x_rec, conv_w, conv_b)
    o_rec = rg_lru(x_rec, ga_w, ga_b, gx_w, gx_b, lam) * jax.nn.gelu(y_gate)
    return jnp.concatenate([o_mla, o_rec], axis=-1) @ w_out


def nsa_compress(k, pe, w1, b1, w2):
    B, S, G, DH = k.shape
    nc = (S - CMP_BLOCK) // CMP_STRIDE + 1
    gidx = jnp.arange(nc)[:, None] * CMP_STRIDE + jnp.arange(CMP_BLOCK)[None, :]
    blocks = k[:, gidx] + pe[None, None, :, None, :]
    flat = blocks.transpose(0, 1, 3, 2, 4).reshape(B, nc, G, CMP_BLOCK * DH)
    return jax.nn.gelu(flat @ w1 + b1) @ w2


def nsa_compressed_branch(q, k_cmp, v_cmp):
    B, S, H, DH = q.shape
    nc = k_cmp.shape[1]
    qg = q.reshape(B, S, NSA_GROUPS, NSA_HPG, DH)
    s = jnp.einsum('bsgpd,bngd->bgpsn', qg, k_cmp).astype(F32) * (DH ** -0.5)
    t = jnp.arange(S)
    blk_last = jnp.arange(nc) * CMP_STRIDE + CMP_BLOCK - 1
    valid = blk_last[None, :] <= t[:, None]
    p = jnp.where(valid, jax.nn.softmax(jnp.where(valid, s, NEG_INF), axis=-1), 0.0)
    o = jnp.einsum('bgpsn,bngd->bsgpd', p.astype(v_cmp.dtype), v_cmp).reshape(B, S, H, DH)
    n_sel = S // SEL_BLOCK
    cmp_start = jnp.arange(nc) * CMP_STRIDE
    sel_start = jnp.arange(n_sel) * SEL_BLOCK
    overlap = ((cmp_start[:, None] < sel_start[None, :] + SEL_BLOCK) &
               (cmp_start[:, None] + CMP_BLOCK > sel_start[None, :])).astype(F32)
    imp = jnp.einsum('bgpsn,nj->bgsj', p, overlap)
    cur = t // SEL_BLOCK
    j = jnp.arange(n_sel)
    future = j[None, :] > cur[:, None]
    forced = (j[None, :] == 0) | (j[None, :] == cur[:, None]) | (j[None, :] == cur[:, None] - 1)
    score = jnp.where(future, -1.0, jnp.where(forced, SEL_FORCE, imp))
    _, idx = lax.top_k(score, min(SEL_COUNT, n_sel))
    return o, idx


def nsa_selected_branch(q, k, v, idx):
    B, S, H, DH = q.shape
    nq = S // SEL_Q_BLOCK
    kk = idx.shape[-1]
    n_sel = S // SEL_BLOCK
    qb = q.reshape(B, nq, SEL_Q_BLOCK, NSA_GROUPS, NSA_HPG, DH).transpose(1, 0, 3, 4, 2, 5)
    ib = idx.reshape(B, NSA_GROUPS, nq, SEL_Q_BLOCK, kk).transpose(2, 0, 1, 3, 4)
    tq = jnp.arange(S).reshape(nq, SEL_Q_BLOCK)
    kblk = k.reshape(B, n_sel, SEL_BLOCK, NSA_GROUPS, DH).transpose(0, 3, 1, 2, 4)
    vblk = v.reshape(B, n_sel, SEL_BLOCK, NSA_GROUPS, DH).transpose(0, 3, 1, 2, 4)
    gather = jax.vmap(jax.vmap(lambda blocks, sel: blocks[sel]))

    def one_block(args):
        q_i, i_i, t_i = args
        kg = gather(kblk, i_i).reshape(B, NSA_GROUPS, SEL_Q_BLOCK, kk * SEL_BLOCK, DH)
        vg = gather(vblk, i_i).reshape(B, NSA_GROUPS, SEL_Q_BLOCK, kk * SEL_BLOCK, DH)
        kpos = (i_i[..., None] * SEL_BLOCK + jnp.arange(SEL_BLOCK)).reshape(B, NSA_GROUPS, SEL_Q_BLOCK, kk * SEL_BLOCK)
        mask = (kpos <= t_i[None, None, :, None])[:, :, None]
        s = jnp.einsum('bgpqd,bgqkd->bgpqk', q_i, kg).astype(F32) * (DH ** -0.5)
        pr = jax.nn.softmax(jnp.where(mask, s, NEG_INF), axis=-1)
        return jnp.einsum('bgpqk,bgqkd->bgpqd', pr.astype(vg.dtype), vg)

    o = lax.map(one_block, (qb, ib, tq))
    return o.transpose(1, 0, 4, 2, 3, 5).reshape(B, S, H, DH)


def nsa_window_branch(q, k, v):
    B, S, H, DH = q.shape
    nq = S // Q_BLOCK
    qb = q.reshape(B, nq, Q_BLOCK, NSA_GROUPS, NSA_HPG, DH).transpose(1, 0, 3, 4, 2, 5)
    kpad = jnp.pad(k, ((0, 0), (WINDOW, 0), (0, 0), (0, 0)))
    vpad = jnp.pad(v, ((0, 0), (WINDOW, 0), (0, 0), (0, 0)))
    starts = jnp.arange(nq) * Q_BLOCK
    span = WINDOW + Q_BLOCK

    def one_block(args):
        q_i, s0 = args
        kb = lax.dynamic_slice_in_dim(kpad, s0, span, axis=1)
        vb = lax.dynamic_slice_in_dim(vpad, s0, span, axis=1)
        t_i = s0 + jnp.arange(Q_BLOCK)
        kpos = s0 - WINDOW + jnp.arange(span)
        mask = ((kpos[None, :] <= t_i[:, None]) & (kpos[None, :] > t_i[:, None] - WINDOW) &
                (kpos[None, :] >= 0))
        s = jnp.einsum('bgpqd,bkgd->bgpqk', q_i, kb).astype(F32) * (DH ** -0.5)
        pr = jax.nn.softmax(jnp.where(mask, s, NEG_INF), axis=-1)
        return jnp.einsum('bgpqk,bkgd->bgpqd', pr.astype(vb.dtype), vb)

    o = lax.map(one_block, (qb, starts))
    return o.transpose(1, 0, 4, 2, 3, 5).reshape(B, S, H, DH)


def nsa_mixer(h, w_in, ck_pe, ck_w1, ck_b1, ck_w2, cv_pe, cv_w1, cv_b1, cv_w2, w_out):
    B, S, _ = h.shape
    u = h @ w_in
    offs = [NSA_Q_DIM + n * NSA_KV_DIM for n in range(7)]
    q, kc, vc, ks, vs, kw, vw, g = jnp.split(u, offs, axis=-1)
    q = q.reshape(B, S, NSA_HEADS, NSA_DH)
    kvs = (B, S, NSA_GROUPS, NSA_DH)
    kc, vc, ks, vs, kw, vw = [a.reshape(kvs) for a in (kc, vc, ks, vs, kw, vw)]
    gates = jax.nn.sigmoid(g.astype(F32)).reshape(B, S, NSA_HEADS, 3)
    cos, sin = rope_tables(S, NSA_DH)
    q_rot = apply_rope(q, cos, sin)
    k_cmp = nsa_compress(kc, ck_pe, ck_w1, ck_b1, ck_w2)
    v_cmp = nsa_compress(vc, cv_pe, cv_w1, cv_b1, cv_w2)
    o_cmp, sel_idx = nsa_compressed_branch(q, k_cmp, v_cmp)
    o_sel = nsa_selected_branch(q_rot, apply_rope(ks, cos, sin), vs, sel_idx)
    o_win = nsa_window_branch(q_rot, apply_rope(kw, cos, sin), vw)
    o = (gates[..., 0:1] * o_cmp.astype(F32) + gates[..., 1:2] * o_sel.astype(F32) +
         gates[..., 2:3] * o_win.astype(F32))
    return o.astype(h.dtype).reshape(B, S, NSA_Q_DIM) @ w_out


def swiglu(h, w1, w3, w2):
    return (jax.nn.silu(h @ w1) * (h @ w3)) @ w2


def moe_swiglu(h, w_router, b_router, w1, w3, w2):
    B, S, D = h.shape
    N = B * S
    x = h.reshape(N, D)
    logits = (x @ w_router).astype(F32) + b_router.astype(F32)
    top_logit, top_e = lax.top_k(logits, TOP_K)
    gate = jax.nn.softmax(top_logit, axis=-1)
    flat_e = top_e.reshape(-1)
    flat_tok = jnp.repeat(jnp.arange(N), TOP_K)
    flat_g = gate.reshape(-1)
    order = jnp.argsort(flat_e)
    se, stok, sg = flat_e[order], flat_tok[order], flat_g[order]
    counts = jnp.bincount(flat_e, length=N_EXPERTS)
    padded = (counts + EXPERT_ROW_BLOCK - 1) // EXPERT_ROW_BLOCK * EXPERT_ROW_BLOCK
    start = jnp.cumsum(counts) - counts
    pend = jnp.cumsum(padded)
    pstart = pend - padded
    dest = pstart[se] + (jnp.arange(N * TOP_K) - start[se])
    n_rows = (-(-(N * TOP_K) // EXPERT_ROW_BLOCK) + N_EXPERTS) * EXPERT_ROW_BLOCK
    n_blocks = n_rows // EXPERT_ROW_BLOCK
    row_tok = jnp.full((n_rows,), N, dtype=jnp.int32).at[dest].set(stok.astype(jnp.int32))
    row_g = jnp.zeros((n_rows,), F32).at[dest].set(sg)
    xpad = jnp.concatenate([x, jnp.zeros((1, D), x.dtype)], axis=0)
    xr = xpad[row_tok].reshape(n_blocks, EXPERT_ROW_BLOCK, D)
    blk_e = jnp.minimum(jnp.searchsorted(pend, jnp.arange(n_blocks) * EXPERT_ROW_BLOCK, side='right'),
                        N_EXPERTS - 1)

    def expert_block(args):
        xb, e = args
        return (jax.nn.silu(xb @ w1[e]) * (xb @ w3[e])) @ w2[e]

    yr = lax.map(expert_block, (xr, blk_e)).reshape(n_rows, D)
    y = jax.ops.segment_sum(yr.astype(F32) * row_g[:, None], row_tok, num_segments=N + 1)[:N]
    return y.astype(h.dtype).reshape(B, S, D)


def setup_inputs(seed: int = 0) -> dict:
    key = jax.random.key(seed)
    keys = iter(jax.random.split(key, 48))

    def nrm(shape, scale):
        return jax.random.normal(next(keys), shape, F32) * scale

    def gain(shape):
        return 1.0 + nrm(shape, 0.01)

    ne = (DEPTH + 1) // 2
    no = DEPTH // 2
    D = D_MODEL
    x = nrm((BATCH, SEQ, D), 1.0)
    ev_norm_mix = gain((ne, D))
    ev_w_in = nrm((ne, D, EVEN_IN), D ** -0.5)
    ev_q_norm = gain((ne, MLA_Q_LORA))
    ev_w_q_up = nrm((ne, MLA_Q_LORA, MLA_HEADS * (MLA_NOPE + MLA_ROPE)), MLA_Q_LORA ** -0.5)
    ev_kv_norm = gain((ne, MLA_KV_LORA))
    ev_w_kv_up = nrm((ne, MLA_KV_LORA, MLA_HEADS * (MLA_NOPE + MLA_V)), MLA_KV_LORA ** -0.5)
    ev_conv_w = nrm((ne, CONV_WIDTH, LRU_WIDTH), CONV_WIDTH ** -0.5)
    ev_conv_b = nrm((ne, LRU_WIDTH), 0.01)
    ev_gate_a_w = nrm((ne, LRU_BLOCKS, LRU_BLOCK_W, LRU_BLOCK_W), LRU_BLOCK_W ** -0.5)
    ev_gate_a_b = nrm((ne, LRU_BLOCKS, LRU_BLOCK_W), 0.1)
    ev_gate_x_w = nrm((ne, LRU_BLOCKS, LRU_BLOCK_W, LRU_BLOCK_W), LRU_BLOCK_W ** -0.5)
    ev_gate_x_b = nrm((ne, LRU_BLOCKS, LRU_BLOCK_W), 0.1)
    a_c = jax.random.uniform(next(keys), (ne, LRU_WIDTH), F32, minval=0.9, maxval=0.999)
    a_base = a_c ** (1.0 / LRU_C)
    ev_lru_lambda = jnp.log(a_base) - jnp.log1p(-a_base)
    ev_w_out = nrm((ne, EVEN_MIX, D), EVEN_MIX ** -0.5)
    ev_norm_ffn = gain((ne, D))
    ev_ffn_w1 = nrm((ne, D, DENSE_FF), D ** -0.5)
    ev_ffn_w3 = nrm((ne, D, DENSE_FF), D ** -0.5)
    ev_ffn_w2 = nrm((ne, DENSE_FF, D), DENSE_FF ** -0.5)
    od_norm_mix = gain((no, D))
    od_w_in = nrm((no, D, ODD_IN), D ** -0.5)
    od_cmp_k_pe = nrm((no, CMP_BLOCK, NSA_DH), 0.02)
    od_cmp_k_w1 = nrm((no, CMP_BLOCK * NSA_DH, NSA_DH), (CMP_BLOCK * NSA_DH) ** -0.5)
    od_cmp_k_b1 = nrm((no, NSA_DH), 0.01)
    od_cmp_k_w2 = nrm((no, NSA_DH, NSA_DH), NSA_DH ** -0.5)
    od_cmp_v_pe = nrm((no, CMP_BLOCK, NSA_DH), 0.02)
    od_cmp_v_w1 = nrm((no, CMP_BLOCK * NSA_DH, NSA_DH), (CMP_BLOCK * NSA_DH) ** -0.5)
    od_cmp_v_b1 = nrm((no, NSA_DH), 0.01)
    od_cmp_v_w2 = nrm((no, NSA_DH, NSA_DH), NSA_DH ** -0.5)
    od_w_out = nrm((no, NSA_Q_DIM, D), NSA_Q_DIM ** -0.5)
    od_norm_ffn = gain((no, D))
    od_router_w = nrm((no, D, N_EXPERTS), D ** -0.5)
    od_router_b = nrm((no, N_EXPERTS), 0.01)
    od_exp_w1 = nrm((no, N_EXPERTS, D, EXPERT_FF), D ** -0.5)
    od_exp_w3 = nrm((no, N_EXPERTS, D, EXPERT_FF), D ** -0.5)
    od_exp_w2 = nrm((no, N_EXPERTS, EXPERT_FF, D), EXPERT_FF ** -0.5)
    final_norm = gain((D,))
    return {'x': x,
            'ev_norm_mix': ev_norm_mix, 'ev_w_in': ev_w_in, 'ev_q_norm': ev_q_norm,
            'ev_w_q_up': ev_w_q_up, 'ev_kv_norm': ev_kv_norm, 'ev_w_kv_up': ev_w_kv_up,
            'ev_conv_w': ev_conv_w, 'ev_conv_b': ev_conv_b,
            'ev_gate_a_w': ev_gate_a_w, 'ev_gate_a_b': ev_gate_a_b,
            'ev_gate_x_w': ev_gate_x_w, 'ev_gate_x_b': ev_gate_x_b,
            'ev_lru_lambda': ev_lru_lambda, 'ev_w_out': ev_w_out, 'ev_norm_ffn': ev_norm_ffn,
            'ev_ffn_w1': ev_ffn_w1, 'ev_ffn_w3': ev_ffn_w3, 'ev_ffn_w2': ev_ffn_w2,
            'od_norm_mix': od_norm_mix, 'od_w_in': od_w_in,
            'od_cmp_k_pe': od_cmp_k_pe, 'od_cmp_k_w1': od_cmp_k_w1, 'od_cmp_k_b1': od_cmp_k_b1,
            'od_cmp_k_w2': od_cmp_k_w2, 'od_cmp_v_pe': od_cmp_v_pe, 'od_cmp_v_w1': od_cmp_v_w1,
            'od_cmp_v_b1': od_cmp_v_b1, 'od_cmp_v_w2': od_cmp_v_w2,
            'od_w_out': od_w_out, 'od_norm_ffn': od_norm_ffn,
            'od_router_w': od_router_w, 'od_router_b': od_router_b,
            'od_exp_w1': od_exp_w1, 'od_exp_w3': od_exp_w3, 'od_exp_w2': od_exp_w2,
            'final_norm': final_norm}


def reference(x, ev_norm_mix, ev_w_in, ev_q_norm, ev_w_q_up, ev_kv_norm, ev_w_kv_up,
              ev_conv_w, ev_conv_b, ev_gate_a_w, ev_gate_a_b, ev_gate_x_w, ev_gate_x_b,
              ev_lru_lambda, ev_w_out, ev_norm_ffn, ev_ffn_w1, ev_ffn_w3, ev_ffn_w2,
              od_norm_mix, od_w_in, od_cmp_k_pe, od_cmp_k_w1, od_cmp_k_b1, od_cmp_k_w2,
              od_cmp_v_pe, od_cmp_v_w1, od_cmp_v_b1, od_cmp_v_w2, od_w_out, od_norm_ffn,
              od_router_w, od_router_b, od_exp_w1, od_exp_w3, od_exp_w2, final_norm):
    for layer in range(DEPTH):
        i = layer // 2
        if layer % 2 == 0:
            x = x + mla_rglru_mixer(rmsnorm(x, ev_norm_mix[i]), ev_w_in[i], ev_q_norm[i], ev_w_q_up[i],
                                    ev_kv_norm[i], ev_w_kv_up[i], ev_conv_w[i], ev_conv_b[i],
                                    ev_gate_a_w[i], ev_gate_a_b[i], ev_gate_x_w[i], ev_gate_x_b[i],
                                    ev_lru_lambda[i], ev_w_out[i])
            x = x + swiglu(rmsnorm(x, ev_norm_ffn[i]), ev_ffn_w1[i], ev_ffn_w3[i], ev_ffn_w2[i])
        else:
            x = x + nsa_mixer(rmsnorm(x, od_norm_mix[i]), od_w_in[i],
                              od_cmp_k_pe[i], od_cmp_k_w1[i], od_cmp_k_b1[i], od_cmp_k_w2[i],
                              od_cmp_v_pe[i], od_cmp_v_w1[i], od_cmp_v_b1[i], od_cmp_v_w2[i],
                              od_w_out[i])
            x = x + moe_swiglu(rmsnorm(x, od_norm_ffn[i]), od_router_w[i], od_router_b[i],
                               od_exp_w1[i], od_exp_w3[i], od_exp_w2[i])
    return rmsnorm(x, final_norm)
```

```python
import functools
import math

import numpy as np
import jax
import jax.numpy as jnp
from jax import lax
from jax.experimental import pallas as pl
from jax.experimental.pallas import tpu as pltpu

F32 = jnp.float32
BF16 = jnp.bfloat16

D_MODEL = 2048
DEPTH = 4
RMS_EPS = 1e-6
ROPE_THETA = 10000.0
NEG_INF = -1e30

MLA_HEADS = 8
MLA_Q_LORA = 768
MLA_KV_LORA = 512
MLA_NOPE = 128
MLA_ROPE = 64
MLA_V = 128
MLA_QK_PAD = 256

LRU_WIDTH = D_MODEL // 2
LRU_BLOCKS = 8
LRU_BLOCK_W = LRU_WIDTH // LRU_BLOCKS
LRU_C = 8.0
CONV_WIDTH = 4

NSA_HEADS = 16
NSA_GROUPS = 4
NSA_HPG = NSA_HEADS // NSA_GROUPS
NSA_DH = D_MODEL // NSA_HEADS
CMP_BLOCK = 32
CMP_STRIDE = 16
SEL_BLOCK = 64
SEL_COUNT = 16
SEL_FORCE = 1e4
WINDOW = 512
NSA_Q_DIM = NSA_HEADS * NSA_DH
NSA_KV_DIM = NSA_GROUPS * NSA_DH

DENSE_FF = 5632
N_EXPERTS = 8
TOP_K = 2
EXPERT_FF = 7168

LANES = 128
VMEM_LIMIT = 56 * 1024 * 1024

EV_CQ = 0
EV_CKV = MLA_Q_LORA
EV_REC = MLA_Q_LORA + MLA_KV_LORA
EV_GATE = EV_REC + LRU_WIDTH
EV_PE = EV_GATE + LRU_WIDTH
EV_PACKED = EV_PE + LANES

OD_Q = 0
OD_KC = NSA_Q_DIM
OD_VC = OD_KC + NSA_KV_DIM
OD_KS = OD_VC + NSA_KV_DIM
OD_VS = OD_KS + NSA_KV_DIM
OD_KW = OD_VS + NSA_KV_DIM
OD_VW = OD_KW + NSA_KV_DIM
OD_G = OD_VW + NSA_KV_DIM
OD_PACKED = OD_G + NSA_GROUPS * LANES

MOE_TILE = 1024
MOE_SUB = 256


def _cparams(sem):
    return pltpu.CompilerParams(dimension_semantics=sem, vmem_limit_bytes=VMEM_LIMIT)


def _rms(x, g):
    ms = jnp.mean(x * x, axis=-1, keepdims=True)
    return x * lax.rsqrt(ms + RMS_EPS) * g


def _sigmoid(x):
    return 1.0 / (1.0 + jnp.exp(-x))


def _gelu_tanh(x):
    return 0.5 * x * (1.0 + jnp.tanh(math.sqrt(2.0 / math.pi) * (x + 0.044715 * (x * x * x))))


def _dot(a, b):
    return jnp.dot(a, b, preferred_element_type=F32)


def _dot_nt(a, b):
    return lax.dot_general(a, b, (((1,), (1,)), ((), ())), preferred_element_type=F32)


def _norm_mm_kernel(x_ref, g_ref, w_ref, o_ref, xn_ref):
    @pl.when(pl.program_id(1) == 0)
    def _():
        xn_ref[...] = _rms(x_ref[...], g_ref[...]).astype(BF16)

    o_ref[...] = _dot(xn_ref[...], w_ref[...]).astype(o_ref.dtype)


def norm_matmul(x, g, w, *, tm, tn, out_dtype=F32):
    n, k = x.shape
    m = w.shape[1]
    assert n % tm == 0 and m % tn == 0
    return pl.pallas_call(
        _norm_mm_kernel,
        grid=(n // tm, m // tn),
        in_specs=[pl.BlockSpec((tm, k), lambda i, j: (i, 0)),
                  pl.BlockSpec((1, k), lambda i, j: (0, 0)),
                  pl.BlockSpec((k, tn), lambda i, j: (0, j))],
        out_specs=pl.BlockSpec((tm, tn), lambda i, j: (i, j)),
        out_shape=jax.ShapeDtypeStruct((n, m), out_dtype),
        scratch_shapes=[pltpu.VMEM((tm, k), BF16)],
        compiler_params=_cparams(("parallel", "arbitrary")),
    )(x, g.reshape(1, k), w)


def _mm_res_kernel(*refs, n_in):
    xs = refs[:n_in]
    ws = refs[n_in:2 * n_in]
    res_ref = refs[2 * n_in]
    o_ref = refs[2 * n_in + 1]
    acc = res_ref[...]
    for x_ref, w_ref in zip(xs, ws):
        acc = acc + _dot(x_ref[...], w_ref[...])
    o_ref[...] = acc


def matmul_residual(xs, w, res, *, tm, tn):
    n = res.shape[0]
    m = w.shape[1]
    n_in = len(xs)
    in_specs = [pl.BlockSpec((tm, x.shape[1]), lambda i, j: (i, 0)) for x in xs]
    row = 0
    for x in xs:
        kx = x.shape[1]
        assert row % kx == 0
        in_specs.append(pl.BlockSpec((kx, tn), lambda i, j, rb=row // kx: (rb, j)))
        row += kx
    assert row == w.shape[0]
    in_specs.append(pl.BlockSpec((tm, tn), lambda i, j: (i, j)))
    return pl.pallas_call(
        functools.partial(_mm_res_kernel, n_in=n_in),
        grid=(n // tm, m // tn),
        in_specs=in_specs,
        out_specs=pl.BlockSpec((tm, tn), lambda i, j: (i, j)),
        out_shape=jax.ShapeDtypeStruct((n, m), F32),
        compiler_params=_cparams(("parallel", "arbitrary")),
    )(*xs, *([w] * n_in), res)


def _rope64(x, c, s_lo, s_hi):
    return x * c + pltpu.roll(x, 96, 1) * s_lo + pltpu.roll(x, 32, 1) * s_hi


def _mla_up_kernel(u_ref, pe_ref, qg_ref, kvg_ref, wq_ref, wk_ref, wv_ref,
                   c_ref, slo_ref, shi_ref, q_ref, k_ref, v_ref):
    u = u_ref[...]
    c, s_lo, s_hi = c_ref[...], slo_ref[...], shi_ref[...]
    qn = _rms(u[:, EV_CQ:EV_CQ + MLA_Q_LORA], qg_ref[...]).astype(BF16)
    kvn = _rms(u[:, EV_CKV:EV_CKV + MLA_KV_LORA], kvg_ref[...]).astype(BF16)
    q = _dot(qn, wq_ref[...])
    kn = _dot(kvn, wk_ref[...])
    v_ref[...] = _dot(kvn, wv_ref[...]).astype(v_ref.dtype)
    kpe = _rope64(pe_ref[...], c, s_lo, s_hi).astype(k_ref.dtype)
    for h in range(MLA_HEADS):
        a = h * MLA_QK_PAD
        q_ref[:, a:a + LANES] = q[:, a:a + LANES].astype(q_ref.dtype)
        q_ref[:, a + LANES:a + 2 * LANES] = _rope64(q[:, a + LANES:a + 2 * LANES], c, s_lo, s_hi).astype(q_ref.dtype)
        k_ref[:, a:a + LANES] = kn[:, h * LANES:(h + 1) * LANES].astype(k_ref.dtype)
        k_ref[:, a + LANES:a + 2 * LANES] = kpe


def mla_up(u, q_norm, kv_norm, wq, wk, wv, rope_c, rope_slo, rope_shi, *, seq, tm):
    n = u.shape[0]
    hq = MLA_HEADS * MLA_QK_PAD
    hv = MLA_HEADS * MLA_V
    ab = EV_REC
    assert seq % tm == 0 and EV_PE % LANES == 0
    nsb = seq // tm
    row_spec = pl.BlockSpec((tm, LANES), lambda i: (i % nsb, 0))
    full = lambda a: pl.BlockSpec(a.shape, lambda i: (0,) * a.ndim)
    qg = q_norm.reshape(1, -1)
    kvg = kv_norm.reshape(1, -1)
    return pl.pallas_call(
        _mla_up_kernel,
        grid=(n // tm,),
        in_specs=[pl.BlockSpec((tm, ab), lambda i: (i, 0)),
                  pl.BlockSpec((tm, LANES), lambda i: (i, EV_PE // LANES)),
                  full(qg), full(kvg), full(wq), full(wk), full(wv),
                  row_spec, row_spec, row_spec],
        out_specs=[pl.BlockSpec((tm, hq), lambda i: (i, 0)),
                   pl.BlockSpec((tm, hq), lambda i: (i, 0)),
                   pl.BlockSpec((tm, hv), lambda i: (i, 0))],
        out_shape=[jax.ShapeDtypeStruct((n, hq), BF16),
                   jax.ShapeDtypeStruct((n, hq), BF16),
                   jax.ShapeDtypeStruct((n, hv), BF16)],
        compiler_params=_cparams(("parallel",)),
    )(u, u, qg, kvg, wq, wk, wv, rope_c, rope_slo, rope_shi)


def _softmax_step(s, v, m_ref, l_ref, acc_ref):
    m_prev = m_ref[...]
    m_new = jnp.maximum(m_prev, jnp.max(s, axis=-1, keepdims=True))
    alpha = jnp.exp(m_prev - m_new)
    p = jnp.exp(s - m_new)
    l_ref[...] = alpha * l_ref[...] + jnp.sum(p, axis=-1, keepdims=True)
    acc_ref[...] = alpha * acc_ref[...] + _dot(p.astype(BF16), v)
    m_ref[...] = m_new


def _mla_attn_kernel(q_ref, k_ref, v_ref, o_ref, m_ref, l_ref, acc_ref, *, scale, tq, tk, n_chunks):
    qi = pl.program_id(2)
    q = q_ref[0]
    m_ref[...] = jnp.full(m_ref.shape, NEG_INF, F32)
    l_ref[...] = jnp.zeros(l_ref.shape, F32)
    acc_ref[...] = jnp.zeros(acc_ref.shape, F32)
    qpos = qi * tq + lax.broadcasted_iota(jnp.int32, (tq, 1), 0)
    for c in range(n_chunks):
        @pl.when(c * tk <= qi * tq + (tq - 1))
        def _():
            k = k_ref[0, c * tk:(c + 1) * tk, :]
            v = v_ref[0, c * tk:(c + 1) * tk, :]
            s = _dot_nt(q, k) * scale
            kpos = c * tk + lax.broadcasted_iota(jnp.int32, (1, tk), 1)
            s = jnp.where(kpos <= qpos, s, NEG_INF)
            _softmax_step(s, v, m_ref, l_ref, acc_ref)
    o_ref[0] = (acc_ref[...] / l_ref[...]).astype(o_ref.dtype)


def mla_attention(q, k, v, *, tq, tk):
    b, s, _ = q.shape
    scale = (MLA_NOPE + MLA_ROPE) ** -0.5
    kern = functools.partial(_mla_attn_kernel, scale=scale, tq=tq, tk=tk, n_chunks=s // tk)
    return pl.pallas_call(
        kern,
        grid=(b, MLA_HEADS, s // tq),
        in_specs=[pl.BlockSpec((1, tq, MLA_QK_PAD), lambda b_, h, i: (b_, i, h)),
                  pl.BlockSpec((1, s, MLA_QK_PAD), lambda b_, h, i: (b_, 0, h)),
                  pl.BlockSpec((1, s, MLA_V), lambda b_, h, i: (b_, 0, h))],
        out_specs=pl.BlockSpec((1, tq, MLA_V), lambda b_, h, i: (b_, i, h)),
        out_shape=jax.ShapeDtypeStruct((b, s, MLA_HEADS * MLA_V), BF16),
        scratch_shapes=[pltpu.VMEM((tq, 1), F32), pltpu.VMEM((tq, 1), F32), pltpu.VMEM((tq, MLA_V), F32)],
        compiler_params=_cparams(("parallel", "parallel", "arbitrary")),
    )(q, k, v)


def _rglru_kernel(x_ref, y_ref, cw_ref, cb_ref, gaw_ref, gab_ref, gxw_ref, gxb_ref, lam_ref,
                  o_ref, xbuf, h_ref, *, ts):
    t = pl.program_id(2)

    @pl.when(t == 0)
    def _():
        xbuf[0:8, :] = jnp.zeros((8, LANES), F32)
        h_ref[...] = jnp.zeros(h_ref.shape, F32)

    x = x_ref[0]
    xbuf[8:, :] = x
    cw = cw_ref[...]
    xc = cb_ref[...] + cw[3:4] * x
    for kk in range(CONV_WIDTH - 1):
        back = CONV_WIDTH - 1 - kk
        xc = xc + cw[kk:kk + 1] * xbuf[8 - back:8 - back + ts, :]
    xbuf[0:8, :] = x[ts - 8:, :]

    xcb = xc.astype(BF16)
    r = _sigmoid(_dot(xcb, gaw_ref[0]) + gab_ref[0])
    gi = _sigmoid(_dot(xcb, gxw_ref[0]) + gxb_ref[0])
    z = -lam_ref[...]
    softplus = jnp.maximum(z, 0.0) + jnp.log1p(jnp.exp(-jnp.abs(z)))
    log_a = (-LRU_C) * r * softplus
    a = jnp.exp(log_a)
    mult = jnp.sqrt(-jnp.tanh(log_a) * (a * a + 1.0))
    row = lax.broadcasted_iota(jnp.int32, (ts, 1), 0)
    mult = jnp.where(row + t * ts == 0, 1.0, mult)
    bv = mult * gi * xc

    d = 1
    while d < ts:
        keep = row >= d
        a_sh = jnp.where(keep, pltpu.roll(a, d, 0), 1.0)
        b_sh = jnp.where(keep, pltpu.roll(bv, d, 0), 0.0)
        bv = a * b_sh + bv
        a = a * a_sh
        d *= 2
    h = bv + a * h_ref[...]
    h_ref[...] = h[ts - 1:ts, :]
    o_ref[0] = (h * _gelu_tanh(y_ref[0])).astype(o_ref.dtype)


def rglru(u3, conv_w, conv_b, ga_w, ga_b, gx_w, gx_b, lam, *, ts):
    b, s, _ = u3.shape
    rec0 = EV_REC // LANES
    gate0 = EV_GATE // LANES
    cb = conv_b.reshape(1, LRU_WIDTH)
    gab = ga_b.reshape(LRU_BLOCKS, 1, LRU_BLOCK_W)
    gxb = gx_b.reshape(LRU_BLOCKS, 1, LRU_BLOCK_W)
    lam2 = lam.reshape(1, LRU_WIDTH)
    blk_w = pl.BlockSpec((1, LRU_BLOCK_W, LRU_BLOCK_W), lambda b_, n, t: (n, 0, 0))
    blk_b = pl.BlockSpec((1, 1, LRU_BLOCK_W), lambda b_, n, t: (n, 0, 0))
    vec = pl.BlockSpec((1, LANES), lambda b_, n, t: (0, n))
    return pl.pallas_call(
        functools.partial(_rglru_kernel, ts=ts),
        grid=(b, LRU_BLOCKS, s // ts),
        in_specs=[pl.BlockSpec((1, ts, LANES), lambda b_, n, t: (b_, t, rec0 + n)),
                  pl.BlockSpec((1, ts, LANES), lambda b_, n, t: (b_, t, gate0 + n)),
                  pl.BlockSpec((CONV_WIDTH, LANES), lambda b_, n, t: (0, n)),
                  vec, blk_w, blk_b, blk_w, blk_b, vec],
        out_specs=pl.BlockSpec((1, ts, LANES), lambda b_, n, t: (b_, t, n)),
        out_shape=jax.ShapeDtypeStruct((b, s, LRU_WIDTH), BF16),
        scratch_shapes=[pltpu.VMEM((ts + 8, LANES), F32), pltpu.VMEM((1, LANES), F32)],
        compiler_params=_cparams(("parallel", "parallel", "arbitrary")),
    )(u3, u3, conv_w, cb, ga_w, gab, gx_w, gxb, lam2)


def _ffn_kernel(x_ref, g_ref, w1_ref, w3_ref, w2_ref, o_ref, xn_ref):
    f = pl.program_id(1)

    @pl.when(f == 0)
    def _():
        x = x_ref[...]
        xn_ref[...] = _rms(x, g_ref[...]).astype(BF16)
        o_ref[...] = x

    xn = xn_ref[...]
    h1 = _dot(xn, w1_ref[...])
    h3 = _dot(xn, w3_ref[...])
    hh = (h1 * _sigmoid(h1) * h3).astype(BF16)
    o_ref[...] += _dot(hh, w2_ref[...])


def ffn_dense(x, g, w1, w3, w2, *, tm, tf):
    n, d = x.shape
    ff = w1.shape[1]
    assert n % tm == 0 and ff % tf == 0
    return pl.pallas_call(
        _ffn_kernel,
        grid=(n // tm, ff // tf),
        in_specs=[pl.BlockSpec((tm, d), lambda i, f: (i, 0)),
                  pl.BlockSpec((1, d), lambda i, f: (0, 0)),
                  pl.BlockSpec((d, tf), lambda i, f: (0, f)),
                  pl.BlockSpec((d, tf), lambda i, f: (0, f)),
                  pl.BlockSpec((tf, d), lambda i, f: (f, 0))],
        out_specs=pl.BlockSpec((tm, d), lambda i, f: (i, 0)),
        out_shape=jax.ShapeDtypeStruct((n, d), F32),
        scratch_shapes=[pltpu.VMEM((tm, d), BF16)],
        compiler_params=_cparams(("parallel", "arbitrary")),
    )(x, g.reshape(1, d), w1, w3, w2)


def _compress_kernel(kc_ref, pe_ref, w1_ref, b1_ref, w2_ref, o_ref, *, n_half):
    pe = pe_ref[...]
    half_k = n_half * NSA_DH
    for g in range(NSA_GROUPS):
        lo, hi = [], []
        for l in range(n_half):
            a = l * NSA_KV_DIM + g * NSA_DH
            piece = kc_ref[0, :, a:a + NSA_DH]
            lo.append((piece + pe[l:l + 1]).astype(BF16))
            hi.append((piece + pe[n_half + l:n_half + l + 1]).astype(BF16))
        z0 = _dot(jnp.concatenate(lo, axis=1), w1_ref[0:half_k, :])
        z1 = _dot(jnp.concatenate(hi, axis=1), w1_ref[half_k:2 * half_k, :])
        rows = z1.shape[0]
        pre = z0 + pltpu.roll(z1, rows - 1, 0) + b1_ref[...]
        o_ref[0, :, g * NSA_DH:(g + 1) * NSA_DH] = _dot(_gelu_tanh(pre).astype(BF16), w2_ref[...]).astype(o_ref.dtype)


def nsa_compress(kc3, pe, w1, b1, w2):
    b, nchunk, width = kc3.shape
    n_half = CMP_BLOCK // CMP_STRIDE
    assert CMP_BLOCK == 2 * CMP_STRIDE and width == CMP_STRIDE * NSA_KV_DIM
    full = lambda a: pl.BlockSpec(a.shape, lambda i: (0,) * a.ndim)
    b1r = b1.reshape(1, NSA_DH)
    return pl.pallas_call(
        functools.partial(_compress_kernel, n_half=CMP_STRIDE),
        grid=(b,),
        in_specs=[pl.BlockSpec((1, nchunk, width), lambda i: (i, 0, 0)),
                  full(pe), full(w1), full(b1r), full(w2)],
        out_specs=pl.BlockSpec((1, nchunk, NSA_KV_DIM), lambda i: (i, 0, 0)),
        out_shape=jax.ShapeDtypeStruct((b, nchunk, NSA_KV_DIM), BF16),
        compiler_params=_cparams(("parallel",)),
    )(kc3, pe, w1, b1r, w2)


def _rope128(x, c, s):
    return x * c + pltpu.roll(x, NSA_DH // 2, 1) * s


def _split3(x):
    hi = x.astype(BF16)
    r1 = x - hi.astype(F32)
    mid = r1.astype(BF16)
    lo = (r1 - mid.astype(F32)).astype(BF16)
    return hi, mid, lo


def _nsa_attn_kernel(q_ref, ks_ref, vs_ref, kw_ref, vw_ref, gt_ref, kc_ref, vc_ref,
                     cq_ref, sq_ref, ck_ref, sk_ref, ovl_ref, exp_ref,
                     o_ref, ksr, vsb, kwr, vwb, selx, m_ref, l_ref, acc_ref,
                     *, tq, tk, seq, scale):
    qi = pl.program_id(2)
    hp = NSA_HPG
    n_sel = seq // SEL_BLOCK

    @pl.when(qi == 0)
    def _():
        ck, sk = ck_ref[...], sk_ref[...]
        ksr[...] = _rope128(ks_ref[0], ck, sk).astype(BF16)
        kwr[...] = _rope128(kw_ref[0], ck, sk).astype(BF16)
        vsb[...] = vs_ref[0].astype(BF16)
        vwb[...] = vw_ref[0].astype(BF16)

    q = q_ref[0]
    cq, sq = cq_ref[...], sq_ref[...]
    heads = [q[:, p * NSA_DH:(p + 1) * NSA_DH] for p in range(hp)]
    qu = jnp.concatenate(heads, axis=0).astype(BF16)
    qr = jnp.concatenate([_rope128(h, cq, sq) for h in heads], axis=0).astype(BF16)
    t_row = qi * tq + lax.broadcasted_iota(jnp.int32, (tq, 1), 0)
    lane = lax.broadcasted_iota(jnp.int32, (1, LANES), 1)

    sc = (_dot_nt(qu, kc_ref[0]) * scale).reshape(hp, tq, LANES)
    valid = (lane * CMP_STRIDE + (CMP_BLOCK - 1) <= t_row)[None]
    sm = jnp.where(valid, sc, NEG_INF)
    e = jnp.exp(sm - jnp.max(sm, axis=-1, keepdims=True))
    p = jnp.where(valid, e / jnp.sum(e, axis=-1, keepdims=True), 0.0)
    o_cmp = _dot(p.reshape(hp * tq, LANES).astype(BF16), vc_ref[0])

    psum = p[0]
    for i in range(1, hp):
        psum = psum + p[i]
    ovl = ovl_ref[...]
    imp = sum(_dot(part, ovl) for part in _split3(psum))
    cur = jnp.right_shift(t_row, SEL_BLOCK.bit_length() - 1)
    future = lane > cur
    forced = (lane == 0) | (lane == cur) | (lane == cur - 1)
    score = jnp.where(future, -1.0, jnp.where(forced, SEL_FORCE, imp))
    score = jnp.where(lane < n_sel, score, -2.0)
    cnt = jnp.zeros((tq, LANES), F32)
    for j in range(n_sel):
        col = score[:, j:j + 1]
        beats = (col > score) | ((col == score) & (lane > j))
        cnt = cnt + jnp.where(beats, 1.0, 0.0)
    sel = jnp.where((cnt < SEL_COUNT) & (lane < n_sel), 1.0, 0.0).astype(BF16)
    selx[...] = _dot(sel, exp_ref[...])

    m_ref[...] = jnp.full(m_ref.shape, NEG_INF, F32)
    l_ref[...] = jnp.zeros(l_ref.shape, F32)
    acc_ref[...] = jnp.zeros(acc_ref.shape, F32)
    for c in range(seq // tk):
        @pl.when(c * tk <= qi * tq + (tq - 1))
        def _():
            s = (_dot_nt(qr, ksr[c * tk:(c + 1) * tk, :]) * scale).reshape(hp, tq, tk)
            kpos = c * tk + lax.broadcasted_iota(jnp.int32, (1, tk), 1)
            allowed = (kpos <= t_row) & (selx[:, c * tk:(c + 1) * tk] > 0.5)
            s = jnp.where(allowed[None], s, NEG_INF).reshape(hp * tq, tk)
            _softmax_step(s, vsb[c * tk:(c + 1) * tk, :], m_ref, l_ref, acc_ref)
    o_sel = acc_ref[...] / l_ref[...]

    span = WINDOW + tq
    start = pl.multiple_of(jnp.maximum(qi * tq - WINDOW, 0), tq)
    sw = (_dot_nt(qr, kwr[pl.ds(start, span), :]) * scale).reshape(hp, tq, span)
    kpos = start + lax.broadcasted_iota(jnp.int32, (1, span), 1)
    inwin = ((kpos <= t_row) & (kpos > t_row - WINDOW))[None]
    sw = jnp.where(inwin, sw, NEG_INF)
    ew = jnp.exp(sw - jnp.max(sw, axis=-1, keepdims=True))
    pw = (ew / jnp.sum(ew, axis=-1, keepdims=True)).reshape(hp * tq, span)
    o_win = _dot(pw.astype(BF16), vwb[pl.ds(start, span), :])

    gates = _sigmoid(gt_ref[0])
    for i in range(hp):
        rows = slice(i * tq, (i + 1) * tq)
        o = (gates[:, 3 * i:3 * i + 1] * o_cmp[rows] + gates[:, 3 * i + 1:3 * i + 2] * o_sel[rows]
             + gates[:, 3 * i + 2:3 * i + 3] * o_win[rows])
        o_ref[0, :, i * NSA_DH:(i + 1) * NSA_DH] = o.astype(o_ref.dtype)


def nsa_attention(u3, k_cmp, v_cmp, rope_c, rope_s, ovl, expand, *, tq, tk):
    b, s, _ = u3.shape
    assert s // CMP_STRIDE == LANES and tq % SEL_BLOCK == 0 and WINDOW % tq == 0
    hp = NSA_HPG
    col = lambda off: (lambda b_, g, i: (b_, 0, off // NSA_DH + g))
    seq_blk = lambda off: pl.BlockSpec((1, s, NSA_DH), col(off))
    full = lambda a: pl.BlockSpec(a.shape, lambda b_, g, i: (0,) * a.ndim)
    cmp_blk = pl.BlockSpec((1, LANES, NSA_DH), lambda b_, g, i: (b_, 0, g))
    rope_q = pl.BlockSpec((tq, NSA_DH), lambda b_, g, i: (i, 0))
    kern = functools.partial(_nsa_attn_kernel, tq=tq, tk=tk, seq=s, scale=NSA_DH ** -0.5)
    return pl.pallas_call(
        kern,
        grid=(b, NSA_GROUPS, s // tq),
        in_specs=[pl.BlockSpec((1, tq, hp * NSA_DH), lambda b_, g, i: (b_, i, g)),
                  seq_blk(OD_KS), seq_blk(OD_VS), seq_blk(OD_KW), seq_blk(OD_VW),
                  pl.BlockSpec((1, tq, LANES), lambda b_, g, i: (b_, i, OD_G // LANES + g)),
                  cmp_blk, cmp_blk, rope_q, rope_q, full(rope_c), full(rope_s), full(ovl), full(expand)],
        out_specs=pl.BlockSpec((1, tq, hp * NSA_DH), lambda b_, g, i: (b_, i, g)),
        out_shape=jax.ShapeDtypeStruct((b, s, NSA_Q_DIM), BF16),
        scratch_shapes=[pltpu.VMEM((s, NSA_DH), BF16)] * 4 + [
            pltpu.VMEM((tq, s), F32),
            pltpu.VMEM((hp * tq, 1), F32), pltpu.VMEM((hp * tq, 1), F32), pltpu.VMEM((hp * tq, NSA_DH), F32)],
        compiler_params=_cparams(("parallel", "parallel", "arbitrary")),
    )(u3, u3, u3, u3, u3, u3, k_cmp, v_cmp, rope_c, rope_s, rope_c, rope_s, ovl, expand)


def _router_kernel(x_ref, g_ref, wr_ref, br_ref, xn_ref, meta_ref, cnt_ref, *, tm):
    @pl.when(pl.program_id(0) == 0)
    def _():
        cnt_ref[...] = jnp.zeros(cnt_ref.shape, F32)

    xn = _rms(x_ref[...], g_ref[...])
    xn_ref[...] = xn.astype(xn_ref.dtype)
    xh, xm, _ = _split3(xn)
    wh, wm, _ = _split3(wr_ref[...])
    logits = _dot(xh, wh) + _dot(xh, wm) + _dot(xm, wh) + br_ref[...]
    lane = lax.broadcasted_iota(jnp.int32, (1, LANES), 1).astype(F32)
    lg = jnp.where(lane < N_EXPERTS, logits, NEG_INF)
    m1 = jnp.max(lg, axis=-1, keepdims=True)
    e1 = jnp.min(jnp.where(lg == m1, lane, float(LANES)), axis=-1, keepdims=True)
    lg2 = jnp.where(lane == e1, NEG_INF, lg)
    m2 = jnp.max(lg2, axis=-1, keepdims=True)
    e2 = jnp.min(jnp.where(lg2 == m2, lane, float(LANES)), axis=-1, keepdims=True)
    ex = jnp.exp(m2 - m1)
    den = 1.0 + ex
    g1 = 1.0 / den
    g2 = ex / den
    oh = jnp.where((lane == e1) | (lane == e2), 1.0, 0.0)
    r = lax.broadcasted_iota(jnp.int32, (tm, tm), 0)
    c = lax.broadcasted_iota(jnp.int32, (tm, tm), 1)
    tri = jnp.where(r > c, 1.0, 0.0).astype(BF16)
    cum = _dot(tri, oh.astype(BF16)) + cnt_ref[0:1, :]
    pos1 = jnp.sum(jnp.where(lane == e1, cum, 0.0), axis=-1, keepdims=True)
    pos2 = jnp.sum(jnp.where(lane == e2, cum, 0.0), axis=-1, keepdims=True)
    cnt_ref[...] = cnt_ref[...] + jnp.sum(oh, axis=0, keepdims=True)
    meta = jnp.where(lane == 0, e1, 0.0)
    meta = jnp.where(lane == 1, e2, meta)
    meta = jnp.where(lane == 2, g1, meta)
    meta = jnp.where(lane == 3, g2, meta)
    meta = jnp.where(lane == 4, pos1, meta)
    meta = jnp.where(lane == 5, pos2, meta)
    meta_ref[...] = meta


def moe_router(x, g, wr, br, *, tm):
    n, d = x.shape
    return pl.pallas_call(
        functools.partial(_router_kernel, tm=tm),
        grid=(n // tm,),
        in_specs=[pl.BlockSpec((tm, d), lambda i: (i, 0)),
                  pl.BlockSpec((1, d), lambda i: (0, 0)),
                  pl.BlockSpec((d, LANES), lambda i: (0, 0)),
                  pl.BlockSpec((1, LANES), lambda i: (0, 0))],
        out_specs=[pl.BlockSpec((tm, d), lambda i: (i, 0)),
                   pl.BlockSpec((tm, LANES), lambda i: (i, 0)),
                   pl.BlockSpec((8, LANES), lambda i: (0, 0))],
        out_shape=[jax.ShapeDtypeStruct((n, d), BF16),
                   jax.ShapeDtypeStruct((n, LANES), F32),
                   jax.ShapeDtypeStruct((8, LANES), F32)],
        compiler_params=_cparams(("arbitrary",)),
    )(x, g.reshape(1, d), wr, br)


def _expert_kernel(te_ref, tr_ref, x_ref, w1_ref, w3_ref, w2_ref, o_ref):
    t = pl.program_id(0)
    f = pl.program_id(1)
    rows = tr_ref[t]

    @pl.when(f == 0)
    def _():
        o_ref[...] = jnp.zeros(o_ref.shape, o_ref.dtype)

    @pl.when(rows > 0)
    def _():
        w1 = w1_ref[0].astype(BF16)
        w3 = w3_ref[0].astype(BF16)
        w2 = w2_ref[0].astype(BF16)
        for sb in range(MOE_TILE // MOE_SUB):
            @pl.when(sb * MOE_SUB < rows)
            def _():
                rs = slice(sb * MOE_SUB, (sb + 1) * MOE_SUB)
                xb = x_ref[rs, :]
                h1 = _dot(xb, w1)
                h3 = _dot(xb, w3)
                hh = (h1 * _sigmoid(h1) * h3).astype(BF16)
                o_ref[rs, :] += _dot(hh, w2)


def moe_experts(xr, tile_e, tile_rows, w1, w3, w2, *, tf):
    n_rows, d = xr.shape
    n_tiles = n_rows // MOE_TILE
    ff = w1.shape[2]
    nf = ff // tf

    def f_eff(t, f, tr):
        return jnp.where(tr[t] > 0, f, nf - 1)

    grid_spec = pltpu.PrefetchScalarGridSpec(
        num_scalar_prefetch=2,
        grid=(n_tiles, nf),
        in_specs=[pl.BlockSpec((MOE_TILE, d), lambda t, f, te, tr: (t, 0)),
                  pl.BlockSpec((1, d, tf), lambda t, f, te, tr: (te[t], 0, f_eff(t, f, tr))),
                  pl.BlockSpec((1, d, tf), lambda t, f, te, tr: (te[t], 0, f_eff(t, f, tr))),
                  pl.BlockSpec((1, tf, d), lambda t, f, te, tr: (te[t], f_eff(t, f, tr), 0))],
        out_specs=pl.BlockSpec((MOE_TILE, d), lambda t, f, te, tr: (t, 0)),
    )
    return pl.pallas_call(
        _expert_kernel,
        grid_spec=grid_spec,
        out_shape=jax.ShapeDtypeStruct((n_rows, d), F32),
        compiler_params=_cparams(("parallel", "arbitrary")),
    )(tile_e, tile_rows, xr, w1, w3, w2)


def _combine_kernel(x_ref, y1_ref, y2_ref, meta_ref, g_ref, o_ref, *, final_norm):
    meta = meta_ref[...]
    y = x_ref[...] + (meta[:, 2:3] * y1_ref[...] + meta[:, 3:4] * y2_ref[...])
    if final_norm:
        y = _rms(y, g_ref[...])
    o_ref[...] = y


def moe_combine(x, y1, y2, meta, g, *, final_norm, tm):
    n, d = x.shape
    row = pl.BlockSpec((tm, d), lambda i: (i, 0))
    return pl.pallas_call(
        functools.partial(_combine_kernel, final_norm=final_norm),
        grid=(n // tm,),
        in_specs=[row, row, row, pl.BlockSpec((tm, LANES), lambda i: (i, 0)),
                  pl.BlockSpec((1, d), lambda i: (0, 0))],
        out_specs=row,
        out_shape=jax.ShapeDtypeStruct((n, d), F32),
        compiler_params=_cparams(("parallel",)),
    )(x, y1, y2, meta, g.reshape(1, d))


def _mla_rope_tables(seq):
    half = MLA_ROPE // 2
    pos = jnp.arange(seq, dtype=F32)
    inv = ROPE_THETA ** (-jnp.arange(0, MLA_ROPE, 2, dtype=F32) / MLA_ROPE)
    ang = pos[:, None] * inv[None, :]
    cos, sin = jnp.cos(ang), jnp.sin(ang)
    z = jnp.zeros((seq, half), F32)
    pad = jnp.zeros((seq, LANES - MLA_ROPE), F32)
    c = jnp.concatenate([cos, cos, pad], axis=1)
    s_lo = jnp.concatenate([-sin, z, pad], axis=1)
    s_hi = jnp.concatenate([z, sin, pad], axis=1)
    return c, s_lo, s_hi


def _nsa_rope_tables(seq):
    pos = jnp.arange(seq, dtype=F32)
    inv = ROPE_THETA ** (-jnp.arange(0, NSA_DH, 2, dtype=F32) / NSA_DH)
    ang = pos[:, None] * inv[None, :]
    cos, sin = jnp.cos(ang), jnp.sin(ang)
    return jnp.concatenate([cos, cos], axis=1), jnp.concatenate([-sin, sin], axis=1)


def _selection_constants(seq):
    n_sel = seq // SEL_BLOCK
    nc = (seq - CMP_BLOCK) // CMP_STRIDE + 1
    cmp_start = np.arange(LANES) * CMP_STRIDE
    sel_start = np.arange(LANES) * SEL_BLOCK
    ovl = ((cmp_start[:, None] < sel_start[None, :] + SEL_BLOCK) &
           (cmp_start[:, None] + CMP_BLOCK > sel_start[None, :]))
    ovl &= (np.arange(LANES)[:, None] < nc) & (np.arange(LANES)[None, :] < n_sel)
    expand = (np.arange(seq)[None, :] // SEL_BLOCK == np.arange(LANES)[:, None])
    return jnp.asarray(ovl, BF16), jnp.asarray(expand, BF16)


def even_layer(x, seq, p):
    n = x.shape[0]
    b = n // seq
    (norm_mix, w_in, q_norm, w_q_up, kv_norm, w_kv_up, conv_w, conv_b, ga_w, ga_b, gx_w, gx_b,
     lam, w_out, norm_ffn, w1, w3, w2) = p
    d = D_MODEL
    o1 = MLA_Q_LORA + MLA_KV_LORA
    o2 = o1 + MLA_ROPE
    w_pack = jnp.concatenate(
        [w_in[:, :o1], w_in[:, o2:], w_in[:, o1:o2], jnp.zeros((d, LANES - MLA_ROPE), F32)], axis=1).astype(BF16)
    u = norm_matmul(x, norm_mix, w_pack, tm=512, tn=EV_PACKED // 3)

    wq = w_q_up.reshape(MLA_Q_LORA, MLA_HEADS, MLA_NOPE + MLA_ROPE)
    wq = jnp.pad(wq, ((0, 0), (0, 0), (0, MLA_QK_PAD - MLA_NOPE - MLA_ROPE)))
    wq = wq.reshape(MLA_Q_LORA, MLA_HEADS * MLA_QK_PAD).astype(BF16)
    wkv = w_kv_up.reshape(MLA_KV_LORA, MLA_HEADS, MLA_NOPE + MLA_V)
    wk = wkv[:, :, :MLA_NOPE].reshape(MLA_KV_LORA, MLA_HEADS * MLA_NOPE).astype(BF16)
    wv = wkv[:, :, MLA_NOPE:].reshape(MLA_KV_LORA, MLA_HEADS * MLA_V).astype(BF16)
    rc, rlo, rhi = _mla_rope_tables(seq)
    q, k, v = mla_up(u, q_norm, kv_norm, wq, wk, wv, rc, rlo, rhi, seq=seq, tm=512)
    o_mla = mla_attention(q.reshape(b, seq, -1), k.reshape(b, seq, -1), v.reshape(b, seq, -1), tq=512, tk=512)

    o_rec = rglru(u.reshape(b, seq, EV_PACKED), conv_w, conv_b, ga_w.astype(BF16), ga_b,
                  gx_w.astype(BF16), gx_b, lam, ts=512)
    x = matmul_residual([o_mla.reshape(n, -1), o_rec.reshape(n, -1)], w_out.astype(BF16), x, tm=512, tn=1024)
    return ffn_dense(x, norm_ffn, w1.astype(BF16), w3.astype(BF16), w2.astype(BF16), tm=512, tf=512)


def _moe_dispatch(meta, counts, n):
    e1 = meta[:, 0].astype(jnp.int32)
    e2 = meta[:, 1].astype(jnp.int32)
    pos1 = meta[:, 4].astype(jnp.int32)
    pos2 = meta[:, 5].astype(jnp.int32)
    cnt = counts[0, :N_EXPERTS].astype(jnp.int32)
    n_tiles = (n * TOP_K) // MOE_TILE + N_EXPERTS
    tiles_e = (cnt + MOE_TILE - 1) // MOE_TILE
    tend = jnp.cumsum(tiles_e)
    tstart = tend - tiles_e
    d1 = tstart[e1] * MOE_TILE + pos1
    d2 = tstart[e2] * MOE_TILE + pos2
    tid = jnp.arange(n_tiles, dtype=jnp.int32)
    te = jnp.minimum(jnp.searchsorted(tend, tid, side='right'), N_EXPERTS - 1).astype(jnp.int32)
    rows = jnp.clip(cnt[te] - (tid - tstart[te]) * MOE_TILE, 0, MOE_TILE)
    rows = jnp.where(tid < tend[-1], rows, 0).astype(jnp.int32)
    last_e = te[jnp.maximum(tend[-1] - 1, 0)]
    te = jnp.where(tid < tend[-1], te, last_e).astype(jnp.int32)
    tok = jnp.arange(n, dtype=jnp.int32)
    row_tok = jnp.zeros((n_tiles * MOE_TILE,), jnp.int32).at[d1].set(tok).at[d2].set(tok)
    return d1, d2, row_tok, te, rows


def odd_layer(x, seq, p, final_g):
    n = x.shape[0]
    b = n // seq
    (norm_mix, w_in, ck_pe, ck_w1, ck_b1, ck_w2, cv_pe, cv_w1, cv_b1, cv_w2, w_out, norm_ffn,
     router_w, router_b, ew1, ew3, ew2) = p
    d = D_MODEL
    wg = w_in[:, OD_G:].reshape(d, NSA_GROUPS, NSA_HPG * 3)
    wg = jnp.pad(wg, ((0, 0), (0, 0), (0, LANES - NSA_HPG * 3))).reshape(d, NSA_GROUPS * LANES)
    w_pack = jnp.concatenate([w_in[:, :OD_G], wg], axis=1).astype(BF16)
    u = norm_matmul(x, norm_mix, w_pack, tm=512, tn=512)
    u3 = u.reshape(b, seq, OD_PACKED)

    nchunk = seq // CMP_STRIDE
    kc3 = u3[:, :, OD_KC:OD_KC + NSA_KV_DIM].reshape(b, nchunk, CMP_STRIDE * NSA_KV_DIM)
    vc3 = u3[:, :, OD_VC:OD_VC + NSA_KV_DIM].reshape(b, nchunk, CMP_STRIDE * NSA_KV_DIM)
    k_cmp = nsa_compress(kc3, ck_pe, ck_w1.astype(BF16), ck_b1, ck_w2.astype(BF16))
    v_cmp = nsa_compress(vc3, cv_pe, cv_w1.astype(BF16), cv_b1, cv_w2.astype(BF16))
    rc, rs = _nsa_rope_tables(seq)
    ovl, expand = _selection_constants(seq)
    o = nsa_attention(u3, k_cmp, v_cmp, rc, rs, ovl, expand, tq=256, tk=512)
    x = matmul_residual([o.reshape(n, -1)], w_out.astype(BF16), x, tm=512, tn=1024)

    wr = jnp.pad(router_w, ((0, 0), (0, LANES - N_EXPERTS)))
    br = jnp.pad(router_b, (0, LANES - N_EXPERTS)).reshape(1, LANES)
    xn, meta, counts = moe_router(x, norm_ffn, wr, br, tm=512)
    d1, d2, row_tok, te, rows = _moe_dispatch(meta, counts, n)
    xr = jnp.take(xn, row_tok, axis=0)
    yr = moe_experts(xr, te, rows, ew1, ew3, ew2, tf=256)
    y1 = jnp.take(yr, d1, axis=0)
    y2 = jnp.take(yr, d2, axis=0)
    g = final_g if final_g is not None else norm_ffn
    return moe_combine(x, y1, y2, meta, g, final_norm=final_g is not None, tm=512)


def kernel(x, ev_norm_mix, ev_w_in, ev_q_norm, ev_w_q_up, ev_kv_norm, ev_w_kv_up, ev_conv_w, ev_conv_b, ev_gate_a_w, ev_gate_a_b, ev_gate_x_w, ev_gate_x_b, ev_lru_lambda, ev_w_out, ev_norm_ffn, ev_ffn_w1, ev_ffn_w3, ev_ffn_w2, od_norm_mix, od_w_in, od_cmp_k_pe, od_cmp_k_w1, od_cmp_k_b1, od_cmp_k_w2, od_cmp_v_pe, od_cmp_v_w1, od_cmp_v_b1, od_cmp_v_w2, od_w_out, od_norm_ffn, od_router_w, od_router_b, od_exp_w1, od_exp_w3, od_exp_w2, final_norm):
    bsz, seq, d = x.shape
    ev = (ev_norm_mix, ev_w_in, ev_q_norm, ev_w_q_up, ev_kv_norm, ev_w_kv_up, ev_conv_w, ev_conv_b,
          ev_gate_a_w, ev_gate_a_b, ev_gate_x_w, ev_gate_x_b, ev_lru_lambda, ev_w_out, ev_norm_ffn,
          ev_ffn_w1, ev_ffn_w3, ev_ffn_w2)
    od = (od_norm_mix, od_w_in, od_cmp_k_pe, od_cmp_k_w1, od_cmp_k_b1, od_cmp_k_w2, od_cmp_v_pe,
          od_cmp_v_w1, od_cmp_v_b1, od_cmp_v_w2, od_w_out, od_norm_ffn, od_router_w, od_router_b,
          od_exp_w1, od_exp_w3, od_exp_w2)
    h = x.reshape(bsz * seq, d)
    for layer in range(DEPTH):
        i = layer // 2
        if layer % 2 == 0:
            h = even_layer(h, seq, tuple(a[i] for a in ev))
        else:
            h = odd_layer(h, seq, tuple(a[i] for a in od), final_norm if layer == DEPTH - 1 else None)
    return h.reshape(bsz, seq, d)
```

```python
import functools
import math

import numpy as np
import jax
import jax.numpy as jnp
from jax import lax
from jax.experimental import pallas as pl
from jax.experimental.pallas import tpu as pltpu

F32 = jnp.float32
BF16 = jnp.bfloat16

D_MODEL = 2048
DEPTH = 4
RMS_EPS = 1e-6
ROPE_THETA = 10000.0
NEG_INF = -1e30

MLA_HEADS = 8
MLA_Q_LORA = 768
MLA_KV_LORA = 512
MLA_NOPE = 128
MLA_ROPE = 64
MLA_V = 128
MLA_QK_PAD = 256

LRU_WIDTH = D_MODEL // 2
LRU_BLOCKS = 8
LRU_BLOCK_W = LRU_WIDTH // LRU_BLOCKS
LRU_C = 8.0
CONV_WIDTH = 4

NSA_HEADS = 16
NSA_GROUPS = 4
NSA_HPG = NSA_HEADS // NSA_GROUPS
NSA_DH = D_MODEL // NSA_HEADS
CMP_BLOCK = 32
CMP_STRIDE = 16
SEL_BLOCK = 64
SEL_COUNT = 16
SEL_FORCE = 1e4
WINDOW = 512
NSA_Q_DIM = NSA_HEADS * NSA_DH
NSA_KV_DIM = NSA_GROUPS * NSA_DH

DENSE_FF = 5632
N_EXPERTS = 8
TOP_K = 2
EXPERT_FF = 7168

LANES = 128
VMEM_LIMIT = 56 * 1024 * 1024

EV_CQ = 0
EV_CKV = MLA_Q_LORA
EV_REC = MLA_Q_LORA + MLA_KV_LORA
EV_GATE = EV_REC + LRU_WIDTH
EV_PE = EV_GATE + LRU_WIDTH
EV_PACKED = EV_PE + LANES

OD_Q = 0
OD_KC = NSA_Q_DIM
OD_VC = OD_KC + NSA_KV_DIM
OD_KS = OD_VC + NSA_KV_DIM
OD_VS = OD_KS + NSA_KV_DIM
OD_KW = OD_VS + NSA_KV_DIM
OD_VW = OD_KW + NSA_KV_DIM
OD_G = OD_VW + NSA_KV_DIM
OD_PACKED = OD_G + NSA_GROUPS * LANES

MOE_TILE = 2304
MOE_REGIONS = ((0, 512), (512, 512), (1024, 512), (1536, 512), (2048, 128), (2176, 128))
MOE_CHAIN = 256


def _cparams(sem):
    return pltpu.CompilerParams(dimension_semantics=sem, vmem_limit_bytes=VMEM_LIMIT)


def _rms(x, g):
    ms = jnp.mean(x * x, axis=-1, keepdims=True)
    return x * lax.rsqrt(ms + RMS_EPS) * g


def _sigmoid(x):
    return 1.0 / (1.0 + jnp.exp(-x))


def _gelu_tanh(x):
    return 0.5 * x * (1.0 + jnp.tanh(math.sqrt(2.0 / math.pi) * (x + 0.044715 * (x * x * x))))


def _dot(a, b):
    return jnp.dot(a, b, preferred_element_type=F32)


def _dot_nt(a, b):
    return lax.dot_general(a, b, (((1,), (1,)), ((), ())), preferred_element_type=F32)


def _norm_mm_kernel(x_ref, g_ref, w_ref, o_ref, xn_ref):
    @pl.when(pl.program_id(1) == 0)
    def _():
        xn_ref[...] = _rms(x_ref[...], g_ref[...]).astype(BF16)

    o_ref[...] = _dot(xn_ref[...], w_ref[...]).astype(o_ref.dtype)


def norm_matmul(x, g, w, *, tm, tn, out_dtype=F32):
    n, k = x.shape
    m = w.shape[1]
    assert n % tm == 0 and m % tn == 0
    return pl.pallas_call(
        _norm_mm_kernel,
        grid=(n // tm, m // tn),
        in_specs=[pl.BlockSpec((tm, k), lambda i, j: (i, 0)),
                  pl.BlockSpec((1, k), lambda i, j: (0, 0)),
                  pl.BlockSpec((k, tn), lambda i, j: (0, j))],
        out_specs=pl.BlockSpec((tm, tn), lambda i, j: (i, j)),
        out_shape=jax.ShapeDtypeStruct((n, m), out_dtype),
        scratch_shapes=[pltpu.VMEM((tm, k), BF16)],
        compiler_params=_cparams(("parallel", "arbitrary")),
    )(x, g.reshape(1, k), w)


def _mm_res_kernel(*refs, n_in):
    xs = refs[:n_in]
    ws = refs[n_in:2 * n_in]
    res_ref = refs[2 * n_in]
    o_ref = refs[2 * n_in + 1]
    acc = res_ref[...]
    for x_ref, w_ref in zip(xs, ws):
        acc = acc + _dot(x_ref[...], w_ref[...])
    o_ref[...] = acc


def matmul_residual(xs, w, res, *, tm, tn):
    n = res.shape[0]
    m = w.shape[1]
    n_in = len(xs)
    in_specs = [pl.BlockSpec((tm, x.shape[1]), lambda i, j: (i, 0)) for x in xs]
    row = 0
    for x in xs:
        kx = x.shape[1]
        assert row % kx == 0
        in_specs.append(pl.BlockSpec((kx, tn), lambda i, j, rb=row // kx: (rb, j)))
        row += kx
    assert row == w.shape[0]
    in_specs.append(pl.BlockSpec((tm, tn), lambda i, j: (i, j)))
    return pl.pallas_call(
        functools.partial(_mm_res_kernel, n_in=n_in),
        grid=(n // tm, m // tn),
        in_specs=in_specs,
        out_specs=pl.BlockSpec((tm, tn), lambda i, j: (i, j)),
        out_shape=jax.ShapeDtypeStruct((n, m), F32),
        compiler_params=_cparams(("parallel", "arbitrary")),
    )(*xs, *([w] * n_in), res)


def _rope64(x, c, s_lo, s_hi):
    return x * c + pltpu.roll(x, 96, 1) * s_lo + pltpu.roll(x, 32, 1) * s_hi


def _mla_up_kernel(u_ref, pe_ref, qg_ref, kvg_ref, wq_ref, wk_ref, wv_ref,
                   c_ref, slo_ref, shi_ref, q_ref, k_ref, v_ref):
    u = u_ref[...]
    c, s_lo, s_hi = c_ref[...], slo_ref[...], shi_ref[...]
    qn = _rms(u[:, EV_CQ:EV_CQ + MLA_Q_LORA], qg_ref[...]).astype(BF16)
    kvn = _rms(u[:, EV_CKV:EV_CKV + MLA_KV_LORA], kvg_ref[...]).astype(BF16)
    q = _dot(qn, wq_ref[...]) * ((MLA_NOPE + MLA_ROPE) ** -0.5)
    kn = _dot(kvn, wk_ref[...])
    v_ref[...] = _dot(kvn, wv_ref[...]).astype(v_ref.dtype)
    kpe = _rope64(pe_ref[...], c, s_lo, s_hi).astype(k_ref.dtype)
    for h in range(MLA_HEADS):
        a = h * MLA_QK_PAD
        q_ref[:, a:a + LANES] = q[:, a:a + LANES].astype(q_ref.dtype)
        q_ref[:, a + LANES:a + 2 * LANES] = _rope64(q[:, a + LANES:a + 2 * LANES], c, s_lo, s_hi).astype(q_ref.dtype)
        k_ref[:, a:a + LANES] = kn[:, h * LANES:(h + 1) * LANES].astype(k_ref.dtype)
        k_ref[:, a + LANES:a + 2 * LANES] = kpe


def mla_up(u, q_norm, kv_norm, wq, wk, wv, rope_c, rope_slo, rope_shi, *, seq, tm):
    n = u.shape[0]
    hq = MLA_HEADS * MLA_QK_PAD
    hv = MLA_HEADS * MLA_V
    ab = EV_REC
    assert seq % tm == 0 and EV_PE % LANES == 0
    nsb = seq // tm
    row_spec = pl.BlockSpec((tm, LANES), lambda i: (i % nsb, 0))
    full = lambda a: pl.BlockSpec(a.shape, lambda i: (0,) * a.ndim)
    qg = q_norm.reshape(1, -1)
    kvg = kv_norm.reshape(1, -1)
    return pl.pallas_call(
        _mla_up_kernel,
        grid=(n // tm,),
        in_specs=[pl.BlockSpec((tm, ab), lambda i: (i, 0)),
                  pl.BlockSpec((tm, LANES), lambda i: (i, EV_PE // LANES)),
                  full(qg), full(kvg), full(wq), full(wk), full(wv),
                  row_spec, row_spec, row_spec],
        out_specs=[pl.BlockSpec((tm, hq), lambda i: (i, 0)),
                   pl.BlockSpec((tm, hq), lambda i: (i, 0)),
                   pl.BlockSpec((tm, hv), lambda i: (i, 0))],
        out_shape=[jax.ShapeDtypeStruct((n, hq), BF16),
                   jax.ShapeDtypeStruct((n, hq), BF16),
                   jax.ShapeDtypeStruct((n, hv), BF16)],
        compiler_params=_cparams(("parallel",)),
    )(u, u, qg, kvg, wq, wk, wv, rope_c, rope_slo, rope_shi)


def _softmax_step(s, v, m_ref, l_ref, acc_ref):
    m_prev = m_ref[...]
    m_new = jnp.maximum(m_prev, jnp.max(s, axis=-1, keepdims=True))
    alpha = jnp.exp(m_prev - m_new)
    p = jnp.exp(s - m_new)
    l_ref[...] = alpha * l_ref[...] + jnp.sum(p, axis=-1, keepdims=True)
    acc_ref[...] = alpha * acc_ref[...] + _dot(p.astype(BF16), v)
    m_ref[...] = m_new


def _mla_attn_kernel(q_ref, k_ref, v_ref, cb_ref, o_ref, m_ref, l_ref, acc_ref, *, t, n_chunks):
    qi = pl.program_id(2)
    q = q_ref[0]
    m_ref[...] = jnp.full(m_ref.shape, NEG_INF, F32)
    l_ref[...] = jnp.zeros(l_ref.shape, F32)
    acc_ref[...] = jnp.zeros(acc_ref.shape, F32)
    for c in range(n_chunks):
        @pl.when(c < qi)
        def _():
            s = _dot_nt(q, k_ref[0, c * t:(c + 1) * t, :])
            _softmax_step(s, v_ref[0, c * t:(c + 1) * t, :], m_ref, l_ref, acc_ref)

        @pl.when(c == qi)
        def _():
            s = _dot_nt(q, k_ref[0, c * t:(c + 1) * t, :]) + cb_ref[...]
            _softmax_step(s, v_ref[0, c * t:(c + 1) * t, :], m_ref, l_ref, acc_ref)
    o_ref[0] = (acc_ref[...] / l_ref[...]).astype(o_ref.dtype)


def mla_attention(q, k, v, *, t):
    b, s, _ = q.shape
    causal = jnp.asarray(np.where(np.arange(t)[None, :] <= np.arange(t)[:, None], 0.0, NEG_INF), F32)
    kern = functools.partial(_mla_attn_kernel, t=t, n_chunks=s // t)
    return pl.pallas_call(
        kern,
        grid=(b, MLA_HEADS, s // t),
        in_specs=[pl.BlockSpec((1, t, MLA_QK_PAD), lambda b_, h, i: (b_, i, h)),
                  pl.BlockSpec((1, s, MLA_QK_PAD), lambda b_, h, i: (b_, 0, h)),
                  pl.BlockSpec((1, s, MLA_V), lambda b_, h, i: (b_, 0, h)),
                  pl.BlockSpec((t, t), lambda b_, h, i: (0, 0))],
        out_specs=pl.BlockSpec((1, t, MLA_V), lambda b_, h, i: (b_, i, h)),
        out_shape=jax.ShapeDtypeStruct((b, s, MLA_HEADS * MLA_V), BF16),
        scratch_shapes=[pltpu.VMEM((t, 1), F32), pltpu.VMEM((t, 1), F32), pltpu.VMEM((t, MLA_V), F32)],
        compiler_params=_cparams(("parallel", "parallel", "arbitrary")),
    )(q, k, v, causal)


def _rglru_kernel(x_ref, y_ref, cw_ref, cb_ref, gaw_ref, gab_ref, gxw_ref, gxb_ref, lam_ref,
                  o_ref, xbuf, h_ref, *, ts):
    t = pl.program_id(2)

    @pl.when(t == 0)
    def _():
        xbuf[0:8, :] = jnp.zeros((8, LANES), F32)
        h_ref[...] = jnp.zeros(h_ref.shape, F32)

    x = x_ref[0]
    xbuf[8:, :] = x
    cw = cw_ref[...]
    xc = cb_ref[...] + cw[3:4] * x
    for kk in range(CONV_WIDTH - 1):
        back = CONV_WIDTH - 1 - kk
        xc = xc + cw[kk:kk + 1] * xbuf[8 - back:8 - back + ts, :]
    xbuf[0:8, :] = x[ts - 8:, :]

    xcb = xc.astype(BF16)
    r = _sigmoid(_dot(xcb, gaw_ref[0]) + gab_ref[0])
    gi = _sigmoid(_dot(xcb, gxw_ref[0]) + gxb_ref[0])
    z = -lam_ref[...]
    softplus = jnp.maximum(z, 0.0) + jnp.log1p(jnp.exp(-jnp.abs(z)))
    log_a = (-LRU_C) * r * softplus
    a = jnp.exp(log_a)
    mult = jnp.sqrt(-jnp.tanh(log_a) * (a * a + 1.0))
    row = lax.broadcasted_iota(jnp.int32, (ts, 1), 0)
    mult = jnp.where(row + t * ts == 0, 1.0, mult)
    bv = mult * gi * xc

    d = 1
    while d < ts:
        keep = row >= d
        a_sh = jnp.where(keep, pltpu.roll(a, d, 0), 1.0)
        b_sh = jnp.where(keep, pltpu.roll(bv, d, 0), 0.0)
        bv = a * b_sh + bv
        a = a * a_sh
        d *= 2
    h = bv + a * h_ref[...]
    h_ref[...] = h[ts - 1:ts, :]
    o_ref[0] = (h * _gelu_tanh(y_ref[0])).astype(o_ref.dtype)


def rglru(u3, conv_w, conv_b, ga_w, ga_b, gx_w, gx_b, lam, *, ts):
    b, s, _ = u3.shape
    rec0 = EV_REC // LANES
    gate0 = EV_GATE // LANES
    cb = conv_b.reshape(1, LRU_WIDTH)
    gab = ga_b.reshape(LRU_BLOCKS, 1, LRU_BLOCK_W)
    gxb = gx_b.reshape(LRU_BLOCKS, 1, LRU_BLOCK_W)
    lam2 = lam.reshape(1, LRU_WIDTH)
    blk_w = pl.BlockSpec((1, LRU_BLOCK_W, LRU_BLOCK_W), lambda b_, n, t: (n, 0, 0))
    blk_b = pl.BlockSpec((1, 1, LRU_BLOCK_W), lambda b_, n, t: (n, 0, 0))
    vec = pl.BlockSpec((1, LANES), lambda b_, n, t: (0, n))
    return pl.pallas_call(
        functools.partial(_rglru_kernel, ts=ts),
        grid=(b, LRU_BLOCKS, s // ts),
        in_specs=[pl.BlockSpec((1, ts, LANES), lambda b_, n, t: (b_, t, rec0 + n)),
                  pl.BlockSpec((1, ts, LANES), lambda b_, n, t: (b_, t, gate0 + n)),
                  pl.BlockSpec((CONV_WIDTH, LANES), lambda b_, n, t: (0, n)),
                  vec, blk_w, blk_b, blk_w, blk_b, vec],
        out_specs=pl.BlockSpec((1, ts, LANES), lambda b_, n, t: (b_, t, n)),
        out_shape=jax.ShapeDtypeStruct((b, s, LRU_WIDTH), BF16),
        scratch_shapes=[pltpu.VMEM((ts + 8, LANES), F32), pltpu.VMEM((1, LANES), F32)],
        compiler_params=_cparams(("parallel", "parallel", "arbitrary")),
    )(u3, u3, conv_w, cb, ga_w, gab, gx_w, gxb, lam2)


def _ffn_kernel(x_ref, g_ref, w1_ref, w3_ref, w2_ref, o_ref, xn_ref):
    f = pl.program_id(1)

    @pl.when(f == 0)
    def _():
        x = x_ref[...]
        xn_ref[...] = _rms(x, g_ref[...]).astype(BF16)
        o_ref[...] = x

    xn = xn_ref[...]
    h1 = _dot(xn, w1_ref[...])
    h3 = _dot(xn, w3_ref[...])
    hh = (h1 * _sigmoid(h1) * h3).astype(BF16)
    o_ref[...] += _dot(hh, w2_ref[...])


def ffn_dense(x, g, w1, w3, w2, *, tm, tf):
    n, d = x.shape
    ff = w1.shape[1]
    assert n % tm == 0 and ff % tf == 0
    return pl.pallas_call(
        _ffn_kernel,
        grid=(n // tm, ff // tf),
        in_specs=[pl.BlockSpec((tm, d), lambda i, f: (i, 0)),
                  pl.BlockSpec((1, d), lambda i, f: (0, 0)),
                  pl.BlockSpec((d, tf), lambda i, f: (0, f)),
                  pl.BlockSpec((d, tf), lambda i, f: (0, f)),
                  pl.BlockSpec((tf, d), lambda i, f: (f, 0))],
        out_specs=pl.BlockSpec((tm, d), lambda i, f: (i, 0)),
        out_shape=jax.ShapeDtypeStruct((n, d), F32),
        scratch_shapes=[pltpu.VMEM((tm, d), BF16)],
        compiler_params=_cparams(("parallel", "arbitrary")),
    )(x, g.reshape(1, d), w1, w3, w2)


def _compress_kernel(kc_ref, pe_ref, w1_ref, b1_ref, w2_ref, o_ref, *, n_half):
    pe = pe_ref[...]
    half_k = n_half * NSA_DH
    for g in range(NSA_GROUPS):
        lo, hi = [], []
        for l in range(n_half):
            a = l * NSA_KV_DIM + g * NSA_DH
            piece = kc_ref[0, :, a:a + NSA_DH]
            lo.append((piece + pe[l:l + 1]).astype(BF16))
            hi.append((piece + pe[n_half + l:n_half + l + 1]).astype(BF16))
        z0 = _dot(jnp.concatenate(lo, axis=1), w1_ref[0:half_k, :])
        z1 = _dot(jnp.concatenate(hi, axis=1), w1_ref[half_k:2 * half_k, :])
        rows = z1.shape[0]
        pre = z0 + pltpu.roll(z1, rows - 1, 0) + b1_ref[...]
        o_ref[0, :, g * NSA_DH:(g + 1) * NSA_DH] = _dot(_gelu_tanh(pre).astype(BF16), w2_ref[...]).astype(o_ref.dtype)


def nsa_compress(kc3, pe, w1, b1, w2):
    b, nchunk, width = kc3.shape
    n_half = CMP_BLOCK // CMP_STRIDE
    assert CMP_BLOCK == 2 * CMP_STRIDE and width == CMP_STRIDE * NSA_KV_DIM
    full = lambda a: pl.BlockSpec(a.shape, lambda i: (0,) * a.ndim)
    b1r = b1.reshape(1, NSA_DH)
    return pl.pallas_call(
        functools.partial(_compress_kernel, n_half=CMP_STRIDE),
        grid=(b,),
        in_specs=[pl.BlockSpec((1, nchunk, width), lambda i: (i, 0, 0)),
                  full(pe), full(w1), full(b1r), full(w2)],
        out_specs=pl.BlockSpec((1, nchunk, NSA_KV_DIM), lambda i: (i, 0, 0)),
        out_shape=jax.ShapeDtypeStruct((b, nchunk, NSA_KV_DIM), BF16),
        compiler_params=_cparams(("parallel",)),
    )(kc3, pe, w1, b1r, w2)


def _rope128(x, c, s):
    return x * c + pltpu.roll(x, NSA_DH // 2, 1) * s


def _split3(x):
    hi = x.astype(BF16)
    r1 = x - hi.astype(F32)
    mid = r1.astype(BF16)
    lo = (r1 - mid.astype(F32)).astype(BF16)
    return hi, mid, lo


def _nsa_attn_kernel(q_ref, ks_ref, vs_ref, kw_ref, vw_ref, gt_ref, kc_ref, vc_ref,
                     cq_ref, sq_ref, ck_ref, sk_ref, ovl_ref, exp_ref, wb_ref,
                     o_ref, ksr, vsb, kwr, vwb, selb, m_ref, l_ref, acc_ref,
                     *, tq, tk, seq, scale):
    qi = pl.program_id(2)
    hp = NSA_HPG
    n_sel = seq // SEL_BLOCK

    @pl.when(qi == 0)
    def _():
        ck, sk = ck_ref[...], sk_ref[...]
        ksr[...] = _rope128(ks_ref[0], ck, sk).astype(BF16)
        kwr[...] = _rope128(kw_ref[0], ck, sk).astype(BF16)
        vsb[...] = vs_ref[0].astype(BF16)
        vwb[...] = vw_ref[0].astype(BF16)

    q = q_ref[0] * scale
    cq, sq = cq_ref[...], sq_ref[...]
    heads = [q[:, p * NSA_DH:(p + 1) * NSA_DH] for p in range(hp)]
    qu = jnp.concatenate(heads, axis=0).astype(BF16)
    qr = jnp.concatenate([_rope128(h, cq, sq) for h in heads], axis=0).astype(BF16)
    t_row = qi * tq + lax.broadcasted_iota(jnp.int32, (tq, 1), 0)
    lane = lax.broadcasted_iota(jnp.int32, (1, LANES), 1)

    sc = _dot_nt(qu, kc_ref[0]).reshape(hp, tq, LANES)
    valid = (lane * CMP_STRIDE + (CMP_BLOCK - 1) <= t_row)[None]
    sm = jnp.where(valid, sc, NEG_INF)
    e = jnp.exp(sm - jnp.max(sm, axis=-1, keepdims=True))
    p = jnp.where(valid, e / jnp.sum(e, axis=-1, keepdims=True), 0.0)
    o_cmp = _dot(p.reshape(hp * tq, LANES).astype(BF16), vc_ref[0])

    psum = p[0]
    for i in range(1, hp):
        psum = psum + p[i]
    ovl = ovl_ref[...]
    imp = sum(_dot(part, ovl) for part in _split3(psum))
    cur = jnp.right_shift(t_row, SEL_BLOCK.bit_length() - 1)
    future = lane > cur
    forced = (lane == 0) | (lane == cur) | (lane == cur - 1)
    score = jnp.where(future, -1.0, jnp.where(forced, SEL_FORCE, imp))
    sc_t = score.T[0:n_sel, :]
    blk = lax.broadcasted_iota(jnp.int32, (n_sel, 1), 0)
    cnt = jnp.zeros((n_sel, tq), F32)
    for j in range(n_sel):
        other = sc_t[j:j + 1, :]
        beats = (other > sc_t) | ((other == sc_t) & (blk > j))
        cnt = cnt + jnp.where(beats, 1.0, 0.0)
    sel_t = jnp.where(cnt < SEL_COUNT, 1.0, 0.0)
    sel = jnp.concatenate([sel_t, jnp.zeros((LANES - n_sel, tq), F32)], axis=0).T.astype(BF16)
    picked = _dot(sel, exp_ref[...])
    kall = lax.broadcasted_iota(jnp.int32, (1, seq), 1)
    selb[...] = jnp.where((picked > 0.5) & (kall <= t_row), 0.0, NEG_INF)

    m_ref[...] = jnp.full(m_ref.shape, NEG_INF, F32)
    l_ref[...] = jnp.zeros(l_ref.shape, F32)
    acc_ref[...] = jnp.zeros(acc_ref.shape, F32)
    for c in range(seq // tk):
        @pl.when(c * tk <= qi * tq + (tq - 1))
        def _():
            s = _dot_nt(qr, ksr[c * tk:(c + 1) * tk, :]).reshape(hp, tq, tk)
            s = (s + selb[:, c * tk:(c + 1) * tk][None]).reshape(hp * tq, tk)
            _softmax_step(s, vsb[c * tk:(c + 1) * tk, :], m_ref, l_ref, acc_ref)
    o_sel = acc_ref[...] / l_ref[...]

    span = WINDOW + tq
    start = pl.multiple_of(jnp.maximum(qi * tq - WINDOW, 0), tq)
    sw = _dot_nt(qr, kwr[pl.ds(start, span), :]).reshape(hp, tq, span) + wb_ref[0][None]
    ew = jnp.exp(sw - jnp.max(sw, axis=-1, keepdims=True))
    den = jnp.sum(ew, axis=-1, keepdims=True).reshape(hp * tq, 1)
    o_win = _dot(ew.reshape(hp * tq, span).astype(BF16), vwb[pl.ds(start, span), :]) / den

    gates = _sigmoid(gt_ref[0])
    for i in range(hp):
        rows = slice(i * tq, (i + 1) * tq)
        o = (gates[:, 3 * i:3 * i + 1] * o_cmp[rows] + gates[:, 3 * i + 1:3 * i + 2] * o_sel[rows]
             + gates[:, 3 * i + 2:3 * i + 3] * o_win[rows])
        o_ref[0, :, i * NSA_DH:(i + 1) * NSA_DH] = o.astype(o_ref.dtype)


def _window_bias(tq):
    span = WINDOW + tq
    out = []
    for qi in range(WINDOW // tq + 1):
        start = max(qi * tq - WINDOW, 0)
        t = qi * tq + np.arange(tq)[:, None]
        kpos = start + np.arange(span)[None, :]
        out.append(np.where((kpos <= t) & (kpos > t - WINDOW), 0.0, NEG_INF))
    return jnp.asarray(np.stack(out), F32)


def nsa_attention(u3, k_cmp, v_cmp, rope_c, rope_s, ovl, expand, *, tq, tk):
    b, s, _ = u3.shape
    assert s // CMP_STRIDE == LANES and tq % SEL_BLOCK == 0 and WINDOW % tq == 0
    hp = NSA_HPG
    nwb = WINDOW // tq
    wbias = _window_bias(tq)
    col = lambda off: (lambda b_, g, i: (b_, 0, off // NSA_DH + g))
    seq_blk = lambda off: pl.BlockSpec((1, s, NSA_DH), col(off))
    full = lambda a: pl.BlockSpec(a.shape, lambda b_, g, i: (0,) * a.ndim)
    cmp_blk = pl.BlockSpec((1, LANES, NSA_DH), lambda b_, g, i: (b_, 0, g))
    rope_q = pl.BlockSpec((tq, NSA_DH), lambda b_, g, i: (i, 0))
    kern = functools.partial(_nsa_attn_kernel, tq=tq, tk=tk, seq=s, scale=NSA_DH ** -0.5)
    return pl.pallas_call(
        kern,
        grid=(b, NSA_GROUPS, s // tq),
        in_specs=[pl.BlockSpec((1, tq, hp * NSA_DH), lambda b_, g, i: (b_, i, g)),
                  seq_blk(OD_KS), seq_blk(OD_VS), seq_blk(OD_KW), seq_blk(OD_VW),
                  pl.BlockSpec((1, tq, LANES), lambda b_, g, i: (b_, i, OD_G // LANES + g)),
                  cmp_blk, cmp_blk, rope_q, rope_q, full(rope_c), full(rope_s), full(ovl), full(expand),
                  pl.BlockSpec((1, tq, WINDOW + tq), lambda b_, g, i: (jnp.minimum(i, nwb), 0, 0))],
        out_specs=pl.BlockSpec((1, tq, hp * NSA_DH), lambda b_, g, i: (b_, i, g)),
        out_shape=jax.ShapeDtypeStruct((b, s, NSA_Q_DIM), BF16),
        scratch_shapes=[pltpu.VMEM((s, NSA_DH), BF16)] * 4 + [
            pltpu.VMEM((tq, s), F32),
            pltpu.VMEM((hp * tq, 1), F32), pltpu.VMEM((hp * tq, 1), F32), pltpu.VMEM((hp * tq, NSA_DH), F32)],
        compiler_params=_cparams(("parallel", "parallel", "arbitrary")),
    )(u3, u3, u3, u3, u3, u3, k_cmp, v_cmp, rope_c, rope_s, rope_c, rope_s, ovl, expand, wbias)


def _router_kernel(x_ref, g_ref, wr_ref, br_ref, xn_ref, meta_ref, cnt_ref, *, tm):
    @pl.when(pl.program_id(0) == 0)
    def _():
        cnt_ref[...] = jnp.zeros(cnt_ref.shape, F32)

    xn = _rms(x_ref[...], g_ref[...])
    xn_ref[...] = xn.astype(xn_ref.dtype)
    xh, xm, _ = _split3(xn)
    wh, wm, _ = _split3(wr_ref[...])
    logits = _dot(xh, wh) + _dot(xh, wm) + _dot(xm, wh) + br_ref[...]
    lane = lax.broadcasted_iota(jnp.int32, (1, LANES), 1).astype(F32)
    lg = jnp.where(lane < N_EXPERTS, logits, NEG_INF)
    m1 = jnp.max(lg, axis=-1, keepdims=True)
    e1 = jnp.min(jnp.where(lg == m1, lane, float(LANES)), axis=-1, keepdims=True)
    lg2 = jnp.where(lane == e1, NEG_INF, lg)
    m2 = jnp.max(lg2, axis=-1, keepdims=True)
    e2 = jnp.min(jnp.where(lg2 == m2, lane, float(LANES)), axis=-1, keepdims=True)
    ex = jnp.exp(m2 - m1)
    den = 1.0 + ex
    g1 = 1.0 / den
    g2 = ex / den
    oh = jnp.where((lane == e1) | (lane == e2), 1.0, 0.0)
    r = lax.broadcasted_iota(jnp.int32, (tm, tm), 0)
    c = lax.broadcasted_iota(jnp.int32, (tm, tm), 1)
    tri = jnp.where(r > c, 1.0, 0.0).astype(BF16)
    cum = _dot(tri, oh.astype(BF16)) + cnt_ref[0:1, :]
    pos1 = jnp.sum(jnp.where(lane == e1, cum, 0.0), axis=-1, keepdims=True)
    pos2 = jnp.sum(jnp.where(lane == e2, cum, 0.0), axis=-1, keepdims=True)
    cnt_ref[...] = cnt_ref[...] + jnp.sum(oh, axis=0, keepdims=True)
    meta = jnp.where(lane == 0, e1, 0.0)
    meta = jnp.where(lane == 1, e2, meta)
    meta = jnp.where(lane == 2, g1, meta)
    meta = jnp.where(lane == 3, g2, meta)
    meta = jnp.where(lane == 4, pos1, meta)
    meta = jnp.where(lane == 5, pos2, meta)
    meta_ref[...] = meta


def moe_router(x, g, wr, br, *, tm):
    n, d = x.shape
    return pl.pallas_call(
        functools.partial(_router_kernel, tm=tm),
        grid=(n // tm,),
        in_specs=[pl.BlockSpec((tm, d), lambda i: (i, 0)),
                  pl.BlockSpec((1, d), lambda i: (0, 0)),
                  pl.BlockSpec((d, LANES), lambda i: (0, 0)),
                  pl.BlockSpec((1, LANES), lambda i: (0, 0))],
        out_specs=[pl.BlockSpec((tm, d), lambda i: (i, 0)),
                   pl.BlockSpec((tm, LANES), lambda i: (i, 0)),
                   pl.BlockSpec((8, LANES), lambda i: (0, 0))],
        out_shape=[jax.ShapeDtypeStruct((n, d), BF16),
                   jax.ShapeDtypeStruct((n, LANES), F32),
                   jax.ShapeDtypeStruct((8, LANES), F32)],
        compiler_params=_cparams(("arbitrary",)),
    )(x, g.reshape(1, d), wr, br)


def _expert_kernel(te_ref, tr_ref, x_ref, w1_ref, w3_ref, w2_ref, o_ref):
    t = pl.program_id(0)
    f = pl.program_id(1)
    rows = tr_ref[t]

    @pl.when(f == 0)
    def _():
        o_ref[...] = jnp.zeros(o_ref.shape, o_ref.dtype)

    @pl.when(rows > 0)
    def _():
        w1 = w1_ref[0, 0].astype(BF16)
        w3 = w3_ref[0, 0].astype(BF16)
        w2 = w2_ref[0, 0].astype(BF16)
        for start, size in MOE_REGIONS:
            @pl.when(start < rows)
            def _():
                for c0 in range(start, start + size, MOE_CHAIN):
                    rs = slice(c0, c0 + min(MOE_CHAIN, size))
                    xb = x_ref[rs, :]
                    h1 = _dot(xb, w1)
                    h3 = _dot(xb, w3)
                    hh = (h1 * _sigmoid(h1) * h3).astype(BF16)
                    o_ref[rs, :] += _dot(hh, w2)


def moe_experts(xr, tile_e, tile_rows, w1, w3, w2, layer, *, tf):
    n_rows, d = xr.shape
    n_tiles = n_rows // MOE_TILE
    ff = w1.shape[3]
    nf = ff // tf

    def f_eff(t, f, tr):
        return jnp.where(tr[t] > 0, f, nf - 1)

    single = pl.Buffered(1)
    grid_spec = pltpu.PrefetchScalarGridSpec(
        num_scalar_prefetch=2,
        grid=(n_tiles, nf),
        in_specs=[pl.BlockSpec((MOE_TILE, d), lambda t, f, te, tr: (t, 0), pipeline_mode=single),
                  pl.BlockSpec((1, 1, d, tf), lambda t, f, te, tr: (layer, te[t], 0, f_eff(t, f, tr))),
                  pl.BlockSpec((1, 1, d, tf), lambda t, f, te, tr: (layer, te[t], 0, f_eff(t, f, tr))),
                  pl.BlockSpec((1, 1, tf, d), lambda t, f, te, tr: (layer, te[t], f_eff(t, f, tr), 0))],
        out_specs=pl.BlockSpec((MOE_TILE, d), lambda t, f, te, tr: (t, 0), pipeline_mode=single),
    )
    return pl.pallas_call(
        _expert_kernel,
        grid_spec=grid_spec,
        out_shape=jax.ShapeDtypeStruct((n_rows, d), F32),
        compiler_params=_cparams(("parallel", "arbitrary")),
    )(tile_e, tile_rows, xr, w1, w3, w2)


def _combine_kernel(x_ref, y1_ref, y2_ref, meta_ref, g_ref, o_ref, *, final_norm):
    meta = meta_ref[...]
    y = x_ref[...] + (meta[:, 2:3] * y1_ref[...] + meta[:, 3:4] * y2_ref[...])
    if final_norm:
        y = _rms(y, g_ref[...])
    o_ref[...] = y


def moe_combine(x, y1, y2, meta, g, *, final_norm, tm):
    n, d = x.shape
    row = pl.BlockSpec((tm, d), lambda i: (i, 0))
    return pl.pallas_call(
        functools.partial(_combine_kernel, final_norm=final_norm),
        grid=(n // tm,),
        in_specs=[row, row, row, pl.BlockSpec((tm, LANES), lambda i: (i, 0)),
                  pl.BlockSpec((1, d), lambda i: (0, 0))],
        out_specs=row,
        out_shape=jax.ShapeDtypeStruct((n, d), F32),
        compiler_params=_cparams(("parallel",)),
    )(x, y1, y2, meta, g.reshape(1, d))


def _mla_rope_tables(seq):
    half = MLA_ROPE // 2
    pos = jnp.arange(seq, dtype=F32)
    inv = ROPE_THETA ** (-jnp.arange(0, MLA_ROPE, 2, dtype=F32) / MLA_ROPE)
    ang = pos[:, None] * inv[None, :]
    cos, sin = jnp.cos(ang), jnp.sin(ang)
    z = jnp.zeros((seq, half), F32)
    pad = jnp.zeros((seq, LANES - MLA_ROPE), F32)
    c = jnp.concatenate([cos, cos, pad], axis=1)
    s_lo = jnp.concatenate([-sin, z, pad], axis=1)
    s_hi = jnp.concatenate([z, sin, pad], axis=1)
    return c, s_lo, s_hi


def _nsa_rope_tables(seq):
    pos = jnp.arange(seq, dtype=F32)
    inv = ROPE_THETA ** (-jnp.arange(0, NSA_DH, 2, dtype=F32) / NSA_DH)
    ang = pos[:, None] * inv[None, :]
    cos, sin = jnp.cos(ang), jnp.sin(ang)
    return jnp.concatenate([cos, cos], axis=1), jnp.concatenate([-sin, sin], axis=1)


def _selection_constants(seq):
    n_sel = seq // SEL_BLOCK
    nc = (seq - CMP_BLOCK) // CMP_STRIDE + 1
    cmp_start = np.arange(LANES) * CMP_STRIDE
    sel_start = np.arange(LANES) * SEL_BLOCK
    ovl = ((cmp_start[:, None] < sel_start[None, :] + SEL_BLOCK) &
           (cmp_start[:, None] + CMP_BLOCK > sel_start[None, :]))
    ovl &= (np.arange(LANES)[:, None] < nc) & (np.arange(LANES)[None, :] < n_sel)
    expand = (np.arange(seq)[None, :] // SEL_BLOCK == np.arange(LANES)[:, None])
    return jnp.asarray(ovl, BF16), jnp.asarray(expand, BF16)


def even_layer(x, seq, p):
    n = x.shape[0]
    b = n // seq
    (norm_mix, w_in, q_norm, w_q_up, kv_norm, w_kv_up, conv_w, conv_b, ga_w, ga_b, gx_w, gx_b,
     lam, w_out, norm_ffn, w1, w3, w2) = p
    d = D_MODEL
    o1 = MLA_Q_LORA + MLA_KV_LORA
    o2 = o1 + MLA_ROPE
    w_pack = jnp.concatenate(
        [w_in[:, :o1], w_in[:, o2:], w_in[:, o1:o2], jnp.zeros((d, LANES - MLA_ROPE), F32)], axis=1).astype(BF16)
    u = norm_matmul(x, norm_mix, w_pack, tm=512, tn=EV_PACKED // 3)

    wq = w_q_up.reshape(MLA_Q_LORA, MLA_HEADS, MLA_NOPE + MLA_ROPE)
    wq = jnp.pad(wq, ((0, 0), (0, 0), (0, MLA_QK_PAD - MLA_NOPE - MLA_ROPE)))
    wq = wq.reshape(MLA_Q_LORA, MLA_HEADS * MLA_QK_PAD).astype(BF16)
    wkv = w_kv_up.reshape(MLA_KV_LORA, MLA_HEADS, MLA_NOPE + MLA_V)
    wk = wkv[:, :, :MLA_NOPE].reshape(MLA_KV_LORA, MLA_HEADS * MLA_NOPE).astype(BF16)
    wv = wkv[:, :, MLA_NOPE:].reshape(MLA_KV_LORA, MLA_HEADS * MLA_V).astype(BF16)
    rc, rlo, rhi = _mla_rope_tables(seq)
    q, k, v = mla_up(u, q_norm, kv_norm, wq, wk, wv, rc, rlo, rhi, seq=seq, tm=512)
    o_mla = mla_attention(q.reshape(b, seq, -1), k.reshape(b, seq, -1), v.reshape(b, seq, -1), t=512)

    o_rec = rglru(u.reshape(b, seq, EV_PACKED), conv_w, conv_b, ga_w.astype(BF16), ga_b,
                  gx_w.astype(BF16), gx_b, lam, ts=512)
    x = matmul_residual([o_mla.reshape(n, -1), o_rec.reshape(n, -1)], w_out.astype(BF16), x, tm=512, tn=1024)
    return ffn_dense(x, norm_ffn, w1.astype(BF16), w3.astype(BF16), w2.astype(BF16), tm=512, tf=512)


def _moe_dispatch(meta, counts, n):
    e1 = meta[:, 0].astype(jnp.int32)
    e2 = meta[:, 1].astype(jnp.int32)
    pos1 = meta[:, 4].astype(jnp.int32)
    pos2 = meta[:, 5].astype(jnp.int32)
    cnt = counts[0, :N_EXPERTS].astype(jnp.int32)
    n_tiles = (n * TOP_K) // MOE_TILE + N_EXPERTS
    tiles_e = (cnt + MOE_TILE - 1) // MOE_TILE
    tend = jnp.cumsum(tiles_e)
    tstart = tend - tiles_e
    d1 = tstart[e1] * MOE_TILE + pos1
    d2 = tstart[e2] * MOE_TILE + pos2
    tid = jnp.arange(n_tiles, dtype=jnp.int32)
    te = jnp.minimum(jnp.searchsorted(tend, tid, side='right'), N_EXPERTS - 1).astype(jnp.int32)
    rows = jnp.clip(cnt[te] - (tid - tstart[te]) * MOE_TILE, 0, MOE_TILE)
    rows = jnp.where(tid < tend[-1], rows, 0).astype(jnp.int32)
    last_e = te[jnp.maximum(tend[-1] - 1, 0)]
    te = jnp.where(tid < tend[-1], te, last_e).astype(jnp.int32)
    tok = jnp.arange(n, dtype=jnp.int32)
    row_tok = jnp.zeros((n_tiles * MOE_TILE,), jnp.int32).at[d1].set(tok).at[d2].set(tok)
    return d1, d2, row_tok, te, rows


def odd_layer(x, seq, p, experts, final_g):
    n = x.shape[0]
    b = n // seq
    (norm_mix, w_in, ck_pe, ck_w1, ck_b1, ck_w2, cv_pe, cv_w1, cv_b1, cv_w2, w_out, norm_ffn,
     router_w, router_b) = p
    ew1, ew3, ew2, layer = experts
    d = D_MODEL
    wg = w_in[:, OD_G:].reshape(d, NSA_GROUPS, NSA_HPG * 3)
    wg = jnp.pad(wg, ((0, 0), (0, 0), (0, LANES - NSA_HPG * 3))).reshape(d, NSA_GROUPS * LANES)
    w_pack = jnp.concatenate([w_in[:, :OD_G], wg], axis=1).astype(BF16)
    u = norm_matmul(x, norm_mix, w_pack, tm=512, tn=512)
    u3 = u.reshape(b, seq, OD_PACKED)

    nchunk = seq // CMP_STRIDE
    kc3 = u3[:, :, OD_KC:OD_KC + NSA_KV_DIM].reshape(b, nchunk, CMP_STRIDE * NSA_KV_DIM)
    vc3 = u3[:, :, OD_VC:OD_VC + NSA_KV_DIM].reshape(b, nchunk, CMP_STRIDE * NSA_KV_DIM)
    k_cmp = nsa_compress(kc3, ck_pe, ck_w1.astype(BF16), ck_b1, ck_w2.astype(BF16))
    v_cmp = nsa_compress(vc3, cv_pe, cv_w1.astype(BF16), cv_b1, cv_w2.astype(BF16))
    rc, rs = _nsa_rope_tables(seq)
    ovl, expand = _selection_constants(seq)
    o = nsa_attention(u3, k_cmp, v_cmp, rc, rs, ovl, expand, tq=256, tk=512)
    x = matmul_residual([o.reshape(n, -1)], w_out.astype(BF16), x, tm=512, tn=1024)

    wr = jnp.pad(router_w, ((0, 0), (0, LANES - N_EXPERTS)))
    br = jnp.pad(router_b, (0, LANES - N_EXPERTS)).reshape(1, LANES)
    xn, meta, counts = moe_router(x, norm_ffn, wr, br, tm=512)
    d1, d2, row_tok, te, rows = _moe_dispatch(meta, counts, n)
    xr = jnp.take(xn, row_tok, axis=0)
    yr = moe_experts(xr, te, rows, ew1, ew3, ew2, layer, tf=256)
    y1 = jnp.take(yr, d1, axis=0)
    y2 = jnp.take(yr, d2, axis=0)
    g = final_g if final_g is not None else norm_ffn
    return moe_combine(x, y1, y2, meta, g, final_norm=final_g is not None, tm=512)


def kernel(x, ev_norm_mix, ev_w_in, ev_q_norm, ev_w_q_up, ev_kv_norm, ev_w_kv_up, ev_conv_w, ev_conv_b, ev_gate_a_w, ev_gate_a_b, ev_gate_x_w, ev_gate_x_b, ev_lru_lambda, ev_w_out, ev_norm_ffn, ev_ffn_w1, ev_ffn_w3, ev_ffn_w2, od_norm_mix, od_w_in, od_cmp_k_pe, od_cmp_k_w1, od_cmp_k_b1, od_cmp_k_w2, od_cmp_v_pe, od_cmp_v_w1, od_cmp_v_b1, od_cmp_v_w2, od_w_out, od_norm_ffn, od_router_w, od_router_b, od_exp_w1, od_exp_w3, od_exp_w2, final_norm):
    bsz, seq, d = x.shape
    ev = (ev_norm_mix, ev_w_in, ev_q_norm, ev_w_q_up, ev_kv_norm, ev_w_kv_up, ev_conv_w, ev_conv_b,
          ev_gate_a_w, ev_gate_a_b, ev_gate_x_w, ev_gate_x_b, ev_lru_lambda, ev_w_out, ev_norm_ffn,
          ev_ffn_w1, ev_ffn_w3, ev_ffn_w2)
    od = (od_norm_mix, od_w_in, od_cmp_k_pe, od_cmp_k_w1, od_cmp_k_b1, od_cmp_k_w2, od_cmp_v_pe,
          od_cmp_v_w1, od_cmp_v_b1, od_cmp_v_w2, od_w_out, od_norm_ffn, od_router_w, od_router_b)
    h = x.reshape(bsz * seq, d)
    for layer in range(DEPTH):
        i = layer // 2
        if layer % 2 == 0:
            h = even_layer(h, seq, tuple(a[i] for a in ev))
        else:
            h = odd_layer(h, seq, tuple(a[i] for a in od), (od_exp_w1, od_exp_w3, od_exp_w2, i),
                          final_norm if layer == DEPTH - 1 else None)
    return h.reshape(bsz, seq, d)
```

```python
import functools
import math

import numpy as np
import jax
import jax.numpy as jnp
from jax import lax
from jax.experimental import pallas as pl
from jax.experimental.pallas import tpu as pltpu
from jax.experimental.pallas import tpu_sc as plsc

F32 = jnp.float32
BF16 = jnp.bfloat16

D_MODEL = 2048
DEPTH = 4
RMS_EPS = 1e-6
ROPE_THETA = 10000.0
NEG_INF = -1e30

MLA_HEADS = 8
MLA_Q_LORA = 768
MLA_KV_LORA = 512
MLA_NOPE = 128
MLA_ROPE = 64
MLA_V = 128
MLA_QK_PAD = 256

LRU_WIDTH = D_MODEL // 2
LRU_BLOCKS = 8
LRU_BLOCK_W = LRU_WIDTH // LRU_BLOCKS
LRU_C = 8.0
CONV_WIDTH = 4

NSA_HEADS = 16
NSA_GROUPS = 4
NSA_HPG = NSA_HEADS // NSA_GROUPS
NSA_DH = D_MODEL // NSA_HEADS
CMP_BLOCK = 32
CMP_STRIDE = 16
SEL_BLOCK = 64
SEL_COUNT = 16
SEL_FORCE = 1e4
WINDOW = 512
NSA_Q_DIM = NSA_HEADS * NSA_DH
NSA_KV_DIM = NSA_GROUPS * NSA_DH

DENSE_FF = 5632
N_EXPERTS = 8
TOP_K = 2
EXPERT_FF = 7168

LANES = 128
SC_CORES = 2
SC_SUBCORES = 16
SC_WORKERS = SC_CORES * SC_SUBCORES
SC_GATHER_BYTES = 256 * 1024
VMEM_LIMIT = 56 * 1024 * 1024

EV_CQ = 0
EV_CKV = MLA_Q_LORA
EV_REC = MLA_Q_LORA + MLA_KV_LORA
EV_GATE = EV_REC + LRU_WIDTH
EV_PE = EV_GATE + LRU_WIDTH
EV_PACKED = EV_PE + LANES

OD_Q = 0
OD_KC = NSA_Q_DIM
OD_VC = OD_KC + NSA_KV_DIM
OD_KS = OD_VC + NSA_KV_DIM
OD_VS = OD_KS + NSA_KV_DIM
OD_KW = OD_VS + NSA_KV_DIM
OD_VW = OD_KW + NSA_KV_DIM
OD_G = OD_VW + NSA_KV_DIM
OD_PACKED = OD_G + NSA_GROUPS * LANES

MOE_TILE = 2304
MOE_REGIONS = ((0, 512), (512, 512), (1024, 512), (1536, 512), (2048, 128), (2176, 128))
MOE_CHAIN = 256


def _cparams(sem):
    return pltpu.CompilerParams(dimension_semantics=sem, vmem_limit_bytes=VMEM_LIMIT)


def _rms(x, g):
    ms = jnp.mean(x * x, axis=-1, keepdims=True)
    return x * lax.rsqrt(ms + RMS_EPS) * g


def _sigmoid(x):
    return 1.0 / (1.0 + jnp.exp(-x))


def _gelu_tanh(x):
    return 0.5 * x * (1.0 + jnp.tanh(math.sqrt(2.0 / math.pi) * (x + 0.044715 * (x * x * x))))


def _dot(a, b):
    return jnp.dot(a, b, preferred_element_type=F32)


def _dot_nt(a, b):
    return lax.dot_general(a, b, (((1,), (1,)), ((), ())), preferred_element_type=F32)


def _norm_mm_kernel(x_ref, g_ref, w_ref, o_ref, xn_ref):
    @pl.when(pl.program_id(1) == 0)
    def _():
        xn_ref[...] = _rms(x_ref[...], g_ref[...]).astype(BF16)

    o_ref[...] = _dot(xn_ref[...], w_ref[...]).astype(o_ref.dtype)


def norm_matmul(x, g, w, *, tm, tn, out_dtype=F32):
    n, k = x.shape
    m = w.shape[1]
    assert n % tm == 0 and m % tn == 0
    return pl.pallas_call(
        _norm_mm_kernel,
        grid=(n // tm, m // tn),
        in_specs=[pl.BlockSpec((tm, k), lambda i, j: (i, 0)),
                  pl.BlockSpec((1, k), lambda i, j: (0, 0)),
                  pl.BlockSpec((k, tn), lambda i, j: (0, j))],
        out_specs=pl.BlockSpec((tm, tn), lambda i, j: (i, j)),
        out_shape=jax.ShapeDtypeStruct((n, m), out_dtype),
        scratch_shapes=[pltpu.VMEM((tm, k), BF16)],
        compiler_params=_cparams(("parallel", "arbitrary")),
    )(x, g.reshape(1, k), w)


def _mm_res_kernel(*refs, n_in):
    xs = refs[:n_in]
    ws = refs[n_in:2 * n_in]
    res_ref = refs[2 * n_in]
    o_ref = refs[2 * n_in + 1]
    acc = res_ref[...]
    for x_ref, w_ref in zip(xs, ws):
        acc = acc + _dot(x_ref[...], w_ref[...])
    o_ref[...] = acc


def matmul_residual(xs, w, res, *, tm, tn):
    n = res.shape[0]
    m = w.shape[1]
    n_in = len(xs)
    in_specs = [pl.BlockSpec((tm, x.shape[1]), lambda i, j: (i, 0)) for x in xs]
    row = 0
    for x in xs:
        kx = x.shape[1]
        assert row % kx == 0
        in_specs.append(pl.BlockSpec((kx, tn), lambda i, j, rb=row // kx: (rb, j)))
        row += kx
    assert row == w.shape[0]
    in_specs.append(pl.BlockSpec((tm, tn), lambda i, j: (i, j)))
    return pl.pallas_call(
        functools.partial(_mm_res_kernel, n_in=n_in),
        grid=(n // tm, m // tn),
        in_specs=in_specs,
        out_specs=pl.BlockSpec((tm, tn), lambda i, j: (i, j)),
        out_shape=jax.ShapeDtypeStruct((n, m), F32),
        compiler_params=_cparams(("parallel", "arbitrary")),
    )(*xs, *([w] * n_in), res)


def _rope64(x, c, s_lo, s_hi):
    return x * c + pltpu.roll(x, 96, 1) * s_lo + pltpu.roll(x, 32, 1) * s_hi


def _mla_up_kernel(u_ref, pe_ref, qg_ref, kvg_ref, wq_ref, wk_ref, wv_ref,
                   c_ref, slo_ref, shi_ref, q_ref, k_ref, v_ref):
    u = u_ref[...]
    c, s_lo, s_hi = c_ref[...], slo_ref[...], shi_ref[...]
    qn = _rms(u[:, EV_CQ:EV_CQ + MLA_Q_LORA], qg_ref[...]).astype(BF16)
    kvn = _rms(u[:, EV_CKV:EV_CKV + MLA_KV_LORA], kvg_ref[...]).astype(BF16)
    q = _dot(qn, wq_ref[...]) * ((MLA_NOPE + MLA_ROPE) ** -0.5)
    kn = _dot(kvn, wk_ref[...])
    v_ref[...] = _dot(kvn, wv_ref[...]).astype(v_ref.dtype)
    kpe = _rope64(pe_ref[...], c, s_lo, s_hi).astype(k_ref.dtype)
    for h in range(MLA_HEADS):
        a = h * MLA_QK_PAD
        q_ref[:, a:a + LANES] = q[:, a:a + LANES].astype(q_ref.dtype)
        q_ref[:, a + LANES:a + 2 * LANES] = _rope64(q[:, a + LANES:a + 2 * LANES], c, s_lo, s_hi).astype(q_ref.dtype)
        k_ref[:, a:a + LANES] = kn[:, h * LANES:(h + 1) * LANES].astype(k_ref.dtype)
        k_ref[:, a + LANES:a + 2 * LANES] = kpe


def mla_up(u, q_norm, kv_norm, wq, wk, wv, rope_c, rope_slo, rope_shi, *, seq, tm):
    n = u.shape[0]
    hq = MLA_HEADS * MLA_QK_PAD
    hv = MLA_HEADS * MLA_V
    ab = EV_REC
    assert seq % tm == 0 and EV_PE % LANES == 0
    nsb = seq // tm
    row_spec = pl.BlockSpec((tm, LANES), lambda i: (i % nsb, 0))
    full = lambda a: pl.BlockSpec(a.shape, lambda i: (0,) * a.ndim)
    qg = q_norm.reshape(1, -1)
    kvg = kv_norm.reshape(1, -1)
    return pl.pallas_call(
        _mla_up_kernel,
        grid=(n // tm,),
        in_specs=[pl.BlockSpec((tm, ab), lambda i: (i, 0)),
                  pl.BlockSpec((tm, LANES), lambda i: (i, EV_PE // LANES)),
                  full(qg), full(kvg), full(wq), full(wk), full(wv),
                  row_spec, row_spec, row_spec],
        out_specs=[pl.BlockSpec((tm, hq), lambda i: (i, 0)),
                   pl.BlockSpec((tm, hq), lambda i: (i, 0)),
                   pl.BlockSpec((tm, hv), lambda i: (i, 0))],
        out_shape=[jax.ShapeDtypeStruct((n, hq), BF16),
                   jax.ShapeDtypeStruct((n, hq), BF16),
                   jax.ShapeDtypeStruct((n, hv), BF16)],
        compiler_params=_cparams(("parallel",)),
    )(u, u, qg, kvg, wq, wk, wv, rope_c, rope_slo, rope_shi)


def _softmax_step(s, v, m_ref, l_ref, acc_ref):
    m_prev = m_ref[...]
    m_new = jnp.maximum(m_prev, jnp.max(s, axis=-1, keepdims=True))
    alpha = jnp.exp(m_prev - m_new)
    p = jnp.exp(s - m_new)
    l_ref[...] = alpha * l_ref[...] + jnp.sum(p, axis=-1, keepdims=True)
    acc_ref[...] = alpha * acc_ref[...] + _dot(p.astype(BF16), v)
    m_ref[...] = m_new


def _mla_attn_kernel(q_ref, k_ref, v_ref, cb_ref, o_ref, m_ref, l_ref, acc_ref, *, t, n_chunks):
    qi = pl.program_id(2)
    q = q_ref[0]
    m_ref[...] = jnp.full(m_ref.shape, NEG_INF, F32)
    l_ref[...] = jnp.zeros(l_ref.shape, F32)
    acc_ref[...] = jnp.zeros(acc_ref.shape, F32)
    for c in range(n_chunks):
        @pl.when(c < qi)
        def _():
            s = _dot_nt(q, k_ref[0, c * t:(c + 1) * t, :])
            _softmax_step(s, v_ref[0, c * t:(c + 1) * t, :], m_ref, l_ref, acc_ref)

        @pl.when(c == qi)
        def _():
            s = _dot_nt(q, k_ref[0, c * t:(c + 1) * t, :]) + cb_ref[...]
            _softmax_step(s, v_ref[0, c * t:(c + 1) * t, :], m_ref, l_ref, acc_ref)
    o_ref[0] = (acc_ref[...] / l_ref[...]).astype(o_ref.dtype)


def mla_attention(q, k, v, *, t):
    b, s, _ = q.shape
    causal = jnp.asarray(np.where(np.arange(t)[None, :] <= np.arange(t)[:, None], 0.0, NEG_INF), F32)
    kern = functools.partial(_mla_attn_kernel, t=t, n_chunks=s // t)
    return pl.pallas_call(
        kern,
        grid=(b, MLA_HEADS, s // t),
        in_specs=[pl.BlockSpec((1, t, MLA_QK_PAD), lambda b_, h, i: (b_, i, h)),
                  pl.BlockSpec((1, s, MLA_QK_PAD), lambda b_, h, i: (b_, 0, h)),
                  pl.BlockSpec((1, s, MLA_V), lambda b_, h, i: (b_, 0, h)),
                  pl.BlockSpec((t, t), lambda b_, h, i: (0, 0))],
        out_specs=pl.BlockSpec((1, t, MLA_V), lambda b_, h, i: (b_, i, h)),
        out_shape=jax.ShapeDtypeStruct((b, s, MLA_HEADS * MLA_V), BF16),
        scratch_shapes=[pltpu.VMEM((t, 1), F32), pltpu.VMEM((t, 1), F32), pltpu.VMEM((t, MLA_V), F32)],
        compiler_params=_cparams(("parallel", "parallel", "arbitrary")),
    )(q, k, v, causal)


def _rglru_kernel(x_ref, y_ref, cw_ref, cb_ref, gaw_ref, gab_ref, gxw_ref, gxb_ref, lam_ref,
                  o_ref, xbuf, h_ref, *, ts):
    t = pl.program_id(2)

    @pl.when(t == 0)
    def _():
        xbuf[0:8, :] = jnp.zeros((8, LANES), F32)
        h_ref[...] = jnp.zeros(h_ref.shape, F32)

    x = x_ref[0]
    xbuf[8:, :] = x
    cw = cw_ref[...]
    xc = cb_ref[...] + cw[3:4] * x
    for kk in range(CONV_WIDTH - 1):
        back = CONV_WIDTH - 1 - kk
        xc = xc + cw[kk:kk + 1] * xbuf[8 - back:8 - back + ts, :]
    xbuf[0:8, :] = x[ts - 8:, :]

    xcb = xc.astype(BF16)
    r = _sigmoid(_dot(xcb, gaw_ref[0]) + gab_ref[0])
    gi = _sigmoid(_dot(xcb, gxw_ref[0]) + gxb_ref[0])
    z = -lam_ref[...]
    softplus = jnp.maximum(z, 0.0) + jnp.log1p(jnp.exp(-jnp.abs(z)))
    log_a = (-LRU_C) * r * softplus
    a = jnp.exp(log_a)
    mult = jnp.sqrt(-jnp.tanh(log_a) * (a * a + 1.0))
    row = lax.broadcasted_iota(jnp.int32, (ts, 1), 0)
    mult = jnp.where(row + t * ts == 0, 1.0, mult)
    bv = mult * gi * xc

    d = 1
    while d < ts:
        keep = row >= d
        a_sh = jnp.where(keep, pltpu.roll(a, d, 0), 1.0)
        b_sh = jnp.where(keep, pltpu.roll(bv, d, 0), 0.0)
        bv = a * b_sh + bv
        a = a * a_sh
        d *= 2
    h = bv + a * h_ref[...]
    h_ref[...] = h[ts - 1:ts, :]
    o_ref[0] = (h * _gelu_tanh(y_ref[0])).astype(o_ref.dtype)


def rglru(u3, conv_w, conv_b, ga_w, ga_b, gx_w, gx_b, lam, *, ts):
    b, s, _ = u3.shape
    rec0 = EV_REC // LANES
    gate0 = EV_GATE // LANES
    cb = conv_b.reshape(1, LRU_WIDTH)
    gab = ga_b.reshape(LRU_BLOCKS, 1, LRU_BLOCK_W)
    gxb = gx_b.reshape(LRU_BLOCKS, 1, LRU_BLOCK_W)
    lam2 = lam.reshape(1, LRU_WIDTH)
    blk_w = pl.BlockSpec((1, LRU_BLOCK_W, LRU_BLOCK_W), lambda b_, n, t: (n, 0, 0))
    blk_b = pl.BlockSpec((1, 1, LRU_BLOCK_W), lambda b_, n, t: (n, 0, 0))
    vec = pl.BlockSpec((1, LANES), lambda b_, n, t: (0, n))
    return pl.pallas_call(
        functools.partial(_rglru_kernel, ts=ts),
        grid=(b, LRU_BLOCKS, s // ts),
        in_specs=[pl.BlockSpec((1, ts, LANES), lambda b_, n, t: (b_, t, rec0 + n)),
                  pl.BlockSpec((1, ts, LANES), lambda b_, n, t: (b_, t, gate0 + n)),
                  pl.BlockSpec((CONV_WIDTH, LANES), lambda b_, n, t: (0, n)),
                  vec, blk_w, blk_b, blk_w, blk_b, vec],
        out_specs=pl.BlockSpec((1, ts, LANES), lambda b_, n, t: (b_, t, n)),
        out_shape=jax.ShapeDtypeStruct((b, s, LRU_WIDTH), BF16),
        scratch_shapes=[pltpu.VMEM((ts + 8, LANES), F32), pltpu.VMEM((1, LANES), F32)],
        compiler_params=_cparams(("parallel", "parallel", "arbitrary")),
    )(u3, u3, conv_w, cb, ga_w, gab, gx_w, gxb, lam2)


def _ffn_kernel(x_ref, g_ref, w1_ref, w3_ref, w2_ref, o_ref, xn_ref):
    f = pl.program_id(1)

    @pl.when(f == 0)
    def _():
        x = x_ref[...]
        xn_ref[...] = _rms(x, g_ref[...]).astype(BF16)
        o_ref[...] = x

    xn = xn_ref[...]
    h1 = _dot(xn, w1_ref[...])
    h3 = _dot(xn, w3_ref[...])
    hh = (h1 * _sigmoid(h1) * h3).astype(BF16)
    o_ref[...] += _dot(hh, w2_ref[...])


def ffn_dense(x, g, w1, w3, w2, *, tm, tf):
    n, d = x.shape
    ff = w1.shape[1]
    assert n % tm == 0 and ff % tf == 0
    return pl.pallas_call(
        _ffn_kernel,
        grid=(n // tm, ff // tf),
        in_specs=[pl.BlockSpec((tm, d), lambda i, f: (i, 0)),
                  pl.BlockSpec((1, d), lambda i, f: (0, 0)),
                  pl.BlockSpec((d, tf), lambda i, f: (0, f)),
                  pl.BlockSpec((d, tf), lambda i, f: (0, f)),
                  pl.BlockSpec((tf, d), lambda i, f: (f, 0))],
        out_specs=pl.BlockSpec((tm, d), lambda i, f: (i, 0)),
        out_shape=jax.ShapeDtypeStruct((n, d), F32),
        scratch_shapes=[pltpu.VMEM((tm, d), BF16)],
        compiler_params=_cparams(("parallel", "arbitrary")),
    )(x, g.reshape(1, d), w1, w3, w2)


def _compress_kernel(kc_ref, pe_ref, w1_ref, b1_ref, w2_ref, o_ref, *, n_half):
    pe = pe_ref[...]
    half_k = n_half * NSA_DH
    for g in range(NSA_GROUPS):
        lo, hi = [], []
        for l in range(n_half):
            a = l * NSA_KV_DIM + g * NSA_DH
            piece = kc_ref[0, :, a:a + NSA_DH]
            lo.append((piece + pe[l:l + 1]).astype(BF16))
            hi.append((piece + pe[n_half + l:n_half + l + 1]).astype(BF16))
        z0 = _dot(jnp.concatenate(lo, axis=1), w1_ref[0:half_k, :])
        z1 = _dot(jnp.concatenate(hi, axis=1), w1_ref[half_k:2 * half_k, :])
        rows = z1.shape[0]
        pre = z0 + pltpu.roll(z1, rows - 1, 0) + b1_ref[...]
        o_ref[0, :, g * NSA_DH:(g + 1) * NSA_DH] = _dot(_gelu_tanh(pre).astype(BF16), w2_ref[...]).astype(o_ref.dtype)


def nsa_compress(kc3, pe, w1, b1, w2):
    b, nchunk, width = kc3.shape
    n_half = CMP_BLOCK // CMP_STRIDE
    assert CMP_BLOCK == 2 * CMP_STRIDE and width == CMP_STRIDE * NSA_KV_DIM
    full = lambda a: pl.BlockSpec(a.shape, lambda i: (0,) * a.ndim)
    b1r = b1.reshape(1, NSA_DH)
    return pl.pallas_call(
        functools.partial(_compress_kernel, n_half=CMP_STRIDE),
        grid=(b,),
        in_specs=[pl.BlockSpec((1, nchunk, width), lambda i: (i, 0, 0)),
                  full(pe), full(w1), full(b1r), full(w2)],
        out_specs=pl.BlockSpec((1, nchunk, NSA_KV_DIM), lambda i: (i, 0, 0)),
        out_shape=jax.ShapeDtypeStruct((b, nchunk, NSA_KV_DIM), BF16),
        compiler_params=_cparams(("parallel",)),
    )(kc3, pe, w1, b1r, w2)


def _rope128(x, c, s):
    return x * c + pltpu.roll(x, NSA_DH // 2, 1) * s


def _split3(x):
    hi = x.astype(BF16)
    r1 = x - hi.astype(F32)
    mid = r1.astype(BF16)
    lo = (r1 - mid.astype(F32)).astype(BF16)
    return hi, mid, lo


def _nsa_attn_kernel(q_ref, ks_ref, vs_ref, kw_ref, vw_ref, gt_ref, kc_ref, vc_ref,
                     cq_ref, sq_ref, ck_ref, sk_ref, ovl_ref, exp_ref, wb_ref,
                     o_ref, ksr, vsb, kwr, vwb, selb, m_ref, l_ref, acc_ref,
                     *, tq, tk, seq, scale):
    qi = pl.program_id(2)
    hp = NSA_HPG
    n_sel = seq // SEL_BLOCK

    @pl.when(qi == 0)
    def _():
        ck, sk = ck_ref[...], sk_ref[...]
        ksr[...] = _rope128(ks_ref[0], ck, sk).astype(BF16)
        kwr[...] = _rope128(kw_ref[0], ck, sk).astype(BF16)
        vsb[...] = vs_ref[0].astype(BF16)
        vwb[...] = vw_ref[0].astype(BF16)

    q = q_ref[0] * scale
    cq, sq = cq_ref[...], sq_ref[...]
    heads = [q[:, p * NSA_DH:(p + 1) * NSA_DH] for p in range(hp)]
    qu = jnp.concatenate(heads, axis=0).astype(BF16)
    qr = jnp.concatenate([_rope128(h, cq, sq) for h in heads], axis=0).astype(BF16)
    t_row = qi * tq + lax.broadcasted_iota(jnp.int32, (tq, 1), 0)
    lane = lax.broadcasted_iota(jnp.int32, (1, LANES), 1)

    sc = _dot_nt(qu, kc_ref[0]).reshape(hp, tq, LANES)
    valid = (lane * CMP_STRIDE + (CMP_BLOCK - 1) <= t_row)[None]
    sm = jnp.where(valid, sc, NEG_INF)
    e = jnp.exp(sm - jnp.max(sm, axis=-1, keepdims=True))
    p = jnp.where(valid, e / jnp.sum(e, axis=-1, keepdims=True), 0.0)
    o_cmp = _dot(p.reshape(hp * tq, LANES).astype(BF16), vc_ref[0])

    psum = p[0]
    for i in range(1, hp):
        psum = psum + p[i]
    ovl = ovl_ref[...]
    imp = sum(_dot(part, ovl) for part in _split3(psum))
    cur = jnp.right_shift(t_row, SEL_BLOCK.bit_length() - 1)
    future = lane > cur
    forced = (lane == 0) | (lane == cur) | (lane == cur - 1)
    score = jnp.where(future, -1.0, jnp.where(forced, SEL_FORCE, imp))
    sc_t = score.T[0:n_sel, :]
    blk = lax.broadcasted_iota(jnp.int32, (n_sel, 1), 0)
    cnt = jnp.zeros((n_sel, tq), F32)
    for j in range(n_sel):
        other = sc_t[j:j + 1, :]
        beats = (other > sc_t) | ((other == sc_t) & (blk > j))
        cnt = cnt + jnp.where(beats, 1.0, 0.0)
    sel_t = jnp.where(cnt < SEL_COUNT, 1.0, 0.0)
    sel = jnp.concatenate([sel_t, jnp.zeros((LANES - n_sel, tq), F32)], axis=0).T.astype(BF16)
    picked = _dot(sel, exp_ref[...])
    kall = lax.broadcasted_iota(jnp.int32, (1, seq), 1)
    selb[...] = jnp.where((picked > 0.5) & (kall <= t_row), 0.0, NEG_INF)

    m_ref[...] = jnp.full(m_ref.shape, NEG_INF, F32)
    l_ref[...] = jnp.zeros(l_ref.shape, F32)
    acc_ref[...] = jnp.zeros(acc_ref.shape, F32)
    for c in range(seq // tk):
        @pl.when(c * tk <= qi * tq + (tq - 1))
        def _():
            s = _dot_nt(qr, ksr[c * tk:(c + 1) * tk, :]).reshape(hp, tq, tk)
            s = (s + selb[:, c * tk:(c + 1) * tk][None]).reshape(hp * tq, tk)
            _softmax_step(s, vsb[c * tk:(c + 1) * tk, :], m_ref, l_ref, acc_ref)
    o_sel = acc_ref[...] / l_ref[...]

    span = WINDOW + tq
    start = pl.multiple_of(jnp.maximum(qi * tq - WINDOW, 0), tq)
    sw = _dot_nt(qr, kwr[pl.ds(start, span), :]).reshape(hp, tq, span) + wb_ref[0][None]
    ew = jnp.exp(sw - jnp.max(sw, axis=-1, keepdims=True))
    den = jnp.sum(ew, axis=-1, keepdims=True).reshape(hp * tq, 1)
    o_win = _dot(ew.reshape(hp * tq, span).astype(BF16), vwb[pl.ds(start, span), :]) / den

    gates = _sigmoid(gt_ref[0])
    for i in range(hp):
        rows = slice(i * tq, (i + 1) * tq)
        o = (gates[:, 3 * i:3 * i + 1] * o_cmp[rows] + gates[:, 3 * i + 1:3 * i + 2] * o_sel[rows]
             + gates[:, 3 * i + 2:3 * i + 3] * o_win[rows])
        o_ref[0, :, i * NSA_DH:(i + 1) * NSA_DH] = o.astype(o_ref.dtype)


def _window_bias(tq):
    span = WINDOW + tq
    out = []
    for qi in range(WINDOW // tq + 1):
        start = max(qi * tq - WINDOW, 0)
        t = qi * tq + np.arange(tq)[:, None]
        kpos = start + np.arange(span)[None, :]
        out.append(np.where((kpos <= t) & (kpos > t - WINDOW), 0.0, NEG_INF))
    return jnp.asarray(np.stack(out), F32)


def nsa_attention(u3, k_cmp, v_cmp, rope_c, rope_s, ovl, expand, *, tq, tk):
    b, s, _ = u3.shape
    assert s // CMP_STRIDE == LANES and tq % SEL_BLOCK == 0 and WINDOW % tq == 0
    hp = NSA_HPG
    nwb = WINDOW // tq
    wbias = _window_bias(tq)
    col = lambda off: (lambda b_, g, i: (b_, 0, off // NSA_DH + g))
    seq_blk = lambda off: pl.BlockSpec((1, s, NSA_DH), col(off))
    full = lambda a: pl.BlockSpec(a.shape, lambda b_, g, i: (0,) * a.ndim)
    cmp_blk = pl.BlockSpec((1, LANES, NSA_DH), lambda b_, g, i: (b_, 0, g))
    rope_q = pl.BlockSpec((tq, NSA_DH), lambda b_, g, i: (i, 0))
    kern = functools.partial(_nsa_attn_kernel, tq=tq, tk=tk, seq=s, scale=NSA_DH ** -0.5)
    return pl.pallas_call(
        kern,
        grid=(b, NSA_GROUPS, s // tq),
        in_specs=[pl.BlockSpec((1, tq, hp * NSA_DH), lambda b_, g, i: (b_, i, g)),
                  seq_blk(OD_KS), seq_blk(OD_VS), seq_blk(OD_KW), seq_blk(OD_VW),
                  pl.BlockSpec((1, tq, LANES), lambda b_, g, i: (b_, i, OD_G // LANES + g)),
                  cmp_blk, cmp_blk, rope_q, rope_q, full(rope_c), full(rope_s), full(ovl), full(expand),
                  pl.BlockSpec((1, tq, WINDOW + tq), lambda b_, g, i: (jnp.minimum(i, nwb), 0, 0))],
        out_specs=pl.BlockSpec((1, tq, hp * NSA_DH), lambda b_, g, i: (b_, i, g)),
        out_shape=jax.ShapeDtypeStruct((b, s, NSA_Q_DIM), BF16),
        scratch_shapes=[pltpu.VMEM((s, NSA_DH), BF16)] * 4 + [
            pltpu.VMEM((tq, s), F32),
            pltpu.VMEM((hp * tq, 1), F32), pltpu.VMEM((hp * tq, 1), F32), pltpu.VMEM((hp * tq, NSA_DH), F32)],
        compiler_params=_cparams(("parallel", "parallel", "arbitrary")),
    )(u3, u3, u3, u3, u3, u3, k_cmp, v_cmp, rope_c, rope_s, rope_c, rope_s, ovl, expand, wbias)


def _router_kernel(x_ref, g_ref, wr_ref, br_ref, xn_ref, meta_ref, cnt_ref, *, tm):
    @pl.when(pl.program_id(0) == 0)
    def _():
        cnt_ref[...] = jnp.zeros(cnt_ref.shape, F32)

    xn = _rms(x_ref[...], g_ref[...])
    xn_ref[...] = xn.astype(xn_ref.dtype)
    xh, xm, _ = _split3(xn)
    wh, wm, _ = _split3(wr_ref[...])
    logits = _dot(xh, wh) + _dot(xh, wm) + _dot(xm, wh) + br_ref[...]
    lane = lax.broadcasted_iota(jnp.int32, (1, LANES), 1).astype(F32)
    lg = jnp.where(lane < N_EXPERTS, logits, NEG_INF)
    m1 = jnp.max(lg, axis=-1, keepdims=True)
    e1 = jnp.min(jnp.where(lg == m1, lane, float(LANES)), axis=-1, keepdims=True)
    lg2 = jnp.where(lane == e1, NEG_INF, lg)
    m2 = jnp.max(lg2, axis=-1, keepdims=True)
    e2 = jnp.min(jnp.where(lg2 == m2, lane, float(LANES)), axis=-1, keepdims=True)
    ex = jnp.exp(m2 - m1)
    den = 1.0 + ex
    g1 = 1.0 / den
    g2 = ex / den
    oh = jnp.where((lane == e1) | (lane == e2), 1.0, 0.0)
    r = lax.broadcasted_iota(jnp.int32, (tm, tm), 0)
    c = lax.broadcasted_iota(jnp.int32, (tm, tm), 1)
    tri = jnp.where(r > c, 1.0, 0.0).astype(BF16)
    cum = _dot(tri, oh.astype(BF16)) + cnt_ref[0:1, :]
    pos1 = jnp.sum(jnp.where(lane == e1, cum, 0.0), axis=-1, keepdims=True)
    pos2 = jnp.sum(jnp.where(lane == e2, cum, 0.0), axis=-1, keepdims=True)
    cnt_ref[...] = cnt_ref[...] + jnp.sum(oh, axis=0, keepdims=True)
    meta = jnp.where(lane == 0, e1, 0.0)
    meta = jnp.where(lane == 1, e2, meta)
    meta = jnp.where(lane == 2, g1, meta)
    meta = jnp.where(lane == 3, g2, meta)
    meta = jnp.where(lane == 4, pos1, meta)
    meta = jnp.where(lane == 5, pos2, meta)
    meta_ref[...] = meta


def moe_router(x, g, wr, br, *, tm):
    n, d = x.shape
    return pl.pallas_call(
        functools.partial(_router_kernel, tm=tm),
        grid=(n // tm,),
        in_specs=[pl.BlockSpec((tm, d), lambda i: (i, 0)),
                  pl.BlockSpec((1, d), lambda i: (0, 0)),
                  pl.BlockSpec((d, LANES), lambda i: (0, 0)),
                  pl.BlockSpec((1, LANES), lambda i: (0, 0))],
        out_specs=[pl.BlockSpec((tm, d), lambda i: (i, 0)),
                   pl.BlockSpec((tm, LANES), lambda i: (i, 0)),
                   pl.BlockSpec((8, LANES), lambda i: (0, 0))],
        out_shape=[jax.ShapeDtypeStruct((n, d), BF16),
                   jax.ShapeDtypeStruct((n, LANES), F32),
                   jax.ShapeDtypeStruct((8, LANES), F32)],
        compiler_params=_cparams(("arbitrary",)),
    )(x, g.reshape(1, d), wr, br)


def _gather_chunk(per_worker, row_bytes):
    best = 0
    for c in range(8, per_worker + 1, 8):
        if per_worker % c == 0 and c * row_bytes <= SC_GATHER_BYTES and c <= LANES:
            best = c
    assert best > 0, (per_worker, row_bytes)
    return best


def sc_gather_rows(table, idx):
    _, d = table.shape
    b = idx.shape[0]
    assert b % (8 * SC_WORKERS) == 0 and table.dtype.itemsize == 4
    per_w = b // SC_WORKERS
    chunk = _gather_chunk(per_w, d * 4)
    mesh = plsc.VectorSubcoreMesh(core_axis_name="c", subcore_axis_name="s",
                                  num_cores=SC_CORES, num_subcores=SC_SUBCORES)

    @functools.partial(
        pl.kernel, mesh=mesh,
        out_type=jax.ShapeDtypeStruct((b, d), table.dtype),
        scratch_types=[pltpu.VMEM((chunk,), jnp.int32), pltpu.VMEM((chunk, d), table.dtype),
                       pltpu.SemaphoreType.DMA])
    def gather(table_hbm, idx_hbm, out_hbm, idx_v, rows_v, sem):
        wid = lax.axis_index("s") * SC_CORES + lax.axis_index("c")
        base = wid * per_w

        @pl.loop(0, per_w // chunk)
        def _(c):
            off = pl.multiple_of(base + c * chunk, 8)
            pltpu.sync_copy(idx_hbm.at[pl.ds(off, chunk)], idx_v)
            pltpu.async_copy(table_hbm.at[idx_v], rows_v, sem).wait()
            pltpu.sync_copy(rows_v, out_hbm.at[pl.ds(off, chunk)])

    return gather(table, idx)


def _expert_kernel(te_ref, tr_ref, x_ref, w1_ref, w3_ref, w2_ref, o_ref):
    t = pl.program_id(0)
    f = pl.program_id(1)
    rows = tr_ref[t]

    @pl.when(f == 0)
    def _():
        o_ref[...] = jnp.zeros(o_ref.shape, o_ref.dtype)

    @pl.when(rows > 0)
    def _():
        w1 = w1_ref[0, 0].astype(BF16)
        w3 = w3_ref[0, 0].astype(BF16)
        w2 = w2_ref[0, 0].astype(BF16)
        for start, size in MOE_REGIONS:
            @pl.when(start < rows)
            def _():
                for c0 in range(start, start + size, MOE_CHAIN):
                    rs = slice(c0, c0 + min(MOE_CHAIN, size))
                    xb = x_ref[rs, :]
                    h1 = _dot(xb, w1)
                    h3 = _dot(xb, w3)
                    hh = (h1 * _sigmoid(h1) * h3).astype(BF16)
                    o_ref[rs, :] += _dot(hh, w2)


def moe_experts(xr, tile_e, tile_rows, w1, w3, w2, layer, *, tf):
    n_rows, d = xr.shape
    n_tiles = n_rows // MOE_TILE
    ff = w1.shape[3]
    nf = ff // tf

    def f_eff(t, f, tr):
        return jnp.where(tr[t] > 0, f, nf - 1)

    single = pl.Buffered(1)
    grid_spec = pltpu.PrefetchScalarGridSpec(
        num_scalar_prefetch=2,
        grid=(n_tiles, nf),
        in_specs=[pl.BlockSpec((MOE_TILE, d), lambda t, f, te, tr: (t, 0), pipeline_mode=single),
                  pl.BlockSpec((1, 1, d, tf), lambda t, f, te, tr: (layer, te[t], 0, f_eff(t, f, tr))),
                  pl.BlockSpec((1, 1, d, tf), lambda t, f, te, tr: (layer, te[t], 0, f_eff(t, f, tr))),
                  pl.BlockSpec((1, 1, tf, d), lambda t, f, te, tr: (layer, te[t], f_eff(t, f, tr), 0))],
        out_specs=pl.BlockSpec((MOE_TILE, d), lambda t, f, te, tr: (t, 0), pipeline_mode=single),
    )
    return pl.pallas_call(
        _expert_kernel,
        grid_spec=grid_spec,
        out_shape=jax.ShapeDtypeStruct((n_rows, d), F32),
        compiler_params=_cparams(("parallel", "arbitrary")),
    )(tile_e, tile_rows, xr, w1, w3, w2)


def _combine_kernel(x_ref, y1_ref, y2_ref, meta_ref, g_ref, o_ref, *, final_norm):
    meta = meta_ref[...]
    y = x_ref[...] + (meta[:, 2:3] * y1_ref[...] + meta[:, 3:4] * y2_ref[...])
    if final_norm:
        y = _rms(y, g_ref[...])
    o_ref[...] = y


def moe_combine(x, yg, meta, g, *, final_norm, tm):
    n, d = x.shape
    nb = n // tm
    row = pl.BlockSpec((tm, d), lambda i: (i, 0))
    return pl.pallas_call(
        functools.partial(_combine_kernel, final_norm=final_norm),
        grid=(nb,),
        in_specs=[row, row, pl.BlockSpec((tm, d), lambda i: (i + nb, 0)),
                  pl.BlockSpec((tm, LANES), lambda i: (i, 0)),
                  pl.BlockSpec((1, d), lambda i: (0, 0))],
        out_specs=row,
        out_shape=jax.ShapeDtypeStruct((n, d), F32),
        compiler_params=_cparams(("parallel",)),
    )(x, yg, yg, meta, g.reshape(1, d))


def _mla_rope_tables(seq):
    half = MLA_ROPE // 2
    pos = jnp.arange(seq, dtype=F32)
    inv = ROPE_THETA ** (-jnp.arange(0, MLA_ROPE, 2, dtype=F32) / MLA_ROPE)
    ang = pos[:, None] * inv[None, :]
    cos, sin = jnp.cos(ang), jnp.sin(ang)
    z = jnp.zeros((seq, half), F32)
    pad = jnp.zeros((seq, LANES - MLA_ROPE), F32)
    c = jnp.concatenate([cos, cos, pad], axis=1)
    s_lo = jnp.concatenate([-sin, z, pad], axis=1)
    s_hi = jnp.concatenate([z, sin, pad], axis=1)
    return c, s_lo, s_hi


def _nsa_rope_tables(seq):
    pos = jnp.arange(seq, dtype=F32)
    inv = ROPE_THETA ** (-jnp.arange(0, NSA_DH, 2, dtype=F32) / NSA_DH)
    ang = pos[:, None] * inv[None, :]
    cos, sin = jnp.cos(ang), jnp.sin(ang)
    return jnp.concatenate([cos, cos], axis=1), jnp.concatenate([-sin, sin], axis=1)


def _selection_constants(seq):
    n_sel = seq // SEL_BLOCK
    nc = (seq - CMP_BLOCK) // CMP_STRIDE + 1
    cmp_start = np.arange(LANES) * CMP_STRIDE
    sel_start = np.arange(LANES) * SEL_BLOCK
    ovl = ((cmp_start[:, None] < sel_start[None, :] + SEL_BLOCK) &
           (cmp_start[:, None] + CMP_BLOCK > sel_start[None, :]))
    ovl &= (np.arange(LANES)[:, None] < nc) & (np.arange(LANES)[None, :] < n_sel)
    expand = (np.arange(seq)[None, :] // SEL_BLOCK == np.arange(LANES)[:, None])
    return jnp.asarray(ovl, BF16), jnp.asarray(expand, BF16)


def even_layer(x, seq, p):
    n = x.shape[0]
    b = n // seq
    (norm_mix, w_in, q_norm, w_q_up, kv_norm, w_kv_up, conv_w, conv_b, ga_w, ga_b, gx_w, gx_b,
     lam, w_out, norm_ffn, w1, w3, w2) = p
    d = D_MODEL
    o1 = MLA_Q_LORA + MLA_KV_LORA
    o2 = o1 + MLA_ROPE
    w_pack = jnp.concatenate(
        [w_in[:, :o1], w_in[:, o2:], w_in[:, o1:o2], jnp.zeros((d, LANES - MLA_ROPE), F32)], axis=1).astype(BF16)
    u = norm_matmul(x, norm_mix, w_pack, tm=512, tn=EV_PACKED // 3)

    wq = w_q_up.reshape(MLA_Q_LORA, MLA_HEADS, MLA_NOPE + MLA_ROPE)
    wq = jnp.pad(wq, ((0, 0), (0, 0), (0, MLA_QK_PAD - MLA_NOPE - MLA_ROPE)))
    wq = wq.reshape(MLA_Q_LORA, MLA_HEADS * MLA_QK_PAD).astype(BF16)
    wkv = w_kv_up.reshape(MLA_KV_LORA, MLA_HEADS, MLA_NOPE + MLA_V)
    wk = wkv[:, :, :MLA_NOPE].reshape(MLA_KV_LORA, MLA_HEADS * MLA_NOPE).astype(BF16)
    wv = wkv[:, :, MLA_NOPE:].reshape(MLA_KV_LORA, MLA_HEADS * MLA_V).astype(BF16)
    rc, rlo, rhi = _mla_rope_tables(seq)
    q, k, v = mla_up(u, q_norm, kv_norm, wq, wk, wv, rc, rlo, rhi, seq=seq, tm=512)
    o_mla = mla_attention(q.reshape(b, seq, -1), k.reshape(b, seq, -1), v.reshape(b, seq, -1), t=512)

    o_rec = rglru(u.reshape(b, seq, EV_PACKED), conv_w, conv_b, ga_w.astype(BF16), ga_b,
                  gx_w.astype(BF16), gx_b, lam, ts=512)
    x = matmul_residual([o_mla.reshape(n, -1), o_rec.reshape(n, -1)], w_out.astype(BF16), x, tm=512, tn=1024)
    return ffn_dense(x, norm_ffn, w1.astype(BF16), w3.astype(BF16), w2.astype(BF16), tm=512, tf=512)


def _moe_dispatch(meta, counts, n):
    e1 = meta[:, 0].astype(jnp.int32)
    e2 = meta[:, 1].astype(jnp.int32)
    pos1 = meta[:, 4].astype(jnp.int32)
    pos2 = meta[:, 5].astype(jnp.int32)
    cnt = counts[0, :N_EXPERTS].astype(jnp.int32)
    n_tiles = (n * TOP_K) // MOE_TILE + N_EXPERTS
    tiles_e = (cnt + MOE_TILE - 1) // MOE_TILE
    tend = jnp.cumsum(tiles_e)
    tstart = tend - tiles_e
    d1 = tstart[e1] * MOE_TILE + pos1
    d2 = tstart[e2] * MOE_TILE + pos2
    tid = jnp.arange(n_tiles, dtype=jnp.int32)
    te = jnp.minimum(jnp.searchsorted(tend, tid, side='right'), N_EXPERTS - 1).astype(jnp.int32)
    rows = jnp.clip(cnt[te] - (tid - tstart[te]) * MOE_TILE, 0, MOE_TILE)
    rows = jnp.where(tid < tend[-1], rows, 0).astype(jnp.int32)
    last_e = te[jnp.maximum(tend[-1] - 1, 0)]
    te = jnp.where(tid < tend[-1], te, last_e).astype(jnp.int32)
    tok = jnp.arange(n, dtype=jnp.int32)
    row_tok = jnp.zeros((n_tiles * MOE_TILE,), jnp.int32).at[d1].set(tok).at[d2].set(tok)
    return d1, d2, row_tok, te, rows


def odd_layer(x, seq, p, experts, final_g):
    n = x.shape[0]
    b = n // seq
    (norm_mix, w_in, ck_pe, ck_w1, ck_b1, ck_w2, cv_pe, cv_w1, cv_b1, cv_w2, w_out, norm_ffn,
     router_w, router_b) = p
    ew1, ew3, ew2, layer = experts
    d = D_MODEL
    wg = w_in[:, OD_G:].reshape(d, NSA_GROUPS, NSA_HPG * 3)
    wg = jnp.pad(wg, ((0, 0), (0, 0), (0, LANES - NSA_HPG * 3))).reshape(d, NSA_GROUPS * LANES)
    w_pack = jnp.concatenate([w_in[:, :OD_G], wg], axis=1).astype(BF16)
    u = norm_matmul(x, norm_mix, w_pack, tm=512, tn=512)
    u3 = u.reshape(b, seq, OD_PACKED)

    nchunk = seq // CMP_STRIDE
    kc3 = u3[:, :, OD_KC:OD_KC + NSA_KV_DIM].reshape(b, nchunk, CMP_STRIDE * NSA_KV_DIM)
    vc3 = u3[:, :, OD_VC:OD_VC + NSA_KV_DIM].reshape(b, nchunk, CMP_STRIDE * NSA_KV_DIM)
    k_cmp = nsa_compress(kc3, ck_pe, ck_w1.astype(BF16), ck_b1, ck_w2.astype(BF16))
    v_cmp = nsa_compress(vc3, cv_pe, cv_w1.astype(BF16), cv_b1, cv_w2.astype(BF16))
    rc, rs = _nsa_rope_tables(seq)
    ovl, expand = _selection_constants(seq)
    o = nsa_attention(u3, k_cmp, v_cmp, rc, rs, ovl, expand, tq=256, tk=512)
    x = matmul_residual([o.reshape(n, -1)], w_out.astype(BF16), x, tm=512, tn=1024)

    wr = jnp.pad(router_w, ((0, 0), (0, LANES - N_EXPERTS)))
    br = jnp.pad(router_b, (0, LANES - N_EXPERTS)).reshape(1, LANES)
    xn, meta, counts = moe_router(x, norm_ffn, wr, br, tm=512)
    d1, d2, row_tok, te, rows = _moe_dispatch(meta, counts, n)
    xn32 = lax.bitcast_convert_type(xn.reshape(n, d // 2, 2), F32)
    xr = lax.bitcast_convert_type(sc_gather_rows(xn32, row_tok), BF16).reshape(-1, d)
    yr = moe_experts(xr, te, rows, ew1, ew3, ew2, layer, tf=256)
    yg = sc_gather_rows(yr, jnp.concatenate([d1, d2]))
    g = final_g if final_g is not None else norm_ffn
    return moe_combine(x, yg, meta, g, final_norm=final_g is not None, tm=512)


def kernel(x, ev_norm_mix, ev_w_in, ev_q_norm, ev_w_q_up, ev_kv_norm, ev_w_kv_up, ev_conv_w, ev_conv_b, ev_gate_a_w, ev_gate_a_b, ev_gate_x_w, ev_gate_x_b, ev_lru_lambda, ev_w_out, ev_norm_ffn, ev_ffn_w1, ev_ffn_w3, ev_ffn_w2, od_norm_mix, od_w_in, od_cmp_k_pe, od_cmp_k_w1, od_cmp_k_b1, od_cmp_k_w2, od_cmp_v_pe, od_cmp_v_w1, od_cmp_v_b1, od_cmp_v_w2, od_w_out, od_norm_ffn, od_router_w, od_router_b, od_exp_w1, od_exp_w3, od_exp_w2, final_norm):
    bsz, seq, d = x.shape
    ev = (ev_norm_mix, ev_w_in, ev_q_norm, ev_w_q_up, ev_kv_norm, ev_w_kv_up, ev_conv_w, ev_conv_b,
          ev_gate_a_w, ev_gate_a_b, ev_gate_x_w, ev_gate_x_b, ev_lru_lambda, ev_w_out, ev_norm_ffn,
          ev_ffn_w1, ev_ffn_w3, ev_ffn_w2)
    od = (od_norm_mix, od_w_in, od_cmp_k_pe, od_cmp_k_w1, od_cmp_k_b1, od_cmp_k_w2, od_cmp_v_pe,
          od_cmp_v_w1, od_cmp_v_b1, od_cmp_v_w2, od_w_out, od_norm_ffn, od_router_w, od_router_b)
    h = x.reshape(bsz * seq, d)
    for layer in range(DEPTH):
        i = layer // 2
        if layer % 2 == 0:
            h = even_layer(h, seq, tuple(a[i] for a in ev))
        else:
            h = odd_layer(h, seq, tuple(a[i] for a in od), (od_exp_w1, od_exp_w3, od_exp_w2, i),
                          final_norm if layer == DEPTH - 1 else None)
    return h.reshape(bsz, seq, d)
```

```python
import functools
import math

import numpy as np
import jax
import jax.numpy as jnp
from jax import lax
from jax.experimental import pallas as pl
from jax.experimental.pallas import tpu as pltpu
from jax.experimental.pallas import tpu_sc as plsc

F32 = jnp.float32
BF16 = jnp.bfloat16

D_MODEL = 2048
DEPTH = 4
RMS_EPS = 1e-6
ROPE_THETA = 10000.0
NEG_INF = -1e30

MLA_HEADS = 8
MLA_Q_LORA = 768
MLA_KV_LORA = 512
MLA_NOPE = 128
MLA_ROPE = 64
MLA_V = 128
MLA_QK_PAD = 256

LRU_WIDTH = D_MODEL // 2
LRU_BLOCKS = 8
LRU_BLOCK_W = LRU_WIDTH // LRU_BLOCKS
LRU_C = 8.0
CONV_WIDTH = 4

NSA_HEADS = 16
NSA_GROUPS = 4
NSA_HPG = NSA_HEADS // NSA_GROUPS
NSA_DH = D_MODEL // NSA_HEADS
CMP_BLOCK = 32
CMP_STRIDE = 16
SEL_BLOCK = 64
SEL_COUNT = 16
SEL_FORCE = 1e4
WINDOW = 512
NSA_Q_DIM = NSA_HEADS * NSA_DH
NSA_KV_DIM = NSA_GROUPS * NSA_DH

DENSE_FF = 5632
N_EXPERTS = 8
TOP_K = 2
EXPERT_FF = 7168

LANES = 128
SC_CORES = 2
SC_SUBCORES = 16
SC_WORKERS = SC_CORES * SC_SUBCORES
SC_GATHER_BYTES = 256 * 1024
VMEM_LIMIT = 56 * 1024 * 1024

EV_CQ = 0
EV_CKV = MLA_Q_LORA
EV_REC = MLA_Q_LORA + MLA_KV_LORA
EV_GATE = EV_REC + LRU_WIDTH
EV_PE = EV_GATE + LRU_WIDTH
EV_PACKED = EV_PE + LANES

OD_Q = 0
OD_KC = NSA_Q_DIM
OD_VC = OD_KC + NSA_KV_DIM
OD_KS = OD_VC + NSA_KV_DIM
OD_VS = OD_KS + NSA_KV_DIM
OD_KW = OD_VS + NSA_KV_DIM
OD_VW = OD_KW + NSA_KV_DIM
OD_G = OD_VW + NSA_KV_DIM
OD_PACKED = OD_G + NSA_GROUPS * LANES

MOE_TILE = 2304
MOE_ALIGN = 16
MOE_COPY = 256
MOE_PIECES = (1024, 1024, 512, 256, 128)
MOE_CHAIN = 512


def _cparams(sem):
    return pltpu.CompilerParams(dimension_semantics=sem, vmem_limit_bytes=VMEM_LIMIT)


def _rms(x, g):
    ms = jnp.mean(x * x, axis=-1, keepdims=True)
    return x * lax.rsqrt(ms + RMS_EPS) * g


def _sigmoid(x):
    return 1.0 / (1.0 + jnp.exp(-x))


def _gelu_tanh(x):
    return 0.5 * x * (1.0 + jnp.tanh(math.sqrt(2.0 / math.pi) * (x + 0.044715 * (x * x * x))))


def _dot(a, b):
    return jnp.dot(a, b, preferred_element_type=F32)


def _dot_nt(a, b):
    return lax.dot_general(a, b, (((1,), (1,)), ((), ())), preferred_element_type=F32)


def _norm_mm_kernel(x_ref, g_ref, w_ref, o_ref, xn_ref):
    @pl.when(pl.program_id(1) == 0)
    def _():
        xn_ref[...] = _rms(x_ref[...], g_ref[...]).astype(BF16)

    o_ref[...] = _dot(xn_ref[...], w_ref[...]).astype(o_ref.dtype)


def norm_matmul(x, g, w, *, tm, tn, out_dtype=F32):
    n, k = x.shape
    m = w.shape[1]
    assert n % tm == 0 and m % tn == 0
    return pl.pallas_call(
        _norm_mm_kernel,
        grid=(n // tm, m // tn),
        in_specs=[pl.BlockSpec((tm, k), lambda i, j: (i, 0)),
                  pl.BlockSpec((1, k), lambda i, j: (0, 0)),
                  pl.BlockSpec((k, tn), lambda i, j: (0, j))],
        out_specs=pl.BlockSpec((tm, tn), lambda i, j: (i, j)),
        out_shape=jax.ShapeDtypeStruct((n, m), out_dtype),
        scratch_shapes=[pltpu.VMEM((tm, k), BF16)],
        compiler_params=_cparams(("parallel", "arbitrary")),
    )(x, g.reshape(1, k), w)


def _mm_res_kernel(*refs, n_in):
    xs = refs[:n_in]
    ws = refs[n_in:2 * n_in]
    res_ref = refs[2 * n_in]
    o_ref = refs[2 * n_in + 1]
    acc = res_ref[...]
    for x_ref, w_ref in zip(xs, ws):
        acc = acc + _dot(x_ref[...], w_ref[...])
    o_ref[...] = acc


def matmul_residual(xs, w, res, *, tm, tn):
    n = res.shape[0]
    m = w.shape[1]
    n_in = len(xs)
    in_specs = [pl.BlockSpec((tm, x.shape[1]), lambda i, j: (i, 0)) for x in xs]
    row = 0
    for x in xs:
        kx = x.shape[1]
        assert row % kx == 0
        in_specs.append(pl.BlockSpec((kx, tn), lambda i, j, rb=row // kx: (rb, j)))
        row += kx
    assert row == w.shape[0]
    in_specs.append(pl.BlockSpec((tm, tn), lambda i, j: (i, j)))
    return pl.pallas_call(
        functools.partial(_mm_res_kernel, n_in=n_in),
        grid=(n // tm, m // tn),
        in_specs=in_specs,
        out_specs=pl.BlockSpec((tm, tn), lambda i, j: (i, j)),
        out_shape=jax.ShapeDtypeStruct((n, m), F32),
        compiler_params=_cparams(("parallel", "arbitrary")),
    )(*xs, *([w] * n_in), res)


def _rope64(x, c, s_lo, s_hi):
    return x * c + pltpu.roll(x, 96, 1) * s_lo + pltpu.roll(x, 32, 1) * s_hi


def _mla_up_kernel(u_ref, pe_ref, qg_ref, kvg_ref, wq_ref, wk_ref, wv_ref,
                   c_ref, slo_ref, shi_ref, q_ref, k_ref, v_ref):
    u = u_ref[...]
    c, s_lo, s_hi = c_ref[...], slo_ref[...], shi_ref[...]
    qn = _rms(u[:, EV_CQ:EV_CQ + MLA_Q_LORA], qg_ref[...]).astype(BF16)
    kvn = _rms(u[:, EV_CKV:EV_CKV + MLA_KV_LORA], kvg_ref[...]).astype(BF16)
    q = _dot(qn, wq_ref[...]) * ((MLA_NOPE + MLA_ROPE) ** -0.5)
    kn = _dot(kvn, wk_ref[...])
    v_ref[...] = _dot(kvn, wv_ref[...]).astype(v_ref.dtype)
    kpe = _rope64(pe_ref[...], c, s_lo, s_hi).astype(k_ref.dtype)
    for h in range(MLA_HEADS):
        a = h * MLA_QK_PAD
        q_ref[:, a:a + LANES] = q[:, a:a + LANES].astype(q_ref.dtype)
        q_ref[:, a + LANES:a + 2 * LANES] = _rope64(q[:, a + LANES:a + 2 * LANES], c, s_lo, s_hi).astype(q_ref.dtype)
        k_ref[:, a:a + LANES] = kn[:, h * LANES:(h + 1) * LANES].astype(k_ref.dtype)
        k_ref[:, a + LANES:a + 2 * LANES] = kpe


def mla_up(u, q_norm, kv_norm, wq, wk, wv, rope_c, rope_slo, rope_shi, *, seq, tm):
    n = u.shape[0]
    hq = MLA_HEADS * MLA_QK_PAD
    hv = MLA_HEADS * MLA_V
    ab = EV_REC
    assert seq % tm == 0 and EV_PE % LANES == 0
    nsb = seq // tm
    row_spec = pl.BlockSpec((tm, LANES), lambda i: (i % nsb, 0))
    full = lambda a: pl.BlockSpec(a.shape, lambda i: (0,) * a.ndim)
    qg = q_norm.reshape(1, -1)
    kvg = kv_norm.reshape(1, -1)
    return pl.pallas_call(
        _mla_up_kernel,
        grid=(n // tm,),
        in_specs=[pl.BlockSpec((tm, ab), lambda i: (i, 0)),
                  pl.BlockSpec((tm, LANES), lambda i: (i, EV_PE // LANES)),
                  full(qg), full(kvg), full(wq), full(wk), full(wv),
                  row_spec, row_spec, row_spec],
        out_specs=[pl.BlockSpec((tm, hq), lambda i: (i, 0)),
                   pl.BlockSpec((tm, hq), lambda i: (i, 0)),
                   pl.BlockSpec((tm, hv), lambda i: (i, 0))],
        out_shape=[jax.ShapeDtypeStruct((n, hq), BF16),
                   jax.ShapeDtypeStruct((n, hq), BF16),
                   jax.ShapeDtypeStruct((n, hv), BF16)],
        compiler_params=_cparams(("parallel",)),
    )(u, u, qg, kvg, wq, wk, wv, rope_c, rope_slo, rope_shi)


def _softmax_step(s, v, m_ref, l_ref, acc_ref):
    m_prev = m_ref[...]
    m_new = jnp.maximum(m_prev, jnp.max(s, axis=-1, keepdims=True))
    alpha = jnp.exp(m_prev - m_new)
    p = jnp.exp(s - m_new)
    l_ref[...] = alpha * l_ref[...] + jnp.sum(p, axis=-1, keepdims=True)
    acc_ref[...] = alpha * acc_ref[...] + _dot(p.astype(BF16), v)
    m_ref[...] = m_new


def _mla_attn_kernel(q_ref, k_ref, v_ref, cb_ref, o_ref, m_ref, l_ref, acc_ref, *, t, n_chunks):
    qi = pl.program_id(2)
    q = q_ref[0]
    m_ref[...] = jnp.full(m_ref.shape, NEG_INF, F32)
    l_ref[...] = jnp.zeros(l_ref.shape, F32)
    acc_ref[...] = jnp.zeros(acc_ref.shape, F32)
    for c in range(n_chunks):
        @pl.when(c < qi)
        def _():
            s = _dot_nt(q, k_ref[0, c * t:(c + 1) * t, :])
            _softmax_step(s, v_ref[0, c * t:(c + 1) * t, :], m_ref, l_ref, acc_ref)

        @pl.when(c == qi)
        def _():
            s = _dot_nt(q, k_ref[0, c * t:(c + 1) * t, :]) + cb_ref[...]
            _softmax_step(s, v_ref[0, c * t:(c + 1) * t, :], m_ref, l_ref, acc_ref)
    o_ref[0] = (acc_ref[...] / l_ref[...]).astype(o_ref.dtype)


def mla_attention(q, k, v, *, t):
    b, s, _ = q.shape
    causal = jnp.asarray(np.where(np.arange(t)[None, :] <= np.arange(t)[:, None], 0.0, NEG_INF), F32)
    kern = functools.partial(_mla_attn_kernel, t=t, n_chunks=s // t)
    return pl.pallas_call(
        kern,
        grid=(b, MLA_HEADS, s // t),
        in_specs=[pl.BlockSpec((1, t, MLA_QK_PAD), lambda b_, h, i: (b_, i, h)),
                  pl.BlockSpec((1, s, MLA_QK_PAD), lambda b_, h, i: (b_, 0, h)),
                  pl.BlockSpec((1, s, MLA_V), lambda b_, h, i: (b_, 0, h)),
                  pl.BlockSpec((t, t), lambda b_, h, i: (0, 0))],
        out_specs=pl.BlockSpec((1, t, MLA_V), lambda b_, h, i: (b_, i, h)),
        out_shape=jax.ShapeDtypeStruct((b, s, MLA_HEADS * MLA_V), BF16),
        scratch_shapes=[pltpu.VMEM((t, 1), F32), pltpu.VMEM((t, 1), F32), pltpu.VMEM((t, MLA_V), F32)],
        compiler_params=_cparams(("parallel", "parallel", "arbitrary")),
    )(q, k, v, causal)


def _rglru_kernel(x_ref, y_ref, cw_ref, cb_ref, gaw_ref, gab_ref, gxw_ref, gxb_ref, lam_ref,
                  o_ref, xbuf, h_ref, *, ts):
    t = pl.program_id(2)

    @pl.when(t == 0)
    def _():
        xbuf[0:8, :] = jnp.zeros((8, LANES), F32)
        h_ref[...] = jnp.zeros(h_ref.shape, F32)

    x = x_ref[0]
    xbuf[8:, :] = x
    cw = cw_ref[...]
    xc = cb_ref[...] + cw[3:4] * x
    for kk in range(CONV_WIDTH - 1):
        back = CONV_WIDTH - 1 - kk
        xc = xc + cw[kk:kk + 1] * xbuf[8 - back:8 - back + ts, :]
    xbuf[0:8, :] = x[ts - 8:, :]

    xcb = xc.astype(BF16)
    r = _sigmoid(_dot(xcb, gaw_ref[0]) + gab_ref[0])
    gi = _sigmoid(_dot(xcb, gxw_ref[0]) + gxb_ref[0])
    z = -lam_ref[...]
    softplus = jnp.maximum(z, 0.0) + jnp.log1p(jnp.exp(-jnp.abs(z)))
    log_a = (-LRU_C) * r * softplus
    a = jnp.exp(log_a)
    mult = jnp.sqrt(-jnp.tanh(log_a) * (a * a + 1.0))
    row = lax.broadcasted_iota(jnp.int32, (ts, 1), 0)
    mult = jnp.where(row + t * ts == 0, 1.0, mult)
    bv = mult * gi * xc

    d = 1
    while d < ts:
        keep = row >= d
        a_sh = jnp.where(keep, pltpu.roll(a, d, 0), 1.0)
        b_sh = jnp.where(keep, pltpu.roll(bv, d, 0), 0.0)
        bv = a * b_sh + bv
        a = a * a_sh
        d *= 2
    h = bv + a * h_ref[...]
    h_ref[...] = h[ts - 1:ts, :]
    o_ref[0] = (h * _gelu_tanh(y_ref[0])).astype(o_ref.dtype)


def rglru(u3, conv_w, conv_b, ga_w, ga_b, gx_w, gx_b, lam, *, ts):
    b, s, _ = u3.shape
    rec0 = EV_REC // LANES
    gate0 = EV_GATE // LANES
    cb = conv_b.reshape(1, LRU_WIDTH)
    gab = ga_b.reshape(LRU_BLOCKS, 1, LRU_BLOCK_W)
    gxb = gx_b.reshape(LRU_BLOCKS, 1, LRU_BLOCK_W)
    lam2 = lam.reshape(1, LRU_WIDTH)
    blk_w = pl.BlockSpec((1, LRU_BLOCK_W, LRU_BLOCK_W), lambda b_, n, t: (n, 0, 0))
    blk_b = pl.BlockSpec((1, 1, LRU_BLOCK_W), lambda b_, n, t: (n, 0, 0))
    vec = pl.BlockSpec((1, LANES), lambda b_, n, t: (0, n))
    return pl.pallas_call(
        functools.partial(_rglru_kernel, ts=ts),
        grid=(b, LRU_BLOCKS, s // ts),
        in_specs=[pl.BlockSpec((1, ts, LANES), lambda b_, n, t: (b_, t, rec0 + n)),
                  pl.BlockSpec((1, ts, LANES), lambda b_, n, t: (b_, t, gate0 + n)),
                  pl.BlockSpec((CONV_WIDTH, LANES), lambda b_, n, t: (0, n)),
                  vec, blk_w, blk_b, blk_w, blk_b, vec],
        out_specs=pl.BlockSpec((1, ts, LANES), lambda b_, n, t: (b_, t, n)),
        out_shape=jax.ShapeDtypeStruct((b, s, LRU_WIDTH), BF16),
        scratch_shapes=[pltpu.VMEM((ts + 8, LANES), F32), pltpu.VMEM((1, LANES), F32)],
        compiler_params=_cparams(("parallel", "parallel", "arbitrary")),
    )(u3, u3, conv_w, cb, ga_w, gab, gx_w, gxb, lam2)


def _ffn_kernel(x_ref, g_ref, w1_ref, w3_ref, w2_ref, o_ref, xn_ref):
    f = pl.program_id(1)

    @pl.when(f == 0)
    def _():
        x = x_ref[...]
        xn_ref[...] = _rms(x, g_ref[...]).astype(BF16)
        o_ref[...] = x

    xn = xn_ref[...]
    h1 = _dot(xn, w1_ref[...])
    h3 = _dot(xn, w3_ref[...])
    hh = (h1 * _sigmoid(h1) * h3).astype(BF16)
    o_ref[...] += _dot(hh, w2_ref[...])


def ffn_dense(x, g, w1, w3, w2, *, tm, tf):
    n, d = x.shape
    ff = w1.shape[1]
    assert n % tm == 0 and ff % tf == 0
    return pl.pallas_call(
        _ffn_kernel,
        grid=(n // tm, ff // tf),
        in_specs=[pl.BlockSpec((tm, d), lambda i, f: (i, 0)),
                  pl.BlockSpec((1, d), lambda i, f: (0, 0)),
                  pl.BlockSpec((d, tf), lambda i, f: (0, f)),
                  pl.BlockSpec((d, tf), lambda i, f: (0, f)),
                  pl.BlockSpec((tf, d), lambda i, f: (f, 0))],
        out_specs=pl.BlockSpec((tm, d), lambda i, f: (i, 0)),
        out_shape=jax.ShapeDtypeStruct((n, d), F32),
        scratch_shapes=[pltpu.VMEM((tm, d), BF16)],
        compiler_params=_cparams(("parallel", "arbitrary")),
    )(x, g.reshape(1, d), w1, w3, w2)


def _compress_kernel(kc_ref, pe_ref, w1_ref, b1_ref, w2_ref, o_ref, *, n_half):
    pe = pe_ref[...]
    half_k = n_half * NSA_DH
    for g in range(NSA_GROUPS):
        lo, hi = [], []
        for l in range(n_half):
            a = l * NSA_KV_DIM + g * NSA_DH
            piece = kc_ref[0, :, a:a + NSA_DH]
            lo.append((piece + pe[l:l + 1]).astype(BF16))
            hi.append((piece + pe[n_half + l:n_half + l + 1]).astype(BF16))
        z0 = _dot(jnp.concatenate(lo, axis=1), w1_ref[0:half_k, :])
        z1 = _dot(jnp.concatenate(hi, axis=1), w1_ref[half_k:2 * half_k, :])
        rows = z1.shape[0]
        pre = z0 + pltpu.roll(z1, rows - 1, 0) + b1_ref[...]
        o_ref[0, :, g * NSA_DH:(g + 1) * NSA_DH] = _dot(_gelu_tanh(pre).astype(BF16), w2_ref[...]).astype(o_ref.dtype)


def nsa_compress(kc3, pe, w1, b1, w2):
    b, nchunk, width = kc3.shape
    n_half = CMP_BLOCK // CMP_STRIDE
    assert CMP_BLOCK == 2 * CMP_STRIDE and width == CMP_STRIDE * NSA_KV_DIM
    full = lambda a: pl.BlockSpec(a.shape, lambda i: (0,) * a.ndim)
    b1r = b1.reshape(1, NSA_DH)
    return pl.pallas_call(
        functools.partial(_compress_kernel, n_half=CMP_STRIDE),
        grid=(b,),
        in_specs=[pl.BlockSpec((1, nchunk, width), lambda i: (i, 0, 0)),
                  full(pe), full(w1), full(b1r), full(w2)],
        out_specs=pl.BlockSpec((1, nchunk, NSA_KV_DIM), lambda i: (i, 0, 0)),
        out_shape=jax.ShapeDtypeStruct((b, nchunk, NSA_KV_DIM), BF16),
        compiler_params=_cparams(("parallel",)),
    )(kc3, pe, w1, b1r, w2)


def _rope128(x, c, s):
    return x * c + pltpu.roll(x, NSA_DH // 2, 1) * s


def _pack_bf16_pairs(x):
    w = x.shape[1] // 2
    bits = pltpu.bitcast(x.astype(BF16).astype(F32), jnp.uint32)
    word = bits[:, w:] | lax.shift_right_logical(bits[:, :w], jnp.uint32(16))
    return pltpu.bitcast(word, F32)


def _unpack_bf16_pairs(word):
    bits = pltpu.bitcast(word, jnp.uint32)
    lo = pltpu.bitcast(lax.shift_left(bits, jnp.uint32(16)), F32).astype(BF16)
    hi = pltpu.bitcast(bits & jnp.uint32(0xFFFF0000), F32).astype(BF16)
    return lo, hi


def _split3(x):
    hi = x.astype(BF16)
    r1 = x - hi.astype(F32)
    mid = r1.astype(BF16)
    lo = (r1 - mid.astype(F32)).astype(BF16)
    return hi, mid, lo


def _nsa_attn_kernel(q_ref, ks_ref, vs_ref, kw_ref, vw_ref, gt_ref, kc_ref, vc_ref,
                     cq_ref, sq_ref, ck_ref, sk_ref, ovl_ref, exp_ref, wb_ref,
                     o_ref, ksr, vsb, kwr, vwb, selb, m_ref, l_ref, acc_ref,
                     *, tq, tk, seq, scale):
    qi = pl.program_id(2)
    hp = NSA_HPG
    n_sel = seq // SEL_BLOCK

    @pl.when(qi == 0)
    def _():
        ck, sk = ck_ref[...], sk_ref[...]
        ksr[...] = _rope128(ks_ref[0], ck, sk).astype(BF16)
        kwr[...] = _rope128(kw_ref[0], ck, sk).astype(BF16)
        vsb[...] = vs_ref[0].astype(BF16)
        vwb[...] = vw_ref[0].astype(BF16)

    q = q_ref[0] * scale
    cq, sq = cq_ref[...], sq_ref[...]
    heads = [q[:, p * NSA_DH:(p + 1) * NSA_DH] for p in range(hp)]
    qu = jnp.concatenate(heads, axis=0).astype(BF16)
    qr = jnp.concatenate([_rope128(h, cq, sq) for h in heads], axis=0).astype(BF16)
    t_row = qi * tq + lax.broadcasted_iota(jnp.int32, (tq, 1), 0)
    lane = lax.broadcasted_iota(jnp.int32, (1, LANES), 1)

    sc = _dot_nt(qu, kc_ref[0]).reshape(hp, tq, LANES)
    valid = (lane * CMP_STRIDE + (CMP_BLOCK - 1) <= t_row)[None]
    sm = jnp.where(valid, sc, NEG_INF)
    e = jnp.exp(sm - jnp.max(sm, axis=-1, keepdims=True))
    p = jnp.where(valid, e / jnp.sum(e, axis=-1, keepdims=True), 0.0)
    o_cmp = _dot(p.reshape(hp * tq, LANES).astype(BF16), vc_ref[0])

    psum = p[0]
    for i in range(1, hp):
        psum = psum + p[i]
    ovl = ovl_ref[...]
    imp = sum(_dot(part, ovl) for part in _split3(psum))
    cur = jnp.right_shift(t_row, SEL_BLOCK.bit_length() - 1)
    future = lane > cur
    forced = (lane == 0) | (lane == cur) | (lane == cur - 1)
    score = jnp.where(future, -1.0, jnp.where(forced, SEL_FORCE, imp))
    sc_t = score.T[0:n_sel, :]
    blk = lax.broadcasted_iota(jnp.int32, (n_sel, 1), 0)
    cnt = jnp.zeros((n_sel, tq), F32)
    for j in range(n_sel):
        other = sc_t[j:j + 1, :]
        beats = (other > sc_t) | ((other == sc_t) & (blk > j))
        cnt = cnt + jnp.where(beats, 1.0, 0.0)
    sel_t = jnp.where(cnt < SEL_COUNT, 1.0, 0.0)
    sel = jnp.concatenate([sel_t, jnp.zeros((LANES - n_sel, tq), F32)], axis=0).T.astype(BF16)
    picked = _dot(sel, exp_ref[...])
    kall = lax.broadcasted_iota(jnp.int32, (1, seq), 1)
    selb[...] = jnp.where((picked > 0.5) & (kall <= t_row), 0.0, NEG_INF)

    m_ref[...] = jnp.full(m_ref.shape, NEG_INF, F32)
    l_ref[...] = jnp.zeros(l_ref.shape, F32)
    acc_ref[...] = jnp.zeros(acc_ref.shape, F32)
    for c in range(seq // tk):
        @pl.when(c * tk <= qi * tq + (tq - 1))
        def _():
            s = _dot_nt(qr, ksr[c * tk:(c + 1) * tk, :]).reshape(hp, tq, tk)
            s = (s + selb[:, c * tk:(c + 1) * tk][None]).reshape(hp * tq, tk)
            _softmax_step(s, vsb[c * tk:(c + 1) * tk, :], m_ref, l_ref, acc_ref)
    o_sel = acc_ref[...] / l_ref[...]

    span = WINDOW + tq
    start = pl.multiple_of(jnp.maximum(qi * tq - WINDOW, 0), tq)
    sw = _dot_nt(qr, kwr[pl.ds(start, span), :]).reshape(hp, tq, span) + wb_ref[0][None]
    ew = jnp.exp(sw - jnp.max(sw, axis=-1, keepdims=True))
    den = jnp.sum(ew, axis=-1, keepdims=True).reshape(hp * tq, 1)
    o_win = _dot(ew.reshape(hp * tq, span).astype(BF16), vwb[pl.ds(start, span), :]) / den

    gates = _sigmoid(gt_ref[0])
    for i in range(hp):
        rows = slice(i * tq, (i + 1) * tq)
        o = (gates[:, 3 * i:3 * i + 1] * o_cmp[rows] + gates[:, 3 * i + 1:3 * i + 2] * o_sel[rows]
             + gates[:, 3 * i + 2:3 * i + 3] * o_win[rows])
        o_ref[0, :, i * NSA_DH:(i + 1) * NSA_DH] = o.astype(o_ref.dtype)


def _window_bias(tq):
    span = WINDOW + tq
    out = []
    for qi in range(WINDOW // tq + 1):
        start = max(qi * tq - WINDOW, 0)
        t = qi * tq + np.arange(tq)[:, None]
        kpos = start + np.arange(span)[None, :]
        out.append(np.where((kpos <= t) & (kpos > t - WINDOW), 0.0, NEG_INF))
    return jnp.asarray(np.stack(out), F32)


def nsa_attention(u3, k_cmp, v_cmp, rope_c, rope_s, ovl, expand, *, tq, tk):
    b, s, _ = u3.shape
    assert s // CMP_STRIDE == LANES and tq % SEL_BLOCK == 0 and WINDOW % tq == 0
    hp = NSA_HPG
    nwb = WINDOW // tq
    wbias = _window_bias(tq)
    col = lambda off: (lambda b_, g, i: (b_, 0, off // NSA_DH + g))
    seq_blk = lambda off: pl.BlockSpec((1, s, NSA_DH), col(off))
    full = lambda a: pl.BlockSpec(a.shape, lambda b_, g, i: (0,) * a.ndim)
    cmp_blk = pl.BlockSpec((1, LANES, NSA_DH), lambda b_, g, i: (b_, 0, g))
    rope_q = pl.BlockSpec((tq, NSA_DH), lambda b_, g, i: (i, 0))
    kern = functools.partial(_nsa_attn_kernel, tq=tq, tk=tk, seq=s, scale=NSA_DH ** -0.5)
    return pl.pallas_call(
        kern,
        grid=(b, NSA_GROUPS, s // tq),
        in_specs=[pl.BlockSpec((1, tq, hp * NSA_DH), lambda b_, g, i: (b_, i, g)),
                  seq_blk(OD_KS), seq_blk(OD_VS), seq_blk(OD_KW), seq_blk(OD_VW),
                  pl.BlockSpec((1, tq, LANES), lambda b_, g, i: (b_, i, OD_G // LANES + g)),
                  cmp_blk, cmp_blk, rope_q, rope_q, full(rope_c), full(rope_s), full(ovl), full(expand),
                  pl.BlockSpec((1, tq, WINDOW + tq), lambda b_, g, i: (jnp.minimum(i, nwb), 0, 0))],
        out_specs=pl.BlockSpec((1, tq, hp * NSA_DH), lambda b_, g, i: (b_, i, g)),
        out_shape=jax.ShapeDtypeStruct((b, s, NSA_Q_DIM), BF16),
        scratch_shapes=[pltpu.VMEM((s, NSA_DH), BF16)] * 4 + [
            pltpu.VMEM((tq, s), F32),
            pltpu.VMEM((hp * tq, 1), F32), pltpu.VMEM((hp * tq, 1), F32), pltpu.VMEM((hp * tq, NSA_DH), F32)],
        compiler_params=_cparams(("parallel", "parallel", "arbitrary")),
    )(u3, u3, u3, u3, u3, u3, k_cmp, v_cmp, rope_c, rope_s, rope_c, rope_s, ovl, expand, wbias)


def _router_kernel(x_ref, g_ref, wr_ref, br_ref, xn_ref, meta_ref, cnt_ref, *, tm):
    @pl.when(pl.program_id(0) == 0)
    def _():
        cnt_ref[...] = jnp.zeros(cnt_ref.shape, F32)

    xn = _rms(x_ref[...], g_ref[...])
    xn_ref[...] = _pack_bf16_pairs(xn)
    xh, xm, _ = _split3(xn)
    wh, wm, _ = _split3(wr_ref[...])
    logits = _dot(xh, wh) + _dot(xh, wm) + _dot(xm, wh) + br_ref[...]
    lane = lax.broadcasted_iota(jnp.int32, (1, LANES), 1).astype(F32)
    lg = jnp.where(lane < N_EXPERTS, logits, NEG_INF)
    m1 = jnp.max(lg, axis=-1, keepdims=True)
    e1 = jnp.min(jnp.where(lg == m1, lane, float(LANES)), axis=-1, keepdims=True)
    lg2 = jnp.where(lane == e1, NEG_INF, lg)
    m2 = jnp.max(lg2, axis=-1, keepdims=True)
    e2 = jnp.min(jnp.where(lg2 == m2, lane, float(LANES)), axis=-1, keepdims=True)
    ex = jnp.exp(m2 - m1)
    den = 1.0 + ex
    g1 = 1.0 / den
    g2 = ex / den
    oh = jnp.where((lane == e1) | (lane == e2), 1.0, 0.0)
    r = lax.broadcasted_iota(jnp.int32, (tm, tm), 0)
    c = lax.broadcasted_iota(jnp.int32, (tm, tm), 1)
    tri = jnp.where(r > c, 1.0, 0.0).astype(BF16)
    cum = _dot(tri, oh.astype(BF16)) + cnt_ref[0:1, :]
    pos1 = jnp.sum(jnp.where(lane == e1, cum, 0.0), axis=-1, keepdims=True)
    pos2 = jnp.sum(jnp.where(lane == e2, cum, 0.0), axis=-1, keepdims=True)
    cnt_ref[...] = cnt_ref[...] + jnp.sum(oh, axis=0, keepdims=True)
    meta = jnp.where(lane == 0, e1, 0.0)
    meta = jnp.where(lane == 1, e2, meta)
    meta = jnp.where(lane == 2, g1, meta)
    meta = jnp.where(lane == 3, g2, meta)
    meta = jnp.where(lane == 4, pos1, meta)
    meta = jnp.where(lane == 5, pos2, meta)
    meta_ref[...] = meta


def moe_router(x, g, wr, br, *, tm):
    n, d = x.shape
    return pl.pallas_call(
        functools.partial(_router_kernel, tm=tm),
        grid=(n // tm,),
        in_specs=[pl.BlockSpec((tm, d), lambda i: (i, 0)),
                  pl.BlockSpec((1, d), lambda i: (0, 0)),
                  pl.BlockSpec((d, LANES), lambda i: (0, 0)),
                  pl.BlockSpec((1, LANES), lambda i: (0, 0))],
        out_specs=[pl.BlockSpec((tm, d // 2), lambda i: (i, 0)),
                   pl.BlockSpec((tm, LANES), lambda i: (i, 0)),
                   pl.BlockSpec((8, LANES), lambda i: (0, 0))],
        out_shape=[jax.ShapeDtypeStruct((n, d // 2), F32),
                   jax.ShapeDtypeStruct((n, LANES), F32),
                   jax.ShapeDtypeStruct((8, LANES), F32)],
        compiler_params=_cparams(("arbitrary",)),
    )(x, g.reshape(1, d), wr, br)


def _gather_chunk(per_worker, row_bytes):
    best = 0
    for c in range(8, per_worker + 1, 8):
        if per_worker % c == 0 and c * row_bytes <= SC_GATHER_BYTES and c <= LANES:
            best = c
    assert best > 0, (per_worker, row_bytes)
    return best


def sc_gather_rows(table, idx):
    _, d = table.shape
    b = idx.shape[0]
    assert b % (8 * SC_WORKERS) == 0 and table.dtype.itemsize == 4
    per_w = b // SC_WORKERS
    chunk = _gather_chunk(per_w, d * 4)
    mesh = plsc.VectorSubcoreMesh(core_axis_name="c", subcore_axis_name="s",
                                  num_cores=SC_CORES, num_subcores=SC_SUBCORES)

    @functools.partial(
        pl.kernel, mesh=mesh,
        out_type=jax.ShapeDtypeStruct((b, d), table.dtype),
        scratch_types=[pltpu.VMEM((chunk,), jnp.int32), pltpu.VMEM((chunk, d), table.dtype),
                       pltpu.SemaphoreType.DMA])
    def gather(table_hbm, idx_hbm, out_hbm, idx_v, rows_v, sem):
        wid = lax.axis_index("s") * SC_CORES + lax.axis_index("c")
        base = wid * per_w

        @pl.loop(0, per_w // chunk)
        def _(c):
            off = pl.multiple_of(base + c * chunk, 8)
            pltpu.sync_copy(idx_hbm.at[pl.ds(off, chunk)], idx_v)
            pltpu.async_copy(table_hbm.at[idx_v], rows_v, sem).wait()
            pltpu.sync_copy(rows_v, out_hbm.at[pl.ds(off, chunk)])

    return gather(table, idx)


def _expert_kernel(te_ref, ts_ref, tr_ref, xp_hbm, w1_ref, w3_ref, w2_ref, yr_hbm,
                   xb, acc, stage, in_sem, out_sem, *, nf):
    t = pl.program_id(0)
    f = pl.program_id(1)
    rows = tr_ref[t]
    start = pl.multiple_of(ts_ref[t], MOE_ALIGN)
    n_chunks = MOE_TILE // MOE_COPY
    half = xb.shape[1] // 2

    def in_copy(ci):
        return pltpu.make_async_copy(xp_hbm.at[pl.ds(start + ci * MOE_COPY, MOE_COPY)],
                                     stage.at[ci % 2], in_sem.at[ci % 2])

    def out_copy(ci):
        return pltpu.make_async_copy(acc.at[pl.ds(ci * MOE_COPY, MOE_COPY)],
                                     yr_hbm.at[pl.ds(start + ci * MOE_COPY, MOE_COPY)], out_sem.at[0])

    def when_chunk_live(ci, fn):
        pl.when(ci * MOE_COPY < rows)(fn)

    @pl.when((f == 0) & (rows > 0))
    def _():
        def unpack(ci):
            in_copy(ci).wait()
            lo, hi = _unpack_bf16_pairs(stage[ci % 2])
            xb[ci * MOE_COPY:(ci + 1) * MOE_COPY, :half] = lo
            xb[ci * MOE_COPY:(ci + 1) * MOE_COPY, half:] = hi

        when_chunk_live(0, lambda: in_copy(0).start())
        for ci in range(n_chunks):
            if ci + 1 < n_chunks:
                when_chunk_live(ci + 1, lambda ci=ci: in_copy(ci + 1).start())
            when_chunk_live(ci, lambda ci=ci: unpack(ci))
        acc[...] = jnp.zeros(acc.shape, F32)

    @pl.when(rows > 0)
    def _():
        w1 = w1_ref[0, 0].astype(BF16)
        w3 = w3_ref[0, 0].astype(BF16)
        w2 = w2_ref[0, 0].astype(BF16)

        def chain(r0, size):
            rs = pl.ds(pl.multiple_of(r0, MOE_PIECES[-1]), size)
            xc = xb[rs, :]
            h1 = _dot(xc, w1)
            h3 = _dot(xc, w3)
            hh = (h1 * _sigmoid(h1) * h3).astype(BF16)
            acc[rs, :] += _dot(hh, w2)

        todo = (rows + (MOE_PIECES[-1] - 1)) // MOE_PIECES[-1] * MOE_PIECES[-1]
        off = jnp.int32(0)
        for size in MOE_PIECES:
            take = todo - off >= size

            @pl.when(take)
            def _(off=off, size=size):
                for c0 in range(0, size, MOE_CHAIN):
                    chain(off + c0, min(MOE_CHAIN, size))

            off = off + jnp.where(take, size, 0)

    @pl.when((f == nf - 1) & (rows > 0))
    def _():
        for ci in range(n_chunks):
            when_chunk_live(ci, lambda ci=ci: out_copy(ci).start())
        for ci in range(n_chunks):
            when_chunk_live(ci, lambda ci=ci: out_copy(ci).wait())


def moe_experts(xp, tile_e, tile_start, tile_rows, w1, w3, w2, layer, *, tf):
    n_rows, half = xp.shape
    d = 2 * half
    n_tiles = tile_e.shape[0]
    ff = w1.shape[3]
    nf = ff // tf
    assert MOE_TILE % MOE_COPY == 0 and sum(MOE_PIECES) >= MOE_TILE

    def f_eff(t, f, tr):
        return jnp.where(tr[t] > 0, f, nf - 1)

    grid_spec = pltpu.PrefetchScalarGridSpec(
        num_scalar_prefetch=3,
        grid=(n_tiles, nf),
        in_specs=[pl.BlockSpec(memory_space=pl.ANY),
                  pl.BlockSpec((1, 1, d, tf), lambda t, f, te, ts, tr: (layer, te[t], 0, f_eff(t, f, tr))),
                  pl.BlockSpec((1, 1, d, tf), lambda t, f, te, ts, tr: (layer, te[t], 0, f_eff(t, f, tr))),
                  pl.BlockSpec((1, 1, tf, d), lambda t, f, te, ts, tr: (layer, te[t], f_eff(t, f, tr), 0))],
        out_specs=pl.BlockSpec(memory_space=pl.ANY),
        scratch_shapes=[pltpu.VMEM((MOE_TILE, d), BF16), pltpu.VMEM((MOE_TILE, d), F32),
                        pltpu.VMEM((2, MOE_COPY, half), F32),
                        pltpu.SemaphoreType.DMA((2,)), pltpu.SemaphoreType.DMA((1,))],
    )
    return pl.pallas_call(
        functools.partial(_expert_kernel, nf=nf),
        grid_spec=grid_spec,
        out_shape=jax.ShapeDtypeStruct((n_rows, d), F32),
        compiler_params=_cparams(("arbitrary", "arbitrary")),
    )(tile_e, tile_start, tile_rows, xp, w1, w3, w2)


def _combine_kernel(x_ref, y1_ref, y2_ref, meta_ref, g_ref, o_ref, *, final_norm):
    meta = meta_ref[...]
    y = x_ref[...] + (meta[:, 2:3] * y1_ref[...] + meta[:, 3:4] * y2_ref[...])
    if final_norm:
        y = _rms(y, g_ref[...])
    o_ref[...] = y


def moe_combine(x, yg, meta, g, *, final_norm, tm):
    n, d = x.shape
    nb = n // tm
    row = pl.BlockSpec((tm, d), lambda i: (i, 0))
    return pl.pallas_call(
        functools.partial(_combine_kernel, final_norm=final_norm),
        grid=(nb,),
        in_specs=[row, row, pl.BlockSpec((tm, d), lambda i: (i + nb, 0)),
                  pl.BlockSpec((tm, LANES), lambda i: (i, 0)),
                  pl.BlockSpec((1, d), lambda i: (0, 0))],
        out_specs=row,
        out_shape=jax.ShapeDtypeStruct((n, d), F32),
        compiler_params=_cparams(("parallel",)),
    )(x, yg, yg, meta, g.reshape(1, d))


def _mla_rope_tables(seq):
    half = MLA_ROPE // 2
    pos = jnp.arange(seq, dtype=F32)
    inv = ROPE_THETA ** (-jnp.arange(0, MLA_ROPE, 2, dtype=F32) / MLA_ROPE)
    ang = pos[:, None] * inv[None, :]
    cos, sin = jnp.cos(ang), jnp.sin(ang)
    z = jnp.zeros((seq, half), F32)
    pad = jnp.zeros((seq, LANES - MLA_ROPE), F32)
    c = jnp.concatenate([cos, cos, pad], axis=1)
    s_lo = jnp.concatenate([-sin, z, pad], axis=1)
    s_hi = jnp.concatenate([z, sin, pad], axis=1)
    return c, s_lo, s_hi


def _nsa_rope_tables(seq):
    pos = jnp.arange(seq, dtype=F32)
    inv = ROPE_THETA ** (-jnp.arange(0, NSA_DH, 2, dtype=F32) / NSA_DH)
    ang = pos[:, None] * inv[None, :]
    cos, sin = jnp.cos(ang), jnp.sin(ang)
    return jnp.concatenate([cos, cos], axis=1), jnp.concatenate([-sin, sin], axis=1)


def _selection_constants(seq):
    n_sel = seq // SEL_BLOCK
    nc = (seq - CMP_BLOCK) // CMP_STRIDE + 1
    cmp_start = np.arange(LANES) * CMP_STRIDE
    sel_start = np.arange(LANES) * SEL_BLOCK
    ovl = ((cmp_start[:, None] < sel_start[None, :] + SEL_BLOCK) &
           (cmp_start[:, None] + CMP_BLOCK > sel_start[None, :]))
    ovl &= (np.arange(LANES)[:, None] < nc) & (np.arange(LANES)[None, :] < n_sel)
    expand = (np.arange(seq)[None, :] // SEL_BLOCK == np.arange(LANES)[:, None])
    return jnp.asarray(ovl, BF16), jnp.asarray(expand, BF16)


def even_layer(x, seq, p):
    n = x.shape[0]
    b = n // seq
    (norm_mix, w_in, q_norm, w_q_up, kv_norm, w_kv_up, conv_w, conv_b, ga_w, ga_b, gx_w, gx_b,
     lam, w_out, norm_ffn, w1, w3, w2) = p
    d = D_MODEL
    o1 = MLA_Q_LORA + MLA_KV_LORA
    o2 = o1 + MLA_ROPE
    w_pack = jnp.concatenate(
        [w_in[:, :o1], w_in[:, o2:], w_in[:, o1:o2], jnp.zeros((d, LANES - MLA_ROPE), F32)], axis=1).astype(BF16)
    u = norm_matmul(x, norm_mix, w_pack, tm=512, tn=EV_PACKED // 3)

    wq = w_q_up.reshape(MLA_Q_LORA, MLA_HEADS, MLA_NOPE + MLA_ROPE)
    wq = jnp.pad(wq, ((0, 0), (0, 0), (0, MLA_QK_PAD - MLA_NOPE - MLA_ROPE)))
    wq = wq.reshape(MLA_Q_LORA, MLA_HEADS * MLA_QK_PAD).astype(BF16)
    wkv = w_kv_up.reshape(MLA_KV_LORA, MLA_HEADS, MLA_NOPE + MLA_V)
    wk = wkv[:, :, :MLA_NOPE].reshape(MLA_KV_LORA, MLA_HEADS * MLA_NOPE).astype(BF16)
    wv = wkv[:, :, MLA_NOPE:].reshape(MLA_KV_LORA, MLA_HEADS * MLA_V).astype(BF16)
    rc, rlo, rhi = _mla_rope_tables(seq)
    q, k, v = mla_up(u, q_norm, kv_norm, wq, wk, wv, rc, rlo, rhi, seq=seq, tm=512)
    o_mla = mla_attention(q.reshape(b, seq, -1), k.reshape(b, seq, -1), v.reshape(b, seq, -1), t=512)

    o_rec = rglru(u.reshape(b, seq, EV_PACKED), conv_w, conv_b, ga_w.astype(BF16), ga_b,
                  gx_w.astype(BF16), gx_b, lam, ts=512)
    x = matmul_residual([o_mla.reshape(n, -1), o_rec.reshape(n, -1)], w_out.astype(BF16), x, tm=512, tn=1024)
    return ffn_dense(x, norm_ffn, w1.astype(BF16), w3.astype(BF16), w2.astype(BF16), tm=512, tf=512)


def _moe_dispatch(meta, counts, n):
    e1 = meta[:, 0].astype(jnp.int32)
    e2 = meta[:, 1].astype(jnp.int32)
    pos1 = meta[:, 4].astype(jnp.int32)
    pos2 = meta[:, 5].astype(jnp.int32)
    cnt = counts[0, :N_EXPERTS].astype(jnp.int32)
    span = (cnt + MOE_ALIGN - 1) // MOE_ALIGN * MOE_ALIGN
    row0 = jnp.cumsum(span) - span
    row_unit = 64 * SC_WORKERS
    n_rows = -(-(n * TOP_K + N_EXPERTS * MOE_ALIGN + MOE_TILE) // row_unit) * row_unit
    d1 = row0[e1] + pos1
    d2 = row0[e2] + pos2
    n_tiles = (n * TOP_K) // MOE_TILE + N_EXPERTS
    tiles_e = (cnt + MOE_TILE - 1) // MOE_TILE
    tend = jnp.cumsum(tiles_e)
    tbeg = tend - tiles_e
    tid = jnp.arange(n_tiles, dtype=jnp.int32)
    te = jnp.minimum(jnp.searchsorted(tend, tid, side='right'), N_EXPERTS - 1).astype(jnp.int32)
    used = tid < tend[-1]
    first = (tid - tbeg[te]) * MOE_TILE
    rows = jnp.where(used, jnp.clip(cnt[te] - first, 0, MOE_TILE), 0).astype(jnp.int32)
    tstart = jnp.where(used, row0[te] + first, 0).astype(jnp.int32)
    last_e = te[jnp.maximum(tend[-1] - 1, 0)]
    te = jnp.where(used, te, last_e).astype(jnp.int32)
    tok = jnp.arange(n, dtype=jnp.int32)
    row_tok = (jnp.arange(n_rows, dtype=jnp.int32) % n).at[d1].set(tok).at[d2].set(tok)
    return d1, d2, row_tok, te, tstart, rows


def odd_layer(x, seq, p, experts, final_g):
    n = x.shape[0]
    b = n // seq
    (norm_mix, w_in, ck_pe, ck_w1, ck_b1, ck_w2, cv_pe, cv_w1, cv_b1, cv_w2, w_out, norm_ffn,
     router_w, router_b) = p
    ew1, ew3, ew2, layer = experts
    d = D_MODEL
    wg = w_in[:, OD_G:].reshape(d, NSA_GROUPS, NSA_HPG * 3)
    wg = jnp.pad(wg, ((0, 0), (0, 0), (0, LANES - NSA_HPG * 3))).reshape(d, NSA_GROUPS * LANES)
    w_pack = jnp.concatenate([w_in[:, :OD_G], wg], axis=1).astype(BF16)
    u = norm_matmul(x, norm_mix, w_pack, tm=512, tn=512)
    u3 = u.reshape(b, seq, OD_PACKED)

    nchunk = seq // CMP_STRIDE
    kc3 = u3[:, :, OD_KC:OD_KC + NSA_KV_DIM].reshape(b, nchunk, CMP_STRIDE * NSA_KV_DIM)
    vc3 = u3[:, :, OD_VC:OD_VC + NSA_KV_DIM].reshape(b, nchunk, CMP_STRIDE * NSA_KV_DIM)
    k_cmp = nsa_compress(kc3, ck_pe, ck_w1.astype(BF16), ck_b1, ck_w2.astype(BF16))
    v_cmp = nsa_compress(vc3, cv_pe, cv_w1.astype(BF16), cv_b1, cv_w2.astype(BF16))
    rc, rs = _nsa_rope_tables(seq)
    ovl, expand = _selection_constants(seq)
    o = nsa_attention(u3, k_cmp, v_cmp, rc, rs, ovl, expand, tq=256, tk=512)
    x = matmul_residual([o.reshape(n, -1)], w_out.astype(BF16), x, tm=512, tn=1024)

    wr = jnp.pad(router_w, ((0, 0), (0, LANES - N_EXPERTS)))
    br = jnp.pad(router_b, (0, LANES - N_EXPERTS)).reshape(1, LANES)
    xp, meta, counts = moe_router(x, norm_ffn, wr, br, tm=512)
    d1, d2, row_tok, te, tstart, rows = _moe_dispatch(meta, counts, n)
    yr = moe_experts(sc_gather_rows(xp, row_tok), te, tstart, rows, ew1, ew3, ew2, layer, tf=256)
    yg = sc_gather_rows(yr, jnp.concatenate([d1, d2]))
    g = final_g if final_g is not None else norm_ffn
    return moe_combine(x, yg, meta, g, final_norm=final_g is not None, tm=512)


def kernel(x, ev_norm_mix, ev_w_in, ev_q_norm, ev_w_q_up, ev_kv_norm, ev_w_kv_up, ev_conv_w, ev_conv_b, ev_gate_a_w, ev_gate_a_b, ev_gate_x_w, ev_gate_x_b, ev_lru_lambda, ev_w_out, ev_norm_ffn, ev_ffn_w1, ev_ffn_w3, ev_ffn_w2, od_norm_mix, od_w_in, od_cmp_k_pe, od_cmp_k_w1, od_cmp_k_b1, od_cmp_k_w2, od_cmp_v_pe, od_cmp_v_w1, od_cmp_v_b1, od_cmp_v_w2, od_w_out, od_norm_ffn, od_router_w, od_router_b, od_exp_w1, od_exp_w3, od_exp_w2, final_norm):
    bsz, seq, d = x.shape
    ev = (ev_norm_mix, ev_w_in, ev_q_norm, ev_w_q_up, ev_kv_norm, ev_w_kv_up, ev_conv_w, ev_conv_b,
          ev_gate_a_w, ev_gate_a_b, ev_gate_x_w, ev_gate_x_b, ev_lru_lambda, ev_w_out, ev_norm_ffn,
          ev_ffn_w1, ev_ffn_w3, ev_ffn_w2)
    od = (od_norm_mix, od_w_in, od_cmp_k_pe, od_cmp_k_w1, od_cmp_k_b1, od_cmp_k_w2, od_cmp_v_pe,
          od_cmp_v_w1, od_cmp_v_b1, od_cmp_v_w2, od_w_out, od_norm_ffn, od_router_w, od_router_b)
    h = x.reshape(bsz * seq, d)
    for layer in range(DEPTH):
        i = layer // 2
        if layer % 2 == 0:
            h = even_layer(h, seq, tuple(a[i] for a in ev))
        else:
            h = odd_layer(h, seq, tuple(a[i] for a in od), (od_exp_w1, od_exp_w3, od_exp_w2, i),
                          final_norm if layer == DEPTH - 1 else None)
    return h.reshape(bsz, seq, d)
```

```python
import functools
import math

import numpy as np
import jax
import jax.numpy as jnp
from jax import lax
from jax.experimental import pallas as pl
from jax.experimental.pallas import tpu as pltpu
from jax.experimental.pallas import tpu_sc as plsc

F32 = jnp.float32
BF16 = jnp.bfloat16

D_MODEL = 2048
DEPTH = 4
RMS_EPS = 1e-6
ROPE_THETA = 10000.0
NEG_INF = -1e30

MLA_HEADS = 8
MLA_Q_LORA = 768
MLA_KV_LORA = 512
MLA_NOPE = 128
MLA_ROPE = 64
MLA_V = 128
MLA_QK_PAD = 256

LRU_WIDTH = D_MODEL // 2
LRU_BLOCKS = 8
LRU_BLOCK_W = LRU_WIDTH // LRU_BLOCKS
LRU_C = 8.0
CONV_WIDTH = 4

NSA_HEADS = 16
NSA_GROUPS = 4
NSA_HPG = NSA_HEADS // NSA_GROUPS
NSA_DH = D_MODEL // NSA_HEADS
CMP_BLOCK = 32
CMP_STRIDE = 16
SEL_BLOCK = 64
SEL_COUNT = 16
SEL_FORCE = 1e4
WINDOW = 512
NSA_Q_DIM = NSA_HEADS * NSA_DH
NSA_KV_DIM = NSA_GROUPS * NSA_DH

DENSE_FF = 5632
N_EXPERTS = 8
TOP_K = 2
EXPERT_FF = 7168

LANES = 128
SC_CORES = 2
SC_SUBCORES = 16
SC_WORKERS = SC_CORES * SC_SUBCORES
SC_GATHER_BYTES = 256 * 1024
VMEM_LIMIT = 56 * 1024 * 1024

EV_CQ = 0
EV_CKV = MLA_Q_LORA
EV_REC = MLA_Q_LORA + MLA_KV_LORA
EV_GATE = EV_REC + LRU_WIDTH
EV_PE = EV_GATE + LRU_WIDTH
EV_PACKED = EV_PE + LANES

OD_Q = 0
OD_KC = NSA_Q_DIM
OD_VC = OD_KC + NSA_KV_DIM
OD_KS = OD_VC + NSA_KV_DIM
OD_VS = OD_KS + NSA_KV_DIM
OD_KW = OD_VS + NSA_KV_DIM
OD_VW = OD_KW + NSA_KV_DIM
OD_G = OD_VW + NSA_KV_DIM
OD_PACKED = OD_G + NSA_GROUPS * LANES

MOE_TILE = 2304
MOE_ALIGN = 16
MOE_COPY = 256
MOE_PIECES = (1024, 1024, 512, 256, 128)
MOE_CHAIN = 512
MOE_WSPLIT = 4


def _cparams(sem):
    return pltpu.CompilerParams(dimension_semantics=sem, vmem_limit_bytes=VMEM_LIMIT)


def _rms(x, g):
    ms = jnp.mean(x * x, axis=-1, keepdims=True)
    return x * lax.rsqrt(ms + RMS_EPS) * g


def _sigmoid(x):
    return 1.0 / (1.0 + jnp.exp(-x))


def _gelu_tanh(x):
    return 0.5 * x * (1.0 + jnp.tanh(math.sqrt(2.0 / math.pi) * (x + 0.044715 * (x * x * x))))


def _dot(a, b):
    return jnp.dot(a, b, preferred_element_type=F32)


def _dot_nt(a, b):
    return lax.dot_general(a, b, (((1,), (1,)), ((), ())), preferred_element_type=F32)


def _norm_mm_kernel(x_ref, g_ref, w_ref, o_ref, xn_ref):
    @pl.when(pl.program_id(1) == 0)
    def _():
        xn_ref[...] = _rms(x_ref[...], g_ref[...]).astype(BF16)

    o_ref[...] = _dot(xn_ref[...], w_ref[...]).astype(o_ref.dtype)


def norm_matmul(x, g, w, *, tm, tn, out_dtype=F32):
    n, k = x.shape
    m = w.shape[1]
    assert n % tm == 0 and m % tn == 0
    return pl.pallas_call(
        _norm_mm_kernel,
        grid=(n // tm, m // tn),
        in_specs=[pl.BlockSpec((tm, k), lambda i, j: (i, 0)),
                  pl.BlockSpec((1, k), lambda i, j: (0, 0)),
                  pl.BlockSpec((k, tn), lambda i, j: (0, j))],
        out_specs=pl.BlockSpec((tm, tn), lambda i, j: (i, j)),
        out_shape=jax.ShapeDtypeStruct((n, m), out_dtype),
        scratch_shapes=[pltpu.VMEM((tm, k), BF16)],
        compiler_params=_cparams(("parallel", "arbitrary")),
    )(x, g.reshape(1, k), w)


def _mm_res_kernel(*refs, n_in):
    xs = refs[:n_in]
    ws = refs[n_in:2 * n_in]
    res_ref = refs[2 * n_in]
    o_ref = refs[2 * n_in + 1]
    acc = res_ref[...]
    for x_ref, w_ref in zip(xs, ws):
        acc = acc + _dot(x_ref[...], w_ref[...])
    o_ref[...] = acc


def matmul_residual(xs, w, res, *, tm, tn):
    n = res.shape[0]
    m = w.shape[1]
    n_in = len(xs)
    in_specs = [pl.BlockSpec((tm, x.shape[1]), lambda i, j: (i, 0)) for x in xs]
    row = 0
    for x in xs:
        kx = x.shape[1]
        assert row % kx == 0
        in_specs.append(pl.BlockSpec((kx, tn), lambda i, j, rb=row // kx: (rb, j)))
        row += kx
    assert row == w.shape[0]
    in_specs.append(pl.BlockSpec((tm, tn), lambda i, j: (i, j)))
    return pl.pallas_call(
        functools.partial(_mm_res_kernel, n_in=n_in),
        grid=(n // tm, m // tn),
        in_specs=in_specs,
        out_specs=pl.BlockSpec((tm, tn), lambda i, j: (i, j)),
        out_shape=jax.ShapeDtypeStruct((n, m), F32),
        compiler_params=_cparams(("parallel", "arbitrary")),
    )(*xs, *([w] * n_in), res)


def _rope64(x, c, s_lo, s_hi):
    return x * c + pltpu.roll(x, 96, 1) * s_lo + pltpu.roll(x, 32, 1) * s_hi


def _mla_up_kernel(u_ref, pe_ref, qg_ref, kvg_ref, wq_ref, wk_ref, wv_ref,
                   c_ref, slo_ref, shi_ref, q_ref, k_ref, v_ref):
    u = u_ref[...]
    c, s_lo, s_hi = c_ref[...], slo_ref[...], shi_ref[...]
    qn = _rms(u[:, EV_CQ:EV_CQ + MLA_Q_LORA], qg_ref[...]).astype(BF16)
    kvn = _rms(u[:, EV_CKV:EV_CKV + MLA_KV_LORA], kvg_ref[...]).astype(BF16)
    q = _dot(qn, wq_ref[...]) * ((MLA_NOPE + MLA_ROPE) ** -0.5)
    kn = _dot(kvn, wk_ref[...])
    v_ref[...] = _dot(kvn, wv_ref[...]).astype(v_ref.dtype)
    kpe = _rope64(pe_ref[...], c, s_lo, s_hi).astype(k_ref.dtype)
    for h in range(MLA_HEADS):
        a = h * MLA_QK_PAD
        q_ref[:, a:a + LANES] = q[:, a:a + LANES].astype(q_ref.dtype)
        q_ref[:, a + LANES:a + 2 * LANES] = _rope64(q[:, a + LANES:a + 2 * LANES], c, s_lo, s_hi).astype(q_ref.dtype)
        k_ref[:, a:a + LANES] = kn[:, h * LANES:(h + 1) * LANES].astype(k_ref.dtype)
        k_ref[:, a + LANES:a + 2 * LANES] = kpe


def mla_up(u, q_norm, kv_norm, wq, wk, wv, rope_c, rope_slo, rope_shi, *, seq, tm):
    n = u.shape[0]
    hq = MLA_HEADS * MLA_QK_PAD
    hv = MLA_HEADS * MLA_V
    ab = EV_REC
    assert seq % tm == 0 and EV_PE % LANES == 0
    nsb = seq // tm
    row_spec = pl.BlockSpec((tm, LANES), lambda i: (i % nsb, 0))
    full = lambda a: pl.BlockSpec(a.shape, lambda i: (0,) * a.ndim)
    qg = q_norm.reshape(1, -1)
    kvg = kv_norm.reshape(1, -1)
    return pl.pallas_call(
        _mla_up_kernel,
        grid=(n // tm,),
        in_specs=[pl.BlockSpec((tm, ab), lambda i: (i, 0)),
                  pl.BlockSpec((tm, LANES), lambda i: (i, EV_PE // LANES)),
                  full(qg), full(kvg), full(wq), full(wk), full(wv),
                  row_spec, row_spec, row_spec],
        out_specs=[pl.BlockSpec((tm, hq), lambda i: (i, 0)),
                   pl.BlockSpec((tm, hq), lambda i: (i, 0)),
                   pl.BlockSpec((tm, hv), lambda i: (i, 0))],
        out_shape=[jax.ShapeDtypeStruct((n, hq), BF16),
                   jax.ShapeDtypeStruct((n, hq), BF16),
                   jax.ShapeDtypeStruct((n, hv), BF16)],
        compiler_params=_cparams(("parallel",)),
    )(u, u, qg, kvg, wq, wk, wv, rope_c, rope_slo, rope_shi)


def _values_and_ones(v):
    return jnp.concatenate([v.astype(BF16), jnp.ones(v.shape, BF16)], axis=1)


def _softmax_step(s, v1, m_ref, acc_ref):
    m_prev = m_ref[...]
    m_new = jnp.maximum(m_prev, jnp.max(s, axis=-1, keepdims=True))
    alpha = jnp.exp(m_prev - m_new)
    p = jnp.exp(s - pltpu.repeat(m_new, s.shape[1] // LANES, axis=1))
    acc_ref[...] = pltpu.repeat(alpha, 2, axis=1) * acc_ref[...] + _dot(p.astype(BF16), v1)
    m_ref[...] = m_new


def _softmax_result(acc):
    return acc[:, :LANES] / acc[:, LANES:]


def _mla_attn_kernel(q_ref, k_ref, v_ref, cb_ref, o_ref, v1, m_ref, acc_ref, *, t, n_chunks):
    qi = pl.program_id(2)

    @pl.when(qi == 0)
    def _():
        v1[...] = _values_and_ones(v_ref[0])

    q = q_ref[0]
    m_ref[...] = jnp.full(m_ref.shape, NEG_INF, F32)
    acc_ref[...] = jnp.zeros(acc_ref.shape, F32)
    for c in range(n_chunks):
        @pl.when(c < qi)
        def _():
            s = _dot_nt(q, k_ref[0, c * t:(c + 1) * t, :])
            _softmax_step(s, v1[c * t:(c + 1) * t, :], m_ref, acc_ref)

        @pl.when(c == qi)
        def _():
            s = _dot_nt(q, k_ref[0, c * t:(c + 1) * t, :]) + cb_ref[...]
            _softmax_step(s, v1[c * t:(c + 1) * t, :], m_ref, acc_ref)
    o_ref[0] = _softmax_result(acc_ref[...]).astype(o_ref.dtype)


def mla_attention(q, k, v, *, t):
    b, s, _ = q.shape
    causal = jnp.asarray(np.where(np.arange(t)[None, :] <= np.arange(t)[:, None], 0.0, NEG_INF), F32)
    kern = functools.partial(_mla_attn_kernel, t=t, n_chunks=s // t)
    return pl.pallas_call(
        kern,
        grid=(b, MLA_HEADS, s // t),
        in_specs=[pl.BlockSpec((1, t, MLA_QK_PAD), lambda b_, h, i: (b_, i, h)),
                  pl.BlockSpec((1, s, MLA_QK_PAD), lambda b_, h, i: (b_, 0, h)),
                  pl.BlockSpec((1, s, MLA_V), lambda b_, h, i: (b_, 0, h)),
                  pl.BlockSpec((t, t), lambda b_, h, i: (0, 0))],
        out_specs=pl.BlockSpec((1, t, MLA_V), lambda b_, h, i: (b_, i, h)),
        out_shape=jax.ShapeDtypeStruct((b, s, MLA_HEADS * MLA_V), BF16),
        scratch_shapes=[pltpu.VMEM((s, 2 * LANES), BF16), pltpu.VMEM((t, LANES), F32),
                        pltpu.VMEM((t, 2 * LANES), F32)],
        compiler_params=_cparams(("parallel", "parallel", "arbitrary")),
    )(q, k, v, causal)


def _rglru_kernel(x_ref, y_ref, cw_ref, cb_ref, gaw_ref, gab_ref, gxw_ref, gxb_ref, lam_ref,
                  o_ref, xbuf, h_ref, *, ts):
    t = pl.program_id(2)

    @pl.when(t == 0)
    def _():
        xbuf[0:8, :] = jnp.zeros((8, LANES), F32)
        h_ref[...] = jnp.zeros(h_ref.shape, F32)

    x = x_ref[0]
    xbuf[8:, :] = x
    cw = cw_ref[...]
    xc = cb_ref[...] + cw[3:4] * x
    for kk in range(CONV_WIDTH - 1):
        back = CONV_WIDTH - 1 - kk
        xc = xc + cw[kk:kk + 1] * xbuf[8 - back:8 - back + ts, :]
    xbuf[0:8, :] = x[ts - 8:, :]

    xcb = xc.astype(BF16)
    r = _sigmoid(_dot(xcb, gaw_ref[0]) + gab_ref[0])
    gi = _sigmoid(_dot(xcb, gxw_ref[0]) + gxb_ref[0])
    z = -lam_ref[...]
    softplus = jnp.maximum(z, 0.0) + jnp.log1p(jnp.exp(-jnp.abs(z)))
    log_a = (-LRU_C) * r * softplus
    a = jnp.exp(log_a)
    mult = jnp.sqrt(-jnp.tanh(log_a) * (a * a + 1.0))
    row = lax.broadcasted_iota(jnp.int32, (ts, 1), 0)
    mult = jnp.where(row + t * ts == 0, 1.0, mult)
    bv = mult * gi * xc

    d = 1
    while d < ts:
        keep = row >= d
        a_sh = jnp.where(keep, pltpu.roll(a, d, 0), 1.0)
        b_sh = jnp.where(keep, pltpu.roll(bv, d, 0), 0.0)
        bv = a * b_sh + bv
        a = a * a_sh
        d *= 2
    h = bv + a * h_ref[...]
    h_ref[...] = h[ts - 1:ts, :]
    o_ref[0] = (h * _gelu_tanh(y_ref[0])).astype(o_ref.dtype)


def rglru(u3, conv_w, conv_b, ga_w, ga_b, gx_w, gx_b, lam, *, ts):
    b, s, _ = u3.shape
    rec0 = EV_REC // LANES
    gate0 = EV_GATE // LANES
    cb = conv_b.reshape(1, LRU_WIDTH)
    gab = ga_b.reshape(LRU_BLOCKS, 1, LRU_BLOCK_W)
    gxb = gx_b.reshape(LRU_BLOCKS, 1, LRU_BLOCK_W)
    lam2 = lam.reshape(1, LRU_WIDTH)
    blk_w = pl.BlockSpec((1, LRU_BLOCK_W, LRU_BLOCK_W), lambda b_, n, t: (n, 0, 0))
    blk_b = pl.BlockSpec((1, 1, LRU_BLOCK_W), lambda b_, n, t: (n, 0, 0))
    vec = pl.BlockSpec((1, LANES), lambda b_, n, t: (0, n))
    return pl.pallas_call(
        functools.partial(_rglru_kernel, ts=ts),
        grid=(b, LRU_BLOCKS, s // ts),
        in_specs=[pl.BlockSpec((1, ts, LANES), lambda b_, n, t: (b_, t, rec0 + n)),
                  pl.BlockSpec((1, ts, LANES), lambda b_, n, t: (b_, t, gate0 + n)),
                  pl.BlockSpec((CONV_WIDTH, LANES), lambda b_, n, t: (0, n)),
                  vec, blk_w, blk_b, blk_w, blk_b, vec],
        out_specs=pl.BlockSpec((1, ts, LANES), lambda b_, n, t: (b_, t, n)),
        out_shape=jax.ShapeDtypeStruct((b, s, LRU_WIDTH), BF16),
        scratch_shapes=[pltpu.VMEM((ts + 8, LANES), F32), pltpu.VMEM((1, LANES), F32)],
        compiler_params=_cparams(("parallel", "parallel", "arbitrary")),
    )(u3, u3, conv_w, cb, ga_w, gab, gx_w, gxb, lam2)


def _ffn_kernel(x_ref, g_ref, w1_ref, w3_ref, w2_ref, o_ref, xn_ref):
    f = pl.program_id(1)

    @pl.when(f == 0)
    def _():
        x = x_ref[...]
        xn_ref[...] = _rms(x, g_ref[...]).astype(BF16)
        o_ref[...] = x

    xn = xn_ref[...]
    h1 = _dot(xn, w1_ref[...])
    h3 = _dot(xn, w3_ref[...])
    hh = (h1 * _sigmoid(h1) * h3).astype(BF16)
    o_ref[...] += _dot(hh, w2_ref[...])


def ffn_dense(x, g, w1, w3, w2, *, tm, tf):
    n, d = x.shape
    ff = w1.shape[1]
    assert n % tm == 0 and ff % tf == 0
    return pl.pallas_call(
        _ffn_kernel,
        grid=(n // tm, ff // tf),
        in_specs=[pl.BlockSpec((tm, d), lambda i, f: (i, 0)),
                  pl.BlockSpec((1, d), lambda i, f: (0, 0)),
                  pl.BlockSpec((d, tf), lambda i, f: (0, f)),
                  pl.BlockSpec((d, tf), lambda i, f: (0, f)),
                  pl.BlockSpec((tf, d), lambda i, f: (f, 0))],
        out_specs=pl.BlockSpec((tm, d), lambda i, f: (i, 0)),
        out_shape=jax.ShapeDtypeStruct((n, d), F32),
        scratch_shapes=[pltpu.VMEM((tm, d), BF16)],
        compiler_params=_cparams(("parallel", "arbitrary")),
    )(x, g.reshape(1, d), w1, w3, w2)


def _compress_kernel(kc_ref, pe_ref, w1_ref, b1_ref, w2_ref, o_ref, *, n_half):
    pe = pe_ref[...]
    half_k = n_half * NSA_DH
    for g in range(NSA_GROUPS):
        lo, hi = [], []
        for l in range(n_half):
            a = l * NSA_KV_DIM + g * NSA_DH
            piece = kc_ref[0, :, a:a + NSA_DH]
            lo.append((piece + pe[l:l + 1]).astype(BF16))
            hi.append((piece + pe[n_half + l:n_half + l + 1]).astype(BF16))
        z0 = _dot(jnp.concatenate(lo, axis=1), w1_ref[0:half_k, :])
        z1 = _dot(jnp.concatenate(hi, axis=1), w1_ref[half_k:2 * half_k, :])
        rows = z1.shape[0]
        pre = z0 + pltpu.roll(z1, rows - 1, 0) + b1_ref[...]
        o_ref[0, :, g * NSA_DH:(g + 1) * NSA_DH] = _dot(_gelu_tanh(pre).astype(BF16), w2_ref[...]).astype(o_ref.dtype)


def nsa_compress(kc3, pe, w1, b1, w2):
    b, nchunk, width = kc3.shape
    n_half = CMP_BLOCK // CMP_STRIDE
    assert CMP_BLOCK == 2 * CMP_STRIDE and width == CMP_STRIDE * NSA_KV_DIM
    full = lambda a: pl.BlockSpec(a.shape, lambda i: (0,) * a.ndim)
    b1r = b1.reshape(1, NSA_DH)
    return pl.pallas_call(
        functools.partial(_compress_kernel, n_half=CMP_STRIDE),
        grid=(b,),
        in_specs=[pl.BlockSpec((1, nchunk, width), lambda i: (i, 0, 0)),
                  full(pe), full(w1), full(b1r), full(w2)],
        out_specs=pl.BlockSpec((1, nchunk, NSA_KV_DIM), lambda i: (i, 0, 0)),
        out_shape=jax.ShapeDtypeStruct((b, nchunk, NSA_KV_DIM), BF16),
        compiler_params=_cparams(("parallel",)),
    )(kc3, pe, w1, b1r, w2)


def _rope128(x, c, s):
    return x * c + pltpu.roll(x, NSA_DH // 2, 1) * s


def _pack_bf16_pairs(x):
    w = x.shape[1] // 2
    bits = pltpu.bitcast(x.astype(BF16).astype(F32), jnp.uint32)
    word = bits[:, w:] | lax.shift_right_logical(bits[:, :w], jnp.uint32(16))
    return pltpu.bitcast(word, F32)


def _unpack_bf16_pairs(word):
    bits = pltpu.bitcast(word, jnp.uint32)
    lo = pltpu.bitcast(lax.shift_left(bits, jnp.uint32(16)), F32).astype(BF16)
    hi = pltpu.bitcast(bits & jnp.uint32(0xFFFF0000), F32).astype(BF16)
    return lo, hi


def _split3(x):
    hi = x.astype(BF16)
    r1 = x - hi.astype(F32)
    mid = r1.astype(BF16)
    lo = (r1 - mid.astype(F32)).astype(BF16)
    return hi, mid, lo


def _nsa_attn_kernel(q_ref, ks_ref, vs_ref, kw_ref, vw_ref, gt_ref, kc_ref, vc_ref,
                     cq_ref, sq_ref, ck_ref, sk_ref, ovl_ref, exp_ref, wb_ref,
                     o_ref, ksr, vsb, kwr, vwb, selb, m_ref, acc_ref,
                     *, tq, tk, seq, scale):
    qi = pl.program_id(2)
    hp = NSA_HPG
    n_sel = seq // SEL_BLOCK

    @pl.when(qi == 0)
    def _():
        ck, sk = ck_ref[...], sk_ref[...]
        ksr[...] = _rope128(ks_ref[0], ck, sk).astype(BF16)
        kwr[...] = _rope128(kw_ref[0], ck, sk).astype(BF16)
        vsb[...] = _values_and_ones(vs_ref[0])
        vwb[...] = _values_and_ones(vw_ref[0])

    q = q_ref[0] * scale
    cq, sq = cq_ref[...], sq_ref[...]
    heads = [q[:, p * NSA_DH:(p + 1) * NSA_DH] for p in range(hp)]
    qu = jnp.concatenate(heads, axis=0).astype(BF16)
    qr = jnp.concatenate([_rope128(h, cq, sq) for h in heads], axis=0).astype(BF16)
    t_row = qi * tq + lax.broadcasted_iota(jnp.int32, (tq, 1), 0)
    lane = lax.broadcasted_iota(jnp.int32, (1, LANES), 1)

    sc = _dot_nt(qu, kc_ref[0]).reshape(hp, tq, LANES)
    valid = (lane * CMP_STRIDE + (CMP_BLOCK - 1) <= t_row)[None]
    sm = jnp.where(valid, sc, NEG_INF)
    e = jnp.exp(sm - jnp.max(sm, axis=-1, keepdims=True))
    p = jnp.where(valid, e / jnp.sum(e, axis=-1, keepdims=True), 0.0)
    o_cmp = _dot(p.reshape(hp * tq, LANES).astype(BF16), vc_ref[0])

    psum = p[0]
    for i in range(1, hp):
        psum = psum + p[i]
    ovl = ovl_ref[...]
    imp = sum(_dot(part, ovl) for part in _split3(psum))
    cur = jnp.right_shift(t_row, SEL_BLOCK.bit_length() - 1)
    future = lane > cur
    forced = (lane == 0) | (lane == cur) | (lane == cur - 1)
    score = jnp.where(future, -1.0, jnp.where(forced, SEL_FORCE, imp))
    sc_t = score.T[0:n_sel, :]
    blk = lax.broadcasted_iota(jnp.int32, (n_sel, 1), 0)
    cnt = jnp.zeros((n_sel, tq), F32)
    for j in range(n_sel):
        other = sc_t[j:j + 1, :]
        beats = (other > sc_t) | ((other == sc_t) & (blk > j))
        cnt = cnt + jnp.where(beats, 1.0, 0.0)
    sel_t = jnp.where(cnt < SEL_COUNT, 1.0, 0.0)
    sel = jnp.concatenate([sel_t, jnp.zeros((LANES - n_sel, tq), F32)], axis=0).T.astype(BF16)
    picked = _dot(sel, exp_ref[...])
    kall = lax.broadcasted_iota(jnp.int32, (1, seq), 1)
    selb[...] = jnp.where((picked > 0.5) & (kall <= t_row), 0.0, NEG_INF)

    m_ref[...] = jnp.full(m_ref.shape, NEG_INF, F32)
    acc_ref[...] = jnp.zeros(acc_ref.shape, F32)
    for c in range(seq // tk):
        @pl.when(c * tk <= qi * tq + (tq - 1))
        def _():
            s = _dot_nt(qr, ksr[c * tk:(c + 1) * tk, :]).reshape(hp, tq, tk)
            s = (s + selb[:, c * tk:(c + 1) * tk][None]).reshape(hp * tq, tk)
            _softmax_step(s, vsb[c * tk:(c + 1) * tk, :], m_ref, acc_ref)
    o_sel = _softmax_result(acc_ref[...])

    span = WINDOW + tq
    start = pl.multiple_of(jnp.maximum(qi * tq - WINDOW, 0), tq)
    sw = _dot_nt(qr, kwr[pl.ds(start, span), :]).reshape(hp, tq, span) + wb_ref[0][None]
    ew = jnp.exp(sw - jnp.max(sw, axis=-1, keepdims=True))
    o_win = _softmax_result(_dot(ew.reshape(hp * tq, span).astype(BF16), vwb[pl.ds(start, span), :]))

    gates = _sigmoid(gt_ref[0])
    for i in range(hp):
        rows = slice(i * tq, (i + 1) * tq)
        o = (gates[:, 3 * i:3 * i + 1] * o_cmp[rows] + gates[:, 3 * i + 1:3 * i + 2] * o_sel[rows]
             + gates[:, 3 * i + 2:3 * i + 3] * o_win[rows])
        o_ref[0, :, i * NSA_DH:(i + 1) * NSA_DH] = o.astype(o_ref.dtype)


def _window_bias(tq):
    span = WINDOW + tq
    out = []
    for qi in range(WINDOW // tq + 1):
        start = max(qi * tq - WINDOW, 0)
        t = qi * tq + np.arange(tq)[:, None]
        kpos = start + np.arange(span)[None, :]
        out.append(np.where((kpos <= t) & (kpos > t - WINDOW), 0.0, NEG_INF))
    return jnp.asarray(np.stack(out), F32)


def nsa_attention(u3, k_cmp, v_cmp, rope_c, rope_s, ovl, expand, *, tq, tk):
    b, s, _ = u3.shape
    assert s // CMP_STRIDE == LANES and tq % SEL_BLOCK == 0 and WINDOW % tq == 0
    hp = NSA_HPG
    nwb = WINDOW // tq
    wbias = _window_bias(tq)
    col = lambda off: (lambda b_, g, i: (b_, 0, off // NSA_DH + g))
    seq_blk = lambda off: pl.BlockSpec((1, s, NSA_DH), col(off))
    full = lambda a: pl.BlockSpec(a.shape, lambda b_, g, i: (0,) * a.ndim)
    cmp_blk = pl.BlockSpec((1, LANES, NSA_DH), lambda b_, g, i: (b_, 0, g))
    rope_q = pl.BlockSpec((tq, NSA_DH), lambda b_, g, i: (i, 0))
    kern = functools.partial(_nsa_attn_kernel, tq=tq, tk=tk, seq=s, scale=NSA_DH ** -0.5)
    return pl.pallas_call(
        kern,
        grid=(b, NSA_GROUPS, s // tq),
        in_specs=[pl.BlockSpec((1, tq, hp * NSA_DH), lambda b_, g, i: (b_, i, g)),
                  seq_blk(OD_KS), seq_blk(OD_VS), seq_blk(OD_KW), seq_blk(OD_VW),
                  pl.BlockSpec((1, tq, LANES), lambda b_, g, i: (b_, i, OD_G // LANES + g)),
                  cmp_blk, cmp_blk, rope_q, rope_q, full(rope_c), full(rope_s), full(ovl), full(expand),
                  pl.BlockSpec((1, tq, WINDOW + tq), lambda b_, g, i: (jnp.minimum(i, nwb), 0, 0))],
        out_specs=pl.BlockSpec((1, tq, hp * NSA_DH), lambda b_, g, i: (b_, i, g)),
        out_shape=jax.ShapeDtypeStruct((b, s, NSA_Q_DIM), BF16),
        scratch_shapes=[pltpu.VMEM((s, NSA_DH), BF16), pltpu.VMEM((s, 2 * LANES), BF16)] * 2 + [
            pltpu.VMEM((tq, s), F32),
            pltpu.VMEM((hp * tq, LANES), F32), pltpu.VMEM((hp * tq, 2 * LANES), F32)],
        compiler_params=_cparams(("parallel", "parallel", "arbitrary")),
    )(u3, u3, u3, u3, u3, u3, k_cmp, v_cmp, rope_c, rope_s, rope_c, rope_s, ovl, expand, wbias)


def _router_kernel(x_ref, g_ref, wr_ref, br_ref, xn_ref, meta_ref, cnt_ref, *, tm):
    @pl.when(pl.program_id(0) == 0)
    def _():
        cnt_ref[...] = jnp.zeros(cnt_ref.shape, F32)

    xn = _rms(x_ref[...], g_ref[...])
    xn_ref[...] = _pack_bf16_pairs(xn)
    xh, xm, _ = _split3(xn)
    wh, wm, _ = _split3(wr_ref[...])
    logits = _dot(xh, wh) + _dot(xh, wm) + _dot(xm, wh) + br_ref[...]
    lane = lax.broadcasted_iota(jnp.int32, (1, LANES), 1).astype(F32)
    lg = jnp.where(lane < N_EXPERTS, logits, NEG_INF)
    m1 = jnp.max(lg, axis=-1, keepdims=True)
    e1 = jnp.min(jnp.where(lg == m1, lane, float(LANES)), axis=-1, keepdims=True)
    lg2 = jnp.where(lane == e1, NEG_INF, lg)
    m2 = jnp.max(lg2, axis=-1, keepdims=True)
    e2 = jnp.min(jnp.where(lg2 == m2, lane, float(LANES)), axis=-1, keepdims=True)
    ex = jnp.exp(m2 - m1)
    den = 1.0 + ex
    g1 = 1.0 / den
    g2 = ex / den
    oh = jnp.where((lane == e1) | (lane == e2), 1.0, 0.0)
    r = lax.broadcasted_iota(jnp.int32, (tm, tm), 0)
    c = lax.broadcasted_iota(jnp.int32, (tm, tm), 1)
    tri = jnp.where(r > c, 1.0, 0.0).astype(BF16)
    cum = _dot(tri, oh.astype(BF16)) + cnt_ref[0:1, :]
    pos1 = jnp.sum(jnp.where(lane == e1, cum, 0.0), axis=-1, keepdims=True)
    pos2 = jnp.sum(jnp.where(lane == e2, cum, 0.0), axis=-1, keepdims=True)
    cnt_ref[...] = cnt_ref[...] + jnp.sum(oh, axis=0, keepdims=True)
    meta = jnp.where(lane == 0, e1, 0.0)
    meta = jnp.where(lane == 1, e2, meta)
    meta = jnp.where(lane == 2, g1, meta)
    meta = jnp.where(lane == 3, g2, meta)
    meta = jnp.where(lane == 4, pos1, meta)
    meta = jnp.where(lane == 5, pos2, meta)
    meta_ref[...] = meta


def moe_router(x, g, wr, br, *, tm):
    n, d = x.shape
    return pl.pallas_call(
        functools.partial(_router_kernel, tm=tm),
        grid=(n // tm,),
        in_specs=[pl.BlockSpec((tm, d), lambda i: (i, 0)),
                  pl.BlockSpec((1, d), lambda i: (0, 0)),
                  pl.BlockSpec((d, LANES), lambda i: (0, 0)),
                  pl.BlockSpec((1, LANES), lambda i: (0, 0))],
        out_specs=[pl.BlockSpec((tm, d // 2), lambda i: (i, 0)),
                   pl.BlockSpec((tm, LANES), lambda i: (i, 0)),
                   pl.BlockSpec((8, LANES), lambda i: (0, 0))],
        out_shape=[jax.ShapeDtypeStruct((n, d // 2), F32),
                   jax.ShapeDtypeStruct((n, LANES), F32),
                   jax.ShapeDtypeStruct((8, LANES), F32)],
        compiler_params=_cparams(("arbitrary",)),
    )(x, g.reshape(1, d), wr, br)


def _gather_chunk(per_worker, row_bytes):
    best = 0
    for c in range(8, per_worker + 1, 8):
        if per_worker % c == 0 and c * row_bytes <= SC_GATHER_BYTES and c <= LANES:
            best = c
    assert best > 0, (per_worker, row_bytes)
    return best


def sc_gather_rows(table, idx):
    _, d = table.shape
    b = idx.shape[0]
    assert b % (8 * SC_WORKERS) == 0 and table.dtype.itemsize == 4
    per_w = b // SC_WORKERS
    chunk = _gather_chunk(per_w, d * 4)
    mesh = plsc.VectorSubcoreMesh(core_axis_name="c", subcore_axis_name="s",
                                  num_cores=SC_CORES, num_subcores=SC_SUBCORES)

    @functools.partial(
        pl.kernel, mesh=mesh,
        out_type=jax.ShapeDtypeStruct((b, d), table.dtype),
        scratch_types=[pltpu.VMEM((chunk,), jnp.int32), pltpu.VMEM((chunk, d), table.dtype),
                       pltpu.SemaphoreType.DMA])
    def gather(table_hbm, idx_hbm, out_hbm, idx_v, rows_v, sem):
        wid = lax.axis_index("s") * SC_CORES + lax.axis_index("c")
        base = wid * per_w

        @pl.loop(0, per_w // chunk)
        def _(c):
            off = pl.multiple_of(base + c * chunk, 8)
            pltpu.sync_copy(idx_hbm.at[pl.ds(off, chunk)], idx_v)
            pltpu.async_copy(table_hbm.at[idx_v], rows_v, sem).wait()
            pltpu.sync_copy(rows_v, out_hbm.at[pl.ds(off, chunk)])

    return gather(table, idx)


def _expert_kernel(te_ref, ts_ref, tr_ref, xp_hbm, *rest, nf):
    ns = MOE_WSPLIT
    w1_refs, w3_refs, w2_refs = rest[:ns], rest[ns:2 * ns], rest[2 * ns:3 * ns]
    yr_hbm, xb, acc, stage, in_sem, out_sem = rest[3 * ns:]
    t = pl.program_id(0)
    f = pl.program_id(1)
    rows = tr_ref[t]
    start = pl.multiple_of(ts_ref[t], MOE_ALIGN)
    n_chunks = MOE_TILE // MOE_COPY
    half = xb.shape[1] // 2

    def in_copy(ci):
        return pltpu.make_async_copy(xp_hbm.at[pl.ds(start + ci * MOE_COPY, MOE_COPY)],
                                     stage.at[ci % 2], in_sem.at[ci % 2])

    def out_copy(ci):
        return pltpu.make_async_copy(acc.at[pl.ds(ci * MOE_COPY, MOE_COPY)],
                                     yr_hbm.at[pl.ds(start + ci * MOE_COPY, MOE_COPY)], out_sem.at[0])

    def when_chunk_live(ci, fn):
        pl.when(ci * MOE_COPY < rows)(fn)

    @pl.when((f == 0) & (rows > 0))
    def _():
        def unpack(ci):
            in_copy(ci).wait()
            lo, hi = _unpack_bf16_pairs(stage[ci % 2])
            xb[ci * MOE_COPY:(ci + 1) * MOE_COPY, :half] = lo
            xb[ci * MOE_COPY:(ci + 1) * MOE_COPY, half:] = hi

        when_chunk_live(0, lambda: in_copy(0).start())
        for ci in range(n_chunks):
            if ci + 1 < n_chunks:
                when_chunk_live(ci + 1, lambda ci=ci: in_copy(ci + 1).start())
            when_chunk_live(ci, lambda ci=ci: unpack(ci))
        acc[...] = jnp.zeros(acc.shape, F32)

    @pl.when(rows > 0)
    def _():
        w1 = jnp.concatenate([r[0, 0].astype(BF16) for r in w1_refs], axis=0)
        w3 = jnp.concatenate([r[0, 0].astype(BF16) for r in w3_refs], axis=0)
        w2 = jnp.concatenate([r[0, 0].astype(BF16) for r in w2_refs], axis=1)

        def chain(r0, size):
            rs = pl.ds(pl.multiple_of(r0, MOE_PIECES[-1]), size)
            xc = xb[rs, :]
            h1 = _dot(xc, w1)
            h3 = _dot(xc, w3)
            hh = (h1 * _sigmoid(h1) * h3).astype(BF16)
            acc[rs, :] += _dot(hh, w2)

        todo = (rows + (MOE_PIECES[-1] - 1)) // MOE_PIECES[-1] * MOE_PIECES[-1]
        off = jnp.int32(0)
        for size in MOE_PIECES:
            take = todo - off >= size

            @pl.when(take)
            def _(off=off, size=size):
                for c0 in range(0, size, MOE_CHAIN):
                    chain(off + c0, min(MOE_CHAIN, size))

            off = off + jnp.where(take, size, 0)

    @pl.when((f == nf - 1) & (rows > 0))
    def _():
        for ci in range(n_chunks):
            when_chunk_live(ci, lambda ci=ci: out_copy(ci).start())
        for ci in range(n_chunks):
            when_chunk_live(ci, lambda ci=ci: out_copy(ci).wait())


def moe_experts(xp, tile_e, tile_start, tile_rows, w1, w3, w2, layer, *, tf):
    n_rows, half = xp.shape
    d = 2 * half
    n_tiles = tile_e.shape[0]
    ff = w1.shape[3]
    nf = ff // tf
    assert MOE_TILE % MOE_COPY == 0 and sum(MOE_PIECES) >= MOE_TILE

    def f_eff(t, f, tr):
        return jnp.where(tr[t] > 0, f, nf - 1)

    ns = MOE_WSPLIT
    up_specs = [pl.BlockSpec((1, 1, d // ns, tf), lambda t, f, te, ts, tr, k=k: (layer, te[t], k, f_eff(t, f, tr)))
                for k in range(ns)]
    down_specs = [pl.BlockSpec((1, 1, tf, d // ns), lambda t, f, te, ts, tr, k=k: (layer, te[t], f_eff(t, f, tr), k))
                  for k in range(ns)]
    grid_spec = pltpu.PrefetchScalarGridSpec(
        num_scalar_prefetch=3,
        grid=(n_tiles, nf),
        in_specs=[pl.BlockSpec(memory_space=pl.ANY)] + 2 * up_specs + down_specs,
        out_specs=pl.BlockSpec(memory_space=pl.ANY),
        scratch_shapes=[pltpu.VMEM((MOE_TILE, d), BF16), pltpu.VMEM((MOE_TILE, d), F32),
                        pltpu.VMEM((2, MOE_COPY, half), F32),
                        pltpu.SemaphoreType.DMA((2,)), pltpu.SemaphoreType.DMA((1,))],
    )
    return pl.pallas_call(
        functools.partial(_expert_kernel, nf=nf),
        grid_spec=grid_spec,
        out_shape=jax.ShapeDtypeStruct((n_rows, d), F32),
        compiler_params=_cparams(("arbitrary", "arbitrary")),
    )(tile_e, tile_start, tile_rows, xp, *([w1] * ns), *([w3] * ns), *([w2] * ns))


def _combine_kernel(x_ref, y1_ref, y2_ref, meta_ref, g_ref, o_ref, *, final_norm):
    meta = meta_ref[...]
    y = x_ref[...] + (meta[:, 2:3] * y1_ref[...] + meta[:, 3:4] * y2_ref[...])
    if final_norm:
        y = _rms(y, g_ref[...])
    o_ref[...] = y


def moe_combine(x, yg, meta, g, *, final_norm, tm):
    n, d = x.shape
    nb = n // tm
    row = pl.BlockSpec((tm, d), lambda i: (i, 0))
    return pl.pallas_call(
        functools.partial(_combine_kernel, final_norm=final_norm),
        grid=(nb,),
        in_specs=[row, row, pl.BlockSpec((tm, d), lambda i: (i + nb, 0)),
                  pl.BlockSpec((tm, LANES), lambda i: (i, 0)),
                  pl.BlockSpec((1, d), lambda i: (0, 0))],
        out_specs=row,
        out_shape=jax.ShapeDtypeStruct((n, d), F32),
        compiler_params=_cparams(("parallel",)),
    )(x, yg, yg, meta, g.reshape(1, d))


def _mla_rope_tables(seq):
    half = MLA_ROPE // 2
    pos = jnp.arange(seq, dtype=F32)
    inv = ROPE_THETA ** (-jnp.arange(0, MLA_ROPE, 2, dtype=F32) / MLA_ROPE)
    ang = pos[:, None] * inv[None, :]
    cos, sin = jnp.cos(ang), jnp.sin(ang)
    z = jnp.zeros((seq, half), F32)
    pad = jnp.zeros((seq, LANES - MLA_ROPE), F32)
    c = jnp.concatenate([cos, cos, pad], axis=1)
    s_lo = jnp.concatenate([-sin, z, pad], axis=1)
    s_hi = jnp.concatenate([z, sin, pad], axis=1)
    return c, s_lo, s_hi


def _nsa_rope_tables(seq):
    pos = jnp.arange(seq, dtype=F32)
    inv = ROPE_THETA ** (-jnp.arange(0, NSA_DH, 2, dtype=F32) / NSA_DH)
    ang = pos[:, None] * inv[None, :]
    cos, sin = jnp.cos(ang), jnp.sin(ang)
    return jnp.concatenate([cos, cos], axis=1), jnp.concatenate([-sin, sin], axis=1)


def _selection_constants(seq):
    n_sel = seq // SEL_BLOCK
    nc = (seq - CMP_BLOCK) // CMP_STRIDE + 1
    cmp_start = np.arange(LANES) * CMP_STRIDE
    sel_start = np.arange(LANES) * SEL_BLOCK
    ovl = ((cmp_start[:, None] < sel_start[None, :] + SEL_BLOCK) &
           (cmp_start[:, None] + CMP_BLOCK > sel_start[None, :]))
    ovl &= (np.arange(LANES)[:, None] < nc) & (np.arange(LANES)[None, :] < n_sel)
    expand = (np.arange(seq)[None, :] // SEL_BLOCK == np.arange(LANES)[:, None])
    return jnp.asarray(ovl, BF16), jnp.asarray(expand, BF16)


def even_layer(x, seq, p):
    n = x.shape[0]
    b = n // seq
    (norm_mix, w_in, q_norm, w_q_up, kv_norm, w_kv_up, conv_w, conv_b, ga_w, ga_b, gx_w, gx_b,
     lam, w_out, norm_ffn, w1, w3, w2) = p
    d = D_MODEL
    o1 = MLA_Q_LORA + MLA_KV_LORA
    o2 = o1 + MLA_ROPE
    w_pack = jnp.concatenate(
        [w_in[:, :o1], w_in[:, o2:], w_in[:, o1:o2], jnp.zeros((d, LANES - MLA_ROPE), F32)], axis=1).astype(BF16)
    u = norm_matmul(x, norm_mix, w_pack, tm=1024, tn=EV_PACKED // 3)

    wq = w_q_up.reshape(MLA_Q_LORA, MLA_HEADS, MLA_NOPE + MLA_ROPE)
    wq = jnp.pad(wq, ((0, 0), (0, 0), (0, MLA_QK_PAD - MLA_NOPE - MLA_ROPE)))
    wq = wq.reshape(MLA_Q_LORA, MLA_HEADS * MLA_QK_PAD).astype(BF16)
    wkv = w_kv_up.reshape(MLA_KV_LORA, MLA_HEADS, MLA_NOPE + MLA_V)
    wk = wkv[:, :, :MLA_NOPE].reshape(MLA_KV_LORA, MLA_HEADS * MLA_NOPE).astype(BF16)
    wv = wkv[:, :, MLA_NOPE:].reshape(MLA_KV_LORA, MLA_HEADS * MLA_V).astype(BF16)
    rc, rlo, rhi = _mla_rope_tables(seq)
    q, k, v = mla_up(u, q_norm, kv_norm, wq, wk, wv, rc, rlo, rhi, seq=seq, tm=512)
    o_mla = mla_attention(q.reshape(b, seq, -1), k.reshape(b, seq, -1), v.reshape(b, seq, -1), t=512)

    o_rec = rglru(u.reshape(b, seq, EV_PACKED), conv_w, conv_b, ga_w.astype(BF16), ga_b,
                  gx_w.astype(BF16), gx_b, lam, ts=512)
    x = matmul_residual([o_mla.reshape(n, -1), o_rec.reshape(n, -1)], w_out.astype(BF16), x, tm=512, tn=1024)
    return ffn_dense(x, norm_ffn, w1.astype(BF16), w3.astype(BF16), w2.astype(BF16), tm=512, tf=512)


def _moe_dispatch(meta, counts, n):
    e1 = meta[:, 0].astype(jnp.int32)
    e2 = meta[:, 1].astype(jnp.int32)
    pos1 = meta[:, 4].astype(jnp.int32)
    pos2 = meta[:, 5].astype(jnp.int32)
    cnt = counts[0, :N_EXPERTS].astype(jnp.int32)
    span = (cnt + MOE_ALIGN - 1) // MOE_ALIGN * MOE_ALIGN
    row0 = jnp.cumsum(span) - span
    row_unit = 64 * SC_WORKERS
    n_rows = -(-(n * TOP_K + N_EXPERTS * MOE_ALIGN + MOE_TILE) // row_unit) * row_unit
    d1 = row0[e1] + pos1
    d2 = row0[e2] + pos2
    n_tiles = (n * TOP_K) // MOE_TILE + N_EXPERTS
    tiles_e = (cnt + MOE_TILE - 1) // MOE_TILE
    tend = jnp.cumsum(tiles_e)
    tbeg = tend - tiles_e
    tid = jnp.arange(n_tiles, dtype=jnp.int32)
    te = jnp.minimum(jnp.searchsorted(tend, tid, side='right'), N_EXPERTS - 1).astype(jnp.int32)
    used = tid < tend[-1]
    first = (tid - tbeg[te]) * MOE_TILE
    rows = jnp.where(used, jnp.clip(cnt[te] - first, 0, MOE_TILE), 0).astype(jnp.int32)
    tstart = jnp.where(used, row0[te] + first, 0).astype(jnp.int32)
    last_e = te[jnp.maximum(tend[-1] - 1, 0)]
    te = jnp.where(used, te, last_e).astype(jnp.int32)
    tok = jnp.arange(n, dtype=jnp.int32)
    row_tok = (jnp.arange(n_rows, dtype=jnp.int32) % n).at[d1].set(tok).at[d2].set(tok)
    return d1, d2, row_tok, te, tstart, rows


def odd_layer(x, seq, p, experts, final_g):
    n = x.shape[0]
    b = n // seq
    (norm_mix, w_in, ck_pe, ck_w1, ck_b1, ck_w2, cv_pe, cv_w1, cv_b1, cv_w2, w_out, norm_ffn,
     router_w, router_b) = p
    ew1, ew3, ew2, layer = experts
    d = D_MODEL
    wg = w_in[:, OD_G:].reshape(d, NSA_GROUPS, NSA_HPG * 3)
    wg = jnp.pad(wg, ((0, 0), (0, 0), (0, LANES - NSA_HPG * 3))).reshape(d, NSA_GROUPS * LANES)
    w_pack = jnp.concatenate([w_in[:, :OD_G], wg], axis=1).astype(BF16)
    u = norm_matmul(x, norm_mix, w_pack, tm=1024, tn=512)
    u3 = u.reshape(b, seq, OD_PACKED)

    nchunk = seq // CMP_STRIDE
    kc3 = u3[:, :, OD_KC:OD_KC + NSA_KV_DIM].reshape(b, nchunk, CMP_STRIDE * NSA_KV_DIM)
    vc3 = u3[:, :, OD_VC:OD_VC + NSA_KV_DIM].reshape(b, nchunk, CMP_STRIDE * NSA_KV_DIM)
    k_cmp = nsa_compress(kc3, ck_pe, ck_w1.astype(BF16), ck_b1, ck_w2.astype(BF16))
    v_cmp = nsa_compress(vc3, cv_pe, cv_w1.astype(BF16), cv_b1, cv_w2.astype(BF16))
    rc, rs = _nsa_rope_tables(seq)
    ovl, expand = _selection_constants(seq)
    o = nsa_attention(u3, k_cmp, v_cmp, rc, rs, ovl, expand, tq=256, tk=512)
    x = matmul_residual([o.reshape(n, -1)], w_out.astype(BF16), x, tm=512, tn=1024)

    wr = jnp.pad(router_w, ((0, 0), (0, LANES - N_EXPERTS)))
    br = jnp.pad(router_b, (0, LANES - N_EXPERTS)).reshape(1, LANES)
    xp, meta, counts = moe_router(x, norm_ffn, wr, br, tm=512)
    d1, d2, row_tok, te, tstart, rows = _moe_dispatch(meta, counts, n)
    yr = moe_experts(sc_gather_rows(xp, row_tok), te, tstart, rows, ew1, ew3, ew2, layer, tf=256)
    yg = sc_gather_rows(yr, jnp.concatenate([d1, d2]))
    g = final_g if final_g is not None else norm_ffn
    return moe_combine(x, yg, meta, g, final_norm=final_g is not None, tm=512)


def kernel(x, ev_norm_mix, ev_w_in, ev_q_norm, ev_w_q_up, ev_kv_norm, ev_w_kv_up, ev_conv_w, ev_conv_b, ev_gate_a_w, ev_gate_a_b, ev_gate_x_w, ev_gate_x_b, ev_lru_lambda, ev_w_out, ev_norm_ffn, ev_ffn_w1, ev_ffn_w3, ev_ffn_w2, od_norm_mix, od_w_in, od_cmp_k_pe, od_cmp_k_w1, od_cmp_k_b1, od_cmp_k_w2, od_cmp_v_pe, od_cmp_v_w1, od_cmp_v_b1, od_cmp_v_w2, od_w_out, od_norm_ffn, od_router_w, od_router_b, od_exp_w1, od_exp_w3, od_exp_w2, final_norm):
    bsz, seq, d = x.shape
    ev = (ev_norm_mix, ev_w_in, ev_q_norm, ev_w_q_up, ev_kv_norm, ev_w_kv_up, ev_conv_w, ev_conv_b,
          ev_gate_a_w, ev_gate_a_b, ev_gate_x_w, ev_gate_x_b, ev_lru_lambda, ev_w_out, ev_norm_ffn,
          ev_ffn_w1, ev_ffn_w3, ev_ffn_w2)
    od = (od_norm_mix, od_w_in, od_cmp_k_pe, od_cmp_k_w1, od_cmp_k_b1, od_cmp_k_w2, od_cmp_v_pe,
          od_cmp_v_w1, od_cmp_v_b1, od_cmp_v_w2, od_w_out, od_norm_ffn, od_router_w, od_router_b)
    h = x.reshape(bsz * seq, d)
    for layer in range(DEPTH):
        i = layer // 2
        if layer % 2 == 0:
            h = even_layer(h, seq, tuple(a[i] for a in ev))
        else:
            h = odd_layer(h, seq, tuple(a[i] for a in od), (od_exp_w1, od_exp_w3, od_exp_w2, i),
                          final_norm if layer == DEPTH - 1 else None)
    return h.reshape(bsz, seq, d)
```

```python
import functools
import math

import numpy as np
import jax
import jax.numpy as jnp
from jax import lax
from jax.experimental import pallas as pl
from jax.experimental.pallas import tpu as pltpu
from jax.experimental.pallas import tpu_sc as plsc

F32 = jnp.float32
BF16 = jnp.bfloat16

D_MODEL = 2048
DEPTH = 4
RMS_EPS = 1e-6
ROPE_THETA = 10000.0
NEG_INF = -1e30

MLA_HEADS = 8
MLA_Q_LORA = 768
MLA_KV_LORA = 512
MLA_NOPE = 128
MLA_ROPE = 64
MLA_V = 128
MLA_QK_PAD = 256

LRU_WIDTH = D_MODEL // 2
LRU_BLOCKS = 8
LRU_BLOCK_W = LRU_WIDTH // LRU_BLOCKS
LRU_C = 8.0
CONV_WIDTH = 4

NSA_HEADS = 16
NSA_GROUPS = 4
NSA_HPG = NSA_HEADS // NSA_GROUPS
NSA_DH = D_MODEL // NSA_HEADS
CMP_BLOCK = 32
CMP_STRIDE = 16
SEL_BLOCK = 64
SEL_COUNT = 16
SEL_FORCE = 1e4
WINDOW = 512
NSA_Q_DIM = NSA_HEADS * NSA_DH
NSA_KV_DIM = NSA_GROUPS * NSA_DH

DENSE_FF = 5632
N_EXPERTS = 8
TOP_K = 2
EXPERT_FF = 7168

LANES = 128
SC_CORES = 2
SC_SUBCORES = 16
SC_WORKERS = SC_CORES * SC_SUBCORES
SC_GATHER_BYTES = 256 * 1024
VMEM_LIMIT = 60 * 1024 * 1024

EV_CQ = 0
EV_CKV = MLA_Q_LORA
EV_REC = MLA_Q_LORA + MLA_KV_LORA
EV_GATE = EV_REC + LRU_WIDTH
EV_PE = EV_GATE + LRU_WIDTH
EV_PACKED = EV_PE + LANES

OD_Q = 0
OD_KC = NSA_Q_DIM
OD_VC = OD_KC + NSA_KV_DIM
OD_KS = OD_VC + NSA_KV_DIM
OD_VS = OD_KS + NSA_KV_DIM
OD_KW = OD_VS + NSA_KV_DIM
OD_VW = OD_KW + NSA_KV_DIM
OD_G = OD_VW + NSA_KV_DIM
OD_PACKED = OD_G + NSA_GROUPS * LANES

MOE_TILE = 2688
MOE_ALIGN = 16
MOE_COPY = 384
MOE_PIECES = (1024, 1024, 512, 256, 128)
MOE_CHAIN = 512
MOE_WSPLIT = 1


def _cparams(sem):
    return pltpu.CompilerParams(dimension_semantics=sem, vmem_limit_bytes=VMEM_LIMIT)


def _rms(x, g):
    ms = jnp.mean(x * x, axis=-1, keepdims=True)
    return x * lax.rsqrt(ms + RMS_EPS) * g


def _sigmoid(x):
    return 1.0 / (1.0 + jnp.exp(-x))


def _gelu_tanh(x):
    return 0.5 * x * (1.0 + jnp.tanh(math.sqrt(2.0 / math.pi) * (x + 0.044715 * (x * x * x))))


def _dot(a, b):
    return jnp.dot(a, b, preferred_element_type=F32)


def _dot_nt(a, b):
    return lax.dot_general(a, b, (((1,), (1,)), ((), ())), preferred_element_type=F32)


def _norm_mm_kernel(x_ref, g_ref, w_ref, o_ref, xn_ref):
    @pl.when(pl.program_id(1) == 0)
    def _():
        xn_ref[...] = _rms(x_ref[...], g_ref[...]).astype(BF16)

    o_ref[...] = _dot(xn_ref[...], w_ref[...]).astype(o_ref.dtype)


def norm_matmul(x, g, w, *, tm, tn, out_dtype=F32):
    n, k = x.shape
    m = w.shape[1]
    assert n % tm == 0 and m % tn == 0
    return pl.pallas_call(
        _norm_mm_kernel,
        grid=(n // tm, m // tn),
        in_specs=[pl.BlockSpec((tm, k), lambda i, j: (i, 0)),
                  pl.BlockSpec((1, k), lambda i, j: (0, 0)),
                  pl.BlockSpec((k, tn), lambda i, j: (0, j))],
        out_specs=pl.BlockSpec((tm, tn), lambda i, j: (i, j)),
        out_shape=jax.ShapeDtypeStruct((n, m), out_dtype),
        scratch_shapes=[pltpu.VMEM((tm, k), BF16)],
        compiler_params=_cparams(("parallel", "arbitrary")),
    )(x, g.reshape(1, k), w)


def _mm_res_kernel(*refs, n_in):
    xs = refs[:n_in]
    ws = refs[n_in:2 * n_in]
    res_ref = refs[2 * n_in]
    o_ref = refs[2 * n_in + 1]
    acc = res_ref[...]
    for x_ref, w_ref in zip(xs, ws):
        acc = acc + _dot(x_ref[...], w_ref[...])
    o_ref[...] = acc


def matmul_residual(xs, w, res, *, tm, tn):
    n = res.shape[0]
    m = w.shape[1]
    n_in = len(xs)
    in_specs = [pl.BlockSpec((tm, x.shape[1]), lambda i, j: (i, 0)) for x in xs]
    row = 0
    for x in xs:
        kx = x.shape[1]
        assert row % kx == 0
        in_specs.append(pl.BlockSpec((kx, tn), lambda i, j, rb=row // kx: (rb, j)))
        row += kx
    assert row == w.shape[0]
    in_specs.append(pl.BlockSpec((tm, tn), lambda i, j: (i, j)))
    return pl.pallas_call(
        functools.partial(_mm_res_kernel, n_in=n_in),
        grid=(n // tm, m // tn),
        in_specs=in_specs,
        out_specs=pl.BlockSpec((tm, tn), lambda i, j: (i, j)),
        out_shape=jax.ShapeDtypeStruct((n, m), F32),
        compiler_params=_cparams(("parallel", "arbitrary")),
    )(*xs, *([w] * n_in), res)


def _rope64(x, c, s_lo, s_hi):
    return x * c + pltpu.roll(x, 96, 1) * s_lo + pltpu.roll(x, 32, 1) * s_hi


def _mla_up_kernel(u_ref, pe_ref, qg_ref, kvg_ref, wq_ref, wk_ref, wv_ref,
                   c_ref, slo_ref, shi_ref, q_ref, k_ref, v_ref):
    u = u_ref[...]
    c, s_lo, s_hi = c_ref[...], slo_ref[...], shi_ref[...]
    qn = _rms(u[:, EV_CQ:EV_CQ + MLA_Q_LORA], qg_ref[...]).astype(BF16)
    kvn = _rms(u[:, EV_CKV:EV_CKV + MLA_KV_LORA], kvg_ref[...]).astype(BF16)
    q = _dot(qn, wq_ref[...]) * ((MLA_NOPE + MLA_ROPE) ** -0.5)
    kn = _dot(kvn, wk_ref[...])
    v_ref[...] = _dot(kvn, wv_ref[...]).astype(v_ref.dtype)
    kpe = _rope64(pe_ref[...], c, s_lo, s_hi).astype(k_ref.dtype)
    for h in range(MLA_HEADS):
        a = h * MLA_QK_PAD
        q_ref[:, a:a + LANES] = q[:, a:a + LANES].astype(q_ref.dtype)
        q_ref[:, a + LANES:a + 2 * LANES] = _rope64(q[:, a + LANES:a + 2 * LANES], c, s_lo, s_hi).astype(q_ref.dtype)
        k_ref[:, a:a + LANES] = kn[:, h * LANES:(h + 1) * LANES].astype(k_ref.dtype)
        k_ref[:, a + LANES:a + 2 * LANES] = kpe


def mla_up(u, q_norm, kv_norm, wq, wk, wv, rope_c, rope_slo, rope_shi, *, seq, tm):
    n = u.shape[0]
    hq = MLA_HEADS * MLA_QK_PAD
    hv = MLA_HEADS * MLA_V
    ab = EV_REC
    assert seq % tm == 0 and EV_PE % LANES == 0
    nsb = seq // tm
    row_spec = pl.BlockSpec((tm, LANES), lambda i: (i % nsb, 0))
    full = lambda a: pl.BlockSpec(a.shape, lambda i: (0,) * a.ndim)
    qg = q_norm.reshape(1, -1)
    kvg = kv_norm.reshape(1, -1)
    return pl.pallas_call(
        _mla_up_kernel,
        grid=(n // tm,),
        in_specs=[pl.BlockSpec((tm, ab), lambda i: (i, 0)),
                  pl.BlockSpec((tm, LANES), lambda i: (i, EV_PE // LANES)),
                  full(qg), full(kvg), full(wq), full(wk), full(wv),
                  row_spec, row_spec, row_spec],
        out_specs=[pl.BlockSpec((tm, hq), lambda i: (i, 0)),
                   pl.BlockSpec((tm, hq), lambda i: (i, 0)),
                   pl.BlockSpec((tm, hv), lambda i: (i, 0))],
        out_shape=[jax.ShapeDtypeStruct((n, hq), BF16),
                   jax.ShapeDtypeStruct((n, hq), BF16),
                   jax.ShapeDtypeStruct((n, hv), BF16)],
        compiler_params=_cparams(("parallel",)),
    )(u, u, qg, kvg, wq, wk, wv, rope_c, rope_slo, rope_shi)


def _values_and_ones(v):
    return jnp.concatenate([v.astype(BF16), jnp.ones(v.shape, BF16)], axis=1)


def _softmax_step(s, v1, m_ref, acc_ref):
    m_prev = m_ref[...]
    m_new = jnp.maximum(m_prev, jnp.max(s, axis=-1, keepdims=True))
    alpha = jnp.exp(m_prev - m_new)
    p = jnp.exp(s - pltpu.repeat(m_new, s.shape[1] // LANES, axis=1))
    acc_ref[...] = pltpu.repeat(alpha, 2, axis=1) * acc_ref[...] + _dot(p.astype(BF16), v1)
    m_ref[...] = m_new


def _softmax_result(acc):
    return acc[:, :LANES] / acc[:, LANES:]


def _mla_attn_kernel(q_ref, k_ref, v_ref, cb_ref, o_ref, v1, m_ref, acc_ref, *, t, n_chunks):
    qi = pl.program_id(2)

    @pl.when(qi == 0)
    def _():
        v1[...] = _values_and_ones(v_ref[0])

    q = q_ref[0]
    m_ref[...] = jnp.full(m_ref.shape, NEG_INF, F32)
    acc_ref[...] = jnp.zeros(acc_ref.shape, F32)
    for c in range(n_chunks):
        @pl.when(c < qi)
        def _():
            s = _dot_nt(q, k_ref[0, c * t:(c + 1) * t, :])
            _softmax_step(s, v1[c * t:(c + 1) * t, :], m_ref, acc_ref)

        @pl.when(c == qi)
        def _():
            s = _dot_nt(q, k_ref[0, c * t:(c + 1) * t, :]) + cb_ref[...]
            _softmax_step(s, v1[c * t:(c + 1) * t, :], m_ref, acc_ref)
    o_ref[0] = _softmax_result(acc_ref[...]).astype(o_ref.dtype)


def mla_attention(q, k, v, *, t):
    b, s, _ = q.shape
    causal = jnp.asarray(np.where(np.arange(t)[None, :] <= np.arange(t)[:, None], 0.0, NEG_INF), F32)
    kern = functools.partial(_mla_attn_kernel, t=t, n_chunks=s // t)
    return pl.pallas_call(
        kern,
        grid=(b, MLA_HEADS, s // t),
        in_specs=[pl.BlockSpec((1, t, MLA_QK_PAD), lambda b_, h, i: (b_, i, h)),
                  pl.BlockSpec((1, s, MLA_QK_PAD), lambda b_, h, i: (b_, 0, h)),
                  pl.BlockSpec((1, s, MLA_V), lambda b_, h, i: (b_, 0, h)),
                  pl.BlockSpec((t, t), lambda b_, h, i: (0, 0))],
        out_specs=pl.BlockSpec((1, t, MLA_V), lambda b_, h, i: (b_, i, h)),
        out_shape=jax.ShapeDtypeStruct((b, s, MLA_HEADS * MLA_V), BF16),
        scratch_shapes=[pltpu.VMEM((s, 2 * LANES), BF16), pltpu.VMEM((t, LANES), F32),
                        pltpu.VMEM((t, 2 * LANES), F32)],
        compiler_params=_cparams(("parallel", "parallel", "arbitrary")),
    )(q, k, v, causal)


def _rglru_kernel(x_ref, y_ref, cw_ref, cb_ref, gaw_ref, gab_ref, gxw_ref, gxb_ref, lam_ref,
                  o_ref, xbuf, h_ref, *, ts):
    t = pl.program_id(2)

    @pl.when(t == 0)
    def _():
        xbuf[0:8, :] = jnp.zeros((8, LANES), F32)
        h_ref[...] = jnp.zeros(h_ref.shape, F32)

    x = x_ref[0]
    xbuf[8:, :] = x
    cw = cw_ref[...]
    xc = cb_ref[...] + cw[3:4] * x
    for kk in range(CONV_WIDTH - 1):
        back = CONV_WIDTH - 1 - kk
        xc = xc + cw[kk:kk + 1] * xbuf[8 - back:8 - back + ts, :]
    xbuf[0:8, :] = x[ts - 8:, :]

    xcb = xc.astype(BF16)
    r = _sigmoid(_dot(xcb, gaw_ref[0]) + gab_ref[0])
    gi = _sigmoid(_dot(xcb, gxw_ref[0]) + gxb_ref[0])
    z = -lam_ref[...]
    softplus = jnp.maximum(z, 0.0) + jnp.log1p(jnp.exp(-jnp.abs(z)))
    log_a = (-LRU_C) * r * softplus
    a = jnp.exp(log_a)
    mult = jnp.sqrt(-jnp.tanh(log_a) * (a * a + 1.0))
    row = lax.broadcasted_iota(jnp.int32, (ts, 1), 0)
    mult = jnp.where(row + t * ts == 0, 1.0, mult)
    bv = mult * gi * xc

    d = 1
    while d < ts:
        keep = row >= d
        a_sh = jnp.where(keep, pltpu.roll(a, d, 0), 1.0)
        b_sh = jnp.where(keep, pltpu.roll(bv, d, 0), 0.0)
        bv = a * b_sh + bv
        a = a * a_sh
        d *= 2
    h = bv + a * h_ref[...]
    h_ref[...] = h[ts - 1:ts, :]
    o_ref[0] = (h * _gelu_tanh(y_ref[0])).astype(o_ref.dtype)


def rglru(u3, conv_w, conv_b, ga_w, ga_b, gx_w, gx_b, lam, *, ts):
    b, s, _ = u3.shape
    rec0 = EV_REC // LANES
    gate0 = EV_GATE // LANES
    cb = conv_b.reshape(1, LRU_WIDTH)
    gab = ga_b.reshape(LRU_BLOCKS, 1, LRU_BLOCK_W)
    gxb = gx_b.reshape(LRU_BLOCKS, 1, LRU_BLOCK_W)
    lam2 = lam.reshape(1, LRU_WIDTH)
    blk_w = pl.BlockSpec((1, LRU_BLOCK_W, LRU_BLOCK_W), lambda b_, n, t: (n, 0, 0))
    blk_b = pl.BlockSpec((1, 1, LRU_BLOCK_W), lambda b_, n, t: (n, 0, 0))
    vec = pl.BlockSpec((1, LANES), lambda b_, n, t: (0, n))
    return pl.pallas_call(
        functools.partial(_rglru_kernel, ts=ts),
        grid=(b, LRU_BLOCKS, s // ts),
        in_specs=[pl.BlockSpec((1, ts, LANES), lambda b_, n, t: (b_, t, rec0 + n)),
                  pl.BlockSpec((1, ts, LANES), lambda b_, n, t: (b_, t, gate0 + n)),
                  pl.BlockSpec((CONV_WIDTH, LANES), lambda b_, n, t: (0, n)),
                  vec, blk_w, blk_b, blk_w, blk_b, vec],
        out_specs=pl.BlockSpec((1, ts, LANES), lambda b_, n, t: (b_, t, n)),
        out_shape=jax.ShapeDtypeStruct((b, s, LRU_WIDTH), BF16),
        scratch_shapes=[pltpu.VMEM((ts + 8, LANES), F32), pltpu.VMEM((1, LANES), F32)],
        compiler_params=_cparams(("parallel", "parallel", "arbitrary")),
    )(u3, u3, conv_w, cb, ga_w, gab, gx_w, gxb, lam2)


def _ffn_kernel(x_ref, g_ref, w1_ref, w3_ref, w2_ref, o_ref, xn_ref):
    f = pl.program_id(1)

    @pl.when(f == 0)
    def _():
        x = x_ref[...]
        xn_ref[...] = _rms(x, g_ref[...]).astype(BF16)
        o_ref[...] = x

    xn = xn_ref[...]
    h1 = _dot(xn, w1_ref[...])
    h3 = _dot(xn, w3_ref[...])
    hh = (h1 * _sigmoid(h1) * h3).astype(BF16)
    o_ref[...] += _dot(hh, w2_ref[...])


def ffn_dense(x, g, w1, w3, w2, *, tm, tf):
    n, d = x.shape
    ff = w1.shape[1]
    assert n % tm == 0 and ff % tf == 0
    return pl.pallas_call(
        _ffn_kernel,
        grid=(n // tm, ff // tf),
        in_specs=[pl.BlockSpec((tm, d), lambda i, f: (i, 0)),
                  pl.BlockSpec((1, d), lambda i, f: (0, 0)),
                  pl.BlockSpec((d, tf), lambda i, f: (0, f)),
                  pl.BlockSpec((d, tf), lambda i, f: (0, f)),
                  pl.BlockSpec((tf, d), lambda i, f: (f, 0))],
        out_specs=pl.BlockSpec((tm, d), lambda i, f: (i, 0)),
        out_shape=jax.ShapeDtypeStruct((n, d), F32),
        scratch_shapes=[pltpu.VMEM((tm, d), BF16)],
        compiler_params=_cparams(("parallel", "arbitrary")),
    )(x, g.reshape(1, d), w1, w3, w2)


def _compress_kernel(kc_ref, pe_ref, w1_ref, b1_ref, w2_ref, o_ref, *, n_half):
    pe = pe_ref[...]
    half_k = n_half * NSA_DH
    for g in range(NSA_GROUPS):
        lo, hi = [], []
        for l in range(n_half):
            a = l * NSA_KV_DIM + g * NSA_DH
            piece = kc_ref[0, :, a:a + NSA_DH]
            lo.append((piece + pe[l:l + 1]).astype(BF16))
            hi.append((piece + pe[n_half + l:n_half + l + 1]).astype(BF16))
        z0 = _dot(jnp.concatenate(lo, axis=1), w1_ref[0:half_k, :])
        z1 = _dot(jnp.concatenate(hi, axis=1), w1_ref[half_k:2 * half_k, :])
        rows = z1.shape[0]
        pre = z0 + pltpu.roll(z1, rows - 1, 0) + b1_ref[...]
        o_ref[0, :, g * NSA_DH:(g + 1) * NSA_DH] = _dot(_gelu_tanh(pre).astype(BF16), w2_ref[...]).astype(o_ref.dtype)


def nsa_compress(kc3, pe, w1, b1, w2):
    b, nchunk, width = kc3.shape
    n_half = CMP_BLOCK // CMP_STRIDE
    assert CMP_BLOCK == 2 * CMP_STRIDE and width == CMP_STRIDE * NSA_KV_DIM
    full = lambda a: pl.BlockSpec(a.shape, lambda i: (0,) * a.ndim)
    b1r = b1.reshape(1, NSA_DH)
    return pl.pallas_call(
        functools.partial(_compress_kernel, n_half=CMP_STRIDE),
        grid=(b,),
        in_specs=[pl.BlockSpec((1, nchunk, width), lambda i: (i, 0, 0)),
                  full(pe), full(w1), full(b1r), full(w2)],
        out_specs=pl.BlockSpec((1, nchunk, NSA_KV_DIM), lambda i: (i, 0, 0)),
        out_shape=jax.ShapeDtypeStruct((b, nchunk, NSA_KV_DIM), BF16),
        compiler_params=_cparams(("parallel",)),
    )(kc3, pe, w1, b1r, w2)


def _rope128(x, c, s):
    return x * c + pltpu.roll(x, NSA_DH // 2, 1) * s


def _pack_bf16_pairs(x):
    w = x.shape[1] // 2
    bits = pltpu.bitcast(x.astype(BF16).astype(F32), jnp.uint32)
    word = bits[:, w:] | lax.shift_right_logical(bits[:, :w], jnp.uint32(16))
    return pltpu.bitcast(word, F32)


def _unpack_bf16_pairs(word):
    bits = pltpu.bitcast(word, jnp.uint32)
    lo = pltpu.bitcast(lax.shift_left(bits, jnp.uint32(16)), F32).astype(BF16)
    hi = pltpu.bitcast(bits & jnp.uint32(0xFFFF0000), F32).astype(BF16)
    return lo, hi


def _split3(x):
    hi = x.astype(BF16)
    r1 = x - hi.astype(F32)
    mid = r1.astype(BF16)
    lo = (r1 - mid.astype(F32)).astype(BF16)
    return hi, mid, lo


def _nsa_attn_kernel(q_ref, ks_ref, vs_ref, kw_ref, vw_ref, gt_ref, kc_ref, vc_ref,
                     cq_ref, sq_ref, ck_ref, sk_ref, ovl_ref, exp_ref, wb_ref,
                     o_ref, ksr, vsb, kwr, vwb, selb, m_ref, acc_ref,
                     *, tq, tk, seq, scale):
    qi = pl.program_id(2)
    hp = NSA_HPG
    n_sel = seq // SEL_BLOCK

    @pl.when(qi == 0)
    def _():
        ck, sk = ck_ref[...], sk_ref[...]
        ksr[...] = _rope128(ks_ref[0], ck, sk).astype(BF16)
        kwr[...] = _rope128(kw_ref[0], ck, sk).astype(BF16)
        vsb[...] = _values_and_ones(vs_ref[0])
        vwb[...] = _values_and_ones(vw_ref[0])

    q = q_ref[0] * scale
    cq, sq = cq_ref[...], sq_ref[...]
    heads = [q[:, p * NSA_DH:(p + 1) * NSA_DH] for p in range(hp)]
    qu = jnp.concatenate(heads, axis=0).astype(BF16)
    qr = jnp.concatenate([_rope128(h, cq, sq) for h in heads], axis=0).astype(BF16)
    t_row = qi * tq + lax.broadcasted_iota(jnp.int32, (tq, 1), 0)
    lane = lax.broadcasted_iota(jnp.int32, (1, LANES), 1)

    sc = _dot_nt(qu, kc_ref[0]).reshape(hp, tq, LANES)
    valid = (lane * CMP_STRIDE + (CMP_BLOCK - 1) <= t_row)[None]
    sm = jnp.where(valid, sc, NEG_INF)
    e = jnp.exp(sm - jnp.max(sm, axis=-1, keepdims=True))
    p = jnp.where(valid, e / jnp.sum(e, axis=-1, keepdims=True), 0.0)
    o_cmp = _dot(p.reshape(hp * tq, LANES).astype(BF16), vc_ref[0])

    psum = p[0]
    for i in range(1, hp):
        psum = psum + p[i]
    ovl = ovl_ref[...]
    imp = sum(_dot(part, ovl) for part in _split3(psum))
    cur = jnp.right_shift(t_row, SEL_BLOCK.bit_length() - 1)
    future = lane > cur
    forced = (lane == 0) | (lane == cur) | (lane == cur - 1)
    score = jnp.where(future, -1.0, jnp.where(forced, SEL_FORCE, imp))
    sc_t = score.T[0:n_sel, :]
    blk = lax.broadcasted_iota(jnp.int32, (n_sel, 1), 0)
    cnt = jnp.zeros((n_sel, tq), F32)
    for j in range(n_sel):
        other = sc_t[j:j + 1, :]
        beats = (other > sc_t) | ((other == sc_t) & (blk > j))
        cnt = cnt + jnp.where(beats, 1.0, 0.0)
    sel_t = jnp.where(cnt < SEL_COUNT, 1.0, 0.0)
    sel = jnp.concatenate([sel_t, jnp.zeros((LANES - n_sel, tq), F32)], axis=0).T.astype(BF16)
    picked = _dot(sel, exp_ref[...])
    kall = lax.broadcasted_iota(jnp.int32, (1, seq), 1)
    selb[...] = jnp.where((picked > 0.5) & (kall <= t_row), 0.0, NEG_INF)

    m_ref[...] = jnp.full(m_ref.shape, NEG_INF, F32)
    acc_ref[...] = jnp.zeros(acc_ref.shape, F32)
    for c in range(seq // tk):
        @pl.when(c * tk <= qi * tq + (tq - 1))
        def _():
            s = _dot_nt(qr, ksr[c * tk:(c + 1) * tk, :]).reshape(hp, tq, tk)
            s = (s + selb[:, c * tk:(c + 1) * tk][None]).reshape(hp * tq, tk)
            _softmax_step(s, vsb[c * tk:(c + 1) * tk, :], m_ref, acc_ref)
    o_sel = _softmax_result(acc_ref[...])

    span = WINDOW + tq
    start = pl.multiple_of(jnp.maximum(qi * tq - WINDOW, 0), tq)
    sw = _dot_nt(qr, kwr[pl.ds(start, span), :]).reshape(hp, tq, span) + wb_ref[0][None]
    ew = jnp.exp(sw - jnp.max(sw, axis=-1, keepdims=True))
    o_win = _softmax_result(_dot(ew.reshape(hp * tq, span).astype(BF16), vwb[pl.ds(start, span), :]))

    gates = _sigmoid(gt_ref[0])
    for i in range(hp):
        rows = slice(i * tq, (i + 1) * tq)
        o = (gates[:, 3 * i:3 * i + 1] * o_cmp[rows] + gates[:, 3 * i + 1:3 * i + 2] * o_sel[rows]
             + gates[:, 3 * i + 2:3 * i + 3] * o_win[rows])
        o_ref[0, :, i * NSA_DH:(i + 1) * NSA_DH] = o.astype(o_ref.dtype)


def _window_bias(tq):
    span = WINDOW + tq
    out = []
    for qi in range(WINDOW // tq + 1):
        start = max(qi * tq - WINDOW, 0)
        t = qi * tq + np.arange(tq)[:, None]
        kpos = start + np.arange(span)[None, :]
        out.append(np.where((kpos <= t) & (kpos > t - WINDOW), 0.0, NEG_INF))
    return jnp.asarray(np.stack(out), F32)


def nsa_attention(u3, k_cmp, v_cmp, rope_c, rope_s, ovl, expand, *, tq, tk):
    b, s, _ = u3.shape
    assert s // CMP_STRIDE == LANES and tq % SEL_BLOCK == 0 and WINDOW % tq == 0
    hp = NSA_HPG
    nwb = WINDOW // tq
    wbias = _window_bias(tq)
    col = lambda off: (lambda b_, g, i: (b_, 0, off // NSA_DH + g))
    seq_blk = lambda off: pl.BlockSpec((1, s, NSA_DH), col(off))
    full = lambda a: pl.BlockSpec(a.shape, lambda b_, g, i: (0,) * a.ndim)
    cmp_blk = pl.BlockSpec((1, LANES, NSA_DH), lambda b_, g, i: (b_, 0, g))
    rope_q = pl.BlockSpec((tq, NSA_DH), lambda b_, g, i: (i, 0))
    kern = functools.partial(_nsa_attn_kernel, tq=tq, tk=tk, seq=s, scale=NSA_DH ** -0.5)
    return pl.pallas_call(
        kern,
        grid=(b, NSA_GROUPS, s // tq),
        in_specs=[pl.BlockSpec((1, tq, hp * NSA_DH), lambda b_, g, i: (b_, i, g)),
                  seq_blk(OD_KS), seq_blk(OD_VS), seq_blk(OD_KW), seq_blk(OD_VW),
                  pl.BlockSpec((1, tq, LANES), lambda b_, g, i: (b_, i, OD_G // LANES + g)),
                  cmp_blk, cmp_blk, rope_q, rope_q, full(rope_c), full(rope_s), full(ovl), full(expand),
                  pl.BlockSpec((1, tq, WINDOW + tq), lambda b_, g, i: (jnp.minimum(i, nwb), 0, 0))],
        out_specs=pl.BlockSpec((1, tq, hp * NSA_DH), lambda b_, g, i: (b_, i, g)),
        out_shape=jax.ShapeDtypeStruct((b, s, NSA_Q_DIM), BF16),
        scratch_shapes=[pltpu.VMEM((s, NSA_DH), BF16), pltpu.VMEM((s, 2 * LANES), BF16)] * 2 + [
            pltpu.VMEM((tq, s), F32),
            pltpu.VMEM((hp * tq, LANES), F32), pltpu.VMEM((hp * tq, 2 * LANES), F32)],
        compiler_params=_cparams(("parallel", "parallel", "arbitrary")),
    )(u3, u3, u3, u3, u3, u3, k_cmp, v_cmp, rope_c, rope_s, rope_c, rope_s, ovl, expand, wbias)


def _router_kernel(x_ref, g_ref, wr_ref, br_ref, xn_ref, meta_ref, cnt_ref, *, tm):
    @pl.when(pl.program_id(0) == 0)
    def _():
        cnt_ref[...] = jnp.zeros(cnt_ref.shape, F32)

    xn = _rms(x_ref[...], g_ref[...])
    xn_ref[...] = _pack_bf16_pairs(xn)
    xh, xm, _ = _split3(xn)
    wh, wm, _ = _split3(wr_ref[...])
    logits = _dot(xh, wh) + _dot(xh, wm) + _dot(xm, wh) + br_ref[...]
    lane = lax.broadcasted_iota(jnp.int32, (1, LANES), 1).astype(F32)
    lg = jnp.where(lane < N_EXPERTS, logits, NEG_INF)
    m1 = jnp.max(lg, axis=-1, keepdims=True)
    e1 = jnp.min(jnp.where(lg == m1, lane, float(LANES)), axis=-1, keepdims=True)
    lg2 = jnp.where(lane == e1, NEG_INF, lg)
    m2 = jnp.max(lg2, axis=-1, keepdims=True)
    e2 = jnp.min(jnp.where(lg2 == m2, lane, float(LANES)), axis=-1, keepdims=True)
    ex = jnp.exp(m2 - m1)
    den = 1.0 + ex
    g1 = 1.0 / den
    g2 = ex / den
    oh = jnp.where((lane == e1) | (lane == e2), 1.0, 0.0)
    r = lax.broadcasted_iota(jnp.int32, (tm, tm), 0)
    c = lax.broadcasted_iota(jnp.int32, (tm, tm), 1)
    tri = jnp.where(r > c, 1.0, 0.0).astype(BF16)
    cum = _dot(tri, oh.astype(BF16)) + cnt_ref[0:1, :]
    pos1 = jnp.sum(jnp.where(lane == e1, cum, 0.0), axis=-1, keepdims=True)
    pos2 = jnp.sum(jnp.where(lane == e2, cum, 0.0), axis=-1, keepdims=True)
    cnt_ref[...] = cnt_ref[...] + jnp.sum(oh, axis=0, keepdims=True)
    meta = jnp.where(lane == 0, e1, 0.0)
    meta = jnp.where(lane == 1, e2, meta)
    meta = jnp.where(lane == 2, g1, meta)
    meta = jnp.where(lane == 3, g2, meta)
    meta = jnp.where(lane == 4, pos1, meta)
    meta = jnp.where(lane == 5, pos2, meta)
    meta_ref[...] = meta


def moe_router(x, g, wr, br, *, tm):
    n, d = x.shape
    return pl.pallas_call(
        functools.partial(_router_kernel, tm=tm),
        grid=(n // tm,),
        in_specs=[pl.BlockSpec((tm, d), lambda i: (i, 0)),
                  pl.BlockSpec((1, d), lambda i: (0, 0)),
                  pl.BlockSpec((d, LANES), lambda i: (0, 0)),
                  pl.BlockSpec((1, LANES), lambda i: (0, 0))],
        out_specs=[pl.BlockSpec((tm, d // 2), lambda i: (i, 0)),
                   pl.BlockSpec((tm, LANES), lambda i: (i, 0)),
                   pl.BlockSpec((8, LANES), lambda i: (0, 0))],
        out_shape=[jax.ShapeDtypeStruct((n, d // 2), F32),
                   jax.ShapeDtypeStruct((n, LANES), F32),
                   jax.ShapeDtypeStruct((8, LANES), F32)],
        compiler_params=_cparams(("arbitrary",)),
    )(x, g.reshape(1, d), wr, br)


def _gather_chunk(per_worker, row_bytes):
    best = 0
    for c in range(8, per_worker + 1, 8):
        if per_worker % c == 0 and c * row_bytes <= SC_GATHER_BYTES and c <= LANES:
            best = c
    assert best > 0, (per_worker, row_bytes)
    return best


def sc_gather_rows(table, idx):
    _, d = table.shape
    b = idx.shape[0]
    assert b % (8 * SC_WORKERS) == 0 and table.dtype.itemsize == 4
    per_w = b // SC_WORKERS
    chunk = _gather_chunk(per_w, d * 4)
    mesh = plsc.VectorSubcoreMesh(core_axis_name="c", subcore_axis_name="s",
                                  num_cores=SC_CORES, num_subcores=SC_SUBCORES)

    @functools.partial(
        pl.kernel, mesh=mesh,
        out_type=jax.ShapeDtypeStruct((b, d), table.dtype),
        scratch_types=[pltpu.VMEM((chunk,), jnp.int32), pltpu.VMEM((chunk, d), table.dtype),
                       pltpu.SemaphoreType.DMA])
    def gather(table_hbm, idx_hbm, out_hbm, idx_v, rows_v, sem):
        wid = lax.axis_index("s") * SC_CORES + lax.axis_index("c")
        base = wid * per_w

        @pl.loop(0, per_w // chunk)
        def _(c):
            off = pl.multiple_of(base + c * chunk, 8)
            pltpu.sync_copy(idx_hbm.at[pl.ds(off, chunk)], idx_v)
            pltpu.async_copy(table_hbm.at[idx_v], rows_v, sem).wait()
            pltpu.sync_copy(rows_v, out_hbm.at[pl.ds(off, chunk)])

    return gather(table, idx)


def _expert_kernel(te_ref, ts_ref, tr_ref, xp_hbm, *rest, nf):
    ns = MOE_WSPLIT
    w1_refs, w3_refs, w2_refs = rest[:ns], rest[ns:2 * ns], rest[2 * ns:3 * ns]
    yr_hbm, xb, acc, stage, in_sem, out_sem = rest[3 * ns:]
    t = pl.program_id(0)
    f = pl.program_id(1)
    rows = tr_ref[t]
    start = pl.multiple_of(ts_ref[t], MOE_ALIGN)
    n_chunks = MOE_TILE // MOE_COPY
    half = xb.shape[1] // 2

    def in_copy(ci):
        return pltpu.make_async_copy(xp_hbm.at[pl.ds(start + ci * MOE_COPY, MOE_COPY)],
                                     stage.at[ci % 2], in_sem.at[ci % 2])

    def out_copy(ci, first_row):
        return pltpu.make_async_copy(acc.at[pl.ds(ci * MOE_COPY, MOE_COPY)],
                                     yr_hbm.at[pl.ds(first_row + ci * MOE_COPY, MOE_COPY)], out_sem.at[0])

    def when_chunk_live(ci, fn):
        pl.when(ci * MOE_COPY < rows)(fn)

    def wait_outputs(tile):
        tile_rows = tr_ref[tile]
        first_row = pl.multiple_of(ts_ref[tile], MOE_ALIGN)
        for ci in range(n_chunks):
            pl.when(ci * MOE_COPY < tile_rows)(lambda ci=ci: out_copy(ci, first_row).wait())

    @pl.when((f == 0) & (rows > 0))
    def _():
        def unpack(ci):
            in_copy(ci).wait()
            lo, hi = _unpack_bf16_pairs(stage[ci % 2])
            xb[ci * MOE_COPY:(ci + 1) * MOE_COPY, :half] = lo
            xb[ci * MOE_COPY:(ci + 1) * MOE_COPY, half:] = hi

        when_chunk_live(0, lambda: in_copy(0).start())
        for ci in range(n_chunks):
            if ci + 1 < n_chunks:
                when_chunk_live(ci + 1, lambda ci=ci: in_copy(ci + 1).start())
            when_chunk_live(ci, lambda ci=ci: unpack(ci))

    @pl.when((f == 0) & (t > 0))
    def _():
        wait_outputs(jnp.maximum(t - 1, 0))

    @pl.when((f == 0) & (rows > 0))
    def _():
        acc[...] = jnp.zeros(acc.shape, F32)

    @pl.when(rows > 0)
    def _():
        w1 = jnp.concatenate([r[0, 0].astype(BF16) for r in w1_refs], axis=0)
        w3 = jnp.concatenate([r[0, 0].astype(BF16) for r in w3_refs], axis=0)
        w2 = jnp.concatenate([r[0, 0].astype(BF16) for r in w2_refs], axis=1)

        def chain(r0, size):
            rs = pl.ds(pl.multiple_of(r0, MOE_PIECES[-1]), size)
            xc = xb[rs, :]
            h1 = _dot(xc, w1)
            h3 = _dot(xc, w3)
            hh = (h1 * _sigmoid(h1) * h3).astype(BF16)
            acc[rs, :] += _dot(hh, w2)

        todo = (rows + (MOE_PIECES[-1] - 1)) // MOE_PIECES[-1] * MOE_PIECES[-1]
        off = jnp.int32(0)
        for size in MOE_PIECES:
            take = todo - off >= size

            @pl.when(take)
            def _(off=off, size=size):
                for c0 in range(0, size, MOE_CHAIN):
                    chain(off + c0, min(MOE_CHAIN, size))

            off = off + jnp.where(take, size, 0)

    @pl.when((f == nf - 1) & (rows > 0))
    def _():
        for ci in range(n_chunks):
            when_chunk_live(ci, lambda ci=ci: out_copy(ci, start).start())

    @pl.when((f == nf - 1) & (t == pl.num_programs(0) - 1))
    def _():
        wait_outputs(t)


def moe_experts(xp, tile_e, tile_start, tile_rows, w1, w3, w2, layer, *, tf):
    n_rows, half = xp.shape
    d = 2 * half
    n_tiles = tile_e.shape[0]
    ff = w1.shape[3]
    nf = ff // tf
    assert MOE_TILE % MOE_COPY == 0 and sum(MOE_PIECES) >= MOE_TILE

    def f_eff(t, f, tr):
        return jnp.where(tr[t] > 0, f, nf - 1)

    ns = MOE_WSPLIT
    up_specs = [pl.BlockSpec((1, 1, d // ns, tf), lambda t, f, te, ts, tr, k=k: (layer, te[t], k, f_eff(t, f, tr)))
                for k in range(ns)]
    down_specs = [pl.BlockSpec((1, 1, tf, d // ns), lambda t, f, te, ts, tr, k=k: (layer, te[t], f_eff(t, f, tr), k))
                  for k in range(ns)]
    grid_spec = pltpu.PrefetchScalarGridSpec(
        num_scalar_prefetch=3,
        grid=(n_tiles, nf),
        in_specs=[pl.BlockSpec(memory_space=pl.ANY)] + 2 * up_specs + down_specs,
        out_specs=pl.BlockSpec(memory_space=pl.ANY),
        scratch_shapes=[pltpu.VMEM((MOE_TILE, d), BF16), pltpu.VMEM((MOE_TILE, d), F32),
                        pltpu.VMEM((2, MOE_COPY, half), F32),
                        pltpu.SemaphoreType.DMA((2,)), pltpu.SemaphoreType.DMA((1,))],
    )
    return pl.pallas_call(
        functools.partial(_expert_kernel, nf=nf),
        grid_spec=grid_spec,
        out_shape=jax.ShapeDtypeStruct((n_rows, d), F32),
        compiler_params=_cparams(("arbitrary", "arbitrary")),
    )(tile_e, tile_start, tile_rows, xp, *([w1] * ns), *([w3] * ns), *([w2] * ns))


def _combine_kernel(x_ref, y1_ref, y2_ref, meta_ref, g_ref, o_ref, *, final_norm):
    meta = meta_ref[...]
    y = x_ref[...] + (meta[:, 2:3] * y1_ref[...] + meta[:, 3:4] * y2_ref[...])
    if final_norm:
        y = _rms(y, g_ref[...])
    o_ref[...] = y


def moe_combine(x, yg, meta, g, *, final_norm, tm):
    n, d = x.shape
    nb = n // tm
    row = pl.BlockSpec((tm, d), lambda i: (i, 0))
    return pl.pallas_call(
        functools.partial(_combine_kernel, final_norm=final_norm),
        grid=(nb,),
        in_specs=[row, row, pl.BlockSpec((tm, d), lambda i: (i + nb, 0)),
                  pl.BlockSpec((tm, LANES), lambda i: (i, 0)),
                  pl.BlockSpec((1, d), lambda i: (0, 0))],
        out_specs=row,
        out_shape=jax.ShapeDtypeStruct((n, d), F32),
        compiler_params=_cparams(("parallel",)),
    )(x, yg, yg, meta, g.reshape(1, d))


def _mla_rope_tables(seq):
    half = MLA_ROPE // 2
    pos = jnp.arange(seq, dtype=F32)
    inv = ROPE_THETA ** (-jnp.arange(0, MLA_ROPE, 2, dtype=F32) / MLA_ROPE)
    ang = pos[:, None] * inv[None, :]
    cos, sin = jnp.cos(ang), jnp.sin(ang)
    z = jnp.zeros((seq, half), F32)
    pad = jnp.zeros((seq, LANES - MLA_ROPE), F32)
    c = jnp.concatenate([cos, cos, pad], axis=1)
    s_lo = jnp.concatenate([-sin, z, pad], axis=1)
    s_hi = jnp.concatenate([z, sin, pad], axis=1)
    return c, s_lo, s_hi


def _nsa_rope_tables(seq):
    pos = jnp.arange(seq, dtype=F32)
    inv = ROPE_THETA ** (-jnp.arange(0, NSA_DH, 2, dtype=F32) / NSA_DH)
    ang = pos[:, None] * inv[None, :]
    cos, sin = jnp.cos(ang), jnp.sin(ang)
    return jnp.concatenate([cos, cos], axis=1), jnp.concatenate([-sin, sin], axis=1)


def _selection_constants(seq):
    n_sel = seq // SEL_BLOCK
    nc = (seq - CMP_BLOCK) // CMP_STRIDE + 1
    cmp_start = np.arange(LANES) * CMP_STRIDE
    sel_start = np.arange(LANES) * SEL_BLOCK
    ovl = ((cmp_start[:, None] < sel_start[None, :] + SEL_BLOCK) &
           (cmp_start[:, None] + CMP_BLOCK > sel_start[None, :]))
    ovl &= (np.arange(LANES)[:, None] < nc) & (np.arange(LANES)[None, :] < n_sel)
    expand = (np.arange(seq)[None, :] // SEL_BLOCK == np.arange(LANES)[:, None])
    return jnp.asarray(ovl, BF16), jnp.asarray(expand, BF16)


def even_layer(x, seq, p):
    n = x.shape[0]
    b = n // seq
    (norm_mix, w_in, q_norm, w_q_up, kv_norm, w_kv_up, conv_w, conv_b, ga_w, ga_b, gx_w, gx_b,
     lam, w_out, norm_ffn, w1, w3, w2) = p
    d = D_MODEL
    o1 = MLA_Q_LORA + MLA_KV_LORA
    o2 = o1 + MLA_ROPE
    w_pack = jnp.concatenate(
        [w_in[:, :o1], w_in[:, o2:], w_in[:, o1:o2], jnp.zeros((d, LANES - MLA_ROPE), F32)], axis=1).astype(BF16)
    u = norm_matmul(x, norm_mix, w_pack, tm=1024, tn=EV_PACKED // 3)

    wq = w_q_up.reshape(MLA_Q_LORA, MLA_HEADS, MLA_NOPE + MLA_ROPE)
    wq = jnp.pad(wq, ((0, 0), (0, 0), (0, MLA_QK_PAD - MLA_NOPE - MLA_ROPE)))
    wq = wq.reshape(MLA_Q_LORA, MLA_HEADS * MLA_QK_PAD).astype(BF16)
    wkv = w_kv_up.reshape(MLA_KV_LORA, MLA_HEADS, MLA_NOPE + MLA_V)
    wk = wkv[:, :, :MLA_NOPE].reshape(MLA_KV_LORA, MLA_HEADS * MLA_NOPE).astype(BF16)
    wv = wkv[:, :, MLA_NOPE:].reshape(MLA_KV_LORA, MLA_HEADS * MLA_V).astype(BF16)
    rc, rlo, rhi = _mla_rope_tables(seq)
    q, k, v = mla_up(u, q_norm, kv_norm, wq, wk, wv, rc, rlo, rhi, seq=seq, tm=512)
    o_mla = mla_attention(q.reshape(b, seq, -1), k.reshape(b, seq, -1), v.reshape(b, seq, -1), t=512)

    o_rec = rglru(u.reshape(b, seq, EV_PACKED), conv_w, conv_b, ga_w.astype(BF16), ga_b,
                  gx_w.astype(BF16), gx_b, lam, ts=512)
    x = matmul_residual([o_mla.reshape(n, -1), o_rec.reshape(n, -1)], w_out.astype(BF16), x, tm=512, tn=D_MODEL)
    return ffn_dense(x, norm_ffn, w1.astype(BF16), w3.astype(BF16), w2.astype(BF16), tm=512, tf=512)


def _moe_dispatch(meta, counts, n):
    e1 = meta[:, 0].astype(jnp.int32)
    e2 = meta[:, 1].astype(jnp.int32)
    pos1 = meta[:, 4].astype(jnp.int32)
    pos2 = meta[:, 5].astype(jnp.int32)
    cnt = counts[0, :N_EXPERTS].astype(jnp.int32)
    span = (cnt + MOE_ALIGN - 1) // MOE_ALIGN * MOE_ALIGN
    row0 = jnp.cumsum(span) - span
    row_unit = 64 * SC_WORKERS
    n_rows = -(-(n * TOP_K + N_EXPERTS * MOE_ALIGN + MOE_TILE) // row_unit) * row_unit
    d1 = row0[e1] + pos1
    d2 = row0[e2] + pos2
    n_tiles = (n * TOP_K) // MOE_TILE + N_EXPERTS
    tiles_e = (cnt + MOE_TILE - 1) // MOE_TILE
    tend = jnp.cumsum(tiles_e)
    tbeg = tend - tiles_e
    tid = jnp.arange(n_tiles, dtype=jnp.int32)
    te = jnp.minimum(jnp.searchsorted(tend, tid, side='right'), N_EXPERTS - 1).astype(jnp.int32)
    used = tid < tend[-1]
    first = (tid - tbeg[te]) * MOE_TILE
    rows = jnp.where(used, jnp.clip(cnt[te] - first, 0, MOE_TILE), 0).astype(jnp.int32)
    tstart = jnp.where(used, row0[te] + first, 0).astype(jnp.int32)
    last_e = te[jnp.maximum(tend[-1] - 1, 0)]
    te = jnp.where(used, te, last_e).astype(jnp.int32)
    tok = jnp.arange(n, dtype=jnp.int32)
    row_tok = (jnp.arange(n_rows, dtype=jnp.int32) % n).at[d1].set(tok).at[d2].set(tok)
    return d1, d2, row_tok, te, tstart, rows


def odd_layer(x, seq, p, experts, final_g):
    n = x.shape[0]
    b = n // seq
    (norm_mix, w_in, ck_pe, ck_w1, ck_b1, ck_w2, cv_pe, cv_w1, cv_b1, cv_w2, w_out, norm_ffn,
     router_w, router_b) = p
    ew1, ew3, ew2, layer = experts
    d = D_MODEL
    wg = w_in[:, OD_G:].reshape(d, NSA_GROUPS, NSA_HPG * 3)
    wg = jnp.pad(wg, ((0, 0), (0, 0), (0, LANES - NSA_HPG * 3))).reshape(d, NSA_GROUPS * LANES)
    w_pack = jnp.concatenate([w_in[:, :OD_G], wg], axis=1).astype(BF16)
    u = norm_matmul(x, norm_mix, w_pack, tm=1024, tn=512)
    u3 = u.reshape(b, seq, OD_PACKED)

    nchunk = seq // CMP_STRIDE
    kc3 = u3[:, :, OD_KC:OD_KC + NSA_KV_DIM].reshape(b, nchunk, CMP_STRIDE * NSA_KV_DIM)
    vc3 = u3[:, :, OD_VC:OD_VC + NSA_KV_DIM].reshape(b, nchunk, CMP_STRIDE * NSA_KV_DIM)
    k_cmp = nsa_compress(kc3, ck_pe, ck_w1.astype(BF16), ck_b1, ck_w2.astype(BF16))
    v_cmp = nsa_compress(vc3, cv_pe, cv_w1.astype(BF16), cv_b1, cv_w2.astype(BF16))
    rc, rs = _nsa_rope_tables(seq)
    ovl, expand = _selection_constants(seq)
    o = nsa_attention(u3, k_cmp, v_cmp, rc, rs, ovl, expand, tq=256, tk=512)
    x = matmul_residual([o.reshape(n, -1)], w_out.astype(BF16), x, tm=512, tn=D_MODEL)

    wr = jnp.pad(router_w, ((0, 0), (0, LANES - N_EXPERTS)))
    br = jnp.pad(router_b, (0, LANES - N_EXPERTS)).reshape(1, LANES)
    xp, meta, counts = moe_router(x, norm_ffn, wr, br, tm=512)
    d1, d2, row_tok, te, tstart, rows = _moe_dispatch(meta, counts, n)
    yr = moe_experts(sc_gather_rows(xp, row_tok), te, tstart, rows, ew1, ew3, ew2, layer, tf=256)
    yg = sc_gather_rows(yr, jnp.concatenate([d1, d2]))
    g = final_g if final_g is not None else norm_ffn
    return moe_combine(x, yg, meta, g, final_norm=final_g is not None, tm=512)


def kernel(x, ev_norm_mix, ev_w_in, ev_q_norm, ev_w_q_up, ev_kv_norm, ev_w_kv_up, ev_conv_w, ev_conv_b, ev_gate_a_w, ev_gate_a_b, ev_gate_x_w, ev_gate_x_b, ev_lru_lambda, ev_w_out, ev_norm_ffn, ev_ffn_w1, ev_ffn_w3, ev_ffn_w2, od_norm_mix, od_w_in, od_cmp_k_pe, od_cmp_k_w1, od_cmp_k_b1, od_cmp_k_w2, od_cmp_v_pe, od_cmp_v_w1, od_cmp_v_b1, od_cmp_v_w2, od_w_out, od_norm_ffn, od_router_w, od_router_b, od_exp_w1, od_exp_w3, od_exp_w2, final_norm):
    bsz, seq, d = x.shape
    ev = (ev_norm_mix, ev_w_in, ev_q_norm, ev_w_q_up, ev_kv_norm, ev_w_kv_up, ev_conv_w, ev_conv_b,
          ev_gate_a_w, ev_gate_a_b, ev_gate_x_w, ev_gate_x_b, ev_lru_lambda, ev_w_out, ev_norm_ffn,
          ev_ffn_w1, ev_ffn_w3, ev_ffn_w2)
    od = (od_norm_mix, od_w_in, od_cmp_k_pe, od_cmp_k_w1, od_cmp_k_b1, od_cmp_k_w2, od_cmp_v_pe,
          od_cmp_v_w1, od_cmp_v_b1, od_cmp_v_w2, od_w_out, od_norm_ffn, od_router_w, od_router_b)
    h = x.reshape(bsz * seq, d)
    for layer in range(DEPTH):
        i = layer // 2
        if layer % 2 == 0:
            h = even_layer(h, seq, tuple(a[i] for a in ev))
        else:
            h = odd_layer(h, seq, tuple(a[i] for a in od), (od_exp_w1, od_exp_w3, od_exp_w2, i),
                          final_norm if layer == DEPTH - 1 else None)
    return h.reshape(bsz, seq, d)
```

```python
import functools
import math

import numpy as np
import jax
import jax.numpy as jnp
from jax import lax
from jax.experimental import pallas as pl
from jax.experimental.pallas import tpu as pltpu
from jax.experimental.pallas import tpu_sc as plsc

F32 = jnp.float32
BF16 = jnp.bfloat16

D_MODEL = 2048
DEPTH = 4
RMS_EPS = 1e-6
ROPE_THETA = 10000.0
NEG_INF = -1e30
LOG2E = math.log2(math.e)

MLA_HEADS = 8
MLA_Q_LORA = 768
MLA_KV_LORA = 512
MLA_NOPE = 128
MLA_ROPE = 64
MLA_V = 128
MLA_QK_PAD = 256
MLA_ROW_GROUPS = 2

LRU_WIDTH = D_MODEL // 2
LRU_BLOCKS = 8
LRU_BLOCK_W = LRU_WIDTH // LRU_BLOCKS
LRU_C = 8.0
CONV_WIDTH = 4

NSA_HEADS = 16
NSA_GROUPS = 4
NSA_HPG = NSA_HEADS // NSA_GROUPS
NSA_DH = D_MODEL // NSA_HEADS
CMP_BLOCK = 32
CMP_STRIDE = 16
SEL_BLOCK = 64
SEL_COUNT = 16
SEL_FORCE = 1e4
WINDOW = 512
NSA_Q_DIM = NSA_HEADS * NSA_DH
NSA_KV_DIM = NSA_GROUPS * NSA_DH

DENSE_FF = 5632
N_EXPERTS = 8
TOP_K = 2
EXPERT_FF = 7168

LANES = 128
SC_CORES = 2
SC_SUBCORES = 16
SC_WORKERS = SC_CORES * SC_SUBCORES
SC_GATHER_BYTES = 256 * 1024
VMEM_LIMIT = 60 * 1024 * 1024

EV_CQ = 0
EV_CKV = MLA_Q_LORA
EV_REC = MLA_Q_LORA + MLA_KV_LORA
EV_GATE = EV_REC + LRU_WIDTH
EV_PE = EV_GATE + LRU_WIDTH
EV_PACKED = EV_PE + LANES

OD_Q = 0
OD_KC = NSA_Q_DIM
OD_VC = OD_KC + NSA_KV_DIM
OD_KS = OD_VC + NSA_KV_DIM
OD_VS = OD_KS + NSA_KV_DIM
OD_KW = OD_VS + NSA_KV_DIM
OD_VW = OD_KW + NSA_KV_DIM
OD_G = OD_VW + NSA_KV_DIM
OD_PACKED = OD_G + NSA_GROUPS * LANES

MOE_TILE = 2688
MOE_ALIGN = 16
MOE_COPY = 384
MOE_PIECES = (1024, 1024, 512, 256, 128)
MOE_CHAIN = 512
MOE_WSPLIT = 1


def _cparams(sem):
    return pltpu.CompilerParams(dimension_semantics=sem, vmem_limit_bytes=VMEM_LIMIT)


def _rms(x, g):
    ms = jnp.mean(x * x, axis=-1, keepdims=True)
    return x * lax.rsqrt(ms + RMS_EPS) * g


def _sigmoid(x):
    return 1.0 / (1.0 + jnp.exp(-x))


def _gelu_tanh(x):
    return 0.5 * x * (1.0 + jnp.tanh(math.sqrt(2.0 / math.pi) * (x + 0.044715 * (x * x * x))))


def _dot(a, b):
    return jnp.dot(a, b, preferred_element_type=F32)


def _dot_nt(a, b):
    return lax.dot_general(a, b, (((1,), (1,)), ((), ())), preferred_element_type=F32)


def _norm_mm_kernel(x_ref, g_ref, w_ref, o_ref, xn_ref):
    @pl.when(pl.program_id(1) == 0)
    def _():
        xn_ref[...] = _rms(x_ref[...], g_ref[...]).astype(BF16)

    o_ref[...] = _dot(xn_ref[...], w_ref[...]).astype(o_ref.dtype)


def norm_matmul(x, g, w, *, tm, tn, out_dtype=F32):
    n, k = x.shape
    m = w.shape[1]
    assert n % tm == 0 and m % tn == 0
    return pl.pallas_call(
        _norm_mm_kernel,
        grid=(n // tm, m // tn),
        in_specs=[pl.BlockSpec((tm, k), lambda i, j: (i, 0)),
                  pl.BlockSpec((1, k), lambda i, j: (0, 0)),
                  pl.BlockSpec((k, tn), lambda i, j: (0, j))],
        out_specs=pl.BlockSpec((tm, tn), lambda i, j: (i, j)),
        out_shape=jax.ShapeDtypeStruct((n, m), out_dtype),
        scratch_shapes=[pltpu.VMEM((tm, k), BF16)],
        compiler_params=_cparams(("parallel", "arbitrary")),
    )(x, g.reshape(1, k), w)


def _mm_res_kernel(*refs, n_in):
    xs = refs[:n_in]
    ws = refs[n_in:2 * n_in]
    res_ref = refs[2 * n_in]
    o_ref = refs[2 * n_in + 1]
    acc = res_ref[...]
    for x_ref, w_ref in zip(xs, ws):
        acc = acc + _dot(x_ref[...], w_ref[...])
    o_ref[...] = acc


def matmul_residual(xs, w, res, *, tm, tn):
    n = res.shape[0]
    m = w.shape[1]
    n_in = len(xs)
    in_specs = [pl.BlockSpec((tm, x.shape[1]), lambda i, j: (i, 0)) for x in xs]
    row = 0
    for x in xs:
        kx = x.shape[1]
        assert row % kx == 0
        in_specs.append(pl.BlockSpec((kx, tn), lambda i, j, rb=row // kx: (rb, j)))
        row += kx
    assert row == w.shape[0]
    in_specs.append(pl.BlockSpec((tm, tn), lambda i, j: (i, j)))
    return pl.pallas_call(
        functools.partial(_mm_res_kernel, n_in=n_in),
        grid=(n // tm, m // tn),
        in_specs=in_specs,
        out_specs=pl.BlockSpec((tm, tn), lambda i, j: (i, j)),
        out_shape=jax.ShapeDtypeStruct((n, m), F32),
        compiler_params=_cparams(("parallel", "arbitrary")),
    )(*xs, *([w] * n_in), res)


def _rope64(x, c, s_lo, s_hi):
    return x * c + pltpu.roll(x, 96, 1) * s_lo + pltpu.roll(x, 32, 1) * s_hi


def _mla_up_kernel(u_ref, pe_ref, qg_ref, kvg_ref, wq_ref, wk_ref, wv_ref,
                   c_ref, slo_ref, shi_ref, q_ref, k_ref, v_ref):
    u = u_ref[...]
    c, s_lo, s_hi = c_ref[...], slo_ref[...], shi_ref[...]
    qn = _rms(u[:, EV_CQ:EV_CQ + MLA_Q_LORA], qg_ref[...]).astype(BF16)
    kvn = _rms(u[:, EV_CKV:EV_CKV + MLA_KV_LORA], kvg_ref[...]).astype(BF16)
    q = _dot(qn, wq_ref[...]) * ((MLA_NOPE + MLA_ROPE) ** -0.5 * LOG2E)
    kn = _dot(kvn, wk_ref[...])
    v_ref[...] = _dot(kvn, wv_ref[...]).astype(v_ref.dtype)
    kpe = _rope64(pe_ref[...], c, s_lo, s_hi).astype(k_ref.dtype)
    for h in range(MLA_HEADS):
        a = h * MLA_QK_PAD
        q_ref[:, a:a + LANES] = q[:, a:a + LANES].astype(q_ref.dtype)
        q_ref[:, a + LANES:a + 2 * LANES] = _rope64(q[:, a + LANES:a + 2 * LANES], c, s_lo, s_hi).astype(q_ref.dtype)
        k_ref[:, a:a + LANES] = kn[:, h * LANES:(h + 1) * LANES].astype(k_ref.dtype)
        k_ref[:, a + LANES:a + 2 * LANES] = kpe


def mla_up(u, q_norm, kv_norm, wq, wk, wv, rope_c, rope_slo, rope_shi, *, seq, tm):
    n = u.shape[0]
    hq = MLA_HEADS * MLA_QK_PAD
    hv = MLA_HEADS * MLA_V
    ab = EV_REC
    assert seq % tm == 0 and EV_PE % LANES == 0
    nsb = seq // tm
    row_spec = pl.BlockSpec((tm, LANES), lambda i: (i % nsb, 0))
    full = lambda a: pl.BlockSpec(a.shape, lambda i: (0,) * a.ndim)
    qg = q_norm.reshape(1, -1)
    kvg = kv_norm.reshape(1, -1)
    return pl.pallas_call(
        _mla_up_kernel,
        grid=(n // tm,),
        in_specs=[pl.BlockSpec((tm, ab), lambda i: (i, 0)),
                  pl.BlockSpec((tm, LANES), lambda i: (i, EV_PE // LANES)),
                  full(qg), full(kvg), full(wq), full(wk), full(wv),
                  row_spec, row_spec, row_spec],
        out_specs=[pl.BlockSpec((tm, hq), lambda i: (i, 0)),
                   pl.BlockSpec((tm, hq), lambda i: (i, 0)),
                   pl.BlockSpec((tm, hv), lambda i: (i, 0))],
        out_shape=[jax.ShapeDtypeStruct((n, hq), BF16),
                   jax.ShapeDtypeStruct((n, hq), BF16),
                   jax.ShapeDtypeStruct((n, hv), BF16)],
        compiler_params=_cparams(("parallel",)),
    )(u, u, qg, kvg, wq, wk, wv, rope_c, rope_slo, rope_shi)


def _values_and_ones(v):
    return jnp.concatenate([v.astype(BF16), jnp.ones(v.shape, BF16)], axis=1)


def _softmax_step(s, v1, m_ref, acc_ref):
    m_prev = m_ref[...]
    m_new = jnp.maximum(m_prev, jnp.max(s, axis=-1, keepdims=True))
    alpha = jnp.exp2(m_prev - m_new)
    p = jnp.exp2(s - pltpu.repeat(m_new, s.shape[1] // LANES, axis=1))
    acc_ref[...] = pltpu.repeat(alpha, 2, axis=1) * acc_ref[...] + _dot(p.astype(BF16), v1)
    m_ref[...] = m_new


def _softmax_result(acc):
    return acc[:, :LANES] / acc[:, LANES:]


def _mla_attn_kernel(q_ref, k_ref, v_ref, cb_ref, o_ref, v1, m_ref, acc_ref, *, t, n_chunks):
    qi = pl.program_id(2)

    @pl.when(qi == 0)
    def _():
        v1[...] = _values_and_ones(v_ref[0])

    m_ref[...] = jnp.full(m_ref.shape, NEG_INF, F32)
    acc_ref[...] = jnp.zeros(acc_ref.shape, F32)
    rg = t // MLA_ROW_GROUPS

    def chunk(c, masked):
        for i in range(MLA_ROW_GROUPS):
            rows = pl.ds(i * rg, rg)
            s = _dot_nt(q_ref[0, i * rg:(i + 1) * rg, :], k_ref[0, c * t:(c + 1) * t, :])
            if masked:
                s = s + cb_ref[i * rg:(i + 1) * rg, :]
            _softmax_step(s, v1[c * t:(c + 1) * t, :], m_ref.at[rows], acc_ref.at[rows])

    for c in range(n_chunks):
        pl.when(c < qi)(functools.partial(chunk, c, False))
        pl.when(c == qi)(functools.partial(chunk, c, True))
    o_ref[0] = _softmax_result(acc_ref[...]).astype(o_ref.dtype)


def mla_attention(q, k, v, *, t):
    b, s, _ = q.shape
    causal = jnp.asarray(np.where(np.arange(t)[None, :] <= np.arange(t)[:, None], 0.0, NEG_INF), F32)
    kern = functools.partial(_mla_attn_kernel, t=t, n_chunks=s // t)
    return pl.pallas_call(
        kern,
        grid=(b, MLA_HEADS, s // t),
        in_specs=[pl.BlockSpec((1, t, MLA_QK_PAD), lambda b_, h, i: (b_, i, h)),
                  pl.BlockSpec((1, s, MLA_QK_PAD), lambda b_, h, i: (b_, 0, h)),
                  pl.BlockSpec((1, s, MLA_V), lambda b_, h, i: (b_, 0, h)),
                  pl.BlockSpec((t, t), lambda b_, h, i: (0, 0))],
        out_specs=pl.BlockSpec((1, t, MLA_V), lambda b_, h, i: (b_, i, h)),
        out_shape=jax.ShapeDtypeStruct((b, s, MLA_HEADS * MLA_V), BF16),
        scratch_shapes=[pltpu.VMEM((s, 2 * LANES), BF16), pltpu.VMEM((t, LANES), F32),
                        pltpu.VMEM((t, 2 * LANES), F32)],
        compiler_params=_cparams(("parallel", "parallel", "arbitrary")),
    )(q, k, v, causal)


def _rglru_kernel(x_ref, y_ref, cw_ref, cb_ref, gaw_ref, gab_ref, gxw_ref, gxb_ref, lam_ref,
                  o_ref, xbuf, h_ref, *, ts):
    t = pl.program_id(2)

    @pl.when(t == 0)
    def _():
        xbuf[0:8, :] = jnp.zeros((8, LANES), F32)
        h_ref[...] = jnp.zeros(h_ref.shape, F32)

    x = x_ref[0]
    xbuf[8:, :] = x
    cw = cw_ref[...]
    xc = cb_ref[...] + cw[3:4] * x
    for kk in range(CONV_WIDTH - 1):
        back = CONV_WIDTH - 1 - kk
        xc = xc + cw[kk:kk + 1] * xbuf[8 - back:8 - back + ts, :]
    xbuf[0:8, :] = x[ts - 8:, :]

    xcb = xc.astype(BF16)
    r = _sigmoid(_dot(xcb, gaw_ref[0]) + gab_ref[0])
    gi = _sigmoid(_dot(xcb, gxw_ref[0]) + gxb_ref[0])
    z = -lam_ref[...]
    softplus = jnp.maximum(z, 0.0) + jnp.log1p(jnp.exp(-jnp.abs(z)))
    log_a = (-LRU_C) * r * softplus
    a = jnp.exp(log_a)
    mult = jnp.sqrt(-jnp.tanh(log_a) * (a * a + 1.0))
    row = lax.broadcasted_iota(jnp.int32, (ts, 1), 0)
    mult = jnp.where(row + t * ts == 0, 1.0, mult)
    bv = mult * gi * xc

    d = 1
    while d < ts:
        keep = row >= d
        a_sh = jnp.where(keep, pltpu.roll(a, d, 0), 1.0)
        b_sh = jnp.where(keep, pltpu.roll(bv, d, 0), 0.0)
        bv = a * b_sh + bv
        a = a * a_sh
        d *= 2
    h = bv + a * h_ref[...]
    h_ref[...] = h[ts - 1:ts, :]
    o_ref[0] = (h * _gelu_tanh(y_ref[0])).astype(o_ref.dtype)


def rglru(u3, conv_w, conv_b, ga_w, ga_b, gx_w, gx_b, lam, *, ts):
    b, s, _ = u3.shape
    rec0 = EV_REC // LANES
    gate0 = EV_GATE // LANES
    cb = conv_b.reshape(1, LRU_WIDTH)
    gab = ga_b.reshape(LRU_BLOCKS, 1, LRU_BLOCK_W)
    gxb = gx_b.reshape(LRU_BLOCKS, 1, LRU_BLOCK_W)
    lam2 = lam.reshape(1, LRU_WIDTH)
    blk_w = pl.BlockSpec((1, LRU_BLOCK_W, LRU_BLOCK_W), lambda b_, n, t: (n, 0, 0))
    blk_b = pl.BlockSpec((1, 1, LRU_BLOCK_W), lambda b_, n, t: (n, 0, 0))
    vec = pl.BlockSpec((1, LANES), lambda b_, n, t: (0, n))
    return pl.pallas_call(
        functools.partial(_rglru_kernel, ts=ts),
        grid=(b, LRU_BLOCKS, s // ts),
        in_specs=[pl.BlockSpec((1, ts, LANES), lambda b_, n, t: (b_, t, rec0 + n)),
                  pl.BlockSpec((1, ts, LANES), lambda b_, n, t: (b_, t, gate0 + n)),
                  pl.BlockSpec((CONV_WIDTH, LANES), lambda b_, n, t: (0, n)),
                  vec, blk_w, blk_b, blk_w, blk_b, vec],
        out_specs=pl.BlockSpec((1, ts, LANES), lambda b_, n, t: (b_, t, n)),
        out_shape=jax.ShapeDtypeStruct((b, s, LRU_WIDTH), BF16),
        scratch_shapes=[pltpu.VMEM((ts + 8, LANES), F32), pltpu.VMEM((1, LANES), F32)],
        compiler_params=_cparams(("parallel", "parallel", "arbitrary")),
    )(u3, u3, conv_w, cb, ga_w, gab, gx_w, gxb, lam2)


def _ffn_kernel(x_ref, g_ref, w1_ref, w3_ref, w2_ref, o_ref, xn_ref):
    f = pl.program_id(1)

    @pl.when(f == 0)
    def _():
        x = x_ref[...]
        xn_ref[...] = _rms(x, g_ref[...]).astype(BF16)
        o_ref[...] = x

    xn = xn_ref[...]
    h1 = _dot(xn, w1_ref[...])
    h3 = _dot(xn, w3_ref[...])
    hh = (h1 * _sigmoid(h1) * h3).astype(BF16)
    o_ref[...] += _dot(hh, w2_ref[...])


def ffn_dense(x, g, w1, w3, w2, *, tm, tf):
    n, d = x.shape
    ff = w1.shape[1]
    assert n % tm == 0 and ff % tf == 0
    return pl.pallas_call(
        _ffn_kernel,
        grid=(n // tm, ff // tf),
        in_specs=[pl.BlockSpec((tm, d), lambda i, f: (i, 0)),
                  pl.BlockSpec((1, d), lambda i, f: (0, 0)),
                  pl.BlockSpec((d, tf), lambda i, f: (0, f)),
                  pl.BlockSpec((d, tf), lambda i, f: (0, f)),
                  pl.BlockSpec((tf, d), lambda i, f: (f, 0))],
        out_specs=pl.BlockSpec((tm, d), lambda i, f: (i, 0)),
        out_shape=jax.ShapeDtypeStruct((n, d), F32),
        scratch_shapes=[pltpu.VMEM((tm, d), BF16)],
        compiler_params=_cparams(("parallel", "arbitrary")),
    )(x, g.reshape(1, d), w1, w3, w2)


def _compress_kernel(*refs, n_half):
    kc_refs = refs[:NSA_GROUPS]
    pe_ref, w1_ref, b1_ref, w2_ref, o_ref = refs[NSA_GROUPS:]
    pe = pe_ref[...]
    half_k = n_half * NSA_DH
    nchunk = o_ref.shape[1]
    for g in range(NSA_GROUPS):
        lo, hi = [], []
        for l in range(n_half):
            piece = kc_refs[g][0, pl.ds(l, nchunk, stride=n_half), :]
            lo.append((piece + pe[l:l + 1]).astype(BF16))
            hi.append((piece + pe[n_half + l:n_half + l + 1]).astype(BF16))
        z0 = _dot(jnp.concatenate(lo, axis=1), w1_ref[0:half_k, :])
        z1 = _dot(jnp.concatenate(hi, axis=1), w1_ref[half_k:2 * half_k, :])
        rows = z1.shape[0]
        pre = z0 + pltpu.roll(z1, rows - 1, 0) + b1_ref[...]
        o_ref[0, :, g * NSA_DH:(g + 1) * NSA_DH] = _dot(_gelu_tanh(pre).astype(BF16), w2_ref[...]).astype(o_ref.dtype)


def nsa_compress(u3, col, pe, w1, b1, w2):
    b, s, _ = u3.shape
    nchunk = s // CMP_STRIDE
    assert CMP_BLOCK == 2 * CMP_STRIDE and col % NSA_KV_DIM == 0
    full = lambda a: pl.BlockSpec(a.shape, lambda i: (0,) * a.ndim)
    b1r = b1.reshape(1, NSA_DH)
    return pl.pallas_call(
        functools.partial(_compress_kernel, n_half=CMP_STRIDE),
        grid=(b,),
        in_specs=[pl.BlockSpec((1, s, NSA_DH), lambda i, g=g: (i, 0, col // NSA_DH + g)) for g in range(NSA_GROUPS)]
        + [full(pe), full(w1), full(b1r), full(w2)],
        out_specs=pl.BlockSpec((1, nchunk, NSA_KV_DIM), lambda i: (i, 0, 0)),
        out_shape=jax.ShapeDtypeStruct((b, nchunk, NSA_KV_DIM), BF16),
        compiler_params=_cparams(("parallel",)),
    )(*([u3] * NSA_GROUPS), pe, w1, b1r, w2)


def _rope128(x, c, s):
    return x * c + pltpu.roll(x, NSA_DH // 2, 1) * s


def _pack_bf16_pairs(x):
    w = x.shape[1] // 2
    bits = pltpu.bitcast(x.astype(BF16).astype(F32), jnp.uint32)
    word = bits[:, w:] | lax.shift_right_logical(bits[:, :w], jnp.uint32(16))
    return pltpu.bitcast(word, F32)


def _unpack_bf16_pairs(word):
    bits = pltpu.bitcast(word, jnp.uint32)
    lo = pltpu.bitcast(lax.shift_left(bits, jnp.uint32(16)), F32).astype(BF16)
    hi = pltpu.bitcast(bits & jnp.uint32(0xFFFF0000), F32).astype(BF16)
    return lo, hi


def _split3(x):
    hi = x.astype(BF16)
    r1 = x - hi.astype(F32)
    mid = r1.astype(BF16)
    lo = (r1 - mid.astype(F32)).astype(BF16)
    return hi, mid, lo


def _nsa_attn_kernel(q_ref, ks_ref, vs_ref, kw_ref, vw_ref, gt_ref, kc_ref, vc_ref,
                     cq_ref, sq_ref, ck_ref, sk_ref, ovl_ref, exp_ref, wb_ref,
                     o_ref, ksr, vsb, kwr, vwb, selb, m_ref, acc_ref,
                     *, tq, tk, seq, scale):
    qi = pl.program_id(2)
    hp = NSA_HPG
    n_sel = seq // SEL_BLOCK

    @pl.when(qi == 0)
    def _():
        ck, sk = ck_ref[...], sk_ref[...]
        ksr[...] = _rope128(ks_ref[0], ck, sk).astype(BF16)
        kwr[...] = _rope128(kw_ref[0], ck, sk).astype(BF16)
        vsb[...] = _values_and_ones(vs_ref[0])
        vwb[...] = _values_and_ones(vw_ref[0])

    q = q_ref[0] * (scale * LOG2E)
    cq, sq = cq_ref[...], sq_ref[...]
    heads = [q[:, p * NSA_DH:(p + 1) * NSA_DH] for p in range(hp)]
    qu = jnp.concatenate(heads, axis=0).astype(BF16)
    qr = jnp.concatenate([_rope128(h, cq, sq) for h in heads], axis=0).astype(BF16)
    t_row = qi * tq + lax.broadcasted_iota(jnp.int32, (tq, 1), 0)
    lane = lax.broadcasted_iota(jnp.int32, (1, LANES), 1)

    sc = _dot_nt(qu, kc_ref[0]).reshape(hp, tq, LANES)
    valid = (lane * CMP_STRIDE + (CMP_BLOCK - 1) <= t_row)[None]
    sm = jnp.where(valid, sc, NEG_INF)
    e = jnp.exp2(sm - jnp.max(sm, axis=-1, keepdims=True))
    p = jnp.where(valid, e / jnp.sum(e, axis=-1, keepdims=True), 0.0)
    o_cmp = _dot(p.reshape(hp * tq, LANES).astype(BF16), vc_ref[0])

    psum = p[0]
    for i in range(1, hp):
        psum = psum + p[i]
    ovl = ovl_ref[...]
    imp = sum(_dot(part, ovl) for part in _split3(psum))
    cur = jnp.right_shift(t_row, SEL_BLOCK.bit_length() - 1)
    future = lane > cur
    forced = (lane == 0) | (lane == cur) | (lane == cur - 1)
    score = jnp.where(future, -1.0, jnp.where(forced, SEL_FORCE, imp))
    sc_t = score.T[0:n_sel, :]
    blk = lax.broadcasted_iota(jnp.int32, (n_sel, 1), 0)
    cnt = jnp.zeros((n_sel, tq), F32)
    for j in range(n_sel):
        other = sc_t[j:j + 1, :]
        beats = (other > sc_t) | ((other == sc_t) & (blk > j))
        cnt = cnt + jnp.where(beats, 1.0, 0.0)
    sel_t = jnp.where(cnt < SEL_COUNT, 1.0, 0.0)
    sel = jnp.concatenate([sel_t, jnp.zeros((LANES - n_sel, tq), F32)], axis=0).T.astype(BF16)
    picked = _dot(sel, exp_ref[...])
    kall = lax.broadcasted_iota(jnp.int32, (1, seq), 1)
    selb[...] = jnp.where((picked > 0.5) & (kall <= t_row), 0.0, NEG_INF)

    m_ref[...] = jnp.full(m_ref.shape, NEG_INF, F32)
    acc_ref[...] = jnp.zeros(acc_ref.shape, F32)
    for c in range(seq // tk):
        @pl.when(c * tk <= qi * tq + (tq - 1))
        def _():
            for i in range(hp):
                rows = pl.ds(i * tq, tq)
                s = _dot_nt(qr[i * tq:(i + 1) * tq], ksr[c * tk:(c + 1) * tk, :])
                _softmax_step(s + selb[:, c * tk:(c + 1) * tk], vsb[c * tk:(c + 1) * tk, :],
                              m_ref.at[rows], acc_ref.at[rows])
    o_sel = _softmax_result(acc_ref[...])

    span = WINDOW + tq
    start = pl.multiple_of(jnp.maximum(qi * tq - WINDOW, 0), tq)
    wins = []
    for i in range(hp):
        sw = _dot_nt(qr[i * tq:(i + 1) * tq], kwr[pl.ds(start, span), :]) + wb_ref[0]
        ew = jnp.exp2(sw - jnp.max(sw, axis=-1, keepdims=True)).astype(BF16)
        wins.append(_softmax_result(_dot(ew, vwb[pl.ds(start, span), :])))
    o_win = jnp.concatenate(wins, axis=0)

    gates = _sigmoid(gt_ref[0])
    for i in range(hp):
        rows = slice(i * tq, (i + 1) * tq)
        o = (gates[:, 3 * i:3 * i + 1] * o_cmp[rows] + gates[:, 3 * i + 1:3 * i + 2] * o_sel[rows]
             + gates[:, 3 * i + 2:3 * i + 3] * o_win[rows])
        o_ref[0, :, i * NSA_DH:(i + 1) * NSA_DH] = o.astype(o_ref.dtype)


def _window_bias(tq):
    span = WINDOW + tq
    out = []
    for qi in range(WINDOW // tq + 1):
        start = max(qi * tq - WINDOW, 0)
        t = qi * tq + np.arange(tq)[:, None]
        kpos = start + np.arange(span)[None, :]
        out.append(np.where((kpos <= t) & (kpos > t - WINDOW), 0.0, NEG_INF))
    return jnp.asarray(np.stack(out), F32)


def nsa_attention(u3, k_cmp, v_cmp, rope_c, rope_s, ovl, expand, *, tq, tk):
    b, s, _ = u3.shape
    assert s // CMP_STRIDE == LANES and tq % SEL_BLOCK == 0 and WINDOW % tq == 0
    hp = NSA_HPG
    nwb = WINDOW // tq
    wbias = _window_bias(tq)
    col = lambda off: (lambda b_, g, i: (b_, 0, off // NSA_DH + g))
    seq_blk = lambda off: pl.BlockSpec((1, s, NSA_DH), col(off))
    full = lambda a: pl.BlockSpec(a.shape, lambda b_, g, i: (0,) * a.ndim)
    cmp_blk = pl.BlockSpec((1, LANES, NSA_DH), lambda b_, g, i: (b_, 0, g))
    rope_q = pl.BlockSpec((tq, NSA_DH), lambda b_, g, i: (i, 0))
    kern = functools.partial(_nsa_attn_kernel, tq=tq, tk=tk, seq=s, scale=NSA_DH ** -0.5)
    return pl.pallas_call(
        kern,
        grid=(b, NSA_GROUPS, s // tq),
        in_specs=[pl.BlockSpec((1, tq, hp * NSA_DH), lambda b_, g, i: (b_, i, g)),
                  seq_blk(OD_KS), seq_blk(OD_VS), seq_blk(OD_KW), seq_blk(OD_VW),
                  pl.BlockSpec((1, tq, LANES), lambda b_, g, i: (b_, i, OD_G // LANES + g)),
                  cmp_blk, cmp_blk, rope_q, rope_q, full(rope_c), full(rope_s), full(ovl), full(expand),
                  pl.BlockSpec((1, tq, WINDOW + tq), lambda b_, g, i: (jnp.minimum(i, nwb), 0, 0))],
        out_specs=pl.BlockSpec((1, tq, hp * NSA_DH), lambda b_, g, i: (b_, i, g)),
        out_shape=jax.ShapeDtypeStruct((b, s, NSA_Q_DIM), BF16),
        scratch_shapes=[pltpu.VMEM((s, NSA_DH), BF16), pltpu.VMEM((s, 2 * LANES), BF16)] * 2 + [
            pltpu.VMEM((tq, s), F32),
            pltpu.VMEM((hp * tq, LANES), F32), pltpu.VMEM((hp * tq, 2 * LANES), F32)],
        compiler_params=_cparams(("parallel", "parallel", "arbitrary")),
    )(u3, u3, u3, u3, u3, u3, k_cmp, v_cmp, rope_c, rope_s, rope_c, rope_s, ovl, expand, wbias)


def _router_kernel(x_ref, g_ref, wr_ref, br_ref, xn_ref, meta_ref, cnt_ref, *, tm):
    @pl.when(pl.program_id(0) == 0)
    def _():
        cnt_ref[...] = jnp.zeros(cnt_ref.shape, F32)

    xn = _rms(x_ref[...], g_ref[...])
    xn_ref[...] = _pack_bf16_pairs(xn)
    xh, xm, _ = _split3(xn)
    wh, wm, _ = _split3(wr_ref[...])
    logits = _dot(xh, wh) + _dot(xh, wm) + _dot(xm, wh) + br_ref[...]
    lane = lax.broadcasted_iota(jnp.int32, (1, LANES), 1).astype(F32)
    lg = jnp.where(lane < N_EXPERTS, logits, NEG_INF)
    m1 = jnp.max(lg, axis=-1, keepdims=True)
    e1 = jnp.min(jnp.where(lg == m1, lane, float(LANES)), axis=-1, keepdims=True)
    lg2 = jnp.where(lane == e1, NEG_INF, lg)
    m2 = jnp.max(lg2, axis=-1, keepdims=True)
    e2 = jnp.min(jnp.where(lg2 == m2, lane, float(LANES)), axis=-1, keepdims=True)
    ex = jnp.exp(m2 - m1)
    den = 1.0 + ex
    g1 = 1.0 / den
    g2 = ex / den
    oh = jnp.where((lane == e1) | (lane == e2), 1.0, 0.0)
    r = lax.broadcasted_iota(jnp.int32, (tm, tm), 0)
    c = lax.broadcasted_iota(jnp.int32, (tm, tm), 1)
    tri = jnp.where(r > c, 1.0, 0.0).astype(BF16)
    cum = _dot(tri, oh.astype(BF16)) + cnt_ref[0:1, :]
    pos1 = jnp.sum(jnp.where(lane == e1, cum, 0.0), axis=-1, keepdims=True)
    pos2 = jnp.sum(jnp.where(lane == e2, cum, 0.0), axis=-1, keepdims=True)
    cnt_ref[...] = cnt_ref[...] + jnp.sum(oh, axis=0, keepdims=True)
    meta = jnp.where(lane == 0, e1, 0.0)
    meta = jnp.where(lane == 1, e2, meta)
    meta = jnp.where(lane == 2, g1, meta)
    meta = jnp.where(lane == 3, g2, meta)
    meta = jnp.where(lane == 4, pos1, meta)
    meta = jnp.where(lane == 5, pos2, meta)
    meta_ref[...] = meta


def moe_router(x, g, wr, br, *, tm):
    n, d = x.shape
    return pl.pallas_call(
        functools.partial(_router_kernel, tm=tm),
        grid=(n // tm,),
        in_specs=[pl.BlockSpec((tm, d), lambda i: (i, 0)),
                  pl.BlockSpec((1, d), lambda i: (0, 0)),
                  pl.BlockSpec((d, LANES), lambda i: (0, 0)),
                  pl.BlockSpec((1, LANES), lambda i: (0, 0))],
        out_specs=[pl.BlockSpec((tm, d // 2), lambda i: (i, 0)),
                   pl.BlockSpec((tm, LANES), lambda i: (i, 0)),
                   pl.BlockSpec((8, LANES), lambda i: (0, 0))],
        out_shape=[jax.ShapeDtypeStruct((n, d // 2), F32),
                   jax.ShapeDtypeStruct((n, LANES), F32),
                   jax.ShapeDtypeStruct((8, LANES), F32)],
        compiler_params=_cparams(("arbitrary",)),
    )(x, g.reshape(1, d), wr, br)


def _gather_chunk(per_worker, row_bytes):
    best = 0
    for c in range(8, per_worker + 1, 8):
        if per_worker % c == 0 and c * row_bytes <= SC_GATHER_BYTES and c <= LANES:
            best = c
    assert best > 0, (per_worker, row_bytes)
    return best


def sc_gather_rows(table, idx):
    _, d = table.shape
    b = idx.shape[0]
    assert b % (8 * SC_WORKERS) == 0 and table.dtype.itemsize == 4
    per_w = b // SC_WORKERS
    chunk = _gather_chunk(per_w, d * 4)
    mesh = plsc.VectorSubcoreMesh(core_axis_name="c", subcore_axis_name="s",
                                  num_cores=SC_CORES, num_subcores=SC_SUBCORES)

    @functools.partial(
        pl.kernel, mesh=mesh,
        out_type=jax.ShapeDtypeStruct((b, d), table.dtype),
        scratch_types=[pltpu.VMEM((chunk,), jnp.int32), pltpu.VMEM((chunk, d), table.dtype),
                       pltpu.SemaphoreType.DMA])
    def gather(table_hbm, idx_hbm, out_hbm, idx_v, rows_v, sem):
        wid = lax.axis_index("s") * SC_CORES + lax.axis_index("c")
        base = wid * per_w

        @pl.loop(0, per_w // chunk)
        def _(c):
            off = pl.multiple_of(base + c * chunk, 8)
            pltpu.sync_copy(idx_hbm.at[pl.ds(off, chunk)], idx_v)
            pltpu.async_copy(table_hbm.at[idx_v], rows_v, sem).wait()
            pltpu.sync_copy(rows_v, out_hbm.at[pl.ds(off, chunk)])

    return gather(table, idx)


def _expert_kernel(te_ref, ts_ref, tr_ref, xp_hbm, *rest, nf):
    ns = MOE_WSPLIT
    w1_refs, w3_refs, w2_refs = rest[:ns], rest[ns:2 * ns], rest[2 * ns:3 * ns]
    yr_hbm, xb, acc, stage, in_sem, out_sem = rest[3 * ns:]
    t = pl.program_id(0)
    f = pl.program_id(1)
    rows = tr_ref[t]
    start = pl.multiple_of(ts_ref[t], MOE_ALIGN)
    n_chunks = MOE_TILE // MOE_COPY
    half = xb.shape[1] // 2

    def in_copy(ci):
        return pltpu.make_async_copy(xp_hbm.at[pl.ds(start + ci * MOE_COPY, MOE_COPY)],
                                     stage.at[ci % 2], in_sem.at[ci % 2])

    def out_copy(ci, first_row):
        return pltpu.make_async_copy(acc.at[pl.ds(ci * MOE_COPY, MOE_COPY)],
                                     yr_hbm.at[pl.ds(first_row + ci * MOE_COPY, MOE_COPY)], out_sem.at[0])

    def when_chunk_live(ci, fn):
        pl.when(ci * MOE_COPY < rows)(fn)

    def wait_outputs(tile):
        tile_rows = tr_ref[tile]
        first_row = pl.multiple_of(ts_ref[tile], MOE_ALIGN)
        for ci in range(n_chunks):
            pl.when(ci * MOE_COPY < tile_rows)(lambda ci=ci: out_copy(ci, first_row).wait())

    @pl.when((f == 0) & (rows > 0))
    def _():
        def unpack(ci):
            in_copy(ci).wait()
            lo, hi = _unpack_bf16_pairs(stage[ci % 2])
            xb[ci * MOE_COPY:(ci + 1) * MOE_COPY, :half] = lo
            xb[ci * MOE_COPY:(ci + 1) * MOE_COPY, half:] = hi

        when_chunk_live(0, lambda: in_copy(0).start())
        for ci in range(n_chunks):
            if ci + 1 < n_chunks:
                when_chunk_live(ci + 1, lambda ci=ci: in_copy(ci + 1).start())
            when_chunk_live(ci, lambda ci=ci: unpack(ci))

    @pl.when((f == 0) & (t > 0))
    def _():
        wait_outputs(jnp.maximum(t - 1, 0))

    @pl.when((f == 0) & (rows > 0))
    def _():
        acc[...] = jnp.zeros(acc.shape, F32)

    @pl.when(rows > 0)
    def _():
        w1 = jnp.concatenate([r[0, 0].astype(BF16) for r in w1_refs], axis=0)
        w3 = jnp.concatenate([r[0, 0].astype(BF16) for r in w3_refs], axis=0)
        w2 = jnp.concatenate([r[0, 0].astype(BF16) for r in w2_refs], axis=1)

        def chain(r0, size):
            rs = pl.ds(pl.multiple_of(r0, MOE_PIECES[-1]), size)
            xc = xb[rs, :]
            h1 = _dot(xc, w1)
            h3 = _dot(xc, w3)
            hh = (h1 * _sigmoid(h1) * h3).astype(BF16)
            acc[rs, :] += _dot(hh, w2)

        todo = (rows + (MOE_PIECES[-1] - 1)) // MOE_PIECES[-1] * MOE_PIECES[-1]
        off = jnp.int32(0)
        for size in MOE_PIECES:
            take = todo - off >= size

            @pl.when(take)
            def _(off=off, size=size):
                for c0 in range(0, size, MOE_CHAIN):
                    chain(off + c0, min(MOE_CHAIN, size))

            off = off + jnp.where(take, size, 0)

    @pl.when((f == nf - 1) & (rows > 0))
    def _():
        for ci in range(n_chunks):
            when_chunk_live(ci, lambda ci=ci: out_copy(ci, start).start())

    @pl.when((f == nf - 1) & (t == pl.num_programs(0) - 1))
    def _():
        wait_outputs(t)


def moe_experts(xp, tile_e, tile_start, tile_rows, w1, w3, w2, layer, *, tf):
    n_rows, half = xp.shape
    d = 2 * half
    n_tiles = tile_e.shape[0]
    ff = w1.shape[3]
    nf = ff // tf
    assert MOE_TILE % MOE_COPY == 0 and sum(MOE_PIECES) >= MOE_TILE

    def f_eff(t, f, tr):
        return jnp.where(tr[t] > 0, f, nf - 1)

    ns = MOE_WSPLIT
    up_specs = [pl.BlockSpec((1, 1, d // ns, tf), lambda t, f, te, ts, tr, k=k: (layer, te[t], k, f_eff(t, f, tr)))
                for k in range(ns)]
    down_specs = [pl.BlockSpec((1, 1, tf, d // ns), lambda t, f, te, ts, tr, k=k: (layer, te[t], f_eff(t, f, tr), k))
                  for k in range(ns)]
    grid_spec = pltpu.PrefetchScalarGridSpec(
        num_scalar_prefetch=3,
        grid=(n_tiles, nf),
        in_specs=[pl.BlockSpec(memory_space=pl.ANY)] + 2 * up_specs + down_specs,
        out_specs=pl.BlockSpec(memory_space=pl.ANY),
        scratch_shapes=[pltpu.VMEM((MOE_TILE, d), BF16), pltpu.VMEM((MOE_TILE, d), F32),
                        pltpu.VMEM((2, MOE_COPY, half), F32),
                        pltpu.SemaphoreType.DMA((2,)), pltpu.SemaphoreType.DMA((1,))],
    )
    return pl.pallas_call(
        functools.partial(_expert_kernel, nf=nf),
        grid_spec=grid_spec,
        out_shape=jax.ShapeDtypeStruct((n_rows, d), F32),
        compiler_params=_cparams(("arbitrary", "arbitrary")),
    )(tile_e, tile_start, tile_rows, xp, *([w1] * ns), *([w3] * ns), *([w2] * ns))


def _combine_kernel(x_ref, y1_ref, y2_ref, meta_ref, g_ref, o_ref, *, final_norm):
    meta = meta_ref[...]
    y = x_ref[...] + (meta[:, 2:3] * y1_ref[...] + meta[:, 3:4] * y2_ref[...])
    if final_norm:
        y = _rms(y, g_ref[...])
    o_ref[...] = y


def moe_combine(x, yg, meta, g, *, final_norm, tm):
    n, d = x.shape
    nb = n // tm
    row = pl.BlockSpec((tm, d), lambda i: (i, 0))
    return pl.pallas_call(
        functools.partial(_combine_kernel, final_norm=final_norm),
        grid=(nb,),
        in_specs=[row, row, pl.BlockSpec((tm, d), lambda i: (i + nb, 0)),
                  pl.BlockSpec((tm, LANES), lambda i: (i, 0)),
                  pl.BlockSpec((1, d), lambda i: (0, 0))],
        out_specs=row,
        out_shape=jax.ShapeDtypeStruct((n, d), F32),
        compiler_params=_cparams(("parallel",)),
    )(x, yg, yg, meta, g.reshape(1, d))


def _mla_rope_tables(seq):
    half = MLA_ROPE // 2
    pos = jnp.arange(seq, dtype=F32)
    inv = ROPE_THETA ** (-jnp.arange(0, MLA_ROPE, 2, dtype=F32) / MLA_ROPE)
    ang = pos[:, None] * inv[None, :]
    cos, sin = jnp.cos(ang), jnp.sin(ang)
    z = jnp.zeros((seq, half), F32)
    pad = jnp.zeros((seq, LANES - MLA_ROPE), F32)
    c = jnp.concatenate([cos, cos, pad], axis=1)
    s_lo = jnp.concatenate([-sin, z, pad], axis=1)
    s_hi = jnp.concatenate([z, sin, pad], axis=1)
    return c, s_lo, s_hi


def _nsa_rope_tables(seq):
    pos = jnp.arange(seq, dtype=F32)
    inv = ROPE_THETA ** (-jnp.arange(0, NSA_DH, 2, dtype=F32) / NSA_DH)
    ang = pos[:, None] * inv[None, :]
    cos, sin = jnp.cos(ang), jnp.sin(ang)
    return jnp.concatenate([cos, cos], axis=1), jnp.concatenate([-sin, sin], axis=1)


def _selection_constants(seq):
    n_sel = seq // SEL_BLOCK
    nc = (seq - CMP_BLOCK) // CMP_STRIDE + 1
    cmp_start = np.arange(LANES) * CMP_STRIDE
    sel_start = np.arange(LANES) * SEL_BLOCK
    ovl = ((cmp_start[:, None] < sel_start[None, :] + SEL_BLOCK) &
           (cmp_start[:, None] + CMP_BLOCK > sel_start[None, :]))
    ovl &= (np.arange(LANES)[:, None] < nc) & (np.arange(LANES)[None, :] < n_sel)
    expand = (np.arange(seq)[None, :] // SEL_BLOCK == np.arange(LANES)[:, None])
    return jnp.asarray(ovl, BF16), jnp.asarray(expand, BF16)


def even_layer(x, seq, p):
    n = x.shape[0]
    b = n // seq
    (norm_mix, w_in, q_norm, w_q_up, kv_norm, w_kv_up, conv_w, conv_b, ga_w, ga_b, gx_w, gx_b,
     lam, w_out, norm_ffn, w1, w3, w2) = p
    d = D_MODEL
    o1 = MLA_Q_LORA + MLA_KV_LORA
    o2 = o1 + MLA_ROPE
    w_pack = jnp.concatenate(
        [w_in[:, :o1], w_in[:, o2:], w_in[:, o1:o2], jnp.zeros((d, LANES - MLA_ROPE), F32)], axis=1).astype(BF16)
    u = norm_matmul(x, norm_mix, w_pack, tm=1024, tn=EV_PACKED // 3)

    wq = w_q_up.reshape(MLA_Q_LORA, MLA_HEADS, MLA_NOPE + MLA_ROPE)
    wq = jnp.pad(wq, ((0, 0), (0, 0), (0, MLA_QK_PAD - MLA_NOPE - MLA_ROPE)))
    wq = wq.reshape(MLA_Q_LORA, MLA_HEADS * MLA_QK_PAD).astype(BF16)
    wkv = w_kv_up.reshape(MLA_KV_LORA, MLA_HEADS, MLA_NOPE + MLA_V)
    wk = wkv[:, :, :MLA_NOPE].reshape(MLA_KV_LORA, MLA_HEADS * MLA_NOPE).astype(BF16)
    wv = wkv[:, :, MLA_NOPE:].reshape(MLA_KV_LORA, MLA_HEADS * MLA_V).astype(BF16)
    rc, rlo, rhi = _mla_rope_tables(seq)
    q, k, v = mla_up(u, q_norm, kv_norm, wq, wk, wv, rc, rlo, rhi, seq=seq, tm=512)
    o_mla = mla_attention(q.reshape(b, seq, -1), k.reshape(b, seq, -1), v.reshape(b, seq, -1), t=512)

    o_rec = rglru(u.reshape(b, seq, EV_PACKED), conv_w, conv_b, ga_w.astype(BF16), ga_b,
                  gx_w.astype(BF16), gx_b, lam, ts=512)
    x = matmul_residual([o_mla.reshape(n, -1), o_rec.reshape(n, -1)], w_out.astype(BF16), x, tm=512, tn=D_MODEL)
    return ffn_dense(x, norm_ffn, w1.astype(BF16), w3.astype(BF16), w2.astype(BF16), tm=512, tf=512)


def _moe_dispatch(meta, counts, n):
    e1 = meta[:, 0].astype(jnp.int32)
    e2 = meta[:, 1].astype(jnp.int32)
    pos1 = meta[:, 4].astype(jnp.int32)
    pos2 = meta[:, 5].astype(jnp.int32)
    cnt = counts[0, :N_EXPERTS].astype(jnp.int32)
    span = (cnt + MOE_ALIGN - 1) // MOE_ALIGN * MOE_ALIGN
    row0 = jnp.cumsum(span) - span
    row_unit = 64 * SC_WORKERS
    n_rows = -(-(n * TOP_K + N_EXPERTS * MOE_ALIGN + MOE_TILE) // row_unit) * row_unit
    d1 = row0[e1] + pos1
    d2 = row0[e2] + pos2
    n_tiles = (n * TOP_K) // MOE_TILE + N_EXPERTS
    tiles_e = (cnt + MOE_TILE - 1) // MOE_TILE
    tend = jnp.cumsum(tiles_e)
    tbeg = tend - tiles_e
    tid = jnp.arange(n_tiles, dtype=jnp.int32)
    te = jnp.minimum(jnp.searchsorted(tend, tid, side='right'), N_EXPERTS - 1).astype(jnp.int32)
    used = tid < tend[-1]
    first = (tid - tbeg[te]) * MOE_TILE
    rows = jnp.where(used, jnp.clip(cnt[te] - first, 0, MOE_TILE), 0).astype(jnp.int32)
    tstart = jnp.where(used, row0[te] + first, 0).astype(jnp.int32)
    last_e = te[jnp.maximum(tend[-1] - 1, 0)]
    te = jnp.where(used, te, last_e).astype(jnp.int32)
    tok = jnp.arange(n, dtype=jnp.int32)
    row_tok = (jnp.arange(n_rows, dtype=jnp.int32) % n).at[d1].set(tok).at[d2].set(tok)
    return d1, d2, row_tok, te, tstart, rows


def odd_layer(x, seq, p, experts, final_g):
    n = x.shape[0]
    b = n // seq
    (norm_mix, w_in, ck_pe, ck_w1, ck_b1, ck_w2, cv_pe, cv_w1, cv_b1, cv_w2, w_out, norm_ffn,
     router_w, router_b) = p
    ew1, ew3, ew2, layer = experts
    d = D_MODEL
    wg = w_in[:, OD_G:].reshape(d, NSA_GROUPS, NSA_HPG * 3)
    wg = jnp.pad(wg, ((0, 0), (0, 0), (0, LANES - NSA_HPG * 3))).reshape(d, NSA_GROUPS * LANES)
    w_pack = jnp.concatenate([w_in[:, :OD_G], wg], axis=1).astype(BF16)
    u = norm_matmul(x, norm_mix, w_pack, tm=1024, tn=512)
    u3 = u.reshape(b, seq, OD_PACKED)

    k_cmp = nsa_compress(u3, OD_KC, ck_pe, ck_w1.astype(BF16), ck_b1, ck_w2.astype(BF16))
    v_cmp = nsa_compress(u3, OD_VC, cv_pe, cv_w1.astype(BF16), cv_b1, cv_w2.astype(BF16))
    rc, rs = _nsa_rope_tables(seq)
    ovl, expand = _selection_constants(seq)
    o = nsa_attention(u3, k_cmp, v_cmp, rc, rs, ovl, expand, tq=256, tk=512)
    x = matmul_residual([o.reshape(n, -1)], w_out.astype(BF16), x, tm=512, tn=D_MODEL)

    wr = jnp.pad(router_w, ((0, 0), (0, LANES - N_EXPERTS)))
    br = jnp.pad(router_b, (0, LANES - N_EXPERTS)).reshape(1, LANES)
    xp, meta, counts = moe_router(x, norm_ffn, wr, br, tm=512)
    d1, d2, row_tok, te, tstart, rows = _moe_dispatch(meta, counts, n)
    yr = moe_experts(sc_gather_rows(xp, row_tok), te, tstart, rows, ew1, ew3, ew2, layer, tf=256)
    yg = sc_gather_rows(yr, jnp.concatenate([d1, d2]))
    g = final_g if final_g is not None else norm_ffn
    return moe_combine(x, yg, meta, g, final_norm=final_g is not None, tm=512)


def kernel(x, ev_norm_mix, ev_w_in, ev_q_norm, ev_w_q_up, ev_kv_norm, ev_w_kv_up, ev_conv_w, ev_conv_b, ev_gate_a_w, ev_gate_a_b, ev_gate_x_w, ev_gate_x_b, ev_lru_lambda, ev_w_out, ev_norm_ffn, ev_ffn_w1, ev_ffn_w3, ev_ffn_w2, od_norm_mix, od_w_in, od_cmp_k_pe, od_cmp_k_w1, od_cmp_k_b1, od_cmp_k_w2, od_cmp_v_pe, od_cmp_v_w1, od_cmp_v_b1, od_cmp_v_w2, od_w_out, od_norm_ffn, od_router_w, od_router_b, od_exp_w1, od_exp_w3, od_exp_w2, final_norm):
    bsz, seq, d = x.shape
    ev = (ev_norm_mix, ev_w_in, ev_q_norm, ev_w_q_up, ev_kv_norm, ev_w_kv_up, ev_conv_w, ev_conv_b,
          ev_gate_a_w, ev_gate_a_b, ev_gate_x_w, ev_gate_x_b, ev_lru_lambda, ev_w_out, ev_norm_ffn,
          ev_ffn_w1, ev_ffn_w3, ev_ffn_w2)
    od = (od_norm_mix, od_w_in, od_cmp_k_pe, od_cmp_k_w1, od_cmp_k_b1, od_cmp_k_w2, od_cmp_v_pe,
          od_cmp_v_w1, od_cmp_v_b1, od_cmp_v_w2, od_w_out, od_norm_ffn, od_router_w, od_router_b)
    h = x.reshape(bsz * seq, d)
    for layer in range(DEPTH):
        i = layer // 2
        if layer % 2 == 0:
            h = even_layer(h, seq, tuple(a[i] for a in ev))
        else:
            h = odd_layer(h, seq, tuple(a[i] for a in od), (od_exp_w1, od_exp_w3, od_exp_w2, i),
                          final_norm if layer == DEPTH - 1 else None)
    return h.reshape(bsz, seq, d)
```

```python
import functools
import math

import numpy as np
import jax
import jax.numpy as jnp
from jax import lax
from jax.experimental import pallas as pl
from jax.experimental.pallas import tpu as pltpu
from jax.experimental.pallas import tpu_sc as plsc

F32 = jnp.float32
BF16 = jnp.bfloat16

D_MODEL = 2048
DEPTH = 4
RMS_EPS = 1e-6
ROPE_THETA = 10000.0
NEG_INF = -1e30
LOG2E = math.log2(math.e)

MLA_HEADS = 8
MLA_Q_LORA = 768
MLA_KV_LORA = 512
MLA_NOPE = 128
MLA_ROPE = 64
MLA_V = 128
MLA_QK_PAD = 256
MLA_ROW_GROUPS = 2

LRU_WIDTH = D_MODEL // 2
LRU_BLOCKS = 8
LRU_BLOCK_W = LRU_WIDTH // LRU_BLOCKS
LRU_C = 8.0
CONV_WIDTH = 4

NSA_HEADS = 16
NSA_GROUPS = 4
NSA_HPG = NSA_HEADS // NSA_GROUPS
NSA_DH = D_MODEL // NSA_HEADS
CMP_BLOCK = 32
CMP_STRIDE = 16
SEL_BLOCK = 64
SEL_COUNT = 16
SEL_FORCE = 1e4
WINDOW = 512
NSA_Q_DIM = NSA_HEADS * NSA_DH
NSA_KV_DIM = NSA_GROUPS * NSA_DH

DENSE_FF = 5632
N_EXPERTS = 8
TOP_K = 2
EXPERT_FF = 7168

LANES = 128
SC_CORES = 2
SC_SUBCORES = 16
SC_WORKERS = SC_CORES * SC_SUBCORES
SC_GATHER_BYTES = 256 * 1024
VMEM_LIMIT = 60 * 1024 * 1024

EV_CQ = 0
EV_CKV = MLA_Q_LORA
EV_REC = MLA_Q_LORA + MLA_KV_LORA
EV_GATE = EV_REC + LRU_WIDTH
EV_PE = EV_GATE + LRU_WIDTH
EV_PACKED = EV_PE + LANES

OD_Q = 0
OD_KC = NSA_Q_DIM
OD_VC = OD_KC + NSA_KV_DIM
OD_KS = OD_VC + NSA_KV_DIM
OD_VS = OD_KS + NSA_KV_DIM
OD_KW = OD_VS + NSA_KV_DIM
OD_VW = OD_KW + NSA_KV_DIM
OD_G = OD_VW + NSA_KV_DIM
OD_PACKED = OD_G + NSA_GROUPS * LANES

MOE_TILE = 2688
MOE_ALIGN = 16
MOE_COPY = 384
MOE_PIECES = (1024, 1024, 512, 256, 128)
MOE_CHAIN = 512
MOE_WSPLIT = 1


def _cparams(sem):
    return pltpu.CompilerParams(dimension_semantics=sem, vmem_limit_bytes=VMEM_LIMIT)


def _rms(x, g):
    ms = jnp.mean(x * x, axis=-1, keepdims=True)
    return x * lax.rsqrt(ms + RMS_EPS) * g


def _sigmoid(x):
    return 1.0 / (1.0 + jnp.exp(-x))


def _gelu_tanh(x):
    return 0.5 * x * (1.0 + jnp.tanh(math.sqrt(2.0 / math.pi) * (x + 0.044715 * (x * x * x))))


def _dot(a, b):
    return jnp.dot(a, b, preferred_element_type=F32)


def _dot_nt(a, b):
    return lax.dot_general(a, b, (((1,), (1,)), ((), ())), preferred_element_type=F32)


def _norm_mm_kernel(x_ref, g_ref, w_ref, o_ref, xn_ref):
    @pl.when(pl.program_id(1) == 0)
    def _():
        xn_ref[...] = _rms(x_ref[...], g_ref[...]).astype(BF16)

    o_ref[...] = _dot(xn_ref[...], w_ref[...]).astype(o_ref.dtype)


def norm_matmul(x, g, w, *, tm, tn, out_dtype=F32):
    n, k = x.shape
    m = w.shape[1]
    assert n % tm == 0 and m % tn == 0
    return pl.pallas_call(
        _norm_mm_kernel,
        grid=(n // tm, m // tn),
        in_specs=[pl.BlockSpec((tm, k), lambda i, j: (i, 0)),
                  pl.BlockSpec((1, k), lambda i, j: (0, 0)),
                  pl.BlockSpec((k, tn), lambda i, j: (0, j))],
        out_specs=pl.BlockSpec((tm, tn), lambda i, j: (i, j)),
        out_shape=jax.ShapeDtypeStruct((n, m), out_dtype),
        scratch_shapes=[pltpu.VMEM((tm, k), BF16)],
        compiler_params=_cparams(("parallel", "arbitrary")),
    )(x, g.reshape(1, k), w)


def _mm_res_kernel(*refs, n_in):
    xs = refs[:n_in]
    ws = refs[n_in:2 * n_in]
    res_ref = refs[2 * n_in]
    o_ref = refs[2 * n_in + 1]
    acc = res_ref[...]
    for x_ref, w_ref in zip(xs, ws):
        acc = acc + _dot(x_ref[...], w_ref[...])
    o_ref[...] = acc


def matmul_residual(xs, w, res, *, tm, tn):
    n = res.shape[0]
    m = w.shape[1]
    n_in = len(xs)
    in_specs = [pl.BlockSpec((tm, x.shape[1]), lambda i, j: (i, 0)) for x in xs]
    row = 0
    for x in xs:
        kx = x.shape[1]
        assert row % kx == 0
        in_specs.append(pl.BlockSpec((kx, tn), lambda i, j, rb=row // kx: (rb, j)))
        row += kx
    assert row == w.shape[0]
    in_specs.append(pl.BlockSpec((tm, tn), lambda i, j: (i, j)))
    return pl.pallas_call(
        functools.partial(_mm_res_kernel, n_in=n_in),
        grid=(n // tm, m // tn),
        in_specs=in_specs,
        out_specs=pl.BlockSpec((tm, tn), lambda i, j: (i, j)),
        out_shape=jax.ShapeDtypeStruct((n, m), F32),
        compiler_params=_cparams(("parallel", "arbitrary")),
    )(*xs, *([w] * n_in), res)


def _rope64(x, c, s_lo, s_hi):
    return x * c + pltpu.roll(x, 96, 1) * s_lo + pltpu.roll(x, 32, 1) * s_hi


def _mla_up_kernel(u_ref, pe_ref, qg_ref, kvg_ref, wq_ref, wk_ref, wv_ref,
                   c_ref, slo_ref, shi_ref, q_ref, k_ref, v_ref):
    u = u_ref[...]
    c, s_lo, s_hi = c_ref[...], slo_ref[...], shi_ref[...]
    qn = _rms(u[:, EV_CQ:EV_CQ + MLA_Q_LORA], qg_ref[...]).astype(BF16)
    kvn = _rms(u[:, EV_CKV:EV_CKV + MLA_KV_LORA], kvg_ref[...]).astype(BF16)
    q = _dot(qn, wq_ref[...]) * ((MLA_NOPE + MLA_ROPE) ** -0.5 * LOG2E)
    kn = _dot(kvn, wk_ref[...])
    v_ref[...] = _dot(kvn, wv_ref[...]).astype(v_ref.dtype)
    kpe = _rope64(pe_ref[...], c, s_lo, s_hi).astype(k_ref.dtype)
    for h in range(MLA_HEADS):
        a = h * MLA_QK_PAD
        q_ref[:, a:a + LANES] = q[:, a:a + LANES].astype(q_ref.dtype)
        q_ref[:, a + LANES:a + 2 * LANES] = _rope64(q[:, a + LANES:a + 2 * LANES], c, s_lo, s_hi).astype(q_ref.dtype)
        k_ref[:, a:a + LANES] = kn[:, h * LANES:(h + 1) * LANES].astype(k_ref.dtype)
        k_ref[:, a + LANES:a + 2 * LANES] = kpe


def mla_up(u, q_norm, kv_norm, wq, wk, wv, rope_c, rope_slo, rope_shi, *, seq, tm):
    n = u.shape[0]
    hq = MLA_HEADS * MLA_QK_PAD
    hv = MLA_HEADS * MLA_V
    ab = EV_REC
    assert seq % tm == 0 and EV_PE % LANES == 0
    nsb = seq // tm
    row_spec = pl.BlockSpec((tm, LANES), lambda i: (i % nsb, 0))
    full = lambda a: pl.BlockSpec(a.shape, lambda i: (0,) * a.ndim)
    qg = q_norm.reshape(1, -1)
    kvg = kv_norm.reshape(1, -1)
    return pl.pallas_call(
        _mla_up_kernel,
        grid=(n // tm,),
        in_specs=[pl.BlockSpec((tm, ab), lambda i: (i, 0)),
                  pl.BlockSpec((tm, LANES), lambda i: (i, EV_PE // LANES)),
                  full(qg), full(kvg), full(wq), full(wk), full(wv),
                  row_spec, row_spec, row_spec],
        out_specs=[pl.BlockSpec((tm, hq), lambda i: (i, 0)),
                   pl.BlockSpec((tm, hq), lambda i: (i, 0)),
                   pl.BlockSpec((tm, hv), lambda i: (i, 0))],
        out_shape=[jax.ShapeDtypeStruct((n, hq), BF16),
                   jax.ShapeDtypeStruct((n, hq), BF16),
                   jax.ShapeDtypeStruct((n, hv), BF16)],
        compiler_params=_cparams(("parallel",)),
    )(u, u, qg, kvg, wq, wk, wv, rope_c, rope_slo, rope_shi)


def _values_and_ones(v):
    return jnp.concatenate([v.astype(BF16), jnp.ones(v.shape, BF16)], axis=1)


def _lane_tile(x, n):
    return jnp.concatenate([x] * n, axis=1)


def _softmax_step(s, v1, m_ref, acc_ref):
    m_prev = m_ref[...]
    m_new = jnp.maximum(m_prev, jnp.max(s, axis=-1, keepdims=True))
    alpha = jnp.exp2(m_prev - m_new)
    p = jnp.exp2(s - _lane_tile(m_new, s.shape[1] // LANES))
    acc_ref[...] = _lane_tile(alpha, 2) * acc_ref[...] + _dot(p.astype(BF16), v1)
    m_ref[...] = m_new


def _softmax_result(acc):
    return acc[:, :LANES] / acc[:, LANES:]


def _mla_attn_kernel(q_ref, k_ref, v_ref, cb_ref, o_ref, v1, m_ref, acc_ref, *, t, n_chunks):
    qi = pl.program_id(2)

    @pl.when(qi == 0)
    def _():
        v1[...] = _values_and_ones(v_ref[0])

    m_ref[...] = jnp.full(m_ref.shape, NEG_INF, F32)
    acc_ref[...] = jnp.zeros(acc_ref.shape, F32)
    rg = t // MLA_ROW_GROUPS

    def chunk(c, masked):
        for i in range(MLA_ROW_GROUPS):
            rows = pl.ds(i * rg, rg)
            s = _dot_nt(q_ref[0, i * rg:(i + 1) * rg, :], k_ref[0, c * t:(c + 1) * t, :])
            if masked:
                s = s + cb_ref[i * rg:(i + 1) * rg, :]
            _softmax_step(s, v1[c * t:(c + 1) * t, :], m_ref.at[rows], acc_ref.at[rows])

    for c in range(n_chunks):
        pl.when(c < qi)(functools.partial(chunk, c, False))
        pl.when(c == qi)(functools.partial(chunk, c, True))
    o_ref[0] = _softmax_result(acc_ref[...]).astype(o_ref.dtype)


def mla_attention(q, k, v, *, t):
    b, s, _ = q.shape
    causal = jnp.asarray(np.where(np.arange(t)[None, :] <= np.arange(t)[:, None], 0.0, NEG_INF), F32)
    kern = functools.partial(_mla_attn_kernel, t=t, n_chunks=s // t)
    return pl.pallas_call(
        kern,
        grid=(b, MLA_HEADS, s // t),
        in_specs=[pl.BlockSpec((1, t, MLA_QK_PAD), lambda b_, h, i: (b_, i, h)),
                  pl.BlockSpec((1, s, MLA_QK_PAD), lambda b_, h, i: (b_, 0, h)),
                  pl.BlockSpec((1, s, MLA_V), lambda b_, h, i: (b_, 0, h)),
                  pl.BlockSpec((t, t), lambda b_, h, i: (0, 0))],
        out_specs=pl.BlockSpec((1, t, MLA_V), lambda b_, h, i: (b_, i, h)),
        out_shape=jax.ShapeDtypeStruct((b, s, MLA_HEADS * MLA_V), BF16),
        scratch_shapes=[pltpu.VMEM((s, 2 * LANES), BF16), pltpu.VMEM((t, LANES), F32),
                        pltpu.VMEM((t, 2 * LANES), F32)],
        compiler_params=_cparams(("parallel", "parallel", "arbitrary")),
    )(q, k, v, causal)


def _rglru_kernel(x_ref, y_ref, cw_ref, cb_ref, gaw_ref, gab_ref, gxw_ref, gxb_ref, lam_ref,
                  o_ref, xbuf, h_ref, *, ts):
    t = pl.program_id(2)

    @pl.when(t == 0)
    def _():
        xbuf[0:8, :] = jnp.zeros((8, LANES), F32)
        h_ref[...] = jnp.zeros(h_ref.shape, F32)

    x = x_ref[0]
    xbuf[8:, :] = x
    cw = cw_ref[...]
    xc = cb_ref[...] + cw[3:4] * x
    for kk in range(CONV_WIDTH - 1):
        back = CONV_WIDTH - 1 - kk
        xc = xc + cw[kk:kk + 1] * xbuf[8 - back:8 - back + ts, :]
    xbuf[0:8, :] = x[ts - 8:, :]

    xcb = xc.astype(BF16)
    r = _sigmoid(_dot(xcb, gaw_ref[0]) + gab_ref[0])
    gi = _sigmoid(_dot(xcb, gxw_ref[0]) + gxb_ref[0])
    z = -lam_ref[...]
    softplus = jnp.maximum(z, 0.0) + jnp.log1p(jnp.exp(-jnp.abs(z)))
    log_a = (-LRU_C) * r * softplus
    a = jnp.exp(log_a)
    mult = jnp.sqrt(-jnp.tanh(log_a) * (a * a + 1.0))
    row = lax.broadcasted_iota(jnp.int32, (ts, 1), 0)
    mult = jnp.where(row + t * ts == 0, 1.0, mult)
    bv = mult * gi * xc

    d = 1
    while d < ts:
        keep = row >= d
        a_sh = jnp.where(keep, pltpu.roll(a, d, 0), 1.0)
        b_sh = jnp.where(keep, pltpu.roll(bv, d, 0), 0.0)
        bv = a * b_sh + bv
        a = a * a_sh
        d *= 2
    h = bv + a * h_ref[...]
    h_ref[...] = h[ts - 1:ts, :]
    o_ref[0] = (h * _gelu_tanh(y_ref[0])).astype(o_ref.dtype)


def rglru(u3, conv_w, conv_b, ga_w, ga_b, gx_w, gx_b, lam, *, ts):
    b, s, _ = u3.shape
    rec0 = EV_REC // LANES
    gate0 = EV_GATE // LANES
    cb = conv_b.reshape(1, LRU_WIDTH)
    gab = ga_b.reshape(LRU_BLOCKS, 1, LRU_BLOCK_W)
    gxb = gx_b.reshape(LRU_BLOCKS, 1, LRU_BLOCK_W)
    lam2 = lam.reshape(1, LRU_WIDTH)
    blk_w = pl.BlockSpec((1, LRU_BLOCK_W, LRU_BLOCK_W), lambda b_, n, t: (n, 0, 0))
    blk_b = pl.BlockSpec((1, 1, LRU_BLOCK_W), lambda b_, n, t: (n, 0, 0))
    vec = pl.BlockSpec((1, LANES), lambda b_, n, t: (0, n))
    return pl.pallas_call(
        functools.partial(_rglru_kernel, ts=ts),
        grid=(b, LRU_BLOCKS, s // ts),
        in_specs=[pl.BlockSpec((1, ts, LANES), lambda b_, n, t: (b_, t, rec0 + n)),
                  pl.BlockSpec((1, ts, LANES), lambda b_, n, t: (b_, t, gate0 + n)),
                  pl.BlockSpec((CONV_WIDTH, LANES), lambda b_, n, t: (0, n)),
                  vec, blk_w, blk_b, blk_w, blk_b, vec],
        out_specs=pl.BlockSpec((1, ts, LANES), lambda b_, n, t: (b_, t, n)),
        out_shape=jax.ShapeDtypeStruct((b, s, LRU_WIDTH), BF16),
        scratch_shapes=[pltpu.VMEM((ts + 8, LANES), F32), pltpu.VMEM((1, LANES), F32)],
        compiler_params=_cparams(("parallel", "parallel", "arbitrary")),
    )(u3, u3, conv_w, cb, ga_w, gab, gx_w, gxb, lam2)


def _ffn_kernel(x_ref, g_ref, w1_ref, w3_ref, w2_ref, o_ref, xn_ref):
    f = pl.program_id(1)

    @pl.when(f == 0)
    def _():
        x = x_ref[...]
        xn_ref[...] = _rms(x, g_ref[...]).astype(BF16)
        o_ref[...] = x

    xn = xn_ref[...]
    h1 = _dot(xn, w1_ref[...])
    h3 = _dot(xn, w3_ref[...])
    hh = (h1 * _sigmoid(h1) * h3).astype(BF16)
    o_ref[...] += _dot(hh, w2_ref[...])


def ffn_dense(x, g, w1, w3, w2, *, tm, tf):
    n, d = x.shape
    ff = w1.shape[1]
    assert n % tm == 0 and ff % tf == 0
    return pl.pallas_call(
        _ffn_kernel,
        grid=(n // tm, ff // tf),
        in_specs=[pl.BlockSpec((tm, d), lambda i, f: (i, 0)),
                  pl.BlockSpec((1, d), lambda i, f: (0, 0)),
                  pl.BlockSpec((d, tf), lambda i, f: (0, f)),
                  pl.BlockSpec((d, tf), lambda i, f: (0, f)),
                  pl.BlockSpec((tf, d), lambda i, f: (f, 0))],
        out_specs=pl.BlockSpec((tm, d), lambda i, f: (i, 0)),
        out_shape=jax.ShapeDtypeStruct((n, d), F32),
        scratch_shapes=[pltpu.VMEM((tm, d), BF16)],
        compiler_params=_cparams(("parallel", "arbitrary")),
    )(x, g.reshape(1, d), w1, w3, w2)


def _compress_kernel(*refs, n_half):
    kc_refs = refs[:NSA_GROUPS]
    pe_ref, w1_ref, b1_ref, w2_ref, o_ref = refs[NSA_GROUPS:]
    pe = pe_ref[...]
    half_k = n_half * NSA_DH
    nchunk = o_ref.shape[1]
    for g in range(NSA_GROUPS):
        lo, hi = [], []
        for l in range(n_half):
            piece = kc_refs[g][0, pl.ds(l, nchunk, stride=n_half), :]
            lo.append((piece + pe[l:l + 1]).astype(BF16))
            hi.append((piece + pe[n_half + l:n_half + l + 1]).astype(BF16))
        z0 = _dot(jnp.concatenate(lo, axis=1), w1_ref[0:half_k, :])
        z1 = _dot(jnp.concatenate(hi, axis=1), w1_ref[half_k:2 * half_k, :])
        rows = z1.shape[0]
        pre = z0 + pltpu.roll(z1, rows - 1, 0) + b1_ref[...]
        o_ref[0, :, g * NSA_DH:(g + 1) * NSA_DH] = _dot(_gelu_tanh(pre).astype(BF16), w2_ref[...]).astype(o_ref.dtype)


def nsa_compress(u3, col, pe, w1, b1, w2):
    b, s, _ = u3.shape
    nchunk = s // CMP_STRIDE
    assert CMP_BLOCK == 2 * CMP_STRIDE and col % NSA_KV_DIM == 0
    full = lambda a: pl.BlockSpec(a.shape, lambda i: (0,) * a.ndim)
    b1r = b1.reshape(1, NSA_DH)
    return pl.pallas_call(
        functools.partial(_compress_kernel, n_half=CMP_STRIDE),
        grid=(b,),
        in_specs=[pl.BlockSpec((1, s, NSA_DH), lambda i, g=g: (i, 0, col // NSA_DH + g)) for g in range(NSA_GROUPS)]
        + [full(pe), full(w1), full(b1r), full(w2)],
        out_specs=pl.BlockSpec((1, nchunk, NSA_KV_DIM), lambda i: (i, 0, 0)),
        out_shape=jax.ShapeDtypeStruct((b, nchunk, NSA_KV_DIM), BF16),
        compiler_params=_cparams(("parallel",)),
    )(*([u3] * NSA_GROUPS), pe, w1, b1r, w2)


def _rope128(x, c, s):
    return x * c + pltpu.roll(x, NSA_DH // 2, 1) * s


def _pack_bf16_pairs(x):
    w = x.shape[1] // 2
    bits = pltpu.bitcast(x.astype(BF16).astype(F32), jnp.uint32)
    word = bits[:, w:] | lax.shift_right_logical(bits[:, :w], jnp.uint32(16))
    return pltpu.bitcast(word, F32)


def _unpack_bf16_pairs(word):
    bits = pltpu.bitcast(word, jnp.uint32)
    lo = pltpu.bitcast(lax.shift_left(bits, jnp.uint32(16)), F32).astype(BF16)
    hi = pltpu.bitcast(bits & jnp.uint32(0xFFFF0000), F32).astype(BF16)
    return lo, hi


def _split3(x):
    hi = x.astype(BF16)
    r1 = x - hi.astype(F32)
    mid = r1.astype(BF16)
    lo = (r1 - mid.astype(F32)).astype(BF16)
    return hi, mid, lo


def _nsa_attn_kernel(q_ref, ks_ref, vs_ref, kw_ref, vw_ref, gt_ref, kc_ref, vc_ref,
                     cq_ref, sq_ref, ck_ref, sk_ref, ovl_ref, exp_ref, wb_ref,
                     o_ref, ksr, vsb, kwr, vwb, selb, m_ref, acc_ref,
                     *, tq, tk, seq, scale):
    qi = pl.program_id(2)
    hp = NSA_HPG
    n_sel = seq // SEL_BLOCK

    @pl.when(qi == 0)
    def _():
        ck, sk = ck_ref[...], sk_ref[...]
        ksr[...] = _rope128(ks_ref[0], ck, sk).astype(BF16)
        kwr[...] = _rope128(kw_ref[0], ck, sk).astype(BF16)
        vsb[...] = _values_and_ones(vs_ref[0])
        vwb[...] = _values_and_ones(vw_ref[0])

    q = q_ref[0] * (scale * LOG2E)
    cq, sq = cq_ref[...], sq_ref[...]
    heads = [q[:, p * NSA_DH:(p + 1) * NSA_DH] for p in range(hp)]
    qu = jnp.concatenate(heads, axis=0).astype(BF16)
    qr = jnp.concatenate([_rope128(h, cq, sq) for h in heads], axis=0).astype(BF16)
    t_row = qi * tq + lax.broadcasted_iota(jnp.int32, (tq, 1), 0)
    lane = lax.broadcasted_iota(jnp.int32, (1, LANES), 1)

    sc = _dot_nt(qu, kc_ref[0]).reshape(hp, tq, LANES)
    valid = (lane * CMP_STRIDE + (CMP_BLOCK - 1) <= t_row)[None]
    sm = jnp.where(valid, sc, NEG_INF)
    e = jnp.exp2(sm - jnp.max(sm, axis=-1, keepdims=True))
    p = jnp.where(valid, e / jnp.sum(e, axis=-1, keepdims=True), 0.0)
    o_cmp = _dot(p.reshape(hp * tq, LANES).astype(BF16), vc_ref[0])

    psum = p[0]
    for i in range(1, hp):
        psum = psum + p[i]
    ovl = ovl_ref[...]
    imp = sum(_dot(part, ovl) for part in _split3(psum))
    cur = jnp.right_shift(t_row, SEL_BLOCK.bit_length() - 1)
    future = lane > cur
    forced = (lane == 0) | (lane == cur) | (lane == cur - 1)
    score = jnp.where(future, -1.0, jnp.where(forced, SEL_FORCE, imp))
    sc_t = score.T[0:n_sel, :]
    blk = lax.broadcasted_iota(jnp.int32, (n_sel, 1), 0)
    cnt = jnp.zeros((n_sel, tq), F32)
    for j in range(n_sel):
        other = sc_t[j:j + 1, :]
        beats = (other > sc_t) | ((other == sc_t) & (blk > j))
        cnt = cnt + jnp.where(beats, 1.0, 0.0)
    sel_t = jnp.where(cnt < SEL_COUNT, 1.0, 0.0)
    sel = jnp.concatenate([sel_t, jnp.zeros((LANES - n_sel, tq), F32)], axis=0).T.astype(BF16)
    picked = _dot(sel, exp_ref[...])
    kall = lax.broadcasted_iota(jnp.int32, (1, seq), 1)
    selb[...] = jnp.where((picked > 0.5) & (kall <= t_row), 0.0, NEG_INF)

    m_ref[...] = jnp.full(m_ref.shape, NEG_INF, F32)
    acc_ref[...] = jnp.zeros(acc_ref.shape, F32)
    for c in range(seq // tk):
        @pl.when(c * tk <= qi * tq + (tq - 1))
        def _():
            for i in range(hp):
                rows = pl.ds(i * tq, tq)
                s = _dot_nt(qr[i * tq:(i + 1) * tq], ksr[c * tk:(c + 1) * tk, :])
                _softmax_step(s + selb[:, c * tk:(c + 1) * tk], vsb[c * tk:(c + 1) * tk, :],
                              m_ref.at[rows], acc_ref.at[rows])
    o_sel = _softmax_result(acc_ref[...])

    span = WINDOW + tq
    start = pl.multiple_of(jnp.maximum(qi * tq - WINDOW, 0), tq)
    wins = []
    for i in range(hp):
        sw = _dot_nt(qr[i * tq:(i + 1) * tq], kwr[pl.ds(start, span), :]) + wb_ref[0]
        ew = jnp.exp2(sw - jnp.max(sw, axis=-1, keepdims=True)).astype(BF16)
        wins.append(_softmax_result(_dot(ew, vwb[pl.ds(start, span), :])))
    o_win = jnp.concatenate(wins, axis=0)

    gates = _sigmoid(gt_ref[0])
    for i in range(hp):
        rows = slice(i * tq, (i + 1) * tq)
        o = (gates[:, 3 * i:3 * i + 1] * o_cmp[rows] + gates[:, 3 * i + 1:3 * i + 2] * o_sel[rows]
             + gates[:, 3 * i + 2:3 * i + 3] * o_win[rows])
        o_ref[0, :, i * NSA_DH:(i + 1) * NSA_DH] = o.astype(o_ref.dtype)


def _window_bias(tq):
    span = WINDOW + tq
    out = []
    for qi in range(WINDOW // tq + 1):
        start = max(qi * tq - WINDOW, 0)
        t = qi * tq + np.arange(tq)[:, None]
        kpos = start + np.arange(span)[None, :]
        out.append(np.where((kpos <= t) & (kpos > t - WINDOW), 0.0, NEG_INF))
    return jnp.asarray(np.stack(out), F32)


def nsa_attention(u3, k_cmp, v_cmp, rope_c, rope_s, ovl, expand, *, tq, tk):
    b, s, _ = u3.shape
    assert s // CMP_STRIDE == LANES and tq % SEL_BLOCK == 0 and WINDOW % tq == 0
    hp = NSA_HPG
    nwb = WINDOW // tq
    wbias = _window_bias(tq)
    col = lambda off: (lambda b_, g, i: (b_, 0, off // NSA_DH + g))
    seq_blk = lambda off: pl.BlockSpec((1, s, NSA_DH), col(off))
    full = lambda a: pl.BlockSpec(a.shape, lambda b_, g, i: (0,) * a.ndim)
    cmp_blk = pl.BlockSpec((1, LANES, NSA_DH), lambda b_, g, i: (b_, 0, g))
    rope_q = pl.BlockSpec((tq, NSA_DH), lambda b_, g, i: (i, 0))
    kern = functools.partial(_nsa_attn_kernel, tq=tq, tk=tk, seq=s, scale=NSA_DH ** -0.5)
    return pl.pallas_call(
        kern,
        grid=(b, NSA_GROUPS, s // tq),
        in_specs=[pl.BlockSpec((1, tq, hp * NSA_DH), lambda b_, g, i: (b_, i, g)),
                  seq_blk(OD_KS), seq_blk(OD_VS), seq_blk(OD_KW), seq_blk(OD_VW),
                  pl.BlockSpec((1, tq, LANES), lambda b_, g, i: (b_, i, OD_G // LANES + g)),
                  cmp_blk, cmp_blk, rope_q, rope_q, full(rope_c), full(rope_s), full(ovl), full(expand),
                  pl.BlockSpec((1, tq, WINDOW + tq), lambda b_, g, i: (jnp.minimum(i, nwb), 0, 0))],
        out_specs=pl.BlockSpec((1, tq, hp * NSA_DH), lambda b_, g, i: (b_, i, g)),
        out_shape=jax.ShapeDtypeStruct((b, s, NSA_Q_DIM), BF16),
        scratch_shapes=[pltpu.VMEM((s, NSA_DH), BF16), pltpu.VMEM((s, 2 * LANES), BF16)] * 2 + [
            pltpu.VMEM((tq, s), F32),
            pltpu.VMEM((hp * tq, LANES), F32), pltpu.VMEM((hp * tq, 2 * LANES), F32)],
        compiler_params=_cparams(("parallel", "parallel", "arbitrary")),
    )(u3, u3, u3, u3, u3, u3, k_cmp, v_cmp, rope_c, rope_s, rope_c, rope_s, ovl, expand, wbias)


def _router_kernel(x_ref, g_ref, wr_ref, br_ref, xn_ref, meta_ref, cnt_ref, *, tm):
    @pl.when(pl.program_id(0) == 0)
    def _():
        cnt_ref[...] = jnp.zeros(cnt_ref.shape, F32)

    xn = _rms(x_ref[...], g_ref[...])
    xn_ref[...] = _pack_bf16_pairs(xn)
    xh, xm, _ = _split3(xn)
    wh, wm, _ = _split3(wr_ref[...])
    logits = _dot(xh, wh) + _dot(xh, wm) + _dot(xm, wh) + br_ref[...]
    lane = lax.broadcasted_iota(jnp.int32, (1, LANES), 1).astype(F32)
    lg = jnp.where(lane < N_EXPERTS, logits, NEG_INF)
    m1 = jnp.max(lg, axis=-1, keepdims=True)
    e1 = jnp.min(jnp.where(lg == m1, lane, float(LANES)), axis=-1, keepdims=True)
    lg2 = jnp.where(lane == e1, NEG_INF, lg)
    m2 = jnp.max(lg2, axis=-1, keepdims=True)
    e2 = jnp.min(jnp.where(lg2 == m2, lane, float(LANES)), axis=-1, keepdims=True)
    ex = jnp.exp(m2 - m1)
    den = 1.0 + ex
    g1 = 1.0 / den
    g2 = ex / den
    oh = jnp.where((lane == e1) | (lane == e2), 1.0, 0.0)
    r = lax.broadcasted_iota(jnp.int32, (tm, tm), 0)
    c = lax.broadcasted_iota(jnp.int32, (tm, tm), 1)
    tri = jnp.where(r > c, 1.0, 0.0).astype(BF16)
    cum = _dot(tri, oh.astype(BF16)) + cnt_ref[0:1, :]
    pos1 = jnp.sum(jnp.where(lane == e1, cum, 0.0), axis=-1, keepdims=True)
    pos2 = jnp.sum(jnp.where(lane == e2, cum, 0.0), axis=-1, keepdims=True)
    cnt_ref[...] = cnt_ref[...] + jnp.sum(oh, axis=0, keepdims=True)
    meta = jnp.where(lane == 0, e1, 0.0)
    meta = jnp.where(lane == 1, e2, meta)
    meta = jnp.where(lane == 2, g1, meta)
    meta = jnp.where(lane == 3, g2, meta)
    meta = jnp.where(lane == 4, pos1, meta)
    meta = jnp.where(lane == 5, pos2, meta)
    meta_ref[...] = meta


def moe_router(x, g, wr, br, *, tm):
    n, d = x.shape
    return pl.pallas_call(
        functools.partial(_router_kernel, tm=tm),
        grid=(n // tm,),
        in_specs=[pl.BlockSpec((tm, d), lambda i: (i, 0)),
                  pl.BlockSpec((1, d), lambda i: (0, 0)),
                  pl.BlockSpec((d, LANES), lambda i: (0, 0)),
                  pl.BlockSpec((1, LANES), lambda i: (0, 0))],
        out_specs=[pl.BlockSpec((tm, d // 2), lambda i: (i, 0)),
                   pl.BlockSpec((tm, LANES), lambda i: (i, 0)),
                   pl.BlockSpec((8, LANES), lambda i: (0, 0))],
        out_shape=[jax.ShapeDtypeStruct((n, d // 2), F32),
                   jax.ShapeDtypeStruct((n, LANES), F32),
                   jax.ShapeDtypeStruct((8, LANES), F32)],
        compiler_params=_cparams(("arbitrary",)),
    )(x, g.reshape(1, d), wr, br)


def _gather_chunk(per_worker, row_bytes):
    best = 0
    for c in range(8, per_worker + 1, 8):
        if per_worker % c == 0 and c * row_bytes <= SC_GATHER_BYTES and c <= LANES:
            best = c
    assert best > 0, (per_worker, row_bytes)
    return best


def sc_gather_rows(table, idx):
    _, d = table.shape
    b = idx.shape[0]
    assert b % (8 * SC_WORKERS) == 0 and table.dtype.itemsize == 4
    per_w = b // SC_WORKERS
    chunk = _gather_chunk(per_w, d * 4)
    mesh = plsc.VectorSubcoreMesh(core_axis_name="c", subcore_axis_name="s",
                                  num_cores=SC_CORES, num_subcores=SC_SUBCORES)

    @functools.partial(
        pl.kernel, mesh=mesh,
        out_type=jax.ShapeDtypeStruct((b, d), table.dtype),
        scratch_types=[pltpu.VMEM((chunk,), jnp.int32), pltpu.VMEM((chunk, d), table.dtype),
                       pltpu.SemaphoreType.DMA])
    def gather(table_hbm, idx_hbm, out_hbm, idx_v, rows_v, sem):
        wid = lax.axis_index("s") * SC_CORES + lax.axis_index("c")
        base = wid * per_w

        @pl.loop(0, per_w // chunk)
        def _(c):
            off = pl.multiple_of(base + c * chunk, 8)
            pltpu.sync_copy(idx_hbm.at[pl.ds(off, chunk)], idx_v)
            pltpu.async_copy(table_hbm.at[idx_v], rows_v, sem).wait()
            pltpu.sync_copy(rows_v, out_hbm.at[pl.ds(off, chunk)])

    return gather(table, idx)


def sc_scatter_rows(src, dests, n_out):
    n, d = src.shape
    assert n % (8 * SC_WORKERS) == 0 and src.dtype.itemsize == 4
    per_w = n // SC_WORKERS
    chunk = _gather_chunk(per_w, d * 4)
    n_dest = len(dests)
    mesh = plsc.VectorSubcoreMesh(core_axis_name="c", subcore_axis_name="s",
                                  num_cores=SC_CORES, num_subcores=SC_SUBCORES)

    @functools.partial(
        pl.kernel, mesh=mesh,
        out_type=jax.ShapeDtypeStruct((n_out, d), src.dtype),
        scratch_types=[pltpu.VMEM((chunk,), jnp.int32), pltpu.VMEM((chunk, d), src.dtype),
                       pltpu.SemaphoreType.DMA])
    def scatter(src_hbm, *rest):
        dest_hbms = rest[:n_dest]
        out_hbm, idx_v, rows_v, sem = rest[n_dest:]
        wid = lax.axis_index("s") * SC_CORES + lax.axis_index("c")
        base = wid * per_w

        @pl.loop(0, per_w // chunk)
        def _(c):
            off = pl.multiple_of(base + c * chunk, 8)
            pltpu.sync_copy(src_hbm.at[pl.ds(off, chunk)], rows_v)
            for dest_hbm in dest_hbms:
                pltpu.sync_copy(dest_hbm.at[pl.ds(off, chunk)], idx_v)
                pltpu.async_copy(rows_v, out_hbm.at[idx_v], sem).wait()

    return scatter(src, *dests)


def _expert_kernel(te_ref, ts_ref, tr_ref, xp_hbm, *rest, nf):
    ns = MOE_WSPLIT
    w1_refs, w3_refs, w2_refs = rest[:ns], rest[ns:2 * ns], rest[2 * ns:3 * ns]
    yr_hbm, xb, acc, stage, in_sem, out_sem = rest[3 * ns:]
    t = pl.program_id(0)
    f = pl.program_id(1)
    rows = tr_ref[t]
    start = pl.multiple_of(ts_ref[t], MOE_ALIGN)
    n_chunks = MOE_TILE // MOE_COPY
    half = xb.shape[1] // 2

    def in_copy(ci):
        return pltpu.make_async_copy(xp_hbm.at[pl.ds(start + ci * MOE_COPY, MOE_COPY)],
                                     stage.at[ci % 2], in_sem.at[ci % 2])

    def out_copy(ci, first_row):
        return pltpu.make_async_copy(acc.at[pl.ds(ci * MOE_COPY, MOE_COPY)],
                                     yr_hbm.at[pl.ds(first_row + ci * MOE_COPY, MOE_COPY)], out_sem.at[0])

    def when_chunk_live(ci, fn):
        pl.when(ci * MOE_COPY < rows)(fn)

    def wait_outputs(tile):
        tile_rows = tr_ref[tile]
        first_row = pl.multiple_of(ts_ref[tile], MOE_ALIGN)
        for ci in range(n_chunks):
            pl.when(ci * MOE_COPY < tile_rows)(lambda ci=ci: out_copy(ci, first_row).wait())

    @pl.when((f == 0) & (rows > 0))
    def _():
        def unpack(ci):
            in_copy(ci).wait()
            lo, hi = _unpack_bf16_pairs(stage[ci % 2])
            xb[ci * MOE_COPY:(ci + 1) * MOE_COPY, :half] = lo
            xb[ci * MOE_COPY:(ci + 1) * MOE_COPY, half:] = hi

        when_chunk_live(0, lambda: in_copy(0).start())
        for ci in range(n_chunks):
            if ci + 1 < n_chunks:
                when_chunk_live(ci + 1, lambda ci=ci: in_copy(ci + 1).start())
            when_chunk_live(ci, lambda ci=ci: unpack(ci))

    @pl.when((f == 0) & (t > 0))
    def _():
        wait_outputs(jnp.maximum(t - 1, 0))

    @pl.when((f == 0) & (rows > 0))
    def _():
        acc[...] = jnp.zeros(acc.shape, F32)

    @pl.when(rows > 0)
    def _():
        w1 = jnp.concatenate([r[0, 0].astype(BF16) for r in w1_refs], axis=0)
        w3 = jnp.concatenate([r[0, 0].astype(BF16) for r in w3_refs], axis=0)
        w2 = jnp.concatenate([r[0, 0].astype(BF16) for r in w2_refs], axis=1)

        def chain(r0, size):
            rs = pl.ds(pl.multiple_of(r0, MOE_PIECES[-1]), size)
            xc = xb[rs, :]
            h1 = _dot(xc, w1)
            h3 = _dot(xc, w3)
            hh = (h1 * _sigmoid(h1) * h3).astype(BF16)
            acc[rs, :] += _dot(hh, w2)

        todo = (rows + (MOE_PIECES[-1] - 1)) // MOE_PIECES[-1] * MOE_PIECES[-1]
        off = jnp.int32(0)
        for size in MOE_PIECES:
            take = todo - off >= size

            @pl.when(take)
            def _(off=off, size=size):
                for c0 in range(0, size, MOE_CHAIN):
                    chain(off + c0, min(MOE_CHAIN, size))

            off = off + jnp.where(take, size, 0)

    @pl.when((f == nf - 1) & (rows > 0))
    def _():
        for ci in range(n_chunks):
            when_chunk_live(ci, lambda ci=ci: out_copy(ci, start).start())

    @pl.when((f == nf - 1) & (t == pl.num_programs(0) - 1))
    def _():
        wait_outputs(t)


def moe_experts(xp, tile_e, tile_start, tile_rows, w1, w3, w2, layer, *, tf):
    n_rows, half = xp.shape
    d = 2 * half
    n_tiles = tile_e.shape[0]
    ff = w1.shape[3]
    nf = ff // tf
    assert MOE_TILE % MOE_COPY == 0 and sum(MOE_PIECES) >= MOE_TILE

    def f_eff(t, f, tr):
        return jnp.where(tr[t] > 0, f, nf - 1)

    ns = MOE_WSPLIT
    up_specs = [pl.BlockSpec((1, 1, d // ns, tf), lambda t, f, te, ts, tr, k=k: (layer, te[t], k, f_eff(t, f, tr)))
                for k in range(ns)]
    down_specs = [pl.BlockSpec((1, 1, tf, d // ns), lambda t, f, te, ts, tr, k=k: (layer, te[t], f_eff(t, f, tr), k))
                  for k in range(ns)]
    grid_spec = pltpu.PrefetchScalarGridSpec(
        num_scalar_prefetch=3,
        grid=(n_tiles, nf),
        in_specs=[pl.BlockSpec(memory_space=pl.ANY)] + 2 * up_specs + down_specs,
        out_specs=pl.BlockSpec(memory_space=pl.ANY),
        scratch_shapes=[pltpu.VMEM((MOE_TILE, d), BF16), pltpu.VMEM((MOE_TILE, d), F32),
                        pltpu.VMEM((2, MOE_COPY, half), F32),
                        pltpu.SemaphoreType.DMA((2,)), pltpu.SemaphoreType.DMA((1,))],
    )
    return pl.pallas_call(
        functools.partial(_expert_kernel, nf=nf),
        grid_spec=grid_spec,
        out_shape=jax.ShapeDtypeStruct((n_rows, d), F32),
        compiler_params=_cparams(("arbitrary", "arbitrary")),
    )(tile_e, tile_start, tile_rows, xp, *([w1] * ns), *([w3] * ns), *([w2] * ns))


def _combine_kernel(x_ref, y1_ref, y2_ref, meta_ref, g_ref, o_ref, *, final_norm):
    meta = meta_ref[...]
    y = x_ref[...] + (meta[:, 2:3] * y1_ref[...] + meta[:, 3:4] * y2_ref[...])
    if final_norm:
        y = _rms(y, g_ref[...])
    o_ref[...] = y


def moe_combine(x, yg, meta, g, *, final_norm, tm):
    n, d = x.shape
    nb = n // tm
    row = pl.BlockSpec((tm, d), lambda i: (i, 0))
    return pl.pallas_call(
        functools.partial(_combine_kernel, final_norm=final_norm),
        grid=(nb,),
        in_specs=[row, row, pl.BlockSpec((tm, d), lambda i: (i + nb, 0)),
                  pl.BlockSpec((tm, LANES), lambda i: (i, 0)),
                  pl.BlockSpec((1, d), lambda i: (0, 0))],
        out_specs=row,
        out_shape=jax.ShapeDtypeStruct((n, d), F32),
        compiler_params=_cparams(("parallel",)),
    )(x, yg, yg, meta, g.reshape(1, d))


def _mla_rope_tables(seq):
    half = MLA_ROPE // 2
    pos = jnp.arange(seq, dtype=F32)
    inv = ROPE_THETA ** (-jnp.arange(0, MLA_ROPE, 2, dtype=F32) / MLA_ROPE)
    ang = pos[:, None] * inv[None, :]
    cos, sin = jnp.cos(ang), jnp.sin(ang)
    z = jnp.zeros((seq, half), F32)
    pad = jnp.zeros((seq, LANES - MLA_ROPE), F32)
    c = jnp.concatenate([cos, cos, pad], axis=1)
    s_lo = jnp.concatenate([-sin, z, pad], axis=1)
    s_hi = jnp.concatenate([z, sin, pad], axis=1)
    return c, s_lo, s_hi


def _nsa_rope_tables(seq):
    pos = jnp.arange(seq, dtype=F32)
    inv = ROPE_THETA ** (-jnp.arange(0, NSA_DH, 2, dtype=F32) / NSA_DH)
    ang = pos[:, None] * inv[None, :]
    cos, sin = jnp.cos(ang), jnp.sin(ang)
    return jnp.concatenate([cos, cos], axis=1), jnp.concatenate([-sin, sin], axis=1)


def _selection_constants(seq):
    n_sel = seq // SEL_BLOCK
    nc = (seq - CMP_BLOCK) // CMP_STRIDE + 1
    cmp_start = np.arange(LANES) * CMP_STRIDE
    sel_start = np.arange(LANES) * SEL_BLOCK
    ovl = ((cmp_start[:, None] < sel_start[None, :] + SEL_BLOCK) &
           (cmp_start[:, None] + CMP_BLOCK > sel_start[None, :]))
    ovl &= (np.arange(LANES)[:, None] < nc) & (np.arange(LANES)[None, :] < n_sel)
    expand = (np.arange(seq)[None, :] // SEL_BLOCK == np.arange(LANES)[:, None])
    return jnp.asarray(ovl, BF16), jnp.asarray(expand, BF16)


def even_layer(x, seq, p):
    n = x.shape[0]
    b = n // seq
    (norm_mix, w_in, q_norm, w_q_up, kv_norm, w_kv_up, conv_w, conv_b, ga_w, ga_b, gx_w, gx_b,
     lam, w_out, norm_ffn, w1, w3, w2) = p
    d = D_MODEL
    o1 = MLA_Q_LORA + MLA_KV_LORA
    o2 = o1 + MLA_ROPE
    w_pack = jnp.concatenate(
        [w_in[:, :o1], w_in[:, o2:], w_in[:, o1:o2], jnp.zeros((d, LANES - MLA_ROPE), F32)], axis=1).astype(BF16)
    u = norm_matmul(x, norm_mix, w_pack, tm=1024, tn=EV_PACKED // 3)

    wq = w_q_up.reshape(MLA_Q_LORA, MLA_HEADS, MLA_NOPE + MLA_ROPE)
    wq = jnp.pad(wq, ((0, 0), (0, 0), (0, MLA_QK_PAD - MLA_NOPE - MLA_ROPE)))
    wq = wq.reshape(MLA_Q_LORA, MLA_HEADS * MLA_QK_PAD).astype(BF16)
    wkv = w_kv_up.reshape(MLA_KV_LORA, MLA_HEADS, MLA_NOPE + MLA_V)
    wk = wkv[:, :, :MLA_NOPE].reshape(MLA_KV_LORA, MLA_HEADS * MLA_NOPE).astype(BF16)
    wv = wkv[:, :, MLA_NOPE:].reshape(MLA_KV_LORA, MLA_HEADS * MLA_V).astype(BF16)
    rc, rlo, rhi = _mla_rope_tables(seq)
    q, k, v = mla_up(u, q_norm, kv_norm, wq, wk, wv, rc, rlo, rhi, seq=seq, tm=512)
    o_mla = mla_attention(q.reshape(b, seq, -1), k.reshape(b, seq, -1), v.reshape(b, seq, -1), t=512)

    o_rec = rglru(u.reshape(b, seq, EV_PACKED), conv_w, conv_b, ga_w.astype(BF16), ga_b,
                  gx_w.astype(BF16), gx_b, lam, ts=512)
    x = matmul_residual([o_mla.reshape(n, -1), o_rec.reshape(n, -1)], w_out.astype(BF16), x, tm=512, tn=D_MODEL)
    return ffn_dense(x, norm_ffn, w1.astype(BF16), w3.astype(BF16), w2.astype(BF16), tm=1024, tf=512)


def _moe_dispatch(meta, counts, n):
    e1 = meta[:, 0].astype(jnp.int32)
    e2 = meta[:, 1].astype(jnp.int32)
    pos1 = meta[:, 4].astype(jnp.int32)
    pos2 = meta[:, 5].astype(jnp.int32)
    cnt = counts[0, :N_EXPERTS].astype(jnp.int32)
    span = (cnt + MOE_ALIGN - 1) // MOE_ALIGN * MOE_ALIGN
    row0 = jnp.cumsum(span) - span
    row_unit = 64 * SC_WORKERS
    n_rows = -(-(n * TOP_K + N_EXPERTS * MOE_ALIGN + MOE_TILE) // row_unit) * row_unit
    d1 = row0[e1] + pos1
    d2 = row0[e2] + pos2
    n_tiles = (n * TOP_K) // MOE_TILE + N_EXPERTS
    tiles_e = (cnt + MOE_TILE - 1) // MOE_TILE
    tend = jnp.cumsum(tiles_e)
    tbeg = tend - tiles_e
    tid = jnp.arange(n_tiles, dtype=jnp.int32)
    te = jnp.minimum(jnp.searchsorted(tend, tid, side='right'), N_EXPERTS - 1).astype(jnp.int32)
    used = tid < tend[-1]
    first = (tid - tbeg[te]) * MOE_TILE
    rows = jnp.where(used, jnp.clip(cnt[te] - first, 0, MOE_TILE), 0).astype(jnp.int32)
    tstart = jnp.where(used, row0[te] + first, 0).astype(jnp.int32)
    last_e = te[jnp.maximum(tend[-1] - 1, 0)]
    te = jnp.where(used, te, last_e).astype(jnp.int32)
    return d1, d2, n_rows, te, tstart, rows


def odd_layer(x, seq, p, experts, final_g):
    n = x.shape[0]
    b = n // seq
    (norm_mix, w_in, ck_pe, ck_w1, ck_b1, ck_w2, cv_pe, cv_w1, cv_b1, cv_w2, w_out, norm_ffn,
     router_w, router_b) = p
    ew1, ew3, ew2, layer = experts
    d = D_MODEL
    wg = w_in[:, OD_G:].reshape(d, NSA_GROUPS, NSA_HPG * 3)
    wg = jnp.pad(wg, ((0, 0), (0, 0), (0, LANES - NSA_HPG * 3))).reshape(d, NSA_GROUPS * LANES)
    w_pack = jnp.concatenate([w_in[:, :OD_G], wg], axis=1).astype(BF16)
    u = norm_matmul(x, norm_mix, w_pack, tm=1024, tn=512)
    u3 = u.reshape(b, seq, OD_PACKED)

    k_cmp = nsa_compress(u3, OD_KC, ck_pe, ck_w1.astype(BF16), ck_b1, ck_w2.astype(BF16))
    v_cmp = nsa_compress(u3, OD_VC, cv_pe, cv_w1.astype(BF16), cv_b1, cv_w2.astype(BF16))
    rc, rs = _nsa_rope_tables(seq)
    ovl, expand = _selection_constants(seq)
    o = nsa_attention(u3, k_cmp, v_cmp, rc, rs, ovl, expand, tq=256, tk=512)
    x = matmul_residual([o.reshape(n, -1)], w_out.astype(BF16), x, tm=512, tn=D_MODEL)

    wr = jnp.pad(router_w, ((0, 0), (0, LANES - N_EXPERTS)))
    br = jnp.pad(router_b, (0, LANES - N_EXPERTS)).reshape(1, LANES)
    xp, meta, counts = moe_router(x, norm_ffn, wr, br, tm=512)
    d1, d2, n_rows, te, tstart, rows = _moe_dispatch(meta, counts, n)
    yr = moe_experts(sc_scatter_rows(xp, [d1, d2], n_rows), te, tstart, rows, ew1, ew3, ew2, layer, tf=256)
    yg = sc_gather_rows(yr, jnp.concatenate([d1, d2]))
    g = final_g if final_g is not None else norm_ffn
    return moe_combine(x, yg, meta, g, final_norm=final_g is not None, tm=512)


def kernel(x, ev_norm_mix, ev_w_in, ev_q_norm, ev_w_q_up, ev_kv_norm, ev_w_kv_up, ev_conv_w, ev_conv_b, ev_gate_a_w, ev_gate_a_b, ev_gate_x_w, ev_gate_x_b, ev_lru_lambda, ev_w_out, ev_norm_ffn, ev_ffn_w1, ev_ffn_w3, ev_ffn_w2, od_norm_mix, od_w_in, od_cmp_k_pe, od_cmp_k_w1, od_cmp_k_b1, od_cmp_k_w2, od_cmp_v_pe, od_cmp_v_w1, od_cmp_v_b1, od_cmp_v_w2, od_w_out, od_norm_ffn, od_router_w, od_router_b, od_exp_w1, od_exp_w3, od_exp_w2, final_norm):
    bsz, seq, d = x.shape
    ev = (ev_norm_mix, ev_w_in, ev_q_norm, ev_w_q_up, ev_kv_norm, ev_w_kv_up, ev_conv_w, ev_conv_b,
          ev_gate_a_w, ev_gate_a_b, ev_gate_x_w, ev_gate_x_b, ev_lru_lambda, ev_w_out, ev_norm_ffn,
          ev_ffn_w1, ev_ffn_w3, ev_ffn_w2)
    od = (od_norm_mix, od_w_in, od_cmp_k_pe, od_cmp_k_w1, od_cmp_k_b1, od_cmp_k_w2, od_cmp_v_pe,
          od_cmp_v_w1, od_cmp_v_b1, od_cmp_v_w2, od_w_out, od_norm_ffn, od_router_w, od_router_b)
    h = x.reshape(bsz * seq, d)
    for layer in range(DEPTH):
        i = layer // 2
        if layer % 2 == 0:
            h = even_layer(h, seq, tuple(a[i] for a in ev))
        else:
            h = odd_layer(h, seq, tuple(a[i] for a in od), (od_exp_w1, od_exp_w3, od_exp_w2, i),
                          final_norm if layer == DEPTH - 1 else None)
    return h.reshape(bsz, seq, d)
```

```python
import functools
import math

import numpy as np
import jax
import jax.numpy as jnp
from jax import lax
from jax.experimental import pallas as pl
from jax.experimental.pallas import tpu as pltpu
from jax.experimental.pallas import tpu_sc as plsc

F32 = jnp.float32
BF16 = jnp.bfloat16

D_MODEL = 2048
DEPTH = 4
RMS_EPS = 1e-6
ROPE_THETA = 10000.0
NEG_INF = -1e30
LOG2E = math.log2(math.e)

MLA_HEADS = 8
MLA_Q_LORA = 768
MLA_KV_LORA = 512
MLA_NOPE = 128
MLA_ROPE = 64
MLA_V = 128
MLA_QK_PAD = 256
MLA_ROW_GROUPS = 2

LRU_WIDTH = D_MODEL // 2
LRU_BLOCKS = 8
LRU_BLOCK_W = LRU_WIDTH // LRU_BLOCKS
LRU_C = 8.0
CONV_WIDTH = 4

NSA_HEADS = 16
NSA_GROUPS = 4
NSA_HPG = NSA_HEADS // NSA_GROUPS
NSA_DH = D_MODEL // NSA_HEADS
CMP_BLOCK = 32
CMP_STRIDE = 16
SEL_BLOCK = 64
SEL_COUNT = 16
SEL_FORCE = 1e4
WINDOW = 512
NSA_Q_DIM = NSA_HEADS * NSA_DH
NSA_KV_DIM = NSA_GROUPS * NSA_DH

DENSE_FF = 5632
N_EXPERTS = 8
TOP_K = 2
EXPERT_FF = 7168

LANES = 128
SC_CORES = 2
SC_SUBCORES = 16
SC_WORKERS = SC_CORES * SC_SUBCORES
SC_GATHER_BYTES = 256 * 1024
VMEM_LIMIT = 60 * 1024 * 1024

EV_CQ = 0
EV_CKV = MLA_Q_LORA
EV_REC = MLA_Q_LORA + MLA_KV_LORA
EV_GATE = EV_REC + LRU_WIDTH
EV_PE = EV_GATE + LRU_WIDTH
EV_PACKED = EV_PE + LANES

OD_Q = 0
OD_KC = NSA_Q_DIM
OD_VC = OD_KC + NSA_KV_DIM
OD_KS = OD_VC + NSA_KV_DIM
OD_VS = OD_KS + NSA_KV_DIM
OD_KW = OD_VS + NSA_KV_DIM
OD_VW = OD_KW + NSA_KV_DIM
OD_G = OD_VW + NSA_KV_DIM
OD_PACKED = OD_G + NSA_GROUPS * LANES

MOE_TILE = 2688
MOE_ALIGN = 16
MOE_COPY = 384
MOE_PIECES = (1024, 1024, 512, 256, 128)
MOE_CHAIN = 512
MOE_WSPLIT = 1


def _cparams(sem):
    return pltpu.CompilerParams(dimension_semantics=sem, vmem_limit_bytes=VMEM_LIMIT)


def _rms(x, g):
    ms = jnp.mean(x * x, axis=-1, keepdims=True)
    return x * lax.rsqrt(ms + RMS_EPS) * g


def _sigmoid(x):
    return 1.0 / (1.0 + jnp.exp(-x))


def _gelu_tanh(x):
    return 0.5 * x * (1.0 + jnp.tanh(math.sqrt(2.0 / math.pi) * (x + 0.044715 * (x * x * x))))


def _dot(a, b):
    return jnp.dot(a, b, preferred_element_type=F32)


def _dot_nt(a, b):
    return lax.dot_general(a, b, (((1,), (1,)), ((), ())), preferred_element_type=F32)


def _norm_mm_kernel(x_ref, g_ref, w_ref, o_ref, xn_ref):
    @pl.when(pl.program_id(1) == 0)
    def _():
        xn_ref[...] = _rms(x_ref[...], g_ref[...]).astype(BF16)

    o_ref[...] = _dot(xn_ref[...], w_ref[...]).astype(o_ref.dtype)


def norm_matmul(x, g, w, *, tm, tn, out_dtype=F32):
    n, k = x.shape
    m = w.shape[1]
    assert n % tm == 0 and m % tn == 0
    return pl.pallas_call(
        _norm_mm_kernel,
        grid=(n // tm, m // tn),
        in_specs=[pl.BlockSpec((tm, k), lambda i, j: (i, 0)),
                  pl.BlockSpec((1, k), lambda i, j: (0, 0)),
                  pl.BlockSpec((k, tn), lambda i, j: (0, j))],
        out_specs=pl.BlockSpec((tm, tn), lambda i, j: (i, j)),
        out_shape=jax.ShapeDtypeStruct((n, m), out_dtype),
        scratch_shapes=[pltpu.VMEM((tm, k), BF16)],
        compiler_params=_cparams(("parallel", "arbitrary")),
    )(x, g.reshape(1, k), w)


def _mm_res_kernel(*refs, n_in):
    xs = refs[:n_in]
    ws = refs[n_in:2 * n_in]
    res_ref = refs[2 * n_in]
    o_ref = refs[2 * n_in + 1]
    acc = res_ref[...]
    for x_ref, w_ref in zip(xs, ws):
        acc = acc + _dot(x_ref[...], w_ref[...])
    o_ref[...] = acc


def matmul_residual(xs, w, res, *, tm, tn):
    n = res.shape[0]
    m = w.shape[1]
    n_in = len(xs)
    in_specs = [pl.BlockSpec((tm, x.shape[1]), lambda i, j: (i, 0)) for x in xs]
    row = 0
    for x in xs:
        kx = x.shape[1]
        assert row % kx == 0
        in_specs.append(pl.BlockSpec((kx, tn), lambda i, j, rb=row // kx: (rb, j)))
        row += kx
    assert row == w.shape[0]
    in_specs.append(pl.BlockSpec((tm, tn), lambda i, j: (i, j)))
    return pl.pallas_call(
        functools.partial(_mm_res_kernel, n_in=n_in),
        grid=(n // tm, m // tn),
        in_specs=in_specs,
        out_specs=pl.BlockSpec((tm, tn), lambda i, j: (i, j)),
        out_shape=jax.ShapeDtypeStruct((n, m), F32),
        compiler_params=_cparams(("parallel", "arbitrary")),
    )(*xs, *([w] * n_in), res)


def _rope64(x, c, s_lo, s_hi):
    return x * c + pltpu.roll(x, 96, 1) * s_lo + pltpu.roll(x, 32, 1) * s_hi


def _mla_up_kernel(u_ref, pe_ref, qg_ref, kvg_ref, wq_ref, wk_ref, wv_ref,
                   c_ref, slo_ref, shi_ref, q_ref, k_ref, v_ref):
    u = u_ref[...]
    c, s_lo, s_hi = c_ref[...], slo_ref[...], shi_ref[...]
    qn = _rms(u[:, EV_CQ:EV_CQ + MLA_Q_LORA], qg_ref[...]).astype(BF16)
    kvn = _rms(u[:, EV_CKV:EV_CKV + MLA_KV_LORA], kvg_ref[...]).astype(BF16)
    q = _dot(qn, wq_ref[...]) * ((MLA_NOPE + MLA_ROPE) ** -0.5 * LOG2E)
    kn = _dot(kvn, wk_ref[...])
    v_ref[...] = _dot(kvn, wv_ref[...]).astype(v_ref.dtype)
    kpe = _rope64(pe_ref[...], c, s_lo, s_hi).astype(k_ref.dtype)
    for h in range(MLA_HEADS):
        a = h * MLA_QK_PAD
        q_ref[:, a:a + LANES] = q[:, a:a + LANES].astype(q_ref.dtype)
        q_ref[:, a + LANES:a + 2 * LANES] = _rope64(q[:, a + LANES:a + 2 * LANES], c, s_lo, s_hi).astype(q_ref.dtype)
        k_ref[:, a:a + LANES] = kn[:, h * LANES:(h + 1) * LANES].astype(k_ref.dtype)
        k_ref[:, a + LANES:a + 2 * LANES] = kpe


def mla_up(u, q_norm, kv_norm, wq, wk, wv, rope_c, rope_slo, rope_shi, *, seq, tm):
    n = u.shape[0]
    hq = MLA_HEADS * MLA_QK_PAD
    hv = MLA_HEADS * MLA_V
    ab = EV_REC
    assert seq % tm == 0 and EV_PE % LANES == 0
    nsb = seq // tm
    row_spec = pl.BlockSpec((tm, LANES), lambda i: (i % nsb, 0))
    full = lambda a: pl.BlockSpec(a.shape, lambda i: (0,) * a.ndim)
    qg = q_norm.reshape(1, -1)
    kvg = kv_norm.reshape(1, -1)
    return pl.pallas_call(
        _mla_up_kernel,
        grid=(n // tm,),
        in_specs=[pl.BlockSpec((tm, ab), lambda i: (i, 0)),
                  pl.BlockSpec((tm, LANES), lambda i: (i, EV_PE // LANES)),
                  full(qg), full(kvg), full(wq), full(wk), full(wv),
                  row_spec, row_spec, row_spec],
        out_specs=[pl.BlockSpec((tm, hq), lambda i: (i, 0)),
                   pl.BlockSpec((tm, hq), lambda i: (i, 0)),
                   pl.BlockSpec((tm, hv), lambda i: (i, 0))],
        out_shape=[jax.ShapeDtypeStruct((n, hq), BF16),
                   jax.ShapeDtypeStruct((n, hq), BF16),
                   jax.ShapeDtypeStruct((n, hv), BF16)],
        compiler_params=_cparams(("parallel",)),
    )(u, u, qg, kvg, wq, wk, wv, rope_c, rope_slo, rope_shi)


def _values_and_ones(v):
    return jnp.concatenate([v.astype(BF16), jnp.ones(v.shape, BF16)], axis=1)


def _lane_tile(x, n):
    return jnp.concatenate([x] * n, axis=1)


def _softmax_step(s, v1, m_ref, acc_ref):
    m_prev = m_ref[...]
    m_new = jnp.maximum(m_prev, jnp.max(s, axis=-1, keepdims=True))
    alpha = jnp.exp2(m_prev - m_new)
    p = jnp.exp2(s - _lane_tile(m_new, s.shape[1] // LANES))
    acc_ref[...] = _lane_tile(alpha, 2) * acc_ref[...] + _dot(p.astype(BF16), v1)
    m_ref[...] = m_new


def _softmax_result(acc):
    return acc[:, :LANES] / acc[:, LANES:]


def _mla_attn_kernel(q_ref, k_ref, v_ref, cb_ref, o_ref, v1, m_ref, acc_ref, *, t, n_chunks):
    qi = pl.program_id(2)

    @pl.when(qi == 0)
    def _():
        v1[...] = _values_and_ones(v_ref[0])

    m_ref[...] = jnp.full(m_ref.shape, NEG_INF, F32)
    acc_ref[...] = jnp.zeros(acc_ref.shape, F32)
    rg = t // MLA_ROW_GROUPS

    def chunk(c, masked):
        for i in range(MLA_ROW_GROUPS):
            rows = pl.ds(i * rg, rg)
            s = _dot_nt(q_ref[0, i * rg:(i + 1) * rg, :], k_ref[0, c * t:(c + 1) * t, :])
            if masked:
                s = s + cb_ref[i * rg:(i + 1) * rg, :]
            _softmax_step(s, v1[c * t:(c + 1) * t, :], m_ref.at[rows], acc_ref.at[rows])

    for c in range(n_chunks):
        pl.when(c < qi)(functools.partial(chunk, c, False))
        pl.when(c == qi)(functools.partial(chunk, c, True))
    o_ref[0] = _softmax_result(acc_ref[...]).astype(o_ref.dtype)


def mla_attention(q, k, v, *, t):
    b, s, _ = q.shape
    causal = jnp.asarray(np.where(np.arange(t)[None, :] <= np.arange(t)[:, None], 0.0, NEG_INF), F32)
    kern = functools.partial(_mla_attn_kernel, t=t, n_chunks=s // t)
    return pl.pallas_call(
        kern,
        grid=(b, MLA_HEADS, s // t),
        in_specs=[pl.BlockSpec((1, t, MLA_QK_PAD), lambda b_, h, i: (b_, i, h)),
                  pl.BlockSpec((1, s, MLA_QK_PAD), lambda b_, h, i: (b_, 0, h)),
                  pl.BlockSpec((1, s, MLA_V), lambda b_, h, i: (b_, 0, h)),
                  pl.BlockSpec((t, t), lambda b_, h, i: (0, 0))],
        out_specs=pl.BlockSpec((1, t, MLA_V), lambda b_, h, i: (b_, i, h)),
        out_shape=jax.ShapeDtypeStruct((b, s, MLA_HEADS * MLA_V), BF16),
        scratch_shapes=[pltpu.VMEM((s, 2 * LANES), BF16), pltpu.VMEM((t, LANES), F32),
                        pltpu.VMEM((t, 2 * LANES), F32)],
        compiler_params=_cparams(("parallel", "parallel", "arbitrary")),
    )(q, k, v, causal)


def _rglru_kernel(x_ref, y_ref, cw_ref, cb_ref, gaw_ref, gab_ref, gxw_ref, gxb_ref, lam_ref,
                  o_ref, xbuf, h_ref, *, ts):
    t = pl.program_id(2)

    @pl.when(t == 0)
    def _():
        xbuf[0:8, :] = jnp.zeros((8, LANES), F32)
        h_ref[...] = jnp.zeros(h_ref.shape, F32)

    x = x_ref[0]
    xbuf[8:, :] = x
    cw = cw_ref[...]
    xc = cb_ref[...] + cw[3:4] * x
    for kk in range(CONV_WIDTH - 1):
        back = CONV_WIDTH - 1 - kk
        xc = xc + cw[kk:kk + 1] * xbuf[8 - back:8 - back + ts, :]
    xbuf[0:8, :] = x[ts - 8:, :]

    xcb = xc.astype(BF16)
    r = _sigmoid(_dot(xcb, gaw_ref[0]) + gab_ref[0])
    gi = _sigmoid(_dot(xcb, gxw_ref[0]) + gxb_ref[0])
    z = -lam_ref[...]
    softplus = jnp.maximum(z, 0.0) + jnp.log1p(jnp.exp(-jnp.abs(z)))
    log_a = (-LRU_C) * r * softplus
    a = jnp.exp(log_a)
    mult = jnp.sqrt(-jnp.tanh(log_a) * (a * a + 1.0))
    row = lax.broadcasted_iota(jnp.int32, (ts, 1), 0)
    mult = jnp.where(row + t * ts == 0, 1.0, mult)
    bv = mult * gi * xc

    d = 1
    while d < ts:
        keep = row >= d
        a_sh = jnp.where(keep, pltpu.roll(a, d, 0), 1.0)
        b_sh = jnp.where(keep, pltpu.roll(bv, d, 0), 0.0)
        bv = a * b_sh + bv
        a = a * a_sh
        d *= 2
    h = bv + a * h_ref[...]
    h_ref[...] = h[ts - 1:ts, :]
    o_ref[0] = (h * _gelu_tanh(y_ref[0])).astype(o_ref.dtype)


def rglru(u3, conv_w, conv_b, ga_w, ga_b, gx_w, gx_b, lam, *, ts):
    b, s, _ = u3.shape
    rec0 = EV_REC // LANES
    gate0 = EV_GATE // LANES
    cb = conv_b.reshape(1, LRU_WIDTH)
    gab = ga_b.reshape(LRU_BLOCKS, 1, LRU_BLOCK_W)
    gxb = gx_b.reshape(LRU_BLOCKS, 1, LRU_BLOCK_W)
    lam2 = lam.reshape(1, LRU_WIDTH)
    blk_w = pl.BlockSpec((1, LRU_BLOCK_W, LRU_BLOCK_W), lambda b_, n, t: (n, 0, 0))
    blk_b = pl.BlockSpec((1, 1, LRU_BLOCK_W), lambda b_, n, t: (n, 0, 0))
    vec = pl.BlockSpec((1, LANES), lambda b_, n, t: (0, n))
    return pl.pallas_call(
        functools.partial(_rglru_kernel, ts=ts),
        grid=(b, LRU_BLOCKS, s // ts),
        in_specs=[pl.BlockSpec((1, ts, LANES), lambda b_, n, t: (b_, t, rec0 + n)),
                  pl.BlockSpec((1, ts, LANES), lambda b_, n, t: (b_, t, gate0 + n)),
                  pl.BlockSpec((CONV_WIDTH, LANES), lambda b_, n, t: (0, n)),
                  vec, blk_w, blk_b, blk_w, blk_b, vec],
        out_specs=pl.BlockSpec((1, ts, LANES), lambda b_, n, t: (b_, t, n)),
        out_shape=jax.ShapeDtypeStruct((b, s, LRU_WIDTH), BF16),
        scratch_shapes=[pltpu.VMEM((ts + 8, LANES), F32), pltpu.VMEM((1, LANES), F32)],
        compiler_params=_cparams(("parallel", "parallel", "arbitrary")),
    )(u3, u3, conv_w, cb, ga_w, gab, gx_w, gxb, lam2)


def _ffn_kernel(x_ref, g_ref, w1_ref, w3_ref, w2_ref, o_ref, xn_ref):
    f = pl.program_id(1)

    @pl.when(f == 0)
    def _():
        x = x_ref[...]
        xn_ref[...] = _rms(x, g_ref[...]).astype(BF16)
        o_ref[...] = x

    xn = xn_ref[...]
    h1 = _dot(xn, w1_ref[...])
    h3 = _dot(xn, w3_ref[...])
    hh = (h1 * _sigmoid(h1) * h3).astype(BF16)
    o_ref[...] += _dot(hh, w2_ref[...])


def ffn_dense(x, g, w1, w3, w2, *, tm, tf):
    n, d = x.shape
    ff = w1.shape[1]
    assert n % tm == 0 and ff % tf == 0
    return pl.pallas_call(
        _ffn_kernel,
        grid=(n // tm, ff // tf),
        in_specs=[pl.BlockSpec((tm, d), lambda i, f: (i, 0)),
                  pl.BlockSpec((1, d), lambda i, f: (0, 0)),
                  pl.BlockSpec((d, tf), lambda i, f: (0, f)),
                  pl.BlockSpec((d, tf), lambda i, f: (0, f)),
                  pl.BlockSpec((tf, d), lambda i, f: (f, 0))],
        out_specs=pl.BlockSpec((tm, d), lambda i, f: (i, 0)),
        out_shape=jax.ShapeDtypeStruct((n, d), F32),
        scratch_shapes=[pltpu.VMEM((tm, d), BF16)],
        compiler_params=_cparams(("parallel", "arbitrary")),
    )(x, g.reshape(1, d), w1, w3, w2)


def _compress_kernel(*refs, n_half):
    kc_refs = refs[:NSA_GROUPS]
    pe_ref, w1_ref, b1_ref, w2_ref, o_ref = refs[NSA_GROUPS:]
    pe = pe_ref[...]
    half_k = n_half * NSA_DH
    nchunk = o_ref.shape[1]
    for g in range(NSA_GROUPS):
        lo, hi = [], []
        for l in range(n_half):
            piece = kc_refs[g][0, pl.ds(l, nchunk, stride=n_half), :]
            lo.append((piece + pe[l:l + 1]).astype(BF16))
            hi.append((piece + pe[n_half + l:n_half + l + 1]).astype(BF16))
        z0 = _dot(jnp.concatenate(lo, axis=1), w1_ref[0:half_k, :])
        z1 = _dot(jnp.concatenate(hi, axis=1), w1_ref[half_k:2 * half_k, :])
        rows = z1.shape[0]
        pre = z0 + pltpu.roll(z1, rows - 1, 0) + b1_ref[...]
        o_ref[0, :, g * NSA_DH:(g + 1) * NSA_DH] = _dot(_gelu_tanh(pre).astype(BF16), w2_ref[...]).astype(o_ref.dtype)


def nsa_compress(u3, col, pe, w1, b1, w2):
    b, s, _ = u3.shape
    nchunk = s // CMP_STRIDE
    assert CMP_BLOCK == 2 * CMP_STRIDE and col % NSA_KV_DIM == 0
    full = lambda a: pl.BlockSpec(a.shape, lambda i: (0,) * a.ndim)
    b1r = b1.reshape(1, NSA_DH)
    return pl.pallas_call(
        functools.partial(_compress_kernel, n_half=CMP_STRIDE),
        grid=(b,),
        in_specs=[pl.BlockSpec((1, s, NSA_DH), lambda i, g=g: (i, 0, col // NSA_DH + g)) for g in range(NSA_GROUPS)]
        + [full(pe), full(w1), full(b1r), full(w2)],
        out_specs=pl.BlockSpec((1, nchunk, NSA_KV_DIM), lambda i: (i, 0, 0)),
        out_shape=jax.ShapeDtypeStruct((b, nchunk, NSA_KV_DIM), BF16),
        compiler_params=_cparams(("parallel",)),
    )(*([u3] * NSA_GROUPS), pe, w1, b1r, w2)


def _rope128(x, c, s):
    return x * c + pltpu.roll(x, NSA_DH // 2, 1) * s


def _pack_bf16_pairs(x):
    w = x.shape[1] // 2
    bits = pltpu.bitcast(x.astype(BF16).astype(F32), jnp.uint32)
    word = bits[:, w:] | lax.shift_right_logical(bits[:, :w], jnp.uint32(16))
    return pltpu.bitcast(word, F32)


def _unpack_bf16_pairs(word):
    bits = pltpu.bitcast(word, jnp.uint32)
    lo = pltpu.bitcast(lax.shift_left(bits, jnp.uint32(16)), F32).astype(BF16)
    hi = pltpu.bitcast(bits & jnp.uint32(0xFFFF0000), F32).astype(BF16)
    return lo, hi


def _split3(x):
    hi = x.astype(BF16)
    r1 = x - hi.astype(F32)
    mid = r1.astype(BF16)
    lo = (r1 - mid.astype(F32)).astype(BF16)
    return hi, mid, lo


def _nsa_attn_kernel(q_ref, ks_ref, vs_ref, kw_ref, vw_ref, gt_ref, kc_ref, vc_ref,
                     cq_ref, sq_ref, ck_ref, sk_ref, ovl_ref, exp_ref, wb_ref,
                     o_ref, ksr, vsb, kwr, vwb, selb, m_ref, acc_ref,
                     *, tq, tk, seq, scale):
    qi = pl.program_id(2)
    hp = NSA_HPG
    n_sel = seq // SEL_BLOCK

    @pl.when(qi == 0)
    def _():
        ck, sk = ck_ref[...], sk_ref[...]
        ksr[...] = _rope128(ks_ref[0], ck, sk).astype(BF16)
        kwr[...] = _rope128(kw_ref[0], ck, sk).astype(BF16)
        vsb[...] = _values_and_ones(vs_ref[0])
        vwb[...] = _values_and_ones(vw_ref[0])

    q = q_ref[0] * (scale * LOG2E)
    cq, sq = cq_ref[...], sq_ref[...]
    heads = [q[:, p * NSA_DH:(p + 1) * NSA_DH] for p in range(hp)]
    qu = jnp.concatenate(heads, axis=0).astype(BF16)
    qr = jnp.concatenate([_rope128(h, cq, sq) for h in heads], axis=0).astype(BF16)
    t_row = qi * tq + lax.broadcasted_iota(jnp.int32, (tq, 1), 0)
    lane = lax.broadcasted_iota(jnp.int32, (1, LANES), 1)

    sc = _dot_nt(qu, kc_ref[0]).reshape(hp, tq, LANES)
    valid = (lane * CMP_STRIDE + (CMP_BLOCK - 1) <= t_row)[None]
    sm = jnp.where(valid, sc, NEG_INF)
    e = jnp.exp2(sm - jnp.max(sm, axis=-1, keepdims=True))
    p = jnp.where(valid, e / jnp.sum(e, axis=-1, keepdims=True), 0.0)
    o_cmp = _dot(p.reshape(hp * tq, LANES).astype(BF16), vc_ref[0])

    psum = p[0]
    for i in range(1, hp):
        psum = psum + p[i]
    ovl = ovl_ref[...]
    imp = sum(_dot(part, ovl) for part in _split3(psum))
    cur = jnp.right_shift(t_row, SEL_BLOCK.bit_length() - 1)
    future = lane > cur
    forced = (lane == 0) | (lane == cur) | (lane == cur - 1)
    score = jnp.where(future, -1.0, jnp.where(forced, SEL_FORCE, imp))
    sc_t = score.T[0:n_sel, :]
    blk = lax.broadcasted_iota(jnp.int32, (n_sel, 1), 0)
    cnt = jnp.zeros((n_sel, tq), F32)
    for j in range(n_sel):
        other = sc_t[j:j + 1, :]
        beats = (other > sc_t) | ((other == sc_t) & (blk > j))
        cnt = cnt + jnp.where(beats, 1.0, 0.0)
    sel_t = jnp.where(cnt < SEL_COUNT, 1.0, 0.0)
    sel = jnp.concatenate([sel_t, jnp.zeros((LANES - n_sel, tq), F32)], axis=0).T.astype(BF16)
    picked = _dot(sel, exp_ref[...])
    kall = lax.broadcasted_iota(jnp.int32, (1, seq), 1)
    selb[...] = jnp.where((picked > 0.5) & (kall <= t_row), 0.0, NEG_INF)

    m_ref[...] = jnp.full(m_ref.shape, NEG_INF, F32)
    acc_ref[...] = jnp.zeros(acc_ref.shape, F32)
    for c in range(seq // tk):
        @pl.when(c * tk <= qi * tq + (tq - 1))
        def _():
            for i in range(hp):
                rows = pl.ds(i * tq, tq)
                s = _dot_nt(qr[i * tq:(i + 1) * tq], ksr[c * tk:(c + 1) * tk, :])
                _softmax_step(s + selb[:, c * tk:(c + 1) * tk], vsb[c * tk:(c + 1) * tk, :],
                              m_ref.at[rows], acc_ref.at[rows])
    o_sel = _softmax_result(acc_ref[...])

    span = WINDOW + tq
    start = pl.multiple_of(jnp.maximum(qi * tq - WINDOW, 0), tq)
    wins = []
    for i in range(hp):
        sw = _dot_nt(qr[i * tq:(i + 1) * tq], kwr[pl.ds(start, span), :]) + wb_ref[0]
        ew = jnp.exp2(sw - jnp.max(sw, axis=-1, keepdims=True)).astype(BF16)
        wins.append(_softmax_result(_dot(ew, vwb[pl.ds(start, span), :])))
    o_win = jnp.concatenate(wins, axis=0)

    gates = _sigmoid(gt_ref[0])
    for i in range(hp):
        rows = slice(i * tq, (i + 1) * tq)
        o = (gates[:, 3 * i:3 * i + 1] * o_cmp[rows] + gates[:, 3 * i + 1:3 * i + 2] * o_sel[rows]
             + gates[:, 3 * i + 2:3 * i + 3] * o_win[rows])
        o_ref[0, :, i * NSA_DH:(i + 1) * NSA_DH] = o.astype(o_ref.dtype)


def _window_bias(tq):
    span = WINDOW + tq
    out = []
    for qi in range(WINDOW // tq + 1):
        start = max(qi * tq - WINDOW, 0)
        t = qi * tq + np.arange(tq)[:, None]
        kpos = start + np.arange(span)[None, :]
        out.append(np.where((kpos <= t) & (kpos > t - WINDOW), 0.0, NEG_INF))
    return jnp.asarray(np.stack(out), F32)


def nsa_attention(u3, k_cmp, v_cmp, rope_c, rope_s, ovl, expand, *, tq, tk):
    b, s, _ = u3.shape
    assert s // CMP_STRIDE == LANES and tq % SEL_BLOCK == 0 and WINDOW % tq == 0
    hp = NSA_HPG
    nwb = WINDOW // tq
    wbias = _window_bias(tq)
    col = lambda off: (lambda b_, g, i: (b_, 0, off // NSA_DH + g))
    seq_blk = lambda off: pl.BlockSpec((1, s, NSA_DH), col(off))
    full = lambda a: pl.BlockSpec(a.shape, lambda b_, g, i: (0,) * a.ndim)
    cmp_blk = pl.BlockSpec((1, LANES, NSA_DH), lambda b_, g, i: (b_, 0, g))
    rope_q = pl.BlockSpec((tq, NSA_DH), lambda b_, g, i: (i, 0))
    kern = functools.partial(_nsa_attn_kernel, tq=tq, tk=tk, seq=s, scale=NSA_DH ** -0.5)
    return pl.pallas_call(
        kern,
        grid=(b, NSA_GROUPS, s // tq),
        in_specs=[pl.BlockSpec((1, tq, hp * NSA_DH), lambda b_, g, i: (b_, i, g)),
                  seq_blk(OD_KS), seq_blk(OD_VS), seq_blk(OD_KW), seq_blk(OD_VW),
                  pl.BlockSpec((1, tq, LANES), lambda b_, g, i: (b_, i, OD_G // LANES + g)),
                  cmp_blk, cmp_blk, rope_q, rope_q, full(rope_c), full(rope_s), full(ovl), full(expand),
                  pl.BlockSpec((1, tq, WINDOW + tq), lambda b_, g, i: (jnp.minimum(i, nwb), 0, 0))],
        out_specs=pl.BlockSpec((1, tq, hp * NSA_DH), lambda b_, g, i: (b_, i, g)),
        out_shape=jax.ShapeDtypeStruct((b, s, NSA_Q_DIM), BF16),
        scratch_shapes=[pltpu.VMEM((s, NSA_DH), BF16), pltpu.VMEM((s, 2 * LANES), BF16)] * 2 + [
            pltpu.VMEM((tq, s), F32),
            pltpu.VMEM((hp * tq, LANES), F32), pltpu.VMEM((hp * tq, 2 * LANES), F32)],
        compiler_params=_cparams(("parallel", "parallel", "arbitrary")),
    )(u3, u3, u3, u3, u3, u3, k_cmp, v_cmp, rope_c, rope_s, rope_c, rope_s, ovl, expand, wbias)


def _router_kernel(x_ref, g_ref, wr_ref, br_ref, xn_ref, meta_ref, cnt_ref, *, tm):
    @pl.when(pl.program_id(0) == 0)
    def _():
        cnt_ref[...] = jnp.zeros(cnt_ref.shape, F32)

    xn = _rms(x_ref[...], g_ref[...])
    xn_ref[...] = _pack_bf16_pairs(xn)
    xh, xm, _ = _split3(xn)
    wh, wm, _ = _split3(wr_ref[...])
    logits = _dot(xh, wh) + _dot(xh, wm) + _dot(xm, wh) + br_ref[...]
    lane = lax.broadcasted_iota(jnp.int32, (1, LANES), 1).astype(F32)
    lg = jnp.where(lane < N_EXPERTS, logits, NEG_INF)
    m1 = jnp.max(lg, axis=-1, keepdims=True)
    e1 = jnp.min(jnp.where(lg == m1, lane, float(LANES)), axis=-1, keepdims=True)
    lg2 = jnp.where(lane == e1, NEG_INF, lg)
    m2 = jnp.max(lg2, axis=-1, keepdims=True)
    e2 = jnp.min(jnp.where(lg2 == m2, lane, float(LANES)), axis=-1, keepdims=True)
    ex = jnp.exp(m2 - m1)
    den = 1.0 + ex
    g1 = 1.0 / den
    g2 = ex / den
    oh = jnp.where((lane == e1) | (lane == e2), 1.0, 0.0)
    r = lax.broadcasted_iota(jnp.int32, (tm, tm), 0)
    c = lax.broadcasted_iota(jnp.int32, (tm, tm), 1)
    tri = jnp.where(r > c, 1.0, 0.0).astype(BF16)
    cum = _dot(tri, oh.astype(BF16)) + cnt_ref[0:1, :]
    pos1 = jnp.sum(jnp.where(lane == e1, cum, 0.0), axis=-1, keepdims=True)
    pos2 = jnp.sum(jnp.where(lane == e2, cum, 0.0), axis=-1, keepdims=True)
    cnt_ref[...] = cnt_ref[...] + jnp.sum(oh, axis=0, keepdims=True)
    meta = jnp.where(lane == 0, e1, 0.0)
    meta = jnp.where(lane == 1, e2, meta)
    meta = jnp.where(lane == 2, g1, meta)
    meta = jnp.where(lane == 3, g2, meta)
    meta = jnp.where(lane == 4, pos1, meta)
    meta = jnp.where(lane == 5, pos2, meta)
    meta_ref[...] = meta


def moe_router(x, g, wr, br, *, tm):
    n, d = x.shape
    return pl.pallas_call(
        functools.partial(_router_kernel, tm=tm),
        grid=(n // tm,),
        in_specs=[pl.BlockSpec((tm, d), lambda i: (i, 0)),
                  pl.BlockSpec((1, d), lambda i: (0, 0)),
                  pl.BlockSpec((d, LANES), lambda i: (0, 0)),
                  pl.BlockSpec((1, LANES), lambda i: (0, 0))],
        out_specs=[pl.BlockSpec((tm, d // 2), lambda i: (i, 0)),
                   pl.BlockSpec((tm, LANES), lambda i: (i, 0)),
                   pl.BlockSpec((8, LANES), lambda i: (0, 0))],
        out_shape=[jax.ShapeDtypeStruct((n, d // 2), F32),
                   jax.ShapeDtypeStruct((n, LANES), F32),
                   jax.ShapeDtypeStruct((8, LANES), F32)],
        compiler_params=_cparams(("arbitrary",)),
    )(x, g.reshape(1, d), wr, br)


def _gather_chunk(per_worker, row_bytes):
    best = 0
    for c in range(8, per_worker + 1, 8):
        if per_worker % c == 0 and c * row_bytes <= SC_GATHER_BYTES and c <= LANES:
            best = c
    assert best > 0, (per_worker, row_bytes)
    return best


def sc_gather_rows(table, idx):
    _, d = table.shape
    b = idx.shape[0]
    assert b % (8 * SC_WORKERS) == 0 and table.dtype.itemsize == 4
    per_w = b // SC_WORKERS
    chunk = _gather_chunk(per_w, d * 4)
    mesh = plsc.VectorSubcoreMesh(core_axis_name="c", subcore_axis_name="s",
                                  num_cores=SC_CORES, num_subcores=SC_SUBCORES)

    @functools.partial(
        pl.kernel, mesh=mesh,
        out_type=jax.ShapeDtypeStruct((b, d), table.dtype),
        scratch_types=[pltpu.VMEM((chunk,), jnp.int32), pltpu.VMEM((chunk, d), table.dtype),
                       pltpu.SemaphoreType.DMA])
    def gather(table_hbm, idx_hbm, out_hbm, idx_v, rows_v, sem):
        wid = lax.axis_index("s") * SC_CORES + lax.axis_index("c")
        base = wid * per_w

        @pl.loop(0, per_w // chunk)
        def _(c):
            off = pl.multiple_of(base + c * chunk, 8)
            pltpu.sync_copy(idx_hbm.at[pl.ds(off, chunk)], idx_v)
            pltpu.async_copy(table_hbm.at[idx_v], rows_v, sem).wait()
            pltpu.sync_copy(rows_v, out_hbm.at[pl.ds(off, chunk)])

    return gather(table, idx)


def sc_scatter_rows(src, dests, n_out):
    n, d = src.shape
    assert n % (8 * SC_WORKERS) == 0 and src.dtype.itemsize == 4
    per_w = n // SC_WORKERS
    chunk = _gather_chunk(per_w, d * 4)
    n_dest = len(dests)
    mesh = plsc.VectorSubcoreMesh(core_axis_name="c", subcore_axis_name="s",
                                  num_cores=SC_CORES, num_subcores=SC_SUBCORES)

    @functools.partial(
        pl.kernel, mesh=mesh,
        out_type=jax.ShapeDtypeStruct((n_out, d), src.dtype),
        scratch_types=[pltpu.VMEM((chunk,), jnp.int32), pltpu.VMEM((chunk, d), src.dtype),
                       pltpu.SemaphoreType.DMA])
    def scatter(src_hbm, *rest):
        dest_hbms = rest[:n_dest]
        out_hbm, idx_v, rows_v, sem = rest[n_dest:]
        wid = lax.axis_index("s") * SC_CORES + lax.axis_index("c")
        base = wid * per_w

        @pl.loop(0, per_w // chunk)
        def _(c):
            off = pl.multiple_of(base + c * chunk, 8)
            pltpu.sync_copy(src_hbm.at[pl.ds(off, chunk)], rows_v)
            for dest_hbm in dest_hbms:
                pltpu.sync_copy(dest_hbm.at[pl.ds(off, chunk)], idx_v)
                pltpu.async_copy(rows_v, out_hbm.at[idx_v], sem).wait()

    return scatter(src, *dests)


def _expert_kernel(te_ref, ts_ref, tr_ref, xp_hbm, *rest, nf):
    ns = MOE_WSPLIT
    w1_refs, w3_refs, w2_refs = rest[:ns], rest[ns:2 * ns], rest[2 * ns:3 * ns]
    yr_hbm, xb, acc, stage, in_sem, out_sem = rest[3 * ns:]
    t = pl.program_id(0)
    f = pl.program_id(1)
    rows = tr_ref[t]
    start = pl.multiple_of(ts_ref[t], MOE_ALIGN)
    n_chunks = MOE_TILE // MOE_COPY
    half = xb.shape[1] // 2

    def in_copy(ci):
        return pltpu.make_async_copy(xp_hbm.at[pl.ds(start + ci * MOE_COPY, MOE_COPY)],
                                     stage.at[ci % 2], in_sem.at[ci % 2])

    def out_copy(ci, first_row):
        return pltpu.make_async_copy(acc.at[pl.ds(ci * MOE_COPY, MOE_COPY)],
                                     yr_hbm.at[pl.ds(first_row + ci * MOE_COPY, MOE_COPY)], out_sem.at[0])

    def when_chunk_live(ci, fn):
        pl.when(ci * MOE_COPY < rows)(fn)

    def wait_outputs(tile):
        tile_rows = tr_ref[tile]
        first_row = pl.multiple_of(ts_ref[tile], MOE_ALIGN)
        for ci in range(n_chunks):
            pl.when(ci * MOE_COPY < tile_rows)(lambda ci=ci: out_copy(ci, first_row).wait())

    @pl.when((f == 0) & (rows > 0))
    def _():
        def unpack(ci):
            in_copy(ci).wait()
            lo, hi = _unpack_bf16_pairs(stage[ci % 2])
            xb[ci * MOE_COPY:(ci + 1) * MOE_COPY, :half] = lo
            xb[ci * MOE_COPY:(ci + 1) * MOE_COPY, half:] = hi

        when_chunk_live(0, lambda: in_copy(0).start())
        for ci in range(n_chunks):
            if ci + 1 < n_chunks:
                when_chunk_live(ci + 1, lambda ci=ci: in_copy(ci + 1).start())
            when_chunk_live(ci, lambda ci=ci: unpack(ci))

    @pl.when((f == 0) & (t > 0))
    def _():
        wait_outputs(jnp.maximum(t - 1, 0))

    @pl.when((f == 0) & (rows > 0))
    def _():
        acc[...] = jnp.zeros(acc.shape, F32)

    @pl.when(rows > 0)
    def _():
        w1 = jnp.concatenate([r[0, 0].astype(BF16) for r in w1_refs], axis=0)
        w3 = jnp.concatenate([r[0, 0].astype(BF16) for r in w3_refs], axis=0)
        w2 = jnp.concatenate([r[0, 0].astype(BF16) for r in w2_refs], axis=1)

        def chain(r0, size):
            rs = pl.ds(pl.multiple_of(r0, MOE_PIECES[-1]), size)
            xc = xb[rs, :]
            h1 = _dot(xc, w1)
            h3 = _dot(xc, w3)
            hh = (h1 * _sigmoid(h1) * h3).astype(BF16)
            acc[rs, :] += _dot(hh, w2)

        todo = (rows + (MOE_PIECES[-1] - 1)) // MOE_PIECES[-1] * MOE_PIECES[-1]
        off = jnp.int32(0)
        for size in MOE_PIECES:
            take = todo - off >= size

            @pl.when(take)
            def _(off=off, size=size):
                for c0 in range(0, size, MOE_CHAIN):
                    chain(off + c0, min(MOE_CHAIN, size))

            off = off + jnp.where(take, size, 0)

    @pl.when((f == nf - 1) & (rows > 0))
    def _():
        for ci in range(n_chunks):
            when_chunk_live(ci, lambda ci=ci: out_copy(ci, start).start())

    @pl.when((f == nf - 1) & (t == pl.num_programs(0) - 1))
    def _():
        wait_outputs(t)


def moe_experts(xp, tile_e, tile_start, tile_rows, w1, w3, w2, layer, *, tf):
    n_rows, half = xp.shape
    d = 2 * half
    n_tiles = tile_e.shape[0]
    ff = w1.shape[3]
    nf = ff // tf
    assert MOE_TILE % MOE_COPY == 0 and sum(MOE_PIECES) >= MOE_TILE

    def f_eff(t, f, tr):
        return jnp.where(tr[t] > 0, f, nf - 1)

    ns = MOE_WSPLIT
    up_specs = [pl.BlockSpec((1, 1, d // ns, tf), lambda t, f, te, ts, tr, k=k: (layer, te[t], k, f_eff(t, f, tr)))
                for k in range(ns)]
    down_specs = [pl.BlockSpec((1, 1, tf, d // ns), lambda t, f, te, ts, tr, k=k: (layer, te[t], f_eff(t, f, tr), k))
                  for k in range(ns)]
    grid_spec = pltpu.PrefetchScalarGridSpec(
        num_scalar_prefetch=3,
        grid=(n_tiles, nf),
        in_specs=[pl.BlockSpec(memory_space=pl.ANY)] + 2 * up_specs + down_specs,
        out_specs=pl.BlockSpec(memory_space=pl.ANY),
        scratch_shapes=[pltpu.VMEM((MOE_TILE, d), BF16), pltpu.VMEM((MOE_TILE, d), F32),
                        pltpu.VMEM((2, MOE_COPY, half), F32),
                        pltpu.SemaphoreType.DMA((2,)), pltpu.SemaphoreType.DMA((1,))],
    )
    return pl.pallas_call(
        functools.partial(_expert_kernel, nf=nf),
        grid_spec=grid_spec,
        out_shape=jax.ShapeDtypeStruct((n_rows, d), F32),
        compiler_params=_cparams(("arbitrary", "arbitrary")),
    )(tile_e, tile_start, tile_rows, xp, *([w1] * ns), *([w3] * ns), *([w2] * ns))


def _combine_kernel(x_ref, y1_ref, y2_ref, meta_ref, g_ref, o_ref, *, final_norm):
    meta = meta_ref[...]
    y = x_ref[...] + (meta[:, 2:3] * y1_ref[...] + meta[:, 3:4] * y2_ref[...])
    if final_norm:
        y = _rms(y, g_ref[...])
    o_ref[...] = y


def moe_combine(x, yg, meta, g, *, final_norm, tm):
    n, d = x.shape
    nb = n // tm
    row = pl.BlockSpec((tm, d), lambda i: (i, 0))
    return pl.pallas_call(
        functools.partial(_combine_kernel, final_norm=final_norm),
        grid=(nb,),
        in_specs=[row, row, pl.BlockSpec((tm, d), lambda i: (i + nb, 0)),
                  pl.BlockSpec((tm, LANES), lambda i: (i, 0)),
                  pl.BlockSpec((1, d), lambda i: (0, 0))],
        out_specs=row,
        out_shape=jax.ShapeDtypeStruct((n, d), F32),
        compiler_params=_cparams(("parallel",)),
    )(x, yg, yg, meta, g.reshape(1, d))


def _mla_rope_tables(seq):
    half = MLA_ROPE // 2
    pos = jnp.arange(seq, dtype=F32)
    inv = ROPE_THETA ** (-jnp.arange(0, MLA_ROPE, 2, dtype=F32) / MLA_ROPE)
    ang = pos[:, None] * inv[None, :]
    cos, sin = jnp.cos(ang), jnp.sin(ang)
    z = jnp.zeros((seq, half), F32)
    pad = jnp.zeros((seq, LANES - MLA_ROPE), F32)
    c = jnp.concatenate([cos, cos, pad], axis=1)
    s_lo = jnp.concatenate([-sin, z, pad], axis=1)
    s_hi = jnp.concatenate([z, sin, pad], axis=1)
    return c, s_lo, s_hi


def _nsa_rope_tables(seq):
    pos = jnp.arange(seq, dtype=F32)
    inv = ROPE_THETA ** (-jnp.arange(0, NSA_DH, 2, dtype=F32) / NSA_DH)
    ang = pos[:, None] * inv[None, :]
    cos, sin = jnp.cos(ang), jnp.sin(ang)
    return jnp.concatenate([cos, cos], axis=1), jnp.concatenate([-sin, sin], axis=1)


def _selection_constants(seq):
    n_sel = seq // SEL_BLOCK
    nc = (seq - CMP_BLOCK) // CMP_STRIDE + 1
    cmp_start = np.arange(LANES) * CMP_STRIDE
    sel_start = np.arange(LANES) * SEL_BLOCK
    ovl = ((cmp_start[:, None] < sel_start[None, :] + SEL_BLOCK) &
           (cmp_start[:, None] + CMP_BLOCK > sel_start[None, :]))
    ovl &= (np.arange(LANES)[:, None] < nc) & (np.arange(LANES)[None, :] < n_sel)
    expand = (np.arange(seq)[None, :] // SEL_BLOCK == np.arange(LANES)[:, None])
    return jnp.asarray(ovl, BF16), jnp.asarray(expand, BF16)


def even_layer(x, seq, p):
    n = x.shape[0]
    b = n // seq
    (norm_mix, w_in, q_norm, w_q_up, kv_norm, w_kv_up, conv_w, conv_b, ga_w, ga_b, gx_w, gx_b,
     lam, w_out, norm_ffn, w1, w3, w2) = p
    d = D_MODEL
    o1 = MLA_Q_LORA + MLA_KV_LORA
    o2 = o1 + MLA_ROPE
    w_pack = jnp.concatenate(
        [w_in[:, :o1], w_in[:, o2:], w_in[:, o1:o2], jnp.zeros((d, LANES - MLA_ROPE), F32)], axis=1).astype(BF16)
    u = norm_matmul(x, norm_mix, w_pack, tm=1024, tn=EV_PACKED // 3)

    wq = w_q_up.reshape(MLA_Q_LORA, MLA_HEADS, MLA_NOPE + MLA_ROPE)
    wq = jnp.pad(wq, ((0, 0), (0, 0), (0, MLA_QK_PAD - MLA_NOPE - MLA_ROPE)))
    wq = wq.reshape(MLA_Q_LORA, MLA_HEADS * MLA_QK_PAD).astype(BF16)
    wkv = w_kv_up.reshape(MLA_KV_LORA, MLA_HEADS, MLA_NOPE + MLA_V)
    wk = wkv[:, :, :MLA_NOPE].reshape(MLA_KV_LORA, MLA_HEADS * MLA_NOPE).astype(BF16)
    wv = wkv[:, :, MLA_NOPE:].reshape(MLA_KV_LORA, MLA_HEADS * MLA_V).astype(BF16)
    rc, rlo, rhi = _mla_rope_tables(seq)
    q, k, v = mla_up(u, q_norm, kv_norm, wq, wk, wv, rc, rlo, rhi, seq=seq, tm=512)
    o_mla = mla_attention(q.reshape(b, seq, -1), k.reshape(b, seq, -1), v.reshape(b, seq, -1), t=512)

    o_rec = rglru(u.reshape(b, seq, EV_PACKED), conv_w, conv_b, ga_w.astype(BF16), ga_b,
                  gx_w.astype(BF16), gx_b, lam, ts=512)
    x = matmul_residual([o_mla.reshape(n, -1), o_rec.reshape(n, -1)], w_out.astype(BF16), x, tm=512, tn=D_MODEL)
    return ffn_dense(x, norm_ffn, w1.astype(BF16), w3.astype(BF16), w2.astype(BF16), tm=1024, tf=512)


def _moe_dispatch(meta, counts, n):
    e1 = meta[:, 0].astype(jnp.int32)
    e2 = meta[:, 1].astype(jnp.int32)
    pos1 = meta[:, 4].astype(jnp.int32)
    pos2 = meta[:, 5].astype(jnp.int32)
    cnt = counts[0, :N_EXPERTS].astype(jnp.int32)
    span = (cnt + MOE_ALIGN - 1) // MOE_ALIGN * MOE_ALIGN
    row0 = jnp.cumsum(span) - span
    row_unit = 64 * SC_WORKERS
    n_rows = -(-(n * TOP_K + N_EXPERTS * MOE_ALIGN + MOE_TILE) // row_unit) * row_unit
    d1 = row0[e1] + pos1
    d2 = row0[e2] + pos2
    n_tiles = (n * TOP_K) // MOE_TILE + N_EXPERTS
    tiles_e = (cnt + MOE_TILE - 1) // MOE_TILE
    tend = jnp.cumsum(tiles_e)
    tbeg = tend - tiles_e
    tid = jnp.arange(n_tiles, dtype=jnp.int32)
    te = jnp.minimum(jnp.searchsorted(tend, tid, side='right'), N_EXPERTS - 1).astype(jnp.int32)
    used = tid < tend[-1]
    first = (tid - tbeg[te]) * MOE_TILE
    rows = jnp.where(used, jnp.clip(cnt[te] - first, 0, MOE_TILE), 0).astype(jnp.int32)
    tstart = jnp.where(used, row0[te] + first, 0).astype(jnp.int32)
    last_e = te[jnp.maximum(tend[-1] - 1, 0)]
    te = jnp.where(used, te, last_e).astype(jnp.int32)
    return d1, d2, n_rows, te, tstart, rows


def odd_layer(x, seq, p, experts, final_g):
    n = x.shape[0]
    b = n // seq
    (norm_mix, w_in, ck_pe, ck_w1, ck_b1, ck_w2, cv_pe, cv_w1, cv_b1, cv_w2, w_out, norm_ffn,
     router_w, router_b) = p
    ew1, ew3, ew2, layer = experts
    d = D_MODEL
    wg = w_in[:, OD_G:].reshape(d, NSA_GROUPS, NSA_HPG * 3)
    wg = jnp.pad(wg, ((0, 0), (0, 0), (0, LANES - NSA_HPG * 3))).reshape(d, NSA_GROUPS * LANES)
    w_pack = jnp.concatenate([w_in[:, :OD_G], wg], axis=1).astype(BF16)
    u = norm_matmul(x, norm_mix, w_pack, tm=1024, tn=512)
    u3 = u.reshape(b, seq, OD_PACKED)

    k_cmp = nsa_compress(u3, OD_KC, ck_pe, ck_w1.astype(BF16), ck_b1, ck_w2.astype(BF16))
    v_cmp = nsa_compress(u3, OD_VC, cv_pe, cv_w1.astype(BF16), cv_b1, cv_w2.astype(BF16))
    rc, rs = _nsa_rope_tables(seq)
    ovl, expand = _selection_constants(seq)
    o = nsa_attention(u3, k_cmp, v_cmp, rc, rs, ovl, expand, tq=512, tk=512)
    x = matmul_residual([o.reshape(n, -1)], w_out.astype(BF16), x, tm=512, tn=D_MODEL)

    wr = jnp.pad(router_w, ((0, 0), (0, LANES - N_EXPERTS)))
    br = jnp.pad(router_b, (0, LANES - N_EXPERTS)).reshape(1, LANES)
    xp, meta, counts = moe_router(x, norm_ffn, wr, br, tm=512)
    d1, d2, n_rows, te, tstart, rows = _moe_dispatch(meta, counts, n)
    yr = moe_experts(sc_scatter_rows(xp, [d1, d2], n_rows), te, tstart, rows, ew1, ew3, ew2, layer, tf=256)
    yg = sc_gather_rows(yr, jnp.concatenate([d1, d2]))
    g = final_g if final_g is not None else norm_ffn
    return moe_combine(x, yg, meta, g, final_norm=final_g is not None, tm=512)


def kernel(x, ev_norm_mix, ev_w_in, ev_q_norm, ev_w_q_up, ev_kv_norm, ev_w_kv_up, ev_conv_w, ev_conv_b, ev_gate_a_w, ev_gate_a_b, ev_gate_x_w, ev_gate_x_b, ev_lru_lambda, ev_w_out, ev_norm_ffn, ev_ffn_w1, ev_ffn_w3, ev_ffn_w2, od_norm_mix, od_w_in, od_cmp_k_pe, od_cmp_k_w1, od_cmp_k_b1, od_cmp_k_w2, od_cmp_v_pe, od_cmp_v_w1, od_cmp_v_b1, od_cmp_v_w2, od_w_out, od_norm_ffn, od_router_w, od_router_b, od_exp_w1, od_exp_w3, od_exp_w2, final_norm):
    bsz, seq, d = x.shape
    ev = (ev_norm_mix, ev_w_in, ev_q_norm, ev_w_q_up, ev_kv_norm, ev_w_kv_up, ev_conv_w, ev_conv_b,
          ev_gate_a_w, ev_gate_a_b, ev_gate_x_w, ev_gate_x_b, ev_lru_lambda, ev_w_out, ev_norm_ffn,
          ev_ffn_w1, ev_ffn_w3, ev_ffn_w2)
    od = (od_norm_mix, od_w_in, od_cmp_k_pe, od_cmp_k_w1, od_cmp_k_b1, od_cmp_k_w2, od_cmp_v_pe,
          od_cmp_v_w1, od_cmp_v_b1, od_cmp_v_w2, od_w_out, od_norm_ffn, od_router_w, od_router_b)
    h = x.reshape(bsz * seq, d)
    for layer in range(DEPTH):
        i = layer // 2
        if layer % 2 == 0:
            h = even_layer(h, seq, tuple(a[i] for a in ev))
        else:
            h = odd_layer(h, seq, tuple(a[i] for a in od), (od_exp_w1, od_exp_w3, od_exp_w2, i),
                          final_norm if layer == DEPTH - 1 else None)
    return h.reshape(bsz, seq, d)
```

```python
import functools
import math

import numpy as np
import jax
import jax.numpy as jnp
from jax import lax
from jax.experimental import pallas as pl
from jax.experimental.pallas import tpu as pltpu
from jax.experimental.pallas import tpu_sc as plsc

F32 = jnp.float32
BF16 = jnp.bfloat16

D_MODEL = 2048
DEPTH = 4
RMS_EPS = 1e-6
ROPE_THETA = 10000.0
NEG_INF = -1e30
LOG2E = math.log2(math.e)

MLA_HEADS = 8
MLA_Q_LORA = 768
MLA_KV_LORA = 512
MLA_NOPE = 128
MLA_ROPE = 64
MLA_V = 128
MLA_QK_PAD = 256
MLA_ROW_GROUPS = 2
MLA_HEADS_PER_STEP = 2

LRU_WIDTH = D_MODEL // 2
LRU_BLOCKS = 8
LRU_BLOCK_W = LRU_WIDTH // LRU_BLOCKS
LRU_C = 8.0
CONV_WIDTH = 4

NSA_HEADS = 16
NSA_GROUPS = 4
NSA_HPG = NSA_HEADS // NSA_GROUPS
NSA_DH = D_MODEL // NSA_HEADS
CMP_BLOCK = 32
CMP_STRIDE = 16
SEL_BLOCK = 64
SEL_COUNT = 16
SEL_FORCE = 1e4
WINDOW = 512
NSA_WIN_ROWS = 256
NSA_Q_DIM = NSA_HEADS * NSA_DH
NSA_KV_DIM = NSA_GROUPS * NSA_DH

DENSE_FF = 5632
N_EXPERTS = 8
TOP_K = 2
EXPERT_FF = 7168

LANES = 128
SC_CORES = 2
SC_SUBCORES = 16
SC_WORKERS = SC_CORES * SC_SUBCORES
SC_GATHER_BYTES = 256 * 1024
VMEM_LIMIT = 60 * 1024 * 1024

EV_CQ = 0
EV_CKV = MLA_Q_LORA
EV_REC = MLA_Q_LORA + MLA_KV_LORA
EV_GATE = EV_REC + LRU_WIDTH
EV_PE = EV_GATE + LRU_WIDTH
EV_PACKED = EV_PE + LANES

OD_Q = 0
OD_KC = NSA_Q_DIM
OD_VC = OD_KC + NSA_KV_DIM
OD_KS = OD_VC + NSA_KV_DIM
OD_VS = OD_KS + NSA_KV_DIM
OD_KW = OD_VS + NSA_KV_DIM
OD_VW = OD_KW + NSA_KV_DIM
OD_G = OD_VW + NSA_KV_DIM
OD_PACKED = OD_G + NSA_GROUPS * LANES

MOE_TILE = 2688
MOE_ALIGN = 16
MOE_COPY = 384
MOE_PIECES = (1024, 1024, 512, 256, 128)
MOE_CHAIN = 512
MOE_WSPLIT = 1


def _cparams(sem):
    return pltpu.CompilerParams(dimension_semantics=sem, vmem_limit_bytes=VMEM_LIMIT)


def _rms(x, g):
    ms = jnp.mean(x * x, axis=-1, keepdims=True)
    return x * lax.rsqrt(ms + RMS_EPS) * g


def _sigmoid(x):
    return 1.0 / (1.0 + jnp.exp(-x))


def _gelu_tanh(x):
    return 0.5 * x * (1.0 + jnp.tanh(math.sqrt(2.0 / math.pi) * (x + 0.044715 * (x * x * x))))


def _dot(a, b):
    return jnp.dot(a, b, preferred_element_type=F32)


def _dot_nt(a, b):
    return lax.dot_general(a, b, (((1,), (1,)), ((), ())), preferred_element_type=F32)


def _norm_mm_kernel(x_ref, g_ref, w_ref, o_ref, xn_ref):
    @pl.when(pl.program_id(1) == 0)
    def _():
        xn_ref[...] = _rms(x_ref[...], g_ref[...]).astype(BF16)

    o_ref[...] = _dot(xn_ref[...], w_ref[...]).astype(o_ref.dtype)


def norm_matmul(x, g, w, *, tm, tn, out_dtype=F32):
    n, k = x.shape
    m = w.shape[1]
    assert n % tm == 0 and m % tn == 0
    return pl.pallas_call(
        _norm_mm_kernel,
        grid=(n // tm, m // tn),
        in_specs=[pl.BlockSpec((tm, k), lambda i, j: (i, 0)),
                  pl.BlockSpec((1, k), lambda i, j: (0, 0)),
                  pl.BlockSpec((k, tn), lambda i, j: (0, j))],
        out_specs=pl.BlockSpec((tm, tn), lambda i, j: (i, j)),
        out_shape=jax.ShapeDtypeStruct((n, m), out_dtype),
        scratch_shapes=[pltpu.VMEM((tm, k), BF16)],
        compiler_params=_cparams(("parallel", "arbitrary")),
    )(x, g.reshape(1, k), w)


def _mm_res_kernel(*refs, n_in):
    xs = refs[:n_in]
    ws = refs[n_in:2 * n_in]
    res_ref = refs[2 * n_in]
    o_ref = refs[2 * n_in + 1]
    acc = res_ref[...]
    for x_ref, w_ref in zip(xs, ws):
        acc = acc + _dot(x_ref[...], w_ref[...])
    o_ref[...] = acc


def matmul_residual(xs, w, res, *, tm, tn):
    n = res.shape[0]
    m = w.shape[1]
    n_in = len(xs)
    in_specs = [pl.BlockSpec((tm, x.shape[1]), lambda i, j: (i, 0)) for x in xs]
    row = 0
    for x in xs:
        kx = x.shape[1]
        assert row % kx == 0
        in_specs.append(pl.BlockSpec((kx, tn), lambda i, j, rb=row // kx: (rb, j)))
        row += kx
    assert row == w.shape[0]
    in_specs.append(pl.BlockSpec((tm, tn), lambda i, j: (i, j)))
    return pl.pallas_call(
        functools.partial(_mm_res_kernel, n_in=n_in),
        grid=(n // tm, m // tn),
        in_specs=in_specs,
        out_specs=pl.BlockSpec((tm, tn), lambda i, j: (i, j)),
        out_shape=jax.ShapeDtypeStruct((n, m), F32),
        compiler_params=_cparams(("parallel", "arbitrary")),
    )(*xs, *([w] * n_in), res)


def _rope64(x, c, s_lo, s_hi):
    return x * c + pltpu.roll(x, 96, 1) * s_lo + pltpu.roll(x, 32, 1) * s_hi


def _mla_up_kernel(u_ref, pe_ref, qg_ref, kvg_ref, wq_ref, wk_ref, wv_ref,
                   c_ref, slo_ref, shi_ref, q_ref, k_ref, v_ref):
    u = u_ref[...]
    c, s_lo, s_hi = c_ref[...], slo_ref[...], shi_ref[...]
    qn = _rms(u[:, EV_CQ:EV_CQ + MLA_Q_LORA], qg_ref[...]).astype(BF16)
    kvn = _rms(u[:, EV_CKV:EV_CKV + MLA_KV_LORA], kvg_ref[...]).astype(BF16)
    q = _dot(qn, wq_ref[...]) * ((MLA_NOPE + MLA_ROPE) ** -0.5 * LOG2E)
    kn = _dot(kvn, wk_ref[...])
    v_ref[...] = _dot(kvn, wv_ref[...]).astype(v_ref.dtype)
    kpe = _rope64(pe_ref[...], c, s_lo, s_hi).astype(k_ref.dtype)
    for h in range(MLA_HEADS):
        a = h * MLA_QK_PAD
        q_ref[:, a:a + LANES] = q[:, a:a + LANES].astype(q_ref.dtype)
        q_ref[:, a + LANES:a + 2 * LANES] = _rope64(q[:, a + LANES:a + 2 * LANES], c, s_lo, s_hi).astype(q_ref.dtype)
        k_ref[:, a:a + LANES] = kn[:, h * LANES:(h + 1) * LANES].astype(k_ref.dtype)
        k_ref[:, a + LANES:a + 2 * LANES] = kpe


def mla_up(u, q_norm, kv_norm, wq, wk, wv, rope_c, rope_slo, rope_shi, *, seq, tm):
    n = u.shape[0]
    hq = MLA_HEADS * MLA_QK_PAD
    hv = MLA_HEADS * MLA_V
    ab = EV_REC
    assert seq % tm == 0 and EV_PE % LANES == 0
    nsb = seq // tm
    row_spec = pl.BlockSpec((tm, LANES), lambda i: (i % nsb, 0))
    full = lambda a: pl.BlockSpec(a.shape, lambda i: (0,) * a.ndim)
    qg = q_norm.reshape(1, -1)
    kvg = kv_norm.reshape(1, -1)
    return pl.pallas_call(
        _mla_up_kernel,
        grid=(n // tm,),
        in_specs=[pl.BlockSpec((tm, ab), lambda i: (i, 0)),
                  pl.BlockSpec((tm, LANES), lambda i: (i, EV_PE // LANES)),
                  full(qg), full(kvg), full(wq), full(wk), full(wv),
                  row_spec, row_spec, row_spec],
        out_specs=[pl.BlockSpec((tm, hq), lambda i: (i, 0)),
                   pl.BlockSpec((tm, hq), lambda i: (i, 0)),
                   pl.BlockSpec((tm, hv), lambda i: (i, 0))],
        out_shape=[jax.ShapeDtypeStruct((n, hq), BF16),
                   jax.ShapeDtypeStruct((n, hq), BF16),
                   jax.ShapeDtypeStruct((n, hv), BF16)],
        compiler_params=_cparams(("parallel",)),
    )(u, u, qg, kvg, wq, wk, wv, rope_c, rope_slo, rope_shi)


def _values_and_ones(v):
    return jnp.concatenate([v.astype(BF16), jnp.ones(v.shape, BF16)], axis=1)


def _lane_tile(x, n):
    return jnp.concatenate([x] * n, axis=1)


def _softmax_step(s, v1, m_ref, acc_ref):
    m_prev = m_ref[...]
    m_new = jnp.maximum(m_prev, jnp.max(s, axis=-1, keepdims=True))
    alpha = jnp.exp2(m_prev - m_new)
    p = jnp.exp2(s - _lane_tile(m_new, s.shape[1] // LANES))
    acc_ref[...] = _lane_tile(alpha, 2) * acc_ref[...] + _dot(p.astype(BF16), v1)
    m_ref[...] = m_new


def _softmax_result(acc):
    return acc[:, :LANES] / acc[:, LANES:]


def _mla_attn_kernel(q_ref, k_ref, v_ref, cb_ref, o_ref, v1, m_ref, acc_ref, *, t, n_chunks):
    qi = pl.program_id(2)

    hs = MLA_HEADS_PER_STEP

    @pl.when(qi == 0)
    def _():
        for h in range(hs):
            v1[h] = _values_and_ones(v_ref[0, :, h * MLA_V:(h + 1) * MLA_V])

    m_ref[...] = jnp.full(m_ref.shape, NEG_INF, F32)
    acc_ref[...] = jnp.zeros(acc_ref.shape, F32)
    rg = t // MLA_ROW_GROUPS

    def chunk(c, masked):
        for h in range(hs):
            for i in range(MLA_ROW_GROUPS):
                rows = pl.ds(i * rg, rg)
                qh = q_ref[0, i * rg:(i + 1) * rg, h * MLA_QK_PAD:(h + 1) * MLA_QK_PAD]
                s = _dot_nt(qh, k_ref[0, c * t:(c + 1) * t, h * MLA_QK_PAD:(h + 1) * MLA_QK_PAD])
                if masked:
                    s = s + cb_ref[i * rg:(i + 1) * rg, :]
                _softmax_step(s, v1[h, c * t:(c + 1) * t, :], m_ref.at[h, rows], acc_ref.at[h, rows])

    for c in range(n_chunks):
        pl.when(c < qi)(functools.partial(chunk, c, False))
        pl.when(c == qi)(functools.partial(chunk, c, True))
    for h in range(hs):
        o_ref[0, :, h * MLA_V:(h + 1) * MLA_V] = _softmax_result(acc_ref[h]).astype(o_ref.dtype)


def mla_attention(q, k, v, *, t):
    b, s, _ = q.shape
    causal = jnp.asarray(np.where(np.arange(t)[None, :] <= np.arange(t)[:, None], 0.0, NEG_INF), F32)
    kern = functools.partial(_mla_attn_kernel, t=t, n_chunks=s // t)
    hs = MLA_HEADS_PER_STEP
    return pl.pallas_call(
        kern,
        grid=(b, MLA_HEADS // hs, s // t),
        in_specs=[pl.BlockSpec((1, t, hs * MLA_QK_PAD), lambda b_, h, i: (b_, i, h)),
                  pl.BlockSpec((1, s, hs * MLA_QK_PAD), lambda b_, h, i: (b_, 0, h)),
                  pl.BlockSpec((1, s, hs * MLA_V), lambda b_, h, i: (b_, 0, h)),
                  pl.BlockSpec((t, t), lambda b_, h, i: (0, 0))],
        out_specs=pl.BlockSpec((1, t, hs * MLA_V), lambda b_, h, i: (b_, i, h)),
        out_shape=jax.ShapeDtypeStruct((b, s, MLA_HEADS * MLA_V), BF16),
        scratch_shapes=[pltpu.VMEM((hs, s, 2 * LANES), BF16), pltpu.VMEM((hs, t, LANES), F32),
                        pltpu.VMEM((hs, t, 2 * LANES), F32)],
        compiler_params=_cparams(("parallel", "parallel", "arbitrary")),
    )(q, k, v, causal)


def _rglru_kernel(x_ref, y_ref, cw_ref, cb_ref, gaw_ref, gab_ref, gxw_ref, gxb_ref, lam_ref,
                  o_ref, xbuf, h_ref, *, ts):
    t = pl.program_id(2)

    @pl.when(t == 0)
    def _():
        xbuf[0:8, :] = jnp.zeros((8, LANES), F32)
        h_ref[...] = jnp.zeros(h_ref.shape, F32)

    x = x_ref[0]
    xbuf[8:, :] = x
    cw = cw_ref[...]
    xc = cb_ref[...] + cw[3:4] * x
    for kk in range(CONV_WIDTH - 1):
        back = CONV_WIDTH - 1 - kk
        xc = xc + cw[kk:kk + 1] * xbuf[8 - back:8 - back + ts, :]
    xbuf[0:8, :] = x[ts - 8:, :]

    xcb = xc.astype(BF16)
    r = _sigmoid(_dot(xcb, gaw_ref[0]) + gab_ref[0])
    gi = _sigmoid(_dot(xcb, gxw_ref[0]) + gxb_ref[0])
    z = -lam_ref[...]
    softplus = jnp.maximum(z, 0.0) + jnp.log1p(jnp.exp(-jnp.abs(z)))
    log_a = (-LRU_C) * r * softplus
    a = jnp.exp(log_a)
    mult = jnp.sqrt(-jnp.tanh(log_a) * (a * a + 1.0))
    row = lax.broadcasted_iota(jnp.int32, (ts, 1), 0)
    mult = jnp.where(row + t * ts == 0, 1.0, mult)
    bv = mult * gi * xc

    d = 1
    while d < ts:
        keep = row >= d
        a_sh = jnp.where(keep, pltpu.roll(a, d, 0), 1.0)
        b_sh = jnp.where(keep, pltpu.roll(bv, d, 0), 0.0)
        bv = a * b_sh + bv
        a = a * a_sh
        d *= 2
    h = bv + a * h_ref[...]
    h_ref[...] = h[ts - 1:ts, :]
    o_ref[0] = (h * _gelu_tanh(y_ref[0])).astype(o_ref.dtype)


def rglru(u3, conv_w, conv_b, ga_w, ga_b, gx_w, gx_b, lam, *, ts):
    b, s, _ = u3.shape
    rec0 = EV_REC // LANES
    gate0 = EV_GATE // LANES
    cb = conv_b.reshape(1, LRU_WIDTH)
    gab = ga_b.reshape(LRU_BLOCKS, 1, LRU_BLOCK_W)
    gxb = gx_b.reshape(LRU_BLOCKS, 1, LRU_BLOCK_W)
    lam2 = lam.reshape(1, LRU_WIDTH)
    blk_w = pl.BlockSpec((1, LRU_BLOCK_W, LRU_BLOCK_W), lambda b_, n, t: (n, 0, 0))
    blk_b = pl.BlockSpec((1, 1, LRU_BLOCK_W), lambda b_, n, t: (n, 0, 0))
    vec = pl.BlockSpec((1, LANES), lambda b_, n, t: (0, n))
    return pl.pallas_call(
        functools.partial(_rglru_kernel, ts=ts),
        grid=(b, LRU_BLOCKS, s // ts),
        in_specs=[pl.BlockSpec((1, ts, LANES), lambda b_, n, t: (b_, t, rec0 + n)),
                  pl.BlockSpec((1, ts, LANES), lambda b_, n, t: (b_, t, gate0 + n)),
                  pl.BlockSpec((CONV_WIDTH, LANES), lambda b_, n, t: (0, n)),
                  vec, blk_w, blk_b, blk_w, blk_b, vec],
        out_specs=pl.BlockSpec((1, ts, LANES), lambda b_, n, t: (b_, t, n)),
        out_shape=jax.ShapeDtypeStruct((b, s, LRU_WIDTH), BF16),
        scratch_shapes=[pltpu.VMEM((ts + 8, LANES), F32), pltpu.VMEM((1, LANES), F32)],
        compiler_params=_cparams(("parallel", "parallel", "arbitrary")),
    )(u3, u3, conv_w, cb, ga_w, gab, gx_w, gxb, lam2)


def _ffn_kernel(x_ref, g_ref, w1_ref, w3_ref, w2_ref, o_ref, xn_ref):
    f = pl.program_id(1)

    @pl.when(f == 0)
    def _():
        x = x_ref[...]
        xn_ref[...] = _rms(x, g_ref[...]).astype(BF16)
        o_ref[...] = x

    xn = xn_ref[...]
    h1 = _dot(xn, w1_ref[...])
    h3 = _dot(xn, w3_ref[...])
    hh = (h1 * _sigmoid(h1) * h3).astype(BF16)
    o_ref[...] += _dot(hh, w2_ref[...])


def ffn_dense(x, g, w1, w3, w2, *, tm, tf):
    n, d = x.shape
    ff = w1.shape[1]
    assert n % tm == 0 and ff % tf == 0
    return pl.pallas_call(
        _ffn_kernel,
        grid=(n // tm, ff // tf),
        in_specs=[pl.BlockSpec((tm, d), lambda i, f: (i, 0)),
                  pl.BlockSpec((1, d), lambda i, f: (0, 0)),
                  pl.BlockSpec((d, tf), lambda i, f: (0, f)),
                  pl.BlockSpec((d, tf), lambda i, f: (0, f)),
                  pl.BlockSpec((tf, d), lambda i, f: (f, 0))],
        out_specs=pl.BlockSpec((tm, d), lambda i, f: (i, 0)),
        out_shape=jax.ShapeDtypeStruct((n, d), F32),
        scratch_shapes=[pltpu.VMEM((tm, d), BF16)],
        compiler_params=_cparams(("parallel", "arbitrary")),
    )(x, g.reshape(1, d), w1, w3, w2)


def _compress_kernel(*refs, n_half):
    kc_refs = refs[:NSA_GROUPS]
    pe_ref, w1_ref, b1_ref, w2_ref, o_ref = refs[NSA_GROUPS:]
    pe = pe_ref[...]
    half_k = n_half * NSA_DH
    nchunk = o_ref.shape[1]
    for g in range(NSA_GROUPS):
        lo, hi = [], []
        for l in range(n_half):
            piece = kc_refs[g][0, pl.ds(l, nchunk, stride=n_half), :]
            lo.append((piece + pe[l:l + 1]).astype(BF16))
            hi.append((piece + pe[n_half + l:n_half + l + 1]).astype(BF16))
        z0 = _dot(jnp.concatenate(lo, axis=1), w1_ref[0:half_k, :])
        z1 = _dot(jnp.concatenate(hi, axis=1), w1_ref[half_k:2 * half_k, :])
        rows = z1.shape[0]
        pre = z0 + pltpu.roll(z1, rows - 1, 0) + b1_ref[...]
        o_ref[0, :, g * NSA_DH:(g + 1) * NSA_DH] = _dot(_gelu_tanh(pre).astype(BF16), w2_ref[...]).astype(o_ref.dtype)


def nsa_compress(u3, col, pe, w1, b1, w2):
    b, s, _ = u3.shape
    nchunk = s // CMP_STRIDE
    assert CMP_BLOCK == 2 * CMP_STRIDE and col % NSA_KV_DIM == 0
    full = lambda a: pl.BlockSpec(a.shape, lambda i: (0,) * a.ndim)
    b1r = b1.reshape(1, NSA_DH)
    return pl.pallas_call(
        functools.partial(_compress_kernel, n_half=CMP_STRIDE),
        grid=(b,),
        in_specs=[pl.BlockSpec((1, s, NSA_DH), lambda i, g=g: (i, 0, col // NSA_DH + g)) for g in range(NSA_GROUPS)]
        + [full(pe), full(w1), full(b1r), full(w2)],
        out_specs=pl.BlockSpec((1, nchunk, NSA_KV_DIM), lambda i: (i, 0, 0)),
        out_shape=jax.ShapeDtypeStruct((b, nchunk, NSA_KV_DIM), BF16),
        compiler_params=_cparams(("parallel",)),
    )(*([u3] * NSA_GROUPS), pe, w1, b1r, w2)


def _rope128(x, c, s):
    return x * c + pltpu.roll(x, NSA_DH // 2, 1) * s


def _pack_bf16_pairs(x):
    w = x.shape[1] // 2
    bits = pltpu.bitcast(x.astype(BF16).astype(F32), jnp.uint32)
    word = bits[:, w:] | lax.shift_right_logical(bits[:, :w], jnp.uint32(16))
    return pltpu.bitcast(word, F32)


def _unpack_bf16_pairs(word):
    bits = pltpu.bitcast(word, jnp.uint32)
    lo = pltpu.bitcast(lax.shift_left(bits, jnp.uint32(16)), F32).astype(BF16)
    hi = pltpu.bitcast(bits & jnp.uint32(0xFFFF0000), F32).astype(BF16)
    return lo, hi


def _split3(x):
    hi = x.astype(BF16)
    r1 = x - hi.astype(F32)
    mid = r1.astype(BF16)
    lo = (r1 - mid.astype(F32)).astype(BF16)
    return hi, mid, lo


def _nsa_attn_kernel(q_ref, ks_ref, vs_ref, kw_ref, vw_ref, gt_ref, kc_ref, vc_ref,
                     cq_ref, sq_ref, ck_ref, sk_ref, ovl_ref, exp_ref, wb_ref,
                     o_ref, ksr, vsb, kwr, vwb, selb, m_ref, acc_ref,
                     *, tq, tk, seq, scale):
    qi = pl.program_id(2)
    hp = NSA_HPG
    n_sel = seq // SEL_BLOCK

    @pl.when(qi == 0)
    def _():
        ck, sk = ck_ref[...], sk_ref[...]
        ksr[...] = _rope128(ks_ref[0], ck, sk).astype(BF16)
        kwr[...] = _rope128(kw_ref[0], ck, sk).astype(BF16)
        vsb[...] = _values_and_ones(vs_ref[0])
        vwb[...] = _values_and_ones(vw_ref[0])

    q = q_ref[0] * (scale * LOG2E)
    cq, sq = cq_ref[...], sq_ref[...]
    heads = [q[:, p * NSA_DH:(p + 1) * NSA_DH] for p in range(hp)]
    qu = jnp.concatenate(heads, axis=0).astype(BF16)
    qr = jnp.concatenate([_rope128(h, cq, sq) for h in heads], axis=0).astype(BF16)
    t_row = qi * tq + lax.broadcasted_iota(jnp.int32, (tq, 1), 0)
    lane = lax.broadcasted_iota(jnp.int32, (1, LANES), 1)

    sc = _dot_nt(qu, kc_ref[0]).reshape(hp, tq, LANES)
    valid = (lane * CMP_STRIDE + (CMP_BLOCK - 1) <= t_row)[None]
    sm = jnp.where(valid, sc, NEG_INF)
    e = jnp.exp2(sm - jnp.max(sm, axis=-1, keepdims=True))
    p = jnp.where(valid, e / jnp.sum(e, axis=-1, keepdims=True), 0.0)
    o_cmp = _dot(p.reshape(hp * tq, LANES).astype(BF16), vc_ref[0])

    psum = p[0]
    for i in range(1, hp):
        psum = psum + p[i]
    ovl = ovl_ref[...]
    imp = sum(_dot(part, ovl) for part in _split3(psum))
    cur = jnp.right_shift(t_row, SEL_BLOCK.bit_length() - 1)
    future = lane > cur
    forced = (lane == 0) | (lane == cur) | (lane == cur - 1)
    score = jnp.where(future, -1.0, jnp.where(forced, SEL_FORCE, imp))
    sc_t = score.T[0:n_sel, :]
    blk = lax.broadcasted_iota(jnp.int32, (n_sel, 1), 0)
    cnt = jnp.zeros((n_sel, tq), F32)
    for j in range(n_sel):
        other = sc_t[j:j + 1, :]
        beats = (other > sc_t) | ((other == sc_t) & (blk > j))
        cnt = cnt + jnp.where(beats, 1.0, 0.0)
    sel_t = jnp.where(cnt < SEL_COUNT, 1.0, 0.0)
    sel = jnp.concatenate([sel_t, jnp.zeros((LANES - n_sel, tq), F32)], axis=0).T.astype(BF16)
    picked = _dot(sel, exp_ref[...])
    kall = lax.broadcasted_iota(jnp.int32, (1, seq), 1)
    selb[...] = jnp.where((picked > 0.5) & (kall <= t_row), 0.0, NEG_INF)

    m_ref[...] = jnp.full(m_ref.shape, NEG_INF, F32)
    acc_ref[...] = jnp.zeros(acc_ref.shape, F32)
    for c in range(seq // tk):
        @pl.when(c * tk <= qi * tq + (tq - 1))
        def _():
            for i in range(hp):
                rows = pl.ds(i * tq, tq)
                s = _dot_nt(qr[i * tq:(i + 1) * tq], ksr[c * tk:(c + 1) * tk, :])
                _softmax_step(s + selb[:, c * tk:(c + 1) * tk], vsb[c * tk:(c + 1) * tk, :],
                              m_ref.at[rows], acc_ref.at[rows])
    o_sel = _softmax_result(acc_ref[...])

    span = WINDOW + NSA_WIN_ROWS
    wins = []
    for i in range(hp):
        for r in range(tq // NSA_WIN_ROWS):
            group = qi * (tq // NSA_WIN_ROWS) + r
            start = pl.multiple_of(jnp.maximum(group * NSA_WIN_ROWS - WINDOW, 0), NSA_WIN_ROWS)
            bias = wb_ref[jnp.minimum(group, WINDOW // NSA_WIN_ROWS)]
            r0 = i * tq + r * NSA_WIN_ROWS
            sw = _dot_nt(qr[r0:r0 + NSA_WIN_ROWS], kwr[pl.ds(start, span), :]) + bias
            ew = jnp.exp2(sw - jnp.max(sw, axis=-1, keepdims=True)).astype(BF16)
            wins.append(_softmax_result(_dot(ew, vwb[pl.ds(start, span), :])))
    o_win = jnp.concatenate(wins, axis=0)

    gates = _sigmoid(gt_ref[0])
    for i in range(hp):
        rows = slice(i * tq, (i + 1) * tq)
        o = (gates[:, 3 * i:3 * i + 1] * o_cmp[rows] + gates[:, 3 * i + 1:3 * i + 2] * o_sel[rows]
             + gates[:, 3 * i + 2:3 * i + 3] * o_win[rows])
        o_ref[0, :, i * NSA_DH:(i + 1) * NSA_DH] = o.astype(o_ref.dtype)


def _window_bias(tq):
    span = WINDOW + tq
    out = []
    for qi in range(WINDOW // tq + 1):
        start = max(qi * tq - WINDOW, 0)
        t = qi * tq + np.arange(tq)[:, None]
        kpos = start + np.arange(span)[None, :]
        out.append(np.where((kpos <= t) & (kpos > t - WINDOW), 0.0, NEG_INF))
    return jnp.asarray(np.stack(out), F32)


def nsa_attention(u3, k_cmp, v_cmp, rope_c, rope_s, ovl, expand, *, tq, tk):
    b, s, _ = u3.shape
    assert s // CMP_STRIDE == LANES and tq % SEL_BLOCK == 0 and tq % NSA_WIN_ROWS == 0 and WINDOW % NSA_WIN_ROWS == 0
    hp = NSA_HPG
    wbias = _window_bias(NSA_WIN_ROWS)
    col = lambda off: (lambda b_, g, i: (b_, 0, off // NSA_DH + g))
    seq_blk = lambda off: pl.BlockSpec((1, s, NSA_DH), col(off))
    full = lambda a: pl.BlockSpec(a.shape, lambda b_, g, i: (0,) * a.ndim)
    cmp_blk = pl.BlockSpec((1, LANES, NSA_DH), lambda b_, g, i: (b_, 0, g))
    rope_q = pl.BlockSpec((tq, NSA_DH), lambda b_, g, i: (i, 0))
    kern = functools.partial(_nsa_attn_kernel, tq=tq, tk=tk, seq=s, scale=NSA_DH ** -0.5)
    return pl.pallas_call(
        kern,
        grid=(b, NSA_GROUPS, s // tq),
        in_specs=[pl.BlockSpec((1, tq, hp * NSA_DH), lambda b_, g, i: (b_, i, g)),
                  seq_blk(OD_KS), seq_blk(OD_VS), seq_blk(OD_KW), seq_blk(OD_VW),
                  pl.BlockSpec((1, tq, LANES), lambda b_, g, i: (b_, i, OD_G // LANES + g)),
                  cmp_blk, cmp_blk, rope_q, rope_q, full(rope_c), full(rope_s), full(ovl), full(expand),
                  full(wbias)],
        out_specs=pl.BlockSpec((1, tq, hp * NSA_DH), lambda b_, g, i: (b_, i, g)),
        out_shape=jax.ShapeDtypeStruct((b, s, NSA_Q_DIM), BF16),
        scratch_shapes=[pltpu.VMEM((s, NSA_DH), BF16), pltpu.VMEM((s, 2 * LANES), BF16)] * 2 + [
            pltpu.VMEM((tq, s), F32),
            pltpu.VMEM((hp * tq, LANES), F32), pltpu.VMEM((hp * tq, 2 * LANES), F32)],
        compiler_params=_cparams(("parallel", "parallel", "arbitrary")),
    )(u3, u3, u3, u3, u3, u3, k_cmp, v_cmp, rope_c, rope_s, rope_c, rope_s, ovl, expand, wbias)


def _router_kernel(x_ref, g_ref, wr_ref, br_ref, xn_ref, meta_ref, cnt_ref, *, tm):
    @pl.when(pl.program_id(0) == 0)
    def _():
        cnt_ref[...] = jnp.zeros(cnt_ref.shape, F32)

    xn = _rms(x_ref[...], g_ref[...])
    xn_ref[...] = _pack_bf16_pairs(xn)
    xh, xm, _ = _split3(xn)
    wh, wm, _ = _split3(wr_ref[...])
    logits = _dot(xh, wh) + _dot(xh, wm) + _dot(xm, wh) + br_ref[...]
    lane = lax.broadcasted_iota(jnp.int32, (1, LANES), 1).astype(F32)
    lg = jnp.where(lane < N_EXPERTS, logits, NEG_INF)
    m1 = jnp.max(lg, axis=-1, keepdims=True)
    e1 = jnp.min(jnp.where(lg == m1, lane, float(LANES)), axis=-1, keepdims=True)
    lg2 = jnp.where(lane == e1, NEG_INF, lg)
    m2 = jnp.max(lg2, axis=-1, keepdims=True)
    e2 = jnp.min(jnp.where(lg2 == m2, lane, float(LANES)), axis=-1, keepdims=True)
    ex = jnp.exp(m2 - m1)
    den = 1.0 + ex
    g1 = 1.0 / den
    g2 = ex / den
    oh = jnp.where((lane == e1) | (lane == e2), 1.0, 0.0)
    r = lax.broadcasted_iota(jnp.int32, (tm, tm), 0)
    c = lax.broadcasted_iota(jnp.int32, (tm, tm), 1)
    tri = jnp.where(r > c, 1.0, 0.0).astype(BF16)
    cum = _dot(tri, oh.astype(BF16)) + cnt_ref[0:1, :]
    pos1 = jnp.sum(jnp.where(lane == e1, cum, 0.0), axis=-1, keepdims=True)
    pos2 = jnp.sum(jnp.where(lane == e2, cum, 0.0), axis=-1, keepdims=True)
    cnt_ref[...] = cnt_ref[...] + jnp.sum(oh, axis=0, keepdims=True)
    meta = jnp.where(lane == 0, e1, 0.0)
    meta = jnp.where(lane == 1, e2, meta)
    meta = jnp.where(lane == 2, g1, meta)
    meta = jnp.where(lane == 3, g2, meta)
    meta = jnp.where(lane == 4, pos1, meta)
    meta = jnp.where(lane == 5, pos2, meta)
    meta_ref[...] = meta


def moe_router(x, g, wr, br, *, tm):
    n, d = x.shape
    return pl.pallas_call(
        functools.partial(_router_kernel, tm=tm),
        grid=(n // tm,),
        in_specs=[pl.BlockSpec((tm, d), lambda i: (i, 0)),
                  pl.BlockSpec((1, d), lambda i: (0, 0)),
                  pl.BlockSpec((d, LANES), lambda i: (0, 0)),
                  pl.BlockSpec((1, LANES), lambda i: (0, 0))],
        out_specs=[pl.BlockSpec((tm, d // 2), lambda i: (i, 0)),
                   pl.BlockSpec((tm, LANES), lambda i: (i, 0)),
                   pl.BlockSpec((8, LANES), lambda i: (0, 0))],
        out_shape=[jax.ShapeDtypeStruct((n, d // 2), F32),
                   jax.ShapeDtypeStruct((n, LANES), F32),
                   jax.ShapeDtypeStruct((8, LANES), F32)],
        compiler_params=_cparams(("arbitrary",)),
    )(x, g.reshape(1, d), wr, br)


def _gather_chunk(per_worker, row_bytes):
    best = 0
    for c in range(8, per_worker + 1, 8):
        if per_worker % c == 0 and c * row_bytes <= SC_GATHER_BYTES and c <= LANES:
            best = c
    assert best > 0, (per_worker, row_bytes)
    return best


def sc_gather_rows(table, idx):
    _, d = table.shape
    b = idx.shape[0]
    assert b % (8 * SC_WORKERS) == 0 and table.dtype.itemsize == 4
    per_w = b // SC_WORKERS
    chunk = _gather_chunk(per_w, d * 4)
    mesh = plsc.VectorSubcoreMesh(core_axis_name="c", subcore_axis_name="s",
                                  num_cores=SC_CORES, num_subcores=SC_SUBCORES)

    @functools.partial(
        pl.kernel, mesh=mesh,
        out_type=jax.ShapeDtypeStruct((b, d), table.dtype),
        scratch_types=[pltpu.VMEM((chunk,), jnp.int32), pltpu.VMEM((chunk, d), table.dtype),
                       pltpu.SemaphoreType.DMA])
    def gather(table_hbm, idx_hbm, out_hbm, idx_v, rows_v, sem):
        wid = lax.axis_index("s") * SC_CORES + lax.axis_index("c")
        base = wid * per_w

        @pl.loop(0, per_w // chunk)
        def _(c):
            off = pl.multiple_of(base + c * chunk, 8)
            pltpu.sync_copy(idx_hbm.at[pl.ds(off, chunk)], idx_v)
            pltpu.async_copy(table_hbm.at[idx_v], rows_v, sem).wait()
            pltpu.sync_copy(rows_v, out_hbm.at[pl.ds(off, chunk)])

    return gather(table, idx)


def sc_scatter_rows(src, dests, n_out):
    n, d = src.shape
    assert n % (8 * SC_WORKERS) == 0 and src.dtype.itemsize == 4
    per_w = n // SC_WORKERS
    chunk = _gather_chunk(per_w, d * 4)
    n_dest = len(dests)
    mesh = plsc.VectorSubcoreMesh(core_axis_name="c", subcore_axis_name="s",
                                  num_cores=SC_CORES, num_subcores=SC_SUBCORES)

    @functools.partial(
        pl.kernel, mesh=mesh,
        out_type=jax.ShapeDtypeStruct((n_out, d), src.dtype),
        scratch_types=[pltpu.VMEM((chunk,), jnp.int32), pltpu.VMEM((chunk, d), src.dtype),
                       pltpu.SemaphoreType.DMA])
    def scatter(src_hbm, *rest):
        dest_hbms = rest[:n_dest]
        out_hbm, idx_v, rows_v, sem = rest[n_dest:]
        wid = lax.axis_index("s") * SC_CORES + lax.axis_index("c")
        base = wid * per_w

        @pl.loop(0, per_w // chunk)
        def _(c):
            off = pl.multiple_of(base + c * chunk, 8)
            pltpu.sync_copy(src_hbm.at[pl.ds(off, chunk)], rows_v)
            for dest_hbm in dest_hbms:
                pltpu.sync_copy(dest_hbm.at[pl.ds(off, chunk)], idx_v)
                pltpu.async_copy(rows_v, out_hbm.at[idx_v], sem).wait()

    return scatter(src, *dests)


def _expert_kernel(te_ref, ts_ref, tr_ref, xp_hbm, *rest, nf):
    ns = MOE_WSPLIT
    w1_refs, w3_refs, w2_refs = rest[:ns], rest[ns:2 * ns], rest[2 * ns:3 * ns]
    yr_hbm, xb, acc, stage, in_sem, out_sem = rest[3 * ns:]
    t = pl.program_id(0)
    f = pl.program_id(1)
    rows = tr_ref[t]
    start = pl.multiple_of(ts_ref[t], MOE_ALIGN)
    n_chunks = MOE_TILE // MOE_COPY
    half = xb.shape[1] // 2

    def in_copy(ci):
        return pltpu.make_async_copy(xp_hbm.at[pl.ds(start + ci * MOE_COPY, MOE_COPY)],
                                     stage.at[ci % 2], in_sem.at[ci % 2])

    def out_copy(ci, first_row):
        return pltpu.make_async_copy(acc.at[pl.ds(ci * MOE_COPY, MOE_COPY)],
                                     yr_hbm.at[pl.ds(first_row + ci * MOE_COPY, MOE_COPY)], out_sem.at[0])

    def when_chunk_live(ci, fn):
        pl.when(ci * MOE_COPY < rows)(fn)

    def wait_outputs(tile):
        tile_rows = tr_ref[tile]
        first_row = pl.multiple_of(ts_ref[tile], MOE_ALIGN)
        for ci in range(n_chunks):
            pl.when(ci * MOE_COPY < tile_rows)(lambda ci=ci: out_copy(ci, first_row).wait())

    @pl.when((f == 0) & (rows > 0))
    def _():
        def unpack(ci):
            in_copy(ci).wait()
            lo, hi = _unpack_bf16_pairs(stage[ci % 2])
            xb[ci * MOE_COPY:(ci + 1) * MOE_COPY, :half] = lo
            xb[ci * MOE_COPY:(ci + 1) * MOE_COPY, half:] = hi

        when_chunk_live(0, lambda: in_copy(0).start())
        for ci in range(n_chunks):
            if ci + 1 < n_chunks:
                when_chunk_live(ci + 1, lambda ci=ci: in_copy(ci + 1).start())
            when_chunk_live(ci, lambda ci=ci: unpack(ci))

    @pl.when((f == 0) & (t > 0))
    def _():
        wait_outputs(jnp.maximum(t - 1, 0))

    @pl.when((f == 0) & (rows > 0))
    def _():
        acc[...] = jnp.zeros(acc.shape, F32)

    @pl.when(rows > 0)
    def _():
        w1 = jnp.concatenate([r[0, 0].astype(BF16) for r in w1_refs], axis=0)
        w3 = jnp.concatenate([r[0, 0].astype(BF16) for r in w3_refs], axis=0)
        w2 = jnp.concatenate([r[0, 0].astype(BF16) for r in w2_refs], axis=1)

        def chain(r0, size):
            rs = pl.ds(pl.multiple_of(r0, MOE_PIECES[-1]), size)
            xc = xb[rs, :]
            h1 = _dot(xc, w1)
            h3 = _dot(xc, w3)
            hh = (h1 * _sigmoid(h1) * h3).astype(BF16)
            acc[rs, :] += _dot(hh, w2)

        todo = (rows + (MOE_PIECES[-1] - 1)) // MOE_PIECES[-1] * MOE_PIECES[-1]
        off = jnp.int32(0)
        for size in MOE_PIECES:
            take = todo - off >= size

            @pl.when(take)
            def _(off=off, size=size):
                for c0 in range(0, size, MOE_CHAIN):
                    chain(off + c0, min(MOE_CHAIN, size))

            off = off + jnp.where(take, size, 0)

    @pl.when((f == nf - 1) & (rows > 0))
    def _():
        for ci in range(n_chunks):
            when_chunk_live(ci, lambda ci=ci: out_copy(ci, start).start())

    @pl.when((f == nf - 1) & (t == pl.num_programs(0) - 1))
    def _():
        wait_outputs(t)


def moe_experts(xp, tile_e, tile_start, tile_rows, w1, w3, w2, layer, *, tf):
    n_rows, half = xp.shape
    d = 2 * half
    n_tiles = tile_e.shape[0]
    ff = w1.shape[3]
    nf = ff // tf
    assert MOE_TILE % MOE_COPY == 0 and sum(MOE_PIECES) >= MOE_TILE

    def f_eff(t, f, tr):
        return jnp.where(tr[t] > 0, f, nf - 1)

    ns = MOE_WSPLIT
    up_specs = [pl.BlockSpec((1, 1, d // ns, tf), lambda t, f, te, ts, tr, k=k: (layer, te[t], k, f_eff(t, f, tr)))
                for k in range(ns)]
    down_specs = [pl.BlockSpec((1, 1, tf, d // ns), lambda t, f, te, ts, tr, k=k: (layer, te[t], f_eff(t, f, tr), k))
                  for k in range(ns)]
    grid_spec = pltpu.PrefetchScalarGridSpec(
        num_scalar_prefetch=3,
        grid=(n_tiles, nf),
        in_specs=[pl.BlockSpec(memory_space=pl.ANY)] + 2 * up_specs + down_specs,
        out_specs=pl.BlockSpec(memory_space=pl.ANY),
        scratch_shapes=[pltpu.VMEM((MOE_TILE, d), BF16), pltpu.VMEM((MOE_TILE, d), F32),
                        pltpu.VMEM((2, MOE_COPY, half), F32),
                        pltpu.SemaphoreType.DMA((2,)), pltpu.SemaphoreType.DMA((1,))],
    )
    return pl.pallas_call(
        functools.partial(_expert_kernel, nf=nf),
        grid_spec=grid_spec,
        out_shape=jax.ShapeDtypeStruct((n_rows, d), F32),
        compiler_params=_cparams(("arbitrary", "arbitrary")),
    )(tile_e, tile_start, tile_rows, xp, *([w1] * ns), *([w3] * ns), *([w2] * ns))


def _combine_kernel(x_ref, y1_ref, y2_ref, meta_ref, g_ref, o_ref, *, final_norm):
    meta = meta_ref[...]
    y = x_ref[...] + (meta[:, 2:3] * y1_ref[...] + meta[:, 3:4] * y2_ref[...])
    if final_norm:
        y = _rms(y, g_ref[...])
    o_ref[...] = y


def moe_combine(x, yg, meta, g, *, final_norm, tm):
    n, d = x.shape
    nb = n // tm
    row = pl.BlockSpec((tm, d), lambda i: (i, 0))
    return pl.pallas_call(
        functools.partial(_combine_kernel, final_norm=final_norm),
        grid=(nb,),
        in_specs=[row, row, pl.BlockSpec((tm, d), lambda i: (i + nb, 0)),
                  pl.BlockSpec((tm, LANES), lambda i: (i, 0)),
                  pl.BlockSpec((1, d), lambda i: (0, 0))],
        out_specs=row,
        out_shape=jax.ShapeDtypeStruct((n, d), F32),
        compiler_params=_cparams(("parallel",)),
    )(x, yg, yg, meta, g.reshape(1, d))


def _mla_rope_tables(seq):
    half = MLA_ROPE // 2
    pos = jnp.arange(seq, dtype=F32)
    inv = ROPE_THETA ** (-jnp.arange(0, MLA_ROPE, 2, dtype=F32) / MLA_ROPE)
    ang = pos[:, None] * inv[None, :]
    cos, sin = jnp.cos(ang), jnp.sin(ang)
    z = jnp.zeros((seq, half), F32)
    pad = jnp.zeros((seq, LANES - MLA_ROPE), F32)
    c = jnp.concatenate([cos, cos, pad], axis=1)
    s_lo = jnp.concatenate([-sin, z, pad], axis=1)
    s_hi = jnp.concatenate([z, sin, pad], axis=1)
    return c, s_lo, s_hi


def _nsa_rope_tables(seq):
    pos = jnp.arange(seq, dtype=F32)
    inv = ROPE_THETA ** (-jnp.arange(0, NSA_DH, 2, dtype=F32) / NSA_DH)
    ang = pos[:, None] * inv[None, :]
    cos, sin = jnp.cos(ang), jnp.sin(ang)
    return jnp.concatenate([cos, cos], axis=1), jnp.concatenate([-sin, sin], axis=1)


def _selection_constants(seq):
    n_sel = seq // SEL_BLOCK
    nc = (seq - CMP_BLOCK) // CMP_STRIDE + 1
    cmp_start = np.arange(LANES) * CMP_STRIDE
    sel_start = np.arange(LANES) * SEL_BLOCK
    ovl = ((cmp_start[:, None] < sel_start[None, :] + SEL_BLOCK) &
           (cmp_start[:, None] + CMP_BLOCK > sel_start[None, :]))
    ovl &= (np.arange(LANES)[:, None] < nc) & (np.arange(LANES)[None, :] < n_sel)
    expand = (np.arange(seq)[None, :] // SEL_BLOCK == np.arange(LANES)[:, None])
    return jnp.asarray(ovl, BF16), jnp.asarray(expand, BF16)


def even_layer(x, seq, p):
    n = x.shape[0]
    b = n // seq
    (norm_mix, w_in, q_norm, w_q_up, kv_norm, w_kv_up, conv_w, conv_b, ga_w, ga_b, gx_w, gx_b,
     lam, w_out, norm_ffn, w1, w3, w2) = p
    d = D_MODEL
    o1 = MLA_Q_LORA + MLA_KV_LORA
    o2 = o1 + MLA_ROPE
    w_pack = jnp.concatenate(
        [w_in[:, :o1], w_in[:, o2:], w_in[:, o1:o2], jnp.zeros((d, LANES - MLA_ROPE), F32)], axis=1).astype(BF16)
    u = norm_matmul(x, norm_mix, w_pack, tm=1024, tn=EV_PACKED // 3)

    wq = w_q_up.reshape(MLA_Q_LORA, MLA_HEADS, MLA_NOPE + MLA_ROPE)
    wq = jnp.pad(wq, ((0, 0), (0, 0), (0, MLA_QK_PAD - MLA_NOPE - MLA_ROPE)))
    wq = wq.reshape(MLA_Q_LORA, MLA_HEADS * MLA_QK_PAD).astype(BF16)
    wkv = w_kv_up.reshape(MLA_KV_LORA, MLA_HEADS, MLA_NOPE + MLA_V)
    wk = wkv[:, :, :MLA_NOPE].reshape(MLA_KV_LORA, MLA_HEADS * MLA_NOPE).astype(BF16)
    wv = wkv[:, :, MLA_NOPE:].reshape(MLA_KV_LORA, MLA_HEADS * MLA_V).astype(BF16)
    rc, rlo, rhi = _mla_rope_tables(seq)
    q, k, v = mla_up(u, q_norm, kv_norm, wq, wk, wv, rc, rlo, rhi, seq=seq, tm=512)
    o_mla = mla_attention(q.reshape(b, seq, -1), k.reshape(b, seq, -1), v.reshape(b, seq, -1), t=512)

    o_rec = rglru(u.reshape(b, seq, EV_PACKED), conv_w, conv_b, ga_w.astype(BF16), ga_b,
                  gx_w.astype(BF16), gx_b, lam, ts=512)
    x = matmul_residual([o_mla.reshape(n, -1), o_rec.reshape(n, -1)], w_out.astype(BF16), x, tm=512, tn=D_MODEL)
    return ffn_dense(x, norm_ffn, w1.astype(BF16), w3.astype(BF16), w2.astype(BF16), tm=1024, tf=512)


def _moe_dispatch(meta, counts, n):
    e1 = meta[:, 0].astype(jnp.int32)
    e2 = meta[:, 1].astype(jnp.int32)
    pos1 = meta[:, 4].astype(jnp.int32)
    pos2 = meta[:, 5].astype(jnp.int32)
    cnt = counts[0, :N_EXPERTS].astype(jnp.int32)
    span = (cnt + MOE_ALIGN - 1) // MOE_ALIGN * MOE_ALIGN
    row0 = jnp.cumsum(span) - span
    row_unit = 64 * SC_WORKERS
    n_rows = -(-(n * TOP_K + N_EXPERTS * MOE_ALIGN + MOE_TILE) // row_unit) * row_unit
    d1 = row0[e1] + pos1
    d2 = row0[e2] + pos2
    n_tiles = (n * TOP_K) // MOE_TILE + N_EXPERTS
    tiles_e = (cnt + MOE_TILE - 1) // MOE_TILE
    tend = jnp.cumsum(tiles_e)
    tbeg = tend - tiles_e
    tid = jnp.arange(n_tiles, dtype=jnp.int32)
    te = jnp.minimum(jnp.searchsorted(tend, tid, side='right'), N_EXPERTS - 1).astype(jnp.int32)
    used = tid < tend[-1]
    first = (tid - tbeg[te]) * MOE_TILE
    rows = jnp.where(used, jnp.clip(cnt[te] - first, 0, MOE_TILE), 0).astype(jnp.int32)
    tstart = jnp.where(used, row0[te] + first, 0).astype(jnp.int32)
    last_e = te[jnp.maximum(tend[-1] - 1, 0)]
    te = jnp.where(used, te, last_e).astype(jnp.int32)
    return d1, d2, n_rows, te, tstart, rows


def odd_layer(x, seq, p, experts, final_g):
    n = x.shape[0]
    b = n // seq
    (norm_mix, w_in, ck_pe, ck_w1, ck_b1, ck_w2, cv_pe, cv_w1, cv_b1, cv_w2, w_out, norm_ffn,
     router_w, router_b) = p
    ew1, ew3, ew2, layer = experts
    d = D_MODEL
    wg = w_in[:, OD_G:].reshape(d, NSA_GROUPS, NSA_HPG * 3)
    wg = jnp.pad(wg, ((0, 0), (0, 0), (0, LANES - NSA_HPG * 3))).reshape(d, NSA_GROUPS * LANES)
    w_pack = jnp.concatenate([w_in[:, :OD_G], wg], axis=1).astype(BF16)
    u = norm_matmul(x, norm_mix, w_pack, tm=1024, tn=512)
    u3 = u.reshape(b, seq, OD_PACKED)

    k_cmp = nsa_compress(u3, OD_KC, ck_pe, ck_w1.astype(BF16), ck_b1, ck_w2.astype(BF16))
    v_cmp = nsa_compress(u3, OD_VC, cv_pe, cv_w1.astype(BF16), cv_b1, cv_w2.astype(BF16))
    rc, rs = _nsa_rope_tables(seq)
    ovl, expand = _selection_constants(seq)
    o = nsa_attention(u3, k_cmp, v_cmp, rc, rs, ovl, expand, tq=512, tk=512)
    x = matmul_residual([o.reshape(n, -1)], w_out.astype(BF16), x, tm=512, tn=D_MODEL)

    wr = jnp.pad(router_w, ((0, 0), (0, LANES - N_EXPERTS)))
    br = jnp.pad(router_b, (0, LANES - N_EXPERTS)).reshape(1, LANES)
    xp, meta, counts = moe_router(x, norm_ffn, wr, br, tm=512)
    d1, d2, n_rows, te, tstart, rows = _moe_dispatch(meta, counts, n)
    yr = moe_experts(sc_scatter_rows(xp, [d1, d2], n_rows), te, tstart, rows, ew1, ew3, ew2, layer, tf=256)
    yg = sc_gather_rows(yr, jnp.concatenate([d1, d2]))
    g = final_g if final_g is not None else norm_ffn
    return moe_combine(x, yg, meta, g, final_norm=final_g is not None, tm=512)


def kernel(x, ev_norm_mix, ev_w_in, ev_q_norm, ev_w_q_up, ev_kv_norm, ev_w_kv_up, ev_conv_w, ev_conv_b, ev_gate_a_w, ev_gate_a_b, ev_gate_x_w, ev_gate_x_b, ev_lru_lambda, ev_w_out, ev_norm_ffn, ev_ffn_w1, ev_ffn_w3, ev_ffn_w2, od_norm_mix, od_w_in, od_cmp_k_pe, od_cmp_k_w1, od_cmp_k_b1, od_cmp_k_w2, od_cmp_v_pe, od_cmp_v_w1, od_cmp_v_b1, od_cmp_v_w2, od_w_out, od_norm_ffn, od_router_w, od_router_b, od_exp_w1, od_exp_w3, od_exp_w2, final_norm):
    bsz, seq, d = x.shape
    ev = (ev_norm_mix, ev_w_in, ev_q_norm, ev_w_q_up, ev_kv_norm, ev_w_kv_up, ev_conv_w, ev_conv_b,
          ev_gate_a_w, ev_gate_a_b, ev_gate_x_w, ev_gate_x_b, ev_lru_lambda, ev_w_out, ev_norm_ffn,
          ev_ffn_w1, ev_ffn_w3, ev_ffn_w2)
    od = (od_norm_mix, od_w_in, od_cmp_k_pe, od_cmp_k_w1, od_cmp_k_b1, od_cmp_k_w2, od_cmp_v_pe,
          od_cmp_v_w1, od_cmp_v_b1, od_cmp_v_w2, od_w_out, od_norm_ffn, od_router_w, od_router_b)
    h = x.reshape(bsz * seq, d)
    for layer in range(DEPTH):
        i = layer // 2
        if layer % 2 == 0:
            h = even_layer(h, seq, tuple(a[i] for a in ev))
        else:
            h = odd_layer(h, seq, tuple(a[i] for a in od), (od_exp_w1, od_exp_w3, od_exp_w2, i),
                          final_norm if layer == DEPTH - 1 else None)
    return h.reshape(bsz, seq, d)
```

```python
import functools
import math

import numpy as np
import jax
import jax.numpy as jnp
from jax import lax
from jax.experimental import pallas as pl
from jax.experimental.pallas import tpu as pltpu
from jax.experimental.pallas import tpu_sc as plsc

F32 = jnp.float32
BF16 = jnp.bfloat16

D_MODEL = 2048
DEPTH = 4
RMS_EPS = 1e-6
ROPE_THETA = 10000.0
NEG_INF = -1e30
LOG2E = math.log2(math.e)

MLA_HEADS = 8
MLA_Q_LORA = 768
MLA_KV_LORA = 512
MLA_NOPE = 128
MLA_ROPE = 64
MLA_V = 128
MLA_QK_PAD = 256
MLA_ROW_GROUPS = 2
MLA_HEADS_PER_STEP = 4

LRU_WIDTH = D_MODEL // 2
LRU_BLOCKS = 8
LRU_BLOCK_W = LRU_WIDTH // LRU_BLOCKS
LRU_C = 8.0
CONV_WIDTH = 4

NSA_HEADS = 16
NSA_GROUPS = 4
NSA_HPG = NSA_HEADS // NSA_GROUPS
NSA_DH = D_MODEL // NSA_HEADS
CMP_BLOCK = 32
CMP_STRIDE = 16
SEL_BLOCK = 64
SEL_COUNT = 16
SEL_FORCE = 1e4
WINDOW = 512
NSA_WIN_ROWS = 256
NSA_SEL_ROWS = 512
NSA_Q_DIM = NSA_HEADS * NSA_DH
NSA_KV_DIM = NSA_GROUPS * NSA_DH

DENSE_FF = 5632
N_EXPERTS = 8
TOP_K = 2
EXPERT_FF = 7168

LANES = 128
SC_CORES = 2
SC_SUBCORES = 16
SC_WORKERS = SC_CORES * SC_SUBCORES
SC_GATHER_BYTES = 256 * 1024
VMEM_LIMIT = 60 * 1024 * 1024

EV_CQ = 0
EV_CKV = MLA_Q_LORA
EV_REC = MLA_Q_LORA + MLA_KV_LORA
EV_GATE = EV_REC + LRU_WIDTH
EV_PE = EV_GATE + LRU_WIDTH
EV_PACKED = EV_PE + LANES

OD_Q = 0
OD_KC = NSA_Q_DIM
OD_VC = OD_KC + NSA_KV_DIM
OD_KS = OD_VC + NSA_KV_DIM
OD_VS = OD_KS + NSA_KV_DIM
OD_KW = OD_VS + NSA_KV_DIM
OD_VW = OD_KW + NSA_KV_DIM
OD_G = OD_VW + NSA_KV_DIM
OD_PACKED = OD_G + NSA_GROUPS * LANES

MOE_TILE = 2688
MOE_ALIGN = 16
MOE_COPY = 384
MOE_PIECES = (1024, 1024, 512, 256, 128)
MOE_CHAIN = 512
MOE_WSPLIT = 1


def _cparams(sem):
    return pltpu.CompilerParams(dimension_semantics=sem, vmem_limit_bytes=VMEM_LIMIT)


def _rms(x, g):
    ms = jnp.mean(x * x, axis=-1, keepdims=True)
    return x * lax.rsqrt(ms + RMS_EPS) * g


def _sigmoid(x):
    return 1.0 / (1.0 + jnp.exp(-x))


def _gelu_tanh(x):
    return 0.5 * x * (1.0 + jnp.tanh(math.sqrt(2.0 / math.pi) * (x + 0.044715 * (x * x * x))))


def _dot(a, b):
    return jnp.dot(a, b, preferred_element_type=F32)


def _dot_nt(a, b):
    return lax.dot_general(a, b, (((1,), (1,)), ((), ())), preferred_element_type=F32)


def _norm_mm_kernel(x_ref, g_ref, w_ref, o_ref, xn_ref):
    @pl.when(pl.program_id(1) == 0)
    def _():
        xn_ref[...] = _rms(x_ref[...], g_ref[...]).astype(BF16)

    o_ref[...] = _dot(xn_ref[...], w_ref[...]).astype(o_ref.dtype)


def norm_matmul(x, g, w, *, tm, tn, out_dtype=F32):
    n, k = x.shape
    m = w.shape[1]
    assert n % tm == 0 and m % tn == 0
    return pl.pallas_call(
        _norm_mm_kernel,
        grid=(n // tm, m // tn),
        in_specs=[pl.BlockSpec((tm, k), lambda i, j: (i, 0)),
                  pl.BlockSpec((1, k), lambda i, j: (0, 0)),
                  pl.BlockSpec((k, tn), lambda i, j: (0, j))],
        out_specs=pl.BlockSpec((tm, tn), lambda i, j: (i, j)),
        out_shape=jax.ShapeDtypeStruct((n, m), out_dtype),
        scratch_shapes=[pltpu.VMEM((tm, k), BF16)],
        compiler_params=_cparams(("parallel", "arbitrary")),
    )(x, g.reshape(1, k), w)


def _mm_res_kernel(*refs, n_in):
    xs = refs[:n_in]
    ws = refs[n_in:2 * n_in]
    res_ref = refs[2 * n_in]
    o_ref = refs[2 * n_in + 1]
    acc = res_ref[...]
    for x_ref, w_ref in zip(xs, ws):
        acc = acc + _dot(x_ref[...], w_ref[...])
    o_ref[...] = acc


def matmul_residual(xs, w, res, *, tm, tn):
    n = res.shape[0]
    m = w.shape[1]
    n_in = len(xs)
    in_specs = [pl.BlockSpec((tm, x.shape[1]), lambda i, j: (i, 0)) for x in xs]
    row = 0
    for x in xs:
        kx = x.shape[1]
        assert row % kx == 0
        in_specs.append(pl.BlockSpec((kx, tn), lambda i, j, rb=row // kx: (rb, j)))
        row += kx
    assert row == w.shape[0]
    in_specs.append(pl.BlockSpec((tm, tn), lambda i, j: (i, j)))
    return pl.pallas_call(
        functools.partial(_mm_res_kernel, n_in=n_in),
        grid=(n // tm, m // tn),
        in_specs=in_specs,
        out_specs=pl.BlockSpec((tm, tn), lambda i, j: (i, j)),
        out_shape=jax.ShapeDtypeStruct((n, m), F32),
        compiler_params=_cparams(("parallel", "arbitrary")),
    )(*xs, *([w] * n_in), res)


def _rope64(x, c, s_lo, s_hi):
    return x * c + pltpu.roll(x, 96, 1) * s_lo + pltpu.roll(x, 32, 1) * s_hi


def _mla_up_kernel(u_ref, pe_ref, qg_ref, kvg_ref, wq_ref, wk_ref, wv_ref,
                   c_ref, slo_ref, shi_ref, q_ref, k_ref, v_ref):
    u = u_ref[...]
    c, s_lo, s_hi = c_ref[...], slo_ref[...], shi_ref[...]
    qn = _rms(u[:, EV_CQ:EV_CQ + MLA_Q_LORA], qg_ref[...]).astype(BF16)
    kvn = _rms(u[:, EV_CKV:EV_CKV + MLA_KV_LORA], kvg_ref[...]).astype(BF16)
    q = _dot(qn, wq_ref[...]) * ((MLA_NOPE + MLA_ROPE) ** -0.5 * LOG2E)
    kn = _dot(kvn, wk_ref[...])
    v_ref[...] = _dot(kvn, wv_ref[...]).astype(v_ref.dtype)
    kpe = _rope64(pe_ref[...], c, s_lo, s_hi).astype(k_ref.dtype)
    for h in range(MLA_HEADS):
        a = h * MLA_QK_PAD
        q_ref[:, a:a + LANES] = q[:, a:a + LANES].astype(q_ref.dtype)
        q_ref[:, a + LANES:a + 2 * LANES] = _rope64(q[:, a + LANES:a + 2 * LANES], c, s_lo, s_hi).astype(q_ref.dtype)
        k_ref[:, a:a + LANES] = kn[:, h * LANES:(h + 1) * LANES].astype(k_ref.dtype)
        k_ref[:, a + LANES:a + 2 * LANES] = kpe


def mla_up(u, q_norm, kv_norm, wq, wk, wv, rope_c, rope_slo, rope_shi, *, seq, tm):
    n = u.shape[0]
    hq = MLA_HEADS * MLA_QK_PAD
    hv = MLA_HEADS * MLA_V
    ab = EV_REC
    assert seq % tm == 0 and EV_PE % LANES == 0
    nsb = seq // tm
    row_spec = pl.BlockSpec((tm, LANES), lambda i: (i % nsb, 0))
    full = lambda a: pl.BlockSpec(a.shape, lambda i: (0,) * a.ndim)
    qg = q_norm.reshape(1, -1)
    kvg = kv_norm.reshape(1, -1)
    return pl.pallas_call(
        _mla_up_kernel,
        grid=(n // tm,),
        in_specs=[pl.BlockSpec((tm, ab), lambda i: (i, 0)),
                  pl.BlockSpec((tm, LANES), lambda i: (i, EV_PE // LANES)),
                  full(qg), full(kvg), full(wq), full(wk), full(wv),
                  row_spec, row_spec, row_spec],
        out_specs=[pl.BlockSpec((tm, hq), lambda i: (i, 0)),
                   pl.BlockSpec((tm, hq), lambda i: (i, 0)),
                   pl.BlockSpec((tm, hv), lambda i: (i, 0))],
        out_shape=[jax.ShapeDtypeStruct((n, hq), BF16),
                   jax.ShapeDtypeStruct((n, hq), BF16),
                   jax.ShapeDtypeStruct((n, hv), BF16)],
        compiler_params=_cparams(("parallel",)),
    )(u, u, qg, kvg, wq, wk, wv, rope_c, rope_slo, rope_shi)


def _values_and_ones(v):
    return jnp.concatenate([v.astype(BF16), jnp.ones(v.shape, BF16)], axis=1)


def _lane_tile(x, n):
    return jnp.concatenate([x] * n, axis=1)


def _softmax_step(s, v1, m_ref, acc_ref):
    m_prev = m_ref[...]
    m_new = jnp.maximum(m_prev, jnp.max(s, axis=-1, keepdims=True))
    alpha = jnp.exp2(m_prev - m_new)
    p = jnp.exp2(s - _lane_tile(m_new, s.shape[1] // LANES))
    acc_ref[...] = _lane_tile(alpha, 2) * acc_ref[...] + _dot(p.astype(BF16), v1)
    m_ref[...] = m_new


def _softmax_result(acc):
    return acc[:, :LANES] / acc[:, LANES:]


def _mla_attn_kernel(q_ref, k_ref, v_ref, cb_ref, o_ref, v1, m_ref, acc_ref, *, t, n_chunks):
    qi = pl.program_id(2)

    hs = MLA_HEADS_PER_STEP

    @pl.when(qi == 0)
    def _():
        for h in range(hs):
            v1[h] = _values_and_ones(v_ref[0, :, h * MLA_V:(h + 1) * MLA_V])

    m_ref[...] = jnp.full(m_ref.shape, NEG_INF, F32)
    acc_ref[...] = jnp.zeros(acc_ref.shape, F32)
    rg = t // MLA_ROW_GROUPS

    def chunk(c, masked):
        for h in range(hs):
            for i in range(MLA_ROW_GROUPS):
                rows = pl.ds(i * rg, rg)
                qh = q_ref[0, i * rg:(i + 1) * rg, h * MLA_QK_PAD:(h + 1) * MLA_QK_PAD]
                s = _dot_nt(qh, k_ref[0, c * t:(c + 1) * t, h * MLA_QK_PAD:(h + 1) * MLA_QK_PAD])
                if masked:
                    s = s + cb_ref[i * rg:(i + 1) * rg, :]
                _softmax_step(s, v1[h, c * t:(c + 1) * t, :], m_ref.at[h, rows], acc_ref.at[h, rows])

    for c in range(n_chunks):
        pl.when(c < qi)(functools.partial(chunk, c, False))
        pl.when(c == qi)(functools.partial(chunk, c, True))
    for h in range(hs):
        o_ref[0, :, h * MLA_V:(h + 1) * MLA_V] = _softmax_result(acc_ref[h]).astype(o_ref.dtype)


def mla_attention(q, k, v, *, t):
    b, s, _ = q.shape
    causal = jnp.asarray(np.where(np.arange(t)[None, :] <= np.arange(t)[:, None], 0.0, NEG_INF), F32)
    kern = functools.partial(_mla_attn_kernel, t=t, n_chunks=s // t)
    hs = MLA_HEADS_PER_STEP
    return pl.pallas_call(
        kern,
        grid=(b, MLA_HEADS // hs, s // t),
        in_specs=[pl.BlockSpec((1, t, hs * MLA_QK_PAD), lambda b_, h, i: (b_, i, h)),
                  pl.BlockSpec((1, s, hs * MLA_QK_PAD), lambda b_, h, i: (b_, 0, h)),
                  pl.BlockSpec((1, s, hs * MLA_V), lambda b_, h, i: (b_, 0, h)),
                  pl.BlockSpec((t, t), lambda b_, h, i: (0, 0))],
        out_specs=pl.BlockSpec((1, t, hs * MLA_V), lambda b_, h, i: (b_, i, h)),
        out_shape=jax.ShapeDtypeStruct((b, s, MLA_HEADS * MLA_V), BF16),
        scratch_shapes=[pltpu.VMEM((hs, s, 2 * LANES), BF16), pltpu.VMEM((hs, t, LANES), F32),
                        pltpu.VMEM((hs, t, 2 * LANES), F32)],
        compiler_params=_cparams(("parallel", "parallel", "arbitrary")),
    )(q, k, v, causal)


def _rglru_kernel(x_ref, y_ref, cw_ref, cb_ref, gaw_ref, gab_ref, gxw_ref, gxb_ref, lam_ref,
                  o_ref, xbuf, h_ref, *, ts):
    t = pl.program_id(2)

    @pl.when(t == 0)
    def _():
        xbuf[0:8, :] = jnp.zeros((8, LANES), F32)
        h_ref[...] = jnp.zeros(h_ref.shape, F32)

    x = x_ref[0]
    xbuf[8:, :] = x
    cw = cw_ref[...]
    xc = cb_ref[...] + cw[3:4] * x
    for kk in range(CONV_WIDTH - 1):
        back = CONV_WIDTH - 1 - kk
        xc = xc + cw[kk:kk + 1] * xbuf[8 - back:8 - back + ts, :]
    xbuf[0:8, :] = x[ts - 8:, :]

    xcb = xc.astype(BF16)
    r = _sigmoid(_dot(xcb, gaw_ref[0]) + gab_ref[0])
    gi = _sigmoid(_dot(xcb, gxw_ref[0]) + gxb_ref[0])
    z = -lam_ref[...]
    softplus = jnp.maximum(z, 0.0) + jnp.log1p(jnp.exp(-jnp.abs(z)))
    log_a = (-LRU_C) * r * softplus
    a = jnp.exp(log_a)
    mult = jnp.sqrt(-jnp.tanh(log_a) * (a * a + 1.0))
    row = lax.broadcasted_iota(jnp.int32, (ts, 1), 0)
    mult = jnp.where(row + t * ts == 0, 1.0, mult)
    bv = mult * gi * xc

    d = 1
    while d < ts:
        keep = row >= d
        a_sh = jnp.where(keep, pltpu.roll(a, d, 0), 1.0)
        b_sh = jnp.where(keep, pltpu.roll(bv, d, 0), 0.0)
        bv = a * b_sh + bv
        a = a * a_sh
        d *= 2
    h = bv + a * h_ref[...]
    h_ref[...] = h[ts - 1:ts, :]
    o_ref[0] = (h * _gelu_tanh(y_ref[0])).astype(o_ref.dtype)


def rglru(u3, conv_w, conv_b, ga_w, ga_b, gx_w, gx_b, lam, *, ts):
    b, s, _ = u3.shape
    rec0 = EV_REC // LANES
    gate0 = EV_GATE // LANES
    cb = conv_b.reshape(1, LRU_WIDTH)
    gab = ga_b.reshape(LRU_BLOCKS, 1, LRU_BLOCK_W)
    gxb = gx_b.reshape(LRU_BLOCKS, 1, LRU_BLOCK_W)
    lam2 = lam.reshape(1, LRU_WIDTH)
    blk_w = pl.BlockSpec((1, LRU_BLOCK_W, LRU_BLOCK_W), lambda b_, n, t: (n, 0, 0))
    blk_b = pl.BlockSpec((1, 1, LRU_BLOCK_W), lambda b_, n, t: (n, 0, 0))
    vec = pl.BlockSpec((1, LANES), lambda b_, n, t: (0, n))
    return pl.pallas_call(
        functools.partial(_rglru_kernel, ts=ts),
        grid=(b, LRU_BLOCKS, s // ts),
        in_specs=[pl.BlockSpec((1, ts, LANES), lambda b_, n, t: (b_, t, rec0 + n)),
                  pl.BlockSpec((1, ts, LANES), lambda b_, n, t: (b_, t, gate0 + n)),
                  pl.BlockSpec((CONV_WIDTH, LANES), lambda b_, n, t: (0, n)),
                  vec, blk_w, blk_b, blk_w, blk_b, vec],
        out_specs=pl.BlockSpec((1, ts, LANES), lambda b_, n, t: (b_, t, n)),
        out_shape=jax.ShapeDtypeStruct((b, s, LRU_WIDTH), BF16),
        scratch_shapes=[pltpu.VMEM((ts + 8, LANES), F32), pltpu.VMEM((1, LANES), F32)],
        compiler_params=_cparams(("parallel", "parallel", "arbitrary")),
    )(u3, u3, conv_w, cb, ga_w, gab, gx_w, gxb, lam2)


def _ffn_kernel(x_ref, g_ref, w1_ref, w3_ref, w2_ref, o_ref, xn_ref):
    f = pl.program_id(1)

    @pl.when(f == 0)
    def _():
        x = x_ref[...]
        xn_ref[...] = _rms(x, g_ref[...]).astype(BF16)
        o_ref[...] = x

    xn = xn_ref[...]
    h1 = _dot(xn, w1_ref[...])
    h3 = _dot(xn, w3_ref[...])
    hh = (h1 * _sigmoid(h1) * h3).astype(BF16)
    o_ref[...] += _dot(hh, w2_ref[...])


def ffn_dense(x, g, w1, w3, w2, *, tm, tf):
    n, d = x.shape
    ff = w1.shape[1]
    assert n % tm == 0 and ff % tf == 0
    return pl.pallas_call(
        _ffn_kernel,
        grid=(n // tm, ff // tf),
        in_specs=[pl.BlockSpec((tm, d), lambda i, f: (i, 0)),
                  pl.BlockSpec((1, d), lambda i, f: (0, 0)),
                  pl.BlockSpec((d, tf), lambda i, f: (0, f)),
                  pl.BlockSpec((d, tf), lambda i, f: (0, f)),
                  pl.BlockSpec((tf, d), lambda i, f: (f, 0))],
        out_specs=pl.BlockSpec((tm, d), lambda i, f: (i, 0)),
        out_shape=jax.ShapeDtypeStruct((n, d), F32),
        scratch_shapes=[pltpu.VMEM((tm, d), BF16)],
        compiler_params=_cparams(("parallel", "arbitrary")),
    )(x, g.reshape(1, d), w1, w3, w2)


def _compress_kernel(*refs, n_half):
    kc_refs = refs[:NSA_GROUPS]
    pe_ref, w1_ref, b1_ref, w2_ref, o_ref = refs[NSA_GROUPS:]
    pe = pe_ref[...]
    half_k = n_half * NSA_DH
    nchunk = o_ref.shape[1]
    for g in range(NSA_GROUPS):
        lo, hi = [], []
        for l in range(n_half):
            piece = kc_refs[g][0, pl.ds(l, nchunk, stride=n_half), :]
            lo.append((piece + pe[l:l + 1]).astype(BF16))
            hi.append((piece + pe[n_half + l:n_half + l + 1]).astype(BF16))
        z0 = _dot(jnp.concatenate(lo, axis=1), w1_ref[0:half_k, :])
        z1 = _dot(jnp.concatenate(hi, axis=1), w1_ref[half_k:2 * half_k, :])
        rows = z1.shape[0]
        pre = z0 + pltpu.roll(z1, rows - 1, 0) + b1_ref[...]
        o_ref[0, :, g * NSA_DH:(g + 1) * NSA_DH] = _dot(_gelu_tanh(pre).astype(BF16), w2_ref[...]).astype(o_ref.dtype)


def nsa_compress(u3, col, pe, w1, b1, w2):
    b, s, _ = u3.shape
    nchunk = s // CMP_STRIDE
    assert CMP_BLOCK == 2 * CMP_STRIDE and col % NSA_KV_DIM == 0
    full = lambda a: pl.BlockSpec(a.shape, lambda i: (0,) * a.ndim)
    b1r = b1.reshape(1, NSA_DH)
    return pl.pallas_call(
        functools.partial(_compress_kernel, n_half=CMP_STRIDE),
        grid=(b,),
        in_specs=[pl.BlockSpec((1, s, NSA_DH), lambda i, g=g: (i, 0, col // NSA_DH + g)) for g in range(NSA_GROUPS)]
        + [full(pe), full(w1), full(b1r), full(w2)],
        out_specs=pl.BlockSpec((1, nchunk, NSA_KV_DIM), lambda i: (i, 0, 0)),
        out_shape=jax.ShapeDtypeStruct((b, nchunk, NSA_KV_DIM), BF16),
        compiler_params=_cparams(("parallel",)),
    )(*([u3] * NSA_GROUPS), pe, w1, b1r, w2)


def _rope128(x, c, s):
    return x * c + pltpu.roll(x, NSA_DH // 2, 1) * s


def _pack_bf16_pairs(x):
    w = x.shape[1] // 2
    bits = pltpu.bitcast(x.astype(BF16).astype(F32), jnp.uint32)
    word = bits[:, w:] | lax.shift_right_logical(bits[:, :w], jnp.uint32(16))
    return pltpu.bitcast(word, F32)


def _unpack_bf16_pairs(word):
    bits = pltpu.bitcast(word, jnp.uint32)
    lo = pltpu.bitcast(lax.shift_left(bits, jnp.uint32(16)), F32).astype(BF16)
    hi = pltpu.bitcast(bits & jnp.uint32(0xFFFF0000), F32).astype(BF16)
    return lo, hi


def _split3(x):
    hi = x.astype(BF16)
    r1 = x - hi.astype(F32)
    mid = r1.astype(BF16)
    lo = (r1 - mid.astype(F32)).astype(BF16)
    return hi, mid, lo


def _nsa_attn_kernel(q_ref, ks_ref, vs_ref, kw_ref, vw_ref, gt_ref, kc_ref, vc_ref,
                     cq_ref, sq_ref, ck_ref, sk_ref, ovl_ref, exp_ref, wb_ref,
                     o_ref, ksr, vsb, kwr, vwb, selb, m_ref, acc_ref,
                     *, tq, tk, seq, scale):
    qi = pl.program_id(2)
    hp = NSA_HPG
    n_sel = seq // SEL_BLOCK

    @pl.when(qi == 0)
    def _():
        ck, sk = ck_ref[...], sk_ref[...]
        ksr[...] = _rope128(ks_ref[0], ck, sk).astype(BF16)
        kwr[...] = _rope128(kw_ref[0], ck, sk).astype(BF16)
        vsb[...] = _values_and_ones(vs_ref[0])
        vwb[...] = _values_and_ones(vw_ref[0])

    q = q_ref[0] * (scale * LOG2E)
    cq, sq = cq_ref[...], sq_ref[...]
    heads = [q[:, p * NSA_DH:(p + 1) * NSA_DH] for p in range(hp)]
    qu = jnp.concatenate(heads, axis=0).astype(BF16)
    qr = jnp.concatenate([_rope128(h, cq, sq) for h in heads], axis=0).astype(BF16)
    t_row = qi * tq + lax.broadcasted_iota(jnp.int32, (tq, 1), 0)
    lane = lax.broadcasted_iota(jnp.int32, (1, LANES), 1)

    sc = _dot_nt(qu, kc_ref[0]).reshape(hp, tq, LANES)
    valid = (lane * CMP_STRIDE + (CMP_BLOCK - 1) <= t_row)[None]
    sm = jnp.where(valid, sc, NEG_INF)
    e = jnp.exp2(sm - jnp.max(sm, axis=-1, keepdims=True))
    p = jnp.where(valid, e / jnp.sum(e, axis=-1, keepdims=True), 0.0)
    o_cmp = _dot(p.reshape(hp * tq, LANES).astype(BF16), vc_ref[0])

    psum = p[0]
    for i in range(1, hp):
        psum = psum + p[i]
    ovl = ovl_ref[...]
    imp = sum(_dot(part, ovl) for part in _split3(psum))
    cur = jnp.right_shift(t_row, SEL_BLOCK.bit_length() - 1)
    future = lane > cur
    forced = (lane == 0) | (lane == cur) | (lane == cur - 1)
    score = jnp.where(future, -1.0, jnp.where(forced, SEL_FORCE, imp))
    sc_t = score.T[0:n_sel, :]
    blk = lax.broadcasted_iota(jnp.int32, (n_sel, 1), 0)
    cnt = jnp.zeros((n_sel, tq), F32)
    for j in range(n_sel):
        other = sc_t[j:j + 1, :]
        beats = (other > sc_t) | ((other == sc_t) & (blk > j))
        cnt = cnt + jnp.where(beats, 1.0, 0.0)
    sel_t = jnp.where(cnt < SEL_COUNT, 1.0, 0.0)
    sel = jnp.concatenate([sel_t, jnp.zeros((LANES - n_sel, tq), F32)], axis=0).T.astype(BF16)
    picked = _dot(sel, exp_ref[...])
    kall = lax.broadcasted_iota(jnp.int32, (1, seq), 1)
    selb[...] = jnp.where((picked > 0.5) & (kall <= t_row), 0.0, NEG_INF)

    m_ref[...] = jnp.full(m_ref.shape, NEG_INF, F32)
    acc_ref[...] = jnp.zeros(acc_ref.shape, F32)
    for c in range(seq // tk):
        @pl.when(c * tk <= qi * tq + (tq - 1))
        def _():
            for i in range(hp):
                for r0 in range(0, tq, NSA_SEL_ROWS):
                    rows = pl.ds(i * tq + r0, NSA_SEL_ROWS)
                    s = _dot_nt(qr[i * tq + r0:i * tq + r0 + NSA_SEL_ROWS], ksr[c * tk:(c + 1) * tk, :])
                    _softmax_step(s + selb[r0:r0 + NSA_SEL_ROWS, c * tk:(c + 1) * tk], vsb[c * tk:(c + 1) * tk, :],
                                  m_ref.at[rows], acc_ref.at[rows])
    o_sel = _softmax_result(acc_ref[...])

    span = WINDOW + NSA_WIN_ROWS
    wins = []
    for i in range(hp):
        for r in range(tq // NSA_WIN_ROWS):
            group = qi * (tq // NSA_WIN_ROWS) + r
            start = pl.multiple_of(jnp.maximum(group * NSA_WIN_ROWS - WINDOW, 0), NSA_WIN_ROWS)
            bias = wb_ref[jnp.minimum(group, WINDOW // NSA_WIN_ROWS)]
            r0 = i * tq + r * NSA_WIN_ROWS
            sw = _dot_nt(qr[r0:r0 + NSA_WIN_ROWS], kwr[pl.ds(start, span), :]) + bias
            ew = jnp.exp2(sw - jnp.max(sw, axis=-1, keepdims=True)).astype(BF16)
            wins.append(_softmax_result(_dot(ew, vwb[pl.ds(start, span), :])))
    o_win = jnp.concatenate(wins, axis=0)

    gates = _sigmoid(gt_ref[0])
    for i in range(hp):
        rows = slice(i * tq, (i + 1) * tq)
        o = (gates[:, 3 * i:3 * i + 1] * o_cmp[rows] + gates[:, 3 * i + 1:3 * i + 2] * o_sel[rows]
             + gates[:, 3 * i + 2:3 * i + 3] * o_win[rows])
        o_ref[0, :, i * NSA_DH:(i + 1) * NSA_DH] = o.astype(o_ref.dtype)


def _window_bias(tq):
    span = WINDOW + tq
    out = []
    for qi in range(WINDOW // tq + 1):
        start = max(qi * tq - WINDOW, 0)
        t = qi * tq + np.arange(tq)[:, None]
        kpos = start + np.arange(span)[None, :]
        out.append(np.where((kpos <= t) & (kpos > t - WINDOW), 0.0, NEG_INF))
    return jnp.asarray(np.stack(out), F32)


def nsa_attention(u3, k_cmp, v_cmp, rope_c, rope_s, ovl, expand, *, tq, tk):
    b, s, _ = u3.shape
    assert s // CMP_STRIDE == LANES and tq % SEL_BLOCK == 0 and tq % NSA_WIN_ROWS == 0 and WINDOW % NSA_WIN_ROWS == 0
    hp = NSA_HPG
    wbias = _window_bias(NSA_WIN_ROWS)
    col = lambda off: (lambda b_, g, i: (b_, 0, off // NSA_DH + g))
    seq_blk = lambda off: pl.BlockSpec((1, s, NSA_DH), col(off))
    full = lambda a: pl.BlockSpec(a.shape, lambda b_, g, i: (0,) * a.ndim)
    cmp_blk = pl.BlockSpec((1, LANES, NSA_DH), lambda b_, g, i: (b_, 0, g))
    rope_q = pl.BlockSpec((tq, NSA_DH), lambda b_, g, i: (i, 0))
    kern = functools.partial(_nsa_attn_kernel, tq=tq, tk=tk, seq=s, scale=NSA_DH ** -0.5)
    return pl.pallas_call(
        kern,
        grid=(b, NSA_GROUPS, s // tq),
        in_specs=[pl.BlockSpec((1, tq, hp * NSA_DH), lambda b_, g, i: (b_, i, g)),
                  seq_blk(OD_KS), seq_blk(OD_VS), seq_blk(OD_KW), seq_blk(OD_VW),
                  pl.BlockSpec((1, tq, LANES), lambda b_, g, i: (b_, i, OD_G // LANES + g)),
                  cmp_blk, cmp_blk, rope_q, rope_q, full(rope_c), full(rope_s), full(ovl), full(expand),
                  full(wbias)],
        out_specs=pl.BlockSpec((1, tq, hp * NSA_DH), lambda b_, g, i: (b_, i, g)),
        out_shape=jax.ShapeDtypeStruct((b, s, NSA_Q_DIM), BF16),
        scratch_shapes=[pltpu.VMEM((s, NSA_DH), BF16), pltpu.VMEM((s, 2 * LANES), BF16)] * 2 + [
            pltpu.VMEM((tq, s), F32),
            pltpu.VMEM((hp * tq, LANES), F32), pltpu.VMEM((hp * tq, 2 * LANES), F32)],
        compiler_params=_cparams(("parallel", "parallel", "arbitrary")),
    )(u3, u3, u3, u3, u3, u3, k_cmp, v_cmp, rope_c, rope_s, rope_c, rope_s, ovl, expand, wbias)


def _router_kernel(x_ref, g_ref, wr_ref, br_ref, xn_ref, meta_ref, cnt_ref, *, tm):
    @pl.when(pl.program_id(0) == 0)
    def _():
        cnt_ref[...] = jnp.zeros(cnt_ref.shape, F32)

    xn = _rms(x_ref[...], g_ref[...])
    xn_ref[...] = _pack_bf16_pairs(xn)
    xh, xm, _ = _split3(xn)
    wh, wm, _ = _split3(wr_ref[...])
    logits = _dot(xh, wh) + _dot(xh, wm) + _dot(xm, wh) + br_ref[...]
    lane = lax.broadcasted_iota(jnp.int32, (1, LANES), 1).astype(F32)
    lg = jnp.where(lane < N_EXPERTS, logits, NEG_INF)
    m1 = jnp.max(lg, axis=-1, keepdims=True)
    e1 = jnp.min(jnp.where(lg == m1, lane, float(LANES)), axis=-1, keepdims=True)
    lg2 = jnp.where(lane == e1, NEG_INF, lg)
    m2 = jnp.max(lg2, axis=-1, keepdims=True)
    e2 = jnp.min(jnp.where(lg2 == m2, lane, float(LANES)), axis=-1, keepdims=True)
    ex = jnp.exp(m2 - m1)
    den = 1.0 + ex
    g1 = 1.0 / den
    g2 = ex / den
    oh = jnp.where((lane == e1) | (lane == e2), 1.0, 0.0)
    r = lax.broadcasted_iota(jnp.int32, (tm, tm), 0)
    c = lax.broadcasted_iota(jnp.int32, (tm, tm), 1)
    tri = jnp.where(r > c, 1.0, 0.0).astype(BF16)
    cum = _dot(tri, oh.astype(BF16)) + cnt_ref[0:1, :]
    pos1 = jnp.sum(jnp.where(lane == e1, cum, 0.0), axis=-1, keepdims=True)
    pos2 = jnp.sum(jnp.where(lane == e2, cum, 0.0), axis=-1, keepdims=True)
    cnt_ref[...] = cnt_ref[...] + jnp.sum(oh, axis=0, keepdims=True)
    meta = jnp.where(lane == 0, e1, 0.0)
    meta = jnp.where(lane == 1, e2, meta)
    meta = jnp.where(lane == 2, g1, meta)
    meta = jnp.where(lane == 3, g2, meta)
    meta = jnp.where(lane == 4, pos1, meta)
    meta = jnp.where(lane == 5, pos2, meta)
    meta_ref[...] = meta


def moe_router(x, g, wr, br, *, tm):
    n, d = x.shape
    return pl.pallas_call(
        functools.partial(_router_kernel, tm=tm),
        grid=(n // tm,),
        in_specs=[pl.BlockSpec((tm, d), lambda i: (i, 0)),
                  pl.BlockSpec((1, d), lambda i: (0, 0)),
                  pl.BlockSpec((d, LANES), lambda i: (0, 0)),
                  pl.BlockSpec((1, LANES), lambda i: (0, 0))],
        out_specs=[pl.BlockSpec((tm, d // 2), lambda i: (i, 0)),
                   pl.BlockSpec((tm, LANES), lambda i: (i, 0)),
                   pl.BlockSpec((8, LANES), lambda i: (0, 0))],
        out_shape=[jax.ShapeDtypeStruct((n, d // 2), F32),
                   jax.ShapeDtypeStruct((n, LANES), F32),
                   jax.ShapeDtypeStruct((8, LANES), F32)],
        compiler_params=_cparams(("arbitrary",)),
    )(x, g.reshape(1, d), wr, br)


def _gather_chunk(per_worker, row_bytes):
    best = 0
    for c in range(8, per_worker + 1, 8):
        if per_worker % c == 0 and c * row_bytes <= SC_GATHER_BYTES and c <= LANES:
            best = c
    assert best > 0, (per_worker, row_bytes)
    return best


def sc_gather_rows(table, idx):
    _, d = table.shape
    b = idx.shape[0]
    assert b % (8 * SC_WORKERS) == 0 and table.dtype.itemsize == 4
    per_w = b // SC_WORKERS
    chunk = _gather_chunk(per_w, d * 4)
    mesh = plsc.VectorSubcoreMesh(core_axis_name="c", subcore_axis_name="s",
                                  num_cores=SC_CORES, num_subcores=SC_SUBCORES)

    @functools.partial(
        pl.kernel, mesh=mesh,
        out_type=jax.ShapeDtypeStruct((b, d), table.dtype),
        scratch_types=[pltpu.VMEM((chunk,), jnp.int32), pltpu.VMEM((chunk, d), table.dtype),
                       pltpu.SemaphoreType.DMA])
    def gather(table_hbm, idx_hbm, out_hbm, idx_v, rows_v, sem):
        wid = lax.axis_index("s") * SC_CORES + lax.axis_index("c")
        base = wid * per_w

        @pl.loop(0, per_w // chunk)
        def _(c):
            off = pl.multiple_of(base + c * chunk, 8)
            pltpu.sync_copy(idx_hbm.at[pl.ds(off, chunk)], idx_v)
            pltpu.async_copy(table_hbm.at[idx_v], rows_v, sem).wait()
            pltpu.sync_copy(rows_v, out_hbm.at[pl.ds(off, chunk)])

    return gather(table, idx)


def sc_scatter_rows(src, dests, n_out):
    n, d = src.shape
    assert n % (8 * SC_WORKERS) == 0 and src.dtype.itemsize == 4
    per_w = n // SC_WORKERS
    chunk = _gather_chunk(per_w, d * 4)
    n_dest = len(dests)
    mesh = plsc.VectorSubcoreMesh(core_axis_name="c", subcore_axis_name="s",
                                  num_cores=SC_CORES, num_subcores=SC_SUBCORES)

    @functools.partial(
        pl.kernel, mesh=mesh,
        out_type=jax.ShapeDtypeStruct((n_out, d), src.dtype),
        scratch_types=[pltpu.VMEM((chunk,), jnp.int32), pltpu.VMEM((chunk, d), src.dtype),
                       pltpu.SemaphoreType.DMA])
    def scatter(src_hbm, *rest):
        dest_hbms = rest[:n_dest]
        out_hbm, idx_v, rows_v, sem = rest[n_dest:]
        wid = lax.axis_index("s") * SC_CORES + lax.axis_index("c")
        base = wid * per_w

        @pl.loop(0, per_w // chunk)
        def _(c):
            off = pl.multiple_of(base + c * chunk, 8)
            pltpu.sync_copy(src_hbm.at[pl.ds(off, chunk)], rows_v)
            for dest_hbm in dest_hbms:
                pltpu.sync_copy(dest_hbm.at[pl.ds(off, chunk)], idx_v)
                pltpu.async_copy(rows_v, out_hbm.at[idx_v], sem).wait()

    return scatter(src, *dests)


def _expert_kernel(te_ref, ts_ref, tr_ref, xp_hbm, *rest, nf):
    ns = MOE_WSPLIT
    w1_refs, w3_refs, w2_refs = rest[:ns], rest[ns:2 * ns], rest[2 * ns:3 * ns]
    yr_hbm, xb, acc, stage, in_sem, out_sem = rest[3 * ns:]
    t = pl.program_id(0)
    f = pl.program_id(1)
    rows = tr_ref[t]
    start = pl.multiple_of(ts_ref[t], MOE_ALIGN)
    n_chunks = MOE_TILE // MOE_COPY
    half = xb.shape[1] // 2

    def in_copy(ci):
        return pltpu.make_async_copy(xp_hbm.at[pl.ds(start + ci * MOE_COPY, MOE_COPY)],
                                     stage.at[ci % 2], in_sem.at[ci % 2])

    def out_copy(ci, first_row):
        return pltpu.make_async_copy(acc.at[pl.ds(ci * MOE_COPY, MOE_COPY)],
                                     yr_hbm.at[pl.ds(first_row + ci * MOE_COPY, MOE_COPY)], out_sem.at[0])

    def when_chunk_live(ci, fn):
        pl.when(ci * MOE_COPY < rows)(fn)

    def wait_outputs(tile):
        tile_rows = tr_ref[tile]
        first_row = pl.multiple_of(ts_ref[tile], MOE_ALIGN)
        for ci in range(n_chunks):
            pl.when(ci * MOE_COPY < tile_rows)(lambda ci=ci: out_copy(ci, first_row).wait())

    @pl.when((f == 0) & (rows > 0))
    def _():
        def unpack(ci):
            in_copy(ci).wait()
            lo, hi = _unpack_bf16_pairs(stage[ci % 2])
            xb[ci * MOE_COPY:(ci + 1) * MOE_COPY, :half] = lo
            xb[ci * MOE_COPY:(ci + 1) * MOE_COPY, half:] = hi

        when_chunk_live(0, lambda: in_copy(0).start())
        for ci in range(n_chunks):
            if ci + 1 < n_chunks:
                when_chunk_live(ci + 1, lambda ci=ci: in_copy(ci + 1).start())
            when_chunk_live(ci, lambda ci=ci: unpack(ci))

    @pl.when((f == 0) & (t > 0))
    def _():
        wait_outputs(jnp.maximum(t - 1, 0))

    @pl.when((f == 0) & (rows > 0))
    def _():
        acc[...] = jnp.zeros(acc.shape, F32)

    @pl.when(rows > 0)
    def _():
        w1 = jnp.concatenate([r[0, 0].astype(BF16) for r in w1_refs], axis=0)
        w3 = jnp.concatenate([r[0, 0].astype(BF16) for r in w3_refs], axis=0)
        w2 = jnp.concatenate([r[0, 0].astype(BF16) for r in w2_refs], axis=1)

        def chain(r0, size):
            rs = pl.ds(pl.multiple_of(r0, MOE_PIECES[-1]), size)
            xc = xb[rs, :]
            h1 = _dot(xc, w1)
            h3 = _dot(xc, w3)
            hh = (h1 * _sigmoid(h1) * h3).astype(BF16)
            acc[rs, :] += _dot(hh, w2)

        todo = (rows + (MOE_PIECES[-1] - 1)) // MOE_PIECES[-1] * MOE_PIECES[-1]
        off = jnp.int32(0)
        for size in MOE_PIECES:
            take = todo - off >= size

            @pl.when(take)
            def _(off=off, size=size):
                for c0 in range(0, size, MOE_CHAIN):
                    chain(off + c0, min(MOE_CHAIN, size))

            off = off + jnp.where(take, size, 0)

    @pl.when((f == nf - 1) & (rows > 0))
    def _():
        for ci in range(n_chunks):
            when_chunk_live(ci, lambda ci=ci: out_copy(ci, start).start())

    @pl.when((f == nf - 1) & (t == pl.num_programs(0) - 1))
    def _():
        wait_outputs(t)


def moe_experts(xp, tile_e, tile_start, tile_rows, w1, w3, w2, layer, *, tf):
    n_rows, half = xp.shape
    d = 2 * half
    n_tiles = tile_e.shape[0]
    ff = w1.shape[3]
    nf = ff // tf
    assert MOE_TILE % MOE_COPY == 0 and sum(MOE_PIECES) >= MOE_TILE

    def f_eff(t, f, tr):
        return jnp.where(tr[t] > 0, f, nf - 1)

    ns = MOE_WSPLIT
    up_specs = [pl.BlockSpec((1, 1, d // ns, tf), lambda t, f, te, ts, tr, k=k: (layer, te[t], k, f_eff(t, f, tr)))
                for k in range(ns)]
    down_specs = [pl.BlockSpec((1, 1, tf, d // ns), lambda t, f, te, ts, tr, k=k: (layer, te[t], f_eff(t, f, tr), k))
                  for k in range(ns)]
    grid_spec = pltpu.PrefetchScalarGridSpec(
        num_scalar_prefetch=3,
        grid=(n_tiles, nf),
        in_specs=[pl.BlockSpec(memory_space=pl.ANY)] + 2 * up_specs + down_specs,
        out_specs=pl.BlockSpec(memory_space=pl.ANY),
        scratch_shapes=[pltpu.VMEM((MOE_TILE, d), BF16), pltpu.VMEM((MOE_TILE, d), F32),
                        pltpu.VMEM((2, MOE_COPY, half), F32),
                        pltpu.SemaphoreType.DMA((2,)), pltpu.SemaphoreType.DMA((1,))],
    )
    return pl.pallas_call(
        functools.partial(_expert_kernel, nf=nf),
        grid_spec=grid_spec,
        out_shape=jax.ShapeDtypeStruct((n_rows, d), F32),
        compiler_params=_cparams(("arbitrary", "arbitrary")),
    )(tile_e, tile_start, tile_rows, xp, *([w1] * ns), *([w3] * ns), *([w2] * ns))


def _combine_kernel(x_ref, y1_ref, y2_ref, meta_ref, g_ref, o_ref, *, final_norm):
    meta = meta_ref[...]
    y = x_ref[...] + (meta[:, 2:3] * y1_ref[...] + meta[:, 3:4] * y2_ref[...])
    if final_norm:
        y = _rms(y, g_ref[...])
    o_ref[...] = y


def moe_combine(x, yg, meta, g, *, final_norm, tm):
    n, d = x.shape
    nb = n // tm
    row = pl.BlockSpec((tm, d), lambda i: (i, 0))
    return pl.pallas_call(
        functools.partial(_combine_kernel, final_norm=final_norm),
        grid=(nb,),
        in_specs=[row, row, pl.BlockSpec((tm, d), lambda i: (i + nb, 0)),
                  pl.BlockSpec((tm, LANES), lambda i: (i, 0)),
                  pl.BlockSpec((1, d), lambda i: (0, 0))],
        out_specs=row,
        out_shape=jax.ShapeDtypeStruct((n, d), F32),
        compiler_params=_cparams(("parallel",)),
    )(x, yg, yg, meta, g.reshape(1, d))


def _mla_rope_tables(seq):
    half = MLA_ROPE // 2
    pos = jnp.arange(seq, dtype=F32)
    inv = ROPE_THETA ** (-jnp.arange(0, MLA_ROPE, 2, dtype=F32) / MLA_ROPE)
    ang = pos[:, None] * inv[None, :]
    cos, sin = jnp.cos(ang), jnp.sin(ang)
    z = jnp.zeros((seq, half), F32)
    pad = jnp.zeros((seq, LANES - MLA_ROPE), F32)
    c = jnp.concatenate([cos, cos, pad], axis=1)
    s_lo = jnp.concatenate([-sin, z, pad], axis=1)
    s_hi = jnp.concatenate([z, sin, pad], axis=1)
    return c, s_lo, s_hi


def _nsa_rope_tables(seq):
    pos = jnp.arange(seq, dtype=F32)
    inv = ROPE_THETA ** (-jnp.arange(0, NSA_DH, 2, dtype=F32) / NSA_DH)
    ang = pos[:, None] * inv[None, :]
    cos, sin = jnp.cos(ang), jnp.sin(ang)
    return jnp.concatenate([cos, cos], axis=1), jnp.concatenate([-sin, sin], axis=1)


def _selection_constants(seq):
    n_sel = seq // SEL_BLOCK
    nc = (seq - CMP_BLOCK) // CMP_STRIDE + 1
    cmp_start = np.arange(LANES) * CMP_STRIDE
    sel_start = np.arange(LANES) * SEL_BLOCK
    ovl = ((cmp_start[:, None] < sel_start[None, :] + SEL_BLOCK) &
           (cmp_start[:, None] + CMP_BLOCK > sel_start[None, :]))
    ovl &= (np.arange(LANES)[:, None] < nc) & (np.arange(LANES)[None, :] < n_sel)
    expand = (np.arange(seq)[None, :] // SEL_BLOCK == np.arange(LANES)[:, None])
    return jnp.asarray(ovl, BF16), jnp.asarray(expand, BF16)


def even_layer(x, seq, p):
    n = x.shape[0]
    b = n // seq
    (norm_mix, w_in, q_norm, w_q_up, kv_norm, w_kv_up, conv_w, conv_b, ga_w, ga_b, gx_w, gx_b,
     lam, w_out, norm_ffn, w1, w3, w2) = p
    d = D_MODEL
    o1 = MLA_Q_LORA + MLA_KV_LORA
    o2 = o1 + MLA_ROPE
    w_pack = jnp.concatenate(
        [w_in[:, :o1], w_in[:, o2:], w_in[:, o1:o2], jnp.zeros((d, LANES - MLA_ROPE), F32)], axis=1).astype(BF16)
    u = norm_matmul(x, norm_mix, w_pack, tm=1024, tn=EV_PACKED // 3)

    wq = w_q_up.reshape(MLA_Q_LORA, MLA_HEADS, MLA_NOPE + MLA_ROPE)
    wq = jnp.pad(wq, ((0, 0), (0, 0), (0, MLA_QK_PAD - MLA_NOPE - MLA_ROPE)))
    wq = wq.reshape(MLA_Q_LORA, MLA_HEADS * MLA_QK_PAD).astype(BF16)
    wkv = w_kv_up.reshape(MLA_KV_LORA, MLA_HEADS, MLA_NOPE + MLA_V)
    wk = wkv[:, :, :MLA_NOPE].reshape(MLA_KV_LORA, MLA_HEADS * MLA_NOPE).astype(BF16)
    wv = wkv[:, :, MLA_NOPE:].reshape(MLA_KV_LORA, MLA_HEADS * MLA_V).astype(BF16)
    rc, rlo, rhi = _mla_rope_tables(seq)
    q, k, v = mla_up(u, q_norm, kv_norm, wq, wk, wv, rc, rlo, rhi, seq=seq, tm=512)
    o_mla = mla_attention(q.reshape(b, seq, -1), k.reshape(b, seq, -1), v.reshape(b, seq, -1), t=512)

    o_rec = rglru(u.reshape(b, seq, EV_PACKED), conv_w, conv_b, ga_w.astype(BF16), ga_b,
                  gx_w.astype(BF16), gx_b, lam, ts=512)
    x = matmul_residual([o_mla.reshape(n, -1), o_rec.reshape(n, -1)], w_out.astype(BF16), x, tm=512, tn=D_MODEL)
    return ffn_dense(x, norm_ffn, w1.astype(BF16), w3.astype(BF16), w2.astype(BF16), tm=1024, tf=512)


def _moe_dispatch(meta, counts, n):
    e1 = meta[:, 0].astype(jnp.int32)
    e2 = meta[:, 1].astype(jnp.int32)
    pos1 = meta[:, 4].astype(jnp.int32)
    pos2 = meta[:, 5].astype(jnp.int32)
    cnt = counts[0, :N_EXPERTS].astype(jnp.int32)
    span = (cnt + MOE_ALIGN - 1) // MOE_ALIGN * MOE_ALIGN
    row0 = jnp.cumsum(span) - span
    row_unit = 64 * SC_WORKERS
    n_rows = -(-(n * TOP_K + N_EXPERTS * MOE_ALIGN + MOE_TILE) // row_unit) * row_unit
    d1 = row0[e1] + pos1
    d2 = row0[e2] + pos2
    n_tiles = (n * TOP_K) // MOE_TILE + N_EXPERTS
    tiles_e = (cnt + MOE_TILE - 1) // MOE_TILE
    tend = jnp.cumsum(tiles_e)
    tbeg = tend - tiles_e
    tid = jnp.arange(n_tiles, dtype=jnp.int32)
    te = jnp.minimum(jnp.searchsorted(tend, tid, side='right'), N_EXPERTS - 1).astype(jnp.int32)
    used = tid < tend[-1]
    first = (tid - tbeg[te]) * MOE_TILE
    rows = jnp.where(used, jnp.clip(cnt[te] - first, 0, MOE_TILE), 0).astype(jnp.int32)
    tstart = jnp.where(used, row0[te] + first, 0).astype(jnp.int32)
    last_e = te[jnp.maximum(tend[-1] - 1, 0)]
    te = jnp.where(used, te, last_e).astype(jnp.int32)
    return d1, d2, n_rows, te, tstart, rows


def odd_layer(x, seq, p, experts, final_g):
    n = x.shape[0]
    b = n // seq
    (norm_mix, w_in, ck_pe, ck_w1, ck_b1, ck_w2, cv_pe, cv_w1, cv_b1, cv_w2, w_out, norm_ffn,
     router_w, router_b) = p
    ew1, ew3, ew2, layer = experts
    d = D_MODEL
    wg = w_in[:, OD_G:].reshape(d, NSA_GROUPS, NSA_HPG * 3)
    wg = jnp.pad(wg, ((0, 0), (0, 0), (0, LANES - NSA_HPG * 3))).reshape(d, NSA_GROUPS * LANES)
    w_pack = jnp.concatenate([w_in[:, :OD_G], wg], axis=1).astype(BF16)
    u = norm_matmul(x, norm_mix, w_pack, tm=1024, tn=OD_PACKED // 4)
    u3 = u.reshape(b, seq, OD_PACKED)

    k_cmp = nsa_compress(u3, OD_KC, ck_pe, ck_w1.astype(BF16), ck_b1, ck_w2.astype(BF16))
    v_cmp = nsa_compress(u3, OD_VC, cv_pe, cv_w1.astype(BF16), cv_b1, cv_w2.astype(BF16))
    rc, rs = _nsa_rope_tables(seq)
    ovl, expand = _selection_constants(seq)
    o = nsa_attention(u3, k_cmp, v_cmp, rc, rs, ovl, expand, tq=512, tk=512)
    x = matmul_residual([o.reshape(n, -1)], w_out.astype(BF16), x, tm=512, tn=D_MODEL)

    wr = jnp.pad(router_w, ((0, 0), (0, LANES - N_EXPERTS)))
    br = jnp.pad(router_b, (0, LANES - N_EXPERTS)).reshape(1, LANES)
    xp, meta, counts = moe_router(x, norm_ffn, wr, br, tm=512)
    d1, d2, n_rows, te, tstart, rows = _moe_dispatch(meta, counts, n)
    yr = moe_experts(sc_scatter_rows(xp, [d1, d2], n_rows), te, tstart, rows, ew1, ew3, ew2, layer, tf=256)
    yg = sc_gather_rows(yr, jnp.concatenate([d1, d2]))
    g = final_g if final_g is not None else norm_ffn
    return moe_combine(x, yg, meta, g, final_norm=final_g is not None, tm=512)


def kernel(x, ev_norm_mix, ev_w_in, ev_q_norm, ev_w_q_up, ev_kv_norm, ev_w_kv_up, ev_conv_w, ev_conv_b, ev_gate_a_w, ev_gate_a_b, ev_gate_x_w, ev_gate_x_b, ev_lru_lambda, ev_w_out, ev_norm_ffn, ev_ffn_w1, ev_ffn_w3, ev_ffn_w2, od_norm_mix, od_w_in, od_cmp_k_pe, od_cmp_k_w1, od_cmp_k_b1, od_cmp_k_w2, od_cmp_v_pe, od_cmp_v_w1, od_cmp_v_b1, od_cmp_v_w2, od_w_out, od_norm_ffn, od_router_w, od_router_b, od_exp_w1, od_exp_w3, od_exp_w2, final_norm):
    bsz, seq, d = x.shape
    ev = (ev_norm_mix, ev_w_in, ev_q_norm, ev_w_q_up, ev_kv_norm, ev_w_kv_up, ev_conv_w, ev_conv_b,
          ev_gate_a_w, ev_gate_a_b, ev_gate_x_w, ev_gate_x_b, ev_lru_lambda, ev_w_out, ev_norm_ffn,
          ev_ffn_w1, ev_ffn_w3, ev_ffn_w2)
    od = (od_norm_mix, od_w_in, od_cmp_k_pe, od_cmp_k_w1, od_cmp_k_b1, od_cmp_k_w2, od_cmp_v_pe,
          od_cmp_v_w1, od_cmp_v_b1, od_cmp_v_w2, od_w_out, od_norm_ffn, od_router_w, od_router_b)
    h = x.reshape(bsz * seq, d)
    for layer in range(DEPTH):
        i = layer // 2
        if layer % 2 == 0:
            h = even_layer(h, seq, tuple(a[i] for a in ev))
        else:
            h = odd_layer(h, seq, tuple(a[i] for a in od), (od_exp_w1, od_exp_w3, od_exp_w2, i),
                          final_norm if layer == DEPTH - 1 else None)
    return h.reshape(bsz, seq, d)
```

```python
import functools
import math

import numpy as np
import jax
import jax.numpy as jnp
from jax import lax
from jax.experimental import pallas as pl
from jax.experimental.pallas import tpu as pltpu
from jax.experimental.pallas import tpu_sc as plsc

F32 = jnp.float32
BF16 = jnp.bfloat16

D_MODEL = 2048
DEPTH = 4
RMS_EPS = 1e-6
ROPE_THETA = 10000.0
NEG_INF = -1e30
LOG2E = math.log2(math.e)

MLA_HEADS = 8
MLA_Q_LORA = 768
MLA_KV_LORA = 512
MLA_NOPE = 128
MLA_ROPE = 64
MLA_V = 128
MLA_QK_PAD = 256
MLA_ROW_GROUPS = 2
MLA_HEADS_PER_STEP = 4

LRU_WIDTH = D_MODEL // 2
LRU_BLOCKS = 8
LRU_BLOCK_W = LRU_WIDTH // LRU_BLOCKS
LRU_C = 8.0
CONV_WIDTH = 4

NSA_HEADS = 16
NSA_GROUPS = 4
NSA_HPG = NSA_HEADS // NSA_GROUPS
NSA_DH = D_MODEL // NSA_HEADS
CMP_BLOCK = 32
CMP_STRIDE = 16
SEL_BLOCK = 64
SEL_COUNT = 16
SEL_FORCE = 1e4
WINDOW = 512
NSA_WIN_ROWS = 256
NSA_SEL_ROWS = 512
NSA_Q_DIM = NSA_HEADS * NSA_DH
NSA_KV_DIM = NSA_GROUPS * NSA_DH

DENSE_FF = 5632
N_EXPERTS = 8
TOP_K = 2
EXPERT_FF = 7168

LANES = 128
SC_CORES = 2
SC_SUBCORES = 16
SC_WORKERS = SC_CORES * SC_SUBCORES
SC_GATHER_BYTES = 256 * 1024
VMEM_LIMIT = 60 * 1024 * 1024
CAST_BLOCK_BYTES = 4 * 1024 * 1024

EV_CQ = 0
EV_CKV = MLA_Q_LORA
EV_REC = MLA_Q_LORA + MLA_KV_LORA
EV_GATE = EV_REC + LRU_WIDTH
EV_PE = EV_GATE + LRU_WIDTH
EV_PACKED = EV_PE + LANES

OD_Q = 0
OD_KC = NSA_Q_DIM
OD_VC = OD_KC + NSA_KV_DIM
OD_KS = OD_VC + NSA_KV_DIM
OD_VS = OD_KS + NSA_KV_DIM
OD_KW = OD_VS + NSA_KV_DIM
OD_VW = OD_KW + NSA_KV_DIM
OD_G = OD_VW + NSA_KV_DIM
OD_PACKED = OD_G + NSA_GROUPS * LANES

MOE_TILE = 2688
MOE_ALIGN = 16
MOE_COPY = 384
MOE_PIECES = (1024, 1024, 512, 256, 128)
MOE_CHAIN = 512
MOE_WSPLIT = 1


def _cparams(sem):
    return pltpu.CompilerParams(dimension_semantics=sem, vmem_limit_bytes=VMEM_LIMIT)


def _rms(x, g):
    ms = jnp.mean(x * x, axis=-1, keepdims=True)
    return x * lax.rsqrt(ms + RMS_EPS) * g


def _sigmoid(x):
    return 1.0 / (1.0 + jnp.exp(-x))


def _gelu_tanh(x):
    return 0.5 * x * (1.0 + jnp.tanh(math.sqrt(2.0 / math.pi) * (x + 0.044715 * (x * x * x))))


def _dot(a, b):
    return jnp.dot(a, b, preferred_element_type=F32)


def _dot_nt(a, b):
    return lax.dot_general(a, b, (((1,), (1,)), ((), ())), preferred_element_type=F32)


def _norm_mm_kernel(x_ref, g_ref, w_ref, o_ref, xn_ref):
    @pl.when(pl.program_id(1) == 0)
    def _():
        xn_ref[...] = _rms(x_ref[...], g_ref[...]).astype(BF16)

    o_ref[...] = _dot(xn_ref[...], w_ref[...]).astype(o_ref.dtype)


def norm_matmul(x, g, w, *, tm, tn, out_dtype=F32):
    n, k = x.shape
    m = w.shape[1]
    assert n % tm == 0 and m % tn == 0
    return pl.pallas_call(
        _norm_mm_kernel,
        grid=(n // tm, m // tn),
        in_specs=[pl.BlockSpec((tm, k), lambda i, j: (i, 0)),
                  pl.BlockSpec((1, k), lambda i, j: (0, 0)),
                  pl.BlockSpec((k, tn), lambda i, j: (0, j))],
        out_specs=pl.BlockSpec((tm, tn), lambda i, j: (i, j)),
        out_shape=jax.ShapeDtypeStruct((n, m), out_dtype),
        scratch_shapes=[pltpu.VMEM((tm, k), BF16)],
        compiler_params=_cparams(("parallel", "arbitrary")),
    )(x, g.reshape(1, k), w)


def _cast_kernel(w_ref, o_ref):
    o_ref[...] = w_ref[0].astype(o_ref.dtype)


def cast_bf16(ws, layer):
    _, r, c = ws.shape
    rb = max(rb for rb in range(16, r + 1, 16) if r % rb == 0 and rb * c * 4 <= CAST_BLOCK_BYTES)
    return pl.pallas_call(
        _cast_kernel,
        grid=(r // rb,),
        in_specs=[pl.BlockSpec((1, rb, c), lambda i: (layer, i, 0))],
        out_specs=pl.BlockSpec((rb, c), lambda i: (i, 0)),
        out_shape=jax.ShapeDtypeStruct((r, c), BF16),
        compiler_params=_cparams(("parallel",)),
    )(ws)


def _mm_res_kernel(*refs, n_in):
    xs = refs[:n_in]
    ws = refs[n_in:2 * n_in]
    res_ref = refs[2 * n_in]
    o_ref = refs[2 * n_in + 1]
    acc = res_ref[...]
    for x_ref, w_ref in zip(xs, ws):
        acc = acc + _dot(x_ref[...], w_ref[...])
    o_ref[...] = acc


def matmul_residual(xs, w, res, *, tm, tn):
    n = res.shape[0]
    m = w.shape[1]
    n_in = len(xs)
    in_specs = [pl.BlockSpec((tm, x.shape[1]), lambda i, j: (i, 0)) for x in xs]
    row = 0
    for x in xs:
        kx = x.shape[1]
        assert row % kx == 0
        in_specs.append(pl.BlockSpec((kx, tn), lambda i, j, rb=row // kx: (rb, j)))
        row += kx
    assert row == w.shape[0]
    in_specs.append(pl.BlockSpec((tm, tn), lambda i, j: (i, j)))
    return pl.pallas_call(
        functools.partial(_mm_res_kernel, n_in=n_in),
        grid=(n // tm, m // tn),
        in_specs=in_specs,
        out_specs=pl.BlockSpec((tm, tn), lambda i, j: (i, j)),
        out_shape=jax.ShapeDtypeStruct((n, m), F32),
        compiler_params=_cparams(("parallel", "arbitrary")),
    )(*xs, *([w] * n_in), res)


def _rope64(x, c, s_lo, s_hi):
    return x * c + pltpu.roll(x, 96, 1) * s_lo + pltpu.roll(x, 32, 1) * s_hi


def _mla_up_kernel(u_ref, pe_ref, qg_ref, kvg_ref, wq_ref, wk_ref, wv_ref,
                   c_ref, slo_ref, shi_ref, q_ref, k_ref, v_ref):
    u = u_ref[...]
    c, s_lo, s_hi = c_ref[...], slo_ref[...], shi_ref[...]
    qn = _rms(u[:, EV_CQ:EV_CQ + MLA_Q_LORA], qg_ref[...]).astype(BF16)
    kvn = _rms(u[:, EV_CKV:EV_CKV + MLA_KV_LORA], kvg_ref[...]).astype(BF16)
    q = _dot(qn, wq_ref[...]) * ((MLA_NOPE + MLA_ROPE) ** -0.5 * LOG2E)
    kn = _dot(kvn, wk_ref[...])
    v_ref[...] = _dot(kvn, wv_ref[...]).astype(v_ref.dtype)
    kpe = _rope64(pe_ref[...], c, s_lo, s_hi).astype(k_ref.dtype)
    for h in range(MLA_HEADS):
        a = h * MLA_QK_PAD
        q_ref[:, a:a + LANES] = q[:, a:a + LANES].astype(q_ref.dtype)
        q_ref[:, a + LANES:a + 2 * LANES] = _rope64(q[:, a + LANES:a + 2 * LANES], c, s_lo, s_hi).astype(q_ref.dtype)
        k_ref[:, a:a + LANES] = kn[:, h * LANES:(h + 1) * LANES].astype(k_ref.dtype)
        k_ref[:, a + LANES:a + 2 * LANES] = kpe


def mla_up(u, q_norm, kv_norm, wq, wk, wv, rope_c, rope_slo, rope_shi, *, seq, tm):
    n = u.shape[0]
    hq = MLA_HEADS * MLA_QK_PAD
    hv = MLA_HEADS * MLA_V
    ab = EV_REC
    assert seq % tm == 0 and EV_PE % LANES == 0
    nsb = seq // tm
    row_spec = pl.BlockSpec((tm, LANES), lambda i: (i % nsb, 0))
    full = lambda a: pl.BlockSpec(a.shape, lambda i: (0,) * a.ndim)
    qg = q_norm.reshape(1, -1)
    kvg = kv_norm.reshape(1, -1)
    return pl.pallas_call(
        _mla_up_kernel,
        grid=(n // tm,),
        in_specs=[pl.BlockSpec((tm, ab), lambda i: (i, 0)),
                  pl.BlockSpec((tm, LANES), lambda i: (i, EV_PE // LANES)),
                  full(qg), full(kvg), full(wq), full(wk), full(wv),
                  row_spec, row_spec, row_spec],
        out_specs=[pl.BlockSpec((tm, hq), lambda i: (i, 0)),
                   pl.BlockSpec((tm, hq), lambda i: (i, 0)),
                   pl.BlockSpec((tm, hv), lambda i: (i, 0))],
        out_shape=[jax.ShapeDtypeStruct((n, hq), BF16),
                   jax.ShapeDtypeStruct((n, hq), BF16),
                   jax.ShapeDtypeStruct((n, hv), BF16)],
        compiler_params=_cparams(("parallel",)),
    )(u, u, qg, kvg, wq, wk, wv, rope_c, rope_slo, rope_shi)


def _values_and_ones(v):
    return jnp.concatenate([v.astype(BF16), jnp.ones(v.shape, BF16)], axis=1)


def _lane_tile(x, n):
    return jnp.concatenate([x] * n, axis=1)


def _softmax_step(s, v1, m_ref, acc_ref):
    m_prev = m_ref[...]
    m_new = jnp.maximum(m_prev, jnp.max(s, axis=-1, keepdims=True))
    alpha = jnp.exp2(m_prev - m_new)
    p = jnp.exp2(s - _lane_tile(m_new, s.shape[1] // LANES))
    acc_ref[...] = _lane_tile(alpha, 2) * acc_ref[...] + _dot(p.astype(BF16), v1)
    m_ref[...] = m_new


def _softmax_result(acc):
    return acc[:, :LANES] / acc[:, LANES:]


def _mla_attn_kernel(q_ref, k_ref, v_ref, cb_ref, o_ref, v1, m_ref, acc_ref, *, t, n_chunks):
    qi = pl.program_id(2)

    hs = MLA_HEADS_PER_STEP

    @pl.when(qi == 0)
    def _():
        for h in range(hs):
            v1[h] = _values_and_ones(v_ref[0, :, h * MLA_V:(h + 1) * MLA_V])

    m_ref[...] = jnp.full(m_ref.shape, NEG_INF, F32)
    acc_ref[...] = jnp.zeros(acc_ref.shape, F32)
    rg = t // MLA_ROW_GROUPS

    def chunk(c, masked):
        for h in range(hs):
            for i in range(MLA_ROW_GROUPS):
                rows = pl.ds(i * rg, rg)
                qh = q_ref[0, i * rg:(i + 1) * rg, h * MLA_QK_PAD:(h + 1) * MLA_QK_PAD]
                s = _dot_nt(qh, k_ref[0, c * t:(c + 1) * t, h * MLA_QK_PAD:(h + 1) * MLA_QK_PAD])
                if masked:
                    s = s + cb_ref[i * rg:(i + 1) * rg, :]
                _softmax_step(s, v1[h, c * t:(c + 1) * t, :], m_ref.at[h, rows], acc_ref.at[h, rows])

    for c in range(n_chunks):
        pl.when(c < qi)(functools.partial(chunk, c, False))
        pl.when(c == qi)(functools.partial(chunk, c, True))
    for h in range(hs):
        o_ref[0, :, h * MLA_V:(h + 1) * MLA_V] = _softmax_result(acc_ref[h]).astype(o_ref.dtype)


def mla_attention(q, k, v, *, t):
    b, s, _ = q.shape
    causal = jnp.asarray(np.where(np.arange(t)[None, :] <= np.arange(t)[:, None], 0.0, NEG_INF), F32)
    kern = functools.partial(_mla_attn_kernel, t=t, n_chunks=s // t)
    hs = MLA_HEADS_PER_STEP
    return pl.pallas_call(
        kern,
        grid=(b, MLA_HEADS // hs, s // t),
        in_specs=[pl.BlockSpec((1, t, hs * MLA_QK_PAD), lambda b_, h, i: (b_, i, h)),
                  pl.BlockSpec((1, s, hs * MLA_QK_PAD), lambda b_, h, i: (b_, 0, h)),
                  pl.BlockSpec((1, s, hs * MLA_V), lambda b_, h, i: (b_, 0, h)),
                  pl.BlockSpec((t, t), lambda b_, h, i: (0, 0))],
        out_specs=pl.BlockSpec((1, t, hs * MLA_V), lambda b_, h, i: (b_, i, h)),
        out_shape=jax.ShapeDtypeStruct((b, s, MLA_HEADS * MLA_V), BF16),
        scratch_shapes=[pltpu.VMEM((hs, s, 2 * LANES), BF16), pltpu.VMEM((hs, t, LANES), F32),
                        pltpu.VMEM((hs, t, 2 * LANES), F32)],
        compiler_params=_cparams(("parallel", "parallel", "arbitrary")),
    )(q, k, v, causal)


def _rglru_kernel(x_ref, y_ref, cw_ref, cb_ref, gaw_ref, gab_ref, gxw_ref, gxb_ref, lam_ref,
                  o_ref, xbuf, h_ref, *, ts):
    t = pl.program_id(2)

    @pl.when(t == 0)
    def _():
        xbuf[0:8, :] = jnp.zeros((8, LANES), F32)
        h_ref[...] = jnp.zeros(h_ref.shape, F32)

    x = x_ref[0]
    xbuf[8:, :] = x
    cw = cw_ref[...]
    xc = cb_ref[...] + cw[3:4] * x
    for kk in range(CONV_WIDTH - 1):
        back = CONV_WIDTH - 1 - kk
        xc = xc + cw[kk:kk + 1] * xbuf[8 - back:8 - back + ts, :]
    xbuf[0:8, :] = x[ts - 8:, :]

    xcb = xc.astype(BF16)
    r = _sigmoid(_dot(xcb, gaw_ref[0]) + gab_ref[0])
    gi = _sigmoid(_dot(xcb, gxw_ref[0]) + gxb_ref[0])
    z = -lam_ref[...]
    softplus = jnp.maximum(z, 0.0) + jnp.log1p(jnp.exp(-jnp.abs(z)))
    log_a = (-LRU_C) * r * softplus
    a = jnp.exp(log_a)
    mult = jnp.sqrt(-jnp.tanh(log_a) * (a * a + 1.0))
    row = lax.broadcasted_iota(jnp.int32, (ts, 1), 0)
    mult = jnp.where(row + t * ts == 0, 1.0, mult)
    bv = mult * gi * xc

    d = 1
    while d < ts:
        keep = row >= d
        a_sh = jnp.where(keep, pltpu.roll(a, d, 0), 1.0)
        b_sh = jnp.where(keep, pltpu.roll(bv, d, 0), 0.0)
        bv = a * b_sh + bv
        a = a * a_sh
        d *= 2
    h = bv + a * h_ref[...]
    h_ref[...] = h[ts - 1:ts, :]
    o_ref[0] = (h * _gelu_tanh(y_ref[0])).astype(o_ref.dtype)


def rglru(u3, conv_w, conv_b, ga_w, ga_b, gx_w, gx_b, lam, *, ts):
    b, s, _ = u3.shape
    rec0 = EV_REC // LANES
    gate0 = EV_GATE // LANES
    cb = conv_b.reshape(1, LRU_WIDTH)
    gab = ga_b.reshape(LRU_BLOCKS, 1, LRU_BLOCK_W)
    gxb = gx_b.reshape(LRU_BLOCKS, 1, LRU_BLOCK_W)
    lam2 = lam.reshape(1, LRU_WIDTH)
    blk_w = pl.BlockSpec((1, LRU_BLOCK_W, LRU_BLOCK_W), lambda b_, n, t: (n, 0, 0))
    blk_b = pl.BlockSpec((1, 1, LRU_BLOCK_W), lambda b_, n, t: (n, 0, 0))
    vec = pl.BlockSpec((1, LANES), lambda b_, n, t: (0, n))
    return pl.pallas_call(
        functools.partial(_rglru_kernel, ts=ts),
        grid=(b, LRU_BLOCKS, s // ts),
        in_specs=[pl.BlockSpec((1, ts, LANES), lambda b_, n, t: (b_, t, rec0 + n)),
                  pl.BlockSpec((1, ts, LANES), lambda b_, n, t: (b_, t, gate0 + n)),
                  pl.BlockSpec((CONV_WIDTH, LANES), lambda b_, n, t: (0, n)),
                  vec, blk_w, blk_b, blk_w, blk_b, vec],
        out_specs=pl.BlockSpec((1, ts, LANES), lambda b_, n, t: (b_, t, n)),
        out_shape=jax.ShapeDtypeStruct((b, s, LRU_WIDTH), BF16),
        scratch_shapes=[pltpu.VMEM((ts + 8, LANES), F32), pltpu.VMEM((1, LANES), F32)],
        compiler_params=_cparams(("parallel", "parallel", "arbitrary")),
    )(u3, u3, conv_w, cb, ga_w, gab, gx_w, gxb, lam2)


def _ffn_kernel(x_ref, g_ref, w1_ref, w3_ref, w2_ref, o_ref, xn_ref):
    f = pl.program_id(1)

    @pl.when(f == 0)
    def _():
        x = x_ref[...]
        xn_ref[...] = _rms(x, g_ref[...]).astype(BF16)
        o_ref[...] = x

    xn = xn_ref[...]
    h1 = _dot(xn, w1_ref[...])
    h3 = _dot(xn, w3_ref[...])
    hh = (h1 * _sigmoid(h1) * h3).astype(BF16)
    o_ref[...] += _dot(hh, w2_ref[...])


def ffn_dense(x, g, w1, w3, w2, *, tm, tf):
    n, d = x.shape
    ff = w1.shape[1]
    assert n % tm == 0 and ff % tf == 0
    return pl.pallas_call(
        _ffn_kernel,
        grid=(n // tm, ff // tf),
        in_specs=[pl.BlockSpec((tm, d), lambda i, f: (i, 0)),
                  pl.BlockSpec((1, d), lambda i, f: (0, 0)),
                  pl.BlockSpec((d, tf), lambda i, f: (0, f)),
                  pl.BlockSpec((d, tf), lambda i, f: (0, f)),
                  pl.BlockSpec((tf, d), lambda i, f: (f, 0))],
        out_specs=pl.BlockSpec((tm, d), lambda i, f: (i, 0)),
        out_shape=jax.ShapeDtypeStruct((n, d), F32),
        scratch_shapes=[pltpu.VMEM((tm, d), BF16)],
        compiler_params=_cparams(("parallel", "arbitrary")),
    )(x, g.reshape(1, d), w1, w3, w2)


def _compress_kernel(*refs, n_half):
    kc_refs = refs[:NSA_GROUPS]
    pe_ref, w1_ref, b1_ref, w2_ref, o_ref = refs[NSA_GROUPS:]
    pe = pe_ref[...]
    half_k = n_half * NSA_DH
    nchunk = o_ref.shape[1]
    for g in range(NSA_GROUPS):
        lo, hi = [], []
        for l in range(n_half):
            piece = kc_refs[g][0, pl.ds(l, nchunk, stride=n_half), :]
            lo.append((piece + pe[l:l + 1]).astype(BF16))
            hi.append((piece + pe[n_half + l:n_half + l + 1]).astype(BF16))
        z0 = _dot(jnp.concatenate(lo, axis=1), w1_ref[0:half_k, :])
        z1 = _dot(jnp.concatenate(hi, axis=1), w1_ref[half_k:2 * half_k, :])
        rows = z1.shape[0]
        pre = z0 + pltpu.roll(z1, rows - 1, 0) + b1_ref[...]
        o_ref[0, :, g * NSA_DH:(g + 1) * NSA_DH] = _dot(_gelu_tanh(pre).astype(BF16), w2_ref[...]).astype(o_ref.dtype)


def nsa_compress(u3, col, pe, w1, b1, w2):
    b, s, _ = u3.shape
    nchunk = s // CMP_STRIDE
    assert CMP_BLOCK == 2 * CMP_STRIDE and col % NSA_KV_DIM == 0
    full = lambda a: pl.BlockSpec(a.shape, lambda i: (0,) * a.ndim)
    b1r = b1.reshape(1, NSA_DH)
    return pl.pallas_call(
        functools.partial(_compress_kernel, n_half=CMP_STRIDE),
        grid=(b,),
        in_specs=[pl.BlockSpec((1, s, NSA_DH), lambda i, g=g: (i, 0, col // NSA_DH + g)) for g in range(NSA_GROUPS)]
        + [full(pe), full(w1), full(b1r), full(w2)],
        out_specs=pl.BlockSpec((1, nchunk, NSA_KV_DIM), lambda i: (i, 0, 0)),
        out_shape=jax.ShapeDtypeStruct((b, nchunk, NSA_KV_DIM), BF16),
        compiler_params=_cparams(("parallel",)),
    )(*([u3] * NSA_GROUPS), pe, w1, b1r, w2)


def _rope128(x, c, s):
    return x * c + pltpu.roll(x, NSA_DH // 2, 1) * s


def _pack_bf16_pairs(x):
    w = x.shape[1] // 2
    bits = pltpu.bitcast(x.astype(BF16).astype(F32), jnp.uint32)
    word = bits[:, w:] | lax.shift_right_logical(bits[:, :w], jnp.uint32(16))
    return pltpu.bitcast(word, F32)


def _unpack_bf16_pairs(word):
    bits = pltpu.bitcast(word, jnp.uint32)
    lo = pltpu.bitcast(lax.shift_left(bits, jnp.uint32(16)), F32).astype(BF16)
    hi = pltpu.bitcast(bits & jnp.uint32(0xFFFF0000), F32).astype(BF16)
    return lo, hi


def _split3(x):
    hi = x.astype(BF16)
    r1 = x - hi.astype(F32)
    mid = r1.astype(BF16)
    lo = (r1 - mid.astype(F32)).astype(BF16)
    return hi, mid, lo


def _nsa_attn_kernel(q_ref, ks_ref, vs_ref, kw_ref, vw_ref, gt_ref, kc_ref, vc_ref,
                     cq_ref, sq_ref, ck_ref, sk_ref, ovl_ref, exp_ref, wb_ref,
                     o_ref, ksr, vsb, kwr, vwb, selb, m_ref, acc_ref,
                     *, tq, tk, seq, scale):
    qi = pl.program_id(2)
    hp = NSA_HPG
    n_sel = seq // SEL_BLOCK

    @pl.when(qi == 0)
    def _():
        ck, sk = ck_ref[...], sk_ref[...]
        ksr[...] = _rope128(ks_ref[0], ck, sk).astype(BF16)
        kwr[...] = _rope128(kw_ref[0], ck, sk).astype(BF16)
        vsb[...] = _values_and_ones(vs_ref[0])
        vwb[...] = _values_and_ones(vw_ref[0])

    q = q_ref[0] * (scale * LOG2E)
    cq, sq = cq_ref[...], sq_ref[...]
    heads = [q[:, p * NSA_DH:(p + 1) * NSA_DH] for p in range(hp)]
    qu = jnp.concatenate(heads, axis=0).astype(BF16)
    qr = jnp.concatenate([_rope128(h, cq, sq) for h in heads], axis=0).astype(BF16)
    t_row = qi * tq + lax.broadcasted_iota(jnp.int32, (tq, 1), 0)
    lane = lax.broadcasted_iota(jnp.int32, (1, LANES), 1)

    sc = _dot_nt(qu, kc_ref[0]).reshape(hp, tq, LANES)
    valid = (lane * CMP_STRIDE + (CMP_BLOCK - 1) <= t_row)[None]
    sm = jnp.where(valid, sc, NEG_INF)
    e = jnp.exp2(sm - jnp.max(sm, axis=-1, keepdims=True))
    p = jnp.where(valid, e / jnp.sum(e, axis=-1, keepdims=True), 0.0)
    o_cmp = _dot(p.reshape(hp * tq, LANES).astype(BF16), vc_ref[0])

    psum = p[0]
    for i in range(1, hp):
        psum = psum + p[i]
    ovl = ovl_ref[...]
    imp = sum(_dot(part, ovl) for part in _split3(psum))
    cur = jnp.right_shift(t_row, SEL_BLOCK.bit_length() - 1)
    future = lane > cur
    forced = (lane == 0) | (lane == cur) | (lane == cur - 1)
    score = jnp.where(future, -1.0, jnp.where(forced, SEL_FORCE, imp))
    sc_t = score.T[0:n_sel, :]
    blk = lax.broadcasted_iota(jnp.int32, (n_sel, 1), 0)
    cnt = jnp.zeros((n_sel, tq), F32)
    for j in range(n_sel):
        other = sc_t[j:j + 1, :]
        beats = (other > sc_t) | ((other == sc_t) & (blk > j))
        cnt = cnt + jnp.where(beats, 1.0, 0.0)
    sel_t = jnp.where(cnt < SEL_COUNT, 1.0, 0.0)
    sel = jnp.concatenate([sel_t, jnp.zeros((LANES - n_sel, tq), F32)], axis=0).T.astype(BF16)
    picked = _dot(sel, exp_ref[...])
    kall = lax.broadcasted_iota(jnp.int32, (1, seq), 1)
    selb[...] = jnp.where((picked > 0.5) & (kall <= t_row), 0.0, NEG_INF)

    m_ref[...] = jnp.full(m_ref.shape, NEG_INF, F32)
    acc_ref[...] = jnp.zeros(acc_ref.shape, F32)
    for c in range(seq // tk):
        @pl.when(c * tk <= qi * tq + (tq - 1))
        def _():
            for i in range(hp):
                for r0 in range(0, tq, NSA_SEL_ROWS):
                    rows = pl.ds(i * tq + r0, NSA_SEL_ROWS)
                    s = _dot_nt(qr[i * tq + r0:i * tq + r0 + NSA_SEL_ROWS], ksr[c * tk:(c + 1) * tk, :])
                    _softmax_step(s + selb[r0:r0 + NSA_SEL_ROWS, c * tk:(c + 1) * tk], vsb[c * tk:(c + 1) * tk, :],
                                  m_ref.at[rows], acc_ref.at[rows])
    o_sel = _softmax_result(acc_ref[...])

    span = WINDOW + NSA_WIN_ROWS
    wins = []
    for i in range(hp):
        for r in range(tq // NSA_WIN_ROWS):
            group = qi * (tq // NSA_WIN_ROWS) + r
            start = pl.multiple_of(jnp.maximum(group * NSA_WIN_ROWS - WINDOW, 0), NSA_WIN_ROWS)
            bias = wb_ref[jnp.minimum(group, WINDOW // NSA_WIN_ROWS)]
            r0 = i * tq + r * NSA_WIN_ROWS
            sw = _dot_nt(qr[r0:r0 + NSA_WIN_ROWS], kwr[pl.ds(start, span), :]) + bias
            ew = jnp.exp2(sw - jnp.max(sw, axis=-1, keepdims=True)).astype(BF16)
            wins.append(_softmax_result(_dot(ew, vwb[pl.ds(start, span), :])))
    o_win = jnp.concatenate(wins, axis=0)

    gates = _sigmoid(gt_ref[0])
    for i in range(hp):
        rows = slice(i * tq, (i + 1) * tq)
        o = (gates[:, 3 * i:3 * i + 1] * o_cmp[rows] + gates[:, 3 * i + 1:3 * i + 2] * o_sel[rows]
             + gates[:, 3 * i + 2:3 * i + 3] * o_win[rows])
        o_ref[0, :, i * NSA_DH:(i + 1) * NSA_DH] = o.astype(o_ref.dtype)


def _window_bias(tq):
    span = WINDOW + tq
    out = []
    for qi in range(WINDOW // tq + 1):
        start = max(qi * tq - WINDOW, 0)
        t = qi * tq + np.arange(tq)[:, None]
        kpos = start + np.arange(span)[None, :]
        out.append(np.where((kpos <= t) & (kpos > t - WINDOW), 0.0, NEG_INF))
    return jnp.asarray(np.stack(out), F32)


def nsa_attention(u3, k_cmp, v_cmp, rope_c, rope_s, ovl, expand, *, tq, tk):
    b, s, _ = u3.shape
    assert s // CMP_STRIDE == LANES and tq % SEL_BLOCK == 0 and tq % NSA_WIN_ROWS == 0 and WINDOW % NSA_WIN_ROWS == 0
    hp = NSA_HPG
    wbias = _window_bias(NSA_WIN_ROWS)
    col = lambda off: (lambda b_, g, i: (b_, 0, off // NSA_DH + g))
    seq_blk = lambda off: pl.BlockSpec((1, s, NSA_DH), col(off))
    full = lambda a: pl.BlockSpec(a.shape, lambda b_, g, i: (0,) * a.ndim)
    cmp_blk = pl.BlockSpec((1, LANES, NSA_DH), lambda b_, g, i: (b_, 0, g))
    rope_q = pl.BlockSpec((tq, NSA_DH), lambda b_, g, i: (i, 0))
    kern = functools.partial(_nsa_attn_kernel, tq=tq, tk=tk, seq=s, scale=NSA_DH ** -0.5)
    return pl.pallas_call(
        kern,
        grid=(b, NSA_GROUPS, s // tq),
        in_specs=[pl.BlockSpec((1, tq, hp * NSA_DH), lambda b_, g, i: (b_, i, g)),
                  seq_blk(OD_KS), seq_blk(OD_VS), seq_blk(OD_KW), seq_blk(OD_VW),
                  pl.BlockSpec((1, tq, LANES), lambda b_, g, i: (b_, i, OD_G // LANES + g)),
                  cmp_blk, cmp_blk, rope_q, rope_q, full(rope_c), full(rope_s), full(ovl), full(expand),
                  full(wbias)],
        out_specs=pl.BlockSpec((1, tq, hp * NSA_DH), lambda b_, g, i: (b_, i, g)),
        out_shape=jax.ShapeDtypeStruct((b, s, NSA_Q_DIM), BF16),
        scratch_shapes=[pltpu.VMEM((s, NSA_DH), BF16), pltpu.VMEM((s, 2 * LANES), BF16)] * 2 + [
            pltpu.VMEM((tq, s), F32),
            pltpu.VMEM((hp * tq, LANES), F32), pltpu.VMEM((hp * tq, 2 * LANES), F32)],
        compiler_params=_cparams(("parallel", "parallel", "arbitrary")),
    )(u3, u3, u3, u3, u3, u3, k_cmp, v_cmp, rope_c, rope_s, rope_c, rope_s, ovl, expand, wbias)


def _router_kernel(x_ref, g_ref, wr_ref, br_ref, xn_ref, meta_ref, cnt_ref, *, tm):
    @pl.when(pl.program_id(0) == 0)
    def _():
        cnt_ref[...] = jnp.zeros(cnt_ref.shape, F32)

    xn = _rms(x_ref[...], g_ref[...])
    xn_ref[...] = _pack_bf16_pairs(xn)
    xh, xm, _ = _split3(xn)
    wh, wm, _ = _split3(wr_ref[...])
    logits = _dot(xh, wh) + _dot(xh, wm) + _dot(xm, wh) + br_ref[...]
    lane = lax.broadcasted_iota(jnp.int32, (1, LANES), 1).astype(F32)
    lg = jnp.where(lane < N_EXPERTS, logits, NEG_INF)
    m1 = jnp.max(lg, axis=-1, keepdims=True)
    e1 = jnp.min(jnp.where(lg == m1, lane, float(LANES)), axis=-1, keepdims=True)
    lg2 = jnp.where(lane == e1, NEG_INF, lg)
    m2 = jnp.max(lg2, axis=-1, keepdims=True)
    e2 = jnp.min(jnp.where(lg2 == m2, lane, float(LANES)), axis=-1, keepdims=True)
    ex = jnp.exp(m2 - m1)
    den = 1.0 + ex
    g1 = 1.0 / den
    g2 = ex / den
    oh = jnp.where((lane == e1) | (lane == e2), 1.0, 0.0)
    r = lax.broadcasted_iota(jnp.int32, (tm, tm), 0)
    c = lax.broadcasted_iota(jnp.int32, (tm, tm), 1)
    tri = jnp.where(r > c, 1.0, 0.0).astype(BF16)
    cum = _dot(tri, oh.astype(BF16)) + cnt_ref[0:1, :]
    pos1 = jnp.sum(jnp.where(lane == e1, cum, 0.0), axis=-1, keepdims=True)
    pos2 = jnp.sum(jnp.where(lane == e2, cum, 0.0), axis=-1, keepdims=True)
    cnt_ref[...] = cnt_ref[...] + jnp.sum(oh, axis=0, keepdims=True)
    meta = jnp.where(lane == 0, e1, 0.0)
    meta = jnp.where(lane == 1, e2, meta)
    meta = jnp.where(lane == 2, g1, meta)
    meta = jnp.where(lane == 3, g2, meta)
    meta = jnp.where(lane == 4, pos1, meta)
    meta = jnp.where(lane == 5, pos2, meta)
    meta_ref[...] = meta


def moe_router(x, g, wr, br, *, tm):
    n, d = x.shape
    return pl.pallas_call(
        functools.partial(_router_kernel, tm=tm),
        grid=(n // tm,),
        in_specs=[pl.BlockSpec((tm, d), lambda i: (i, 0)),
                  pl.BlockSpec((1, d), lambda i: (0, 0)),
                  pl.BlockSpec((d, LANES), lambda i: (0, 0)),
                  pl.BlockSpec((1, LANES), lambda i: (0, 0))],
        out_specs=[pl.BlockSpec((tm, d // 2), lambda i: (i, 0)),
                   pl.BlockSpec((tm, LANES), lambda i: (i, 0)),
                   pl.BlockSpec((8, LANES), lambda i: (0, 0))],
        out_shape=[jax.ShapeDtypeStruct((n, d // 2), F32),
                   jax.ShapeDtypeStruct((n, LANES), F32),
                   jax.ShapeDtypeStruct((8, LANES), F32)],
        compiler_params=_cparams(("arbitrary",)),
    )(x, g.reshape(1, d), wr, br)


def _gather_chunk(per_worker, row_bytes):
    best = 0
    for c in range(8, per_worker + 1, 8):
        if per_worker % c == 0 and c * row_bytes <= SC_GATHER_BYTES and c <= LANES:
            best = c
    assert best > 0, (per_worker, row_bytes)
    return best


def sc_gather_rows(table, idx):
    _, d = table.shape
    b = idx.shape[0]
    assert b % (8 * SC_WORKERS) == 0 and table.dtype.itemsize == 4
    per_w = b // SC_WORKERS
    chunk = _gather_chunk(per_w, d * 4)
    mesh = plsc.VectorSubcoreMesh(core_axis_name="c", subcore_axis_name="s",
                                  num_cores=SC_CORES, num_subcores=SC_SUBCORES)

    @functools.partial(
        pl.kernel, mesh=mesh,
        out_type=jax.ShapeDtypeStruct((b, d), table.dtype),
        scratch_types=[pltpu.VMEM((chunk,), jnp.int32), pltpu.VMEM((chunk, d), table.dtype),
                       pltpu.SemaphoreType.DMA])
    def gather(table_hbm, idx_hbm, out_hbm, idx_v, rows_v, sem):
        wid = lax.axis_index("s") * SC_CORES + lax.axis_index("c")
        base = wid * per_w

        @pl.loop(0, per_w // chunk)
        def _(c):
            off = pl.multiple_of(base + c * chunk, 8)
            pltpu.sync_copy(idx_hbm.at[pl.ds(off, chunk)], idx_v)
            pltpu.async_copy(table_hbm.at[idx_v], rows_v, sem).wait()
            pltpu.sync_copy(rows_v, out_hbm.at[pl.ds(off, chunk)])

    return gather(table, idx)


def sc_scatter_rows(src, dests, n_out):
    n, d = src.shape
    assert n % (8 * SC_WORKERS) == 0 and src.dtype.itemsize == 4
    per_w = n // SC_WORKERS
    chunk = _gather_chunk(per_w, d * 4)
    n_dest = len(dests)
    mesh = plsc.VectorSubcoreMesh(core_axis_name="c", subcore_axis_name="s",
                                  num_cores=SC_CORES, num_subcores=SC_SUBCORES)

    @functools.partial(
        pl.kernel, mesh=mesh,
        out_type=jax.ShapeDtypeStruct((n_out, d), src.dtype),
        scratch_types=[pltpu.VMEM((chunk,), jnp.int32), pltpu.VMEM((chunk, d), src.dtype),
                       pltpu.SemaphoreType.DMA])
    def scatter(src_hbm, *rest):
        dest_hbms = rest[:n_dest]
        out_hbm, idx_v, rows_v, sem = rest[n_dest:]
        wid = lax.axis_index("s") * SC_CORES + lax.axis_index("c")
        base = wid * per_w

        @pl.loop(0, per_w // chunk)
        def _(c):
            off = pl.multiple_of(base + c * chunk, 8)
            pltpu.sync_copy(src_hbm.at[pl.ds(off, chunk)], rows_v)
            for dest_hbm in dest_hbms:
                pltpu.sync_copy(dest_hbm.at[pl.ds(off, chunk)], idx_v)
                pltpu.async_copy(rows_v, out_hbm.at[idx_v], sem).wait()

    return scatter(src, *dests)


def _expert_kernel(te_ref, ts_ref, tr_ref, xp_hbm, *rest, nf):
    ns = MOE_WSPLIT
    w1_refs, w3_refs, w2_refs = rest[:ns], rest[ns:2 * ns], rest[2 * ns:3 * ns]
    yr_hbm, xb, acc, stage, in_sem, out_sem = rest[3 * ns:]
    t = pl.program_id(0)
    f = pl.program_id(1)
    rows = tr_ref[t]
    start = pl.multiple_of(ts_ref[t], MOE_ALIGN)
    n_chunks = MOE_TILE // MOE_COPY
    half = xb.shape[1] // 2

    def in_copy(ci):
        return pltpu.make_async_copy(xp_hbm.at[pl.ds(start + ci * MOE_COPY, MOE_COPY)],
                                     stage.at[ci % 2], in_sem.at[ci % 2])

    def out_copy(ci, first_row):
        return pltpu.make_async_copy(acc.at[pl.ds(ci * MOE_COPY, MOE_COPY)],
                                     yr_hbm.at[pl.ds(first_row + ci * MOE_COPY, MOE_COPY)], out_sem.at[0])

    def when_chunk_live(ci, fn):
        pl.when(ci * MOE_COPY < rows)(fn)

    def wait_outputs(tile):
        tile_rows = tr_ref[tile]
        first_row = pl.multiple_of(ts_ref[tile], MOE_ALIGN)
        for ci in range(n_chunks):
            pl.when(ci * MOE_COPY < tile_rows)(lambda ci=ci: out_copy(ci, first_row).wait())

    @pl.when((f == 0) & (rows > 0))
    def _():
        def unpack(ci):
            in_copy(ci).wait()
            lo, hi = _unpack_bf16_pairs(stage[ci % 2])
            xb[ci * MOE_COPY:(ci + 1) * MOE_COPY, :half] = lo
            xb[ci * MOE_COPY:(ci + 1) * MOE_COPY, half:] = hi

        when_chunk_live(0, lambda: in_copy(0).start())
        for ci in range(n_chunks):
            if ci + 1 < n_chunks:
                when_chunk_live(ci + 1, lambda ci=ci: in_copy(ci + 1).start())
            when_chunk_live(ci, lambda ci=ci: unpack(ci))

    @pl.when((f == 0) & (t > 0))
    def _():
        wait_outputs(jnp.maximum(t - 1, 0))

    @pl.when((f == 0) & (rows > 0))
    def _():
        acc[...] = jnp.zeros(acc.shape, F32)

    @pl.when(rows > 0)
    def _():
        w1 = jnp.concatenate([r[0, 0].astype(BF16) for r in w1_refs], axis=0)
        w3 = jnp.concatenate([r[0, 0].astype(BF16) for r in w3_refs], axis=0)
        w2 = jnp.concatenate([r[0, 0].astype(BF16) for r in w2_refs], axis=1)

        def chain(r0, size):
            rs = pl.ds(pl.multiple_of(r0, MOE_PIECES[-1]), size)
            xc = xb[rs, :]
            h1 = _dot(xc, w1)
            h3 = _dot(xc, w3)
            hh = (h1 * _sigmoid(h1) * h3).astype(BF16)
            acc[rs, :] += _dot(hh, w2)

        todo = (rows + (MOE_PIECES[-1] - 1)) // MOE_PIECES[-1] * MOE_PIECES[-1]
        off = jnp.int32(0)
        for size in MOE_PIECES:
            take = todo - off >= size

            @pl.when(take)
            def _(off=off, size=size):
                for c0 in range(0, size, MOE_CHAIN):
                    chain(off + c0, min(MOE_CHAIN, size))

            off = off + jnp.where(take, size, 0)

    @pl.when((f == nf - 1) & (rows > 0))
    def _():
        for ci in range(n_chunks):
            when_chunk_live(ci, lambda ci=ci: out_copy(ci, start).start())

    @pl.when((f == nf - 1) & (t == pl.num_programs(0) - 1))
    def _():
        wait_outputs(t)


def moe_experts(xp, tile_e, tile_start, tile_rows, w1, w3, w2, layer, *, tf):
    n_rows, half = xp.shape
    d = 2 * half
    n_tiles = tile_e.shape[0]
    ff = w1.shape[3]
    nf = ff // tf
    assert MOE_TILE % MOE_COPY == 0 and sum(MOE_PIECES) >= MOE_TILE

    def f_eff(t, f, tr):
        return jnp.where(tr[t] > 0, f, nf - 1)

    ns = MOE_WSPLIT
    up_specs = [pl.BlockSpec((1, 1, d // ns, tf), lambda t, f, te, ts, tr, k=k: (layer, te[t], k, f_eff(t, f, tr)))
                for k in range(ns)]
    down_specs = [pl.BlockSpec((1, 1, tf, d // ns), lambda t, f, te, ts, tr, k=k: (layer, te[t], f_eff(t, f, tr), k))
                  for k in range(ns)]
    grid_spec = pltpu.PrefetchScalarGridSpec(
        num_scalar_prefetch=3,
        grid=(n_tiles, nf),
        in_specs=[pl.BlockSpec(memory_space=pl.ANY)] + 2 * up_specs + down_specs,
        out_specs=pl.BlockSpec(memory_space=pl.ANY),
        scratch_shapes=[pltpu.VMEM((MOE_TILE, d), BF16), pltpu.VMEM((MOE_TILE, d), F32),
                        pltpu.VMEM((2, MOE_COPY, half), F32),
                        pltpu.SemaphoreType.DMA((2,)), pltpu.SemaphoreType.DMA((1,))],
    )
    return pl.pallas_call(
        functools.partial(_expert_kernel, nf=nf),
        grid_spec=grid_spec,
        out_shape=jax.ShapeDtypeStruct((n_rows, d), F32),
        compiler_params=_cparams(("arbitrary", "arbitrary")),
    )(tile_e, tile_start, tile_rows, xp, *([w1] * ns), *([w3] * ns), *([w2] * ns))


def _combine_kernel(x_ref, y1_ref, y2_ref, meta_ref, g_ref, o_ref, *, final_norm):
    meta = meta_ref[...]
    y = x_ref[...] + (meta[:, 2:3] * y1_ref[...] + meta[:, 3:4] * y2_ref[...])
    if final_norm:
        y = _rms(y, g_ref[...])
    o_ref[...] = y


def moe_combine(x, yg, meta, g, *, final_norm, tm):
    n, d = x.shape
    nb = n // tm
    row = pl.BlockSpec((tm, d), lambda i: (i, 0))
    return pl.pallas_call(
        functools.partial(_combine_kernel, final_norm=final_norm),
        grid=(nb,),
        in_specs=[row, row, pl.BlockSpec((tm, d), lambda i: (i + nb, 0)),
                  pl.BlockSpec((tm, LANES), lambda i: (i, 0)),
                  pl.BlockSpec((1, d), lambda i: (0, 0))],
        out_specs=row,
        out_shape=jax.ShapeDtypeStruct((n, d), F32),
        compiler_params=_cparams(("parallel",)),
    )(x, yg, yg, meta, g.reshape(1, d))


def _mla_rope_tables(seq):
    half = MLA_ROPE // 2
    pos = jnp.arange(seq, dtype=F32)
    inv = ROPE_THETA ** (-jnp.arange(0, MLA_ROPE, 2, dtype=F32) / MLA_ROPE)
    ang = pos[:, None] * inv[None, :]
    cos, sin = jnp.cos(ang), jnp.sin(ang)
    z = jnp.zeros((seq, half), F32)
    pad = jnp.zeros((seq, LANES - MLA_ROPE), F32)
    c = jnp.concatenate([cos, cos, pad], axis=1)
    s_lo = jnp.concatenate([-sin, z, pad], axis=1)
    s_hi = jnp.concatenate([z, sin, pad], axis=1)
    return c, s_lo, s_hi


def _nsa_rope_tables(seq):
    pos = jnp.arange(seq, dtype=F32)
    inv = ROPE_THETA ** (-jnp.arange(0, NSA_DH, 2, dtype=F32) / NSA_DH)
    ang = pos[:, None] * inv[None, :]
    cos, sin = jnp.cos(ang), jnp.sin(ang)
    return jnp.concatenate([cos, cos], axis=1), jnp.concatenate([-sin, sin], axis=1)


def _selection_constants(seq):
    n_sel = seq // SEL_BLOCK
    nc = (seq - CMP_BLOCK) // CMP_STRIDE + 1
    cmp_start = np.arange(LANES) * CMP_STRIDE
    sel_start = np.arange(LANES) * SEL_BLOCK
    ovl = ((cmp_start[:, None] < sel_start[None, :] + SEL_BLOCK) &
           (cmp_start[:, None] + CMP_BLOCK > sel_start[None, :]))
    ovl &= (np.arange(LANES)[:, None] < nc) & (np.arange(LANES)[None, :] < n_sel)
    expand = (np.arange(seq)[None, :] // SEL_BLOCK == np.arange(LANES)[:, None])
    return jnp.asarray(ovl, BF16), jnp.asarray(expand, BF16)


def even_layer(x, seq, p, stacked):
    n = x.shape[0]
    b = n // seq
    (norm_mix, w_in, q_norm, w_q_up, kv_norm, w_kv_up, conv_w, conv_b, ga_w, ga_b, gx_w, gx_b,
     lam, norm_ffn) = p
    w_out, w1, w3, w2, layer = stacked
    d = D_MODEL
    o1 = MLA_Q_LORA + MLA_KV_LORA
    o2 = o1 + MLA_ROPE
    w_pack = jnp.concatenate(
        [w_in[:, :o1], w_in[:, o2:], w_in[:, o1:o2], jnp.zeros((d, LANES - MLA_ROPE), F32)], axis=1).astype(BF16)
    u = norm_matmul(x, norm_mix, w_pack, tm=1024, tn=EV_PACKED // 3)

    wq = w_q_up.reshape(MLA_Q_LORA, MLA_HEADS, MLA_NOPE + MLA_ROPE)
    wq = jnp.pad(wq, ((0, 0), (0, 0), (0, MLA_QK_PAD - MLA_NOPE - MLA_ROPE)))
    wq = wq.reshape(MLA_Q_LORA, MLA_HEADS * MLA_QK_PAD).astype(BF16)
    wkv = w_kv_up.reshape(MLA_KV_LORA, MLA_HEADS, MLA_NOPE + MLA_V)
    wk = wkv[:, :, :MLA_NOPE].reshape(MLA_KV_LORA, MLA_HEADS * MLA_NOPE).astype(BF16)
    wv = wkv[:, :, MLA_NOPE:].reshape(MLA_KV_LORA, MLA_HEADS * MLA_V).astype(BF16)
    rc, rlo, rhi = _mla_rope_tables(seq)
    q, k, v = mla_up(u, q_norm, kv_norm, wq, wk, wv, rc, rlo, rhi, seq=seq, tm=512)
    o_mla = mla_attention(q.reshape(b, seq, -1), k.reshape(b, seq, -1), v.reshape(b, seq, -1), t=512)

    o_rec = rglru(u.reshape(b, seq, EV_PACKED), conv_w, conv_b, ga_w.astype(BF16), ga_b,
                  gx_w.astype(BF16), gx_b, lam, ts=512)
    x = matmul_residual([o_mla.reshape(n, -1), o_rec.reshape(n, -1)], cast_bf16(w_out, layer), x, tm=512, tn=D_MODEL)
    return ffn_dense(x, norm_ffn, cast_bf16(w1, layer), cast_bf16(w3, layer), cast_bf16(w2, layer), tm=1024, tf=512)


def _moe_dispatch(meta, counts, n):
    e1 = meta[:, 0].astype(jnp.int32)
    e2 = meta[:, 1].astype(jnp.int32)
    pos1 = meta[:, 4].astype(jnp.int32)
    pos2 = meta[:, 5].astype(jnp.int32)
    cnt = counts[0, :N_EXPERTS].astype(jnp.int32)
    span = (cnt + MOE_ALIGN - 1) // MOE_ALIGN * MOE_ALIGN
    row0 = jnp.cumsum(span) - span
    row_unit = 64 * SC_WORKERS
    n_rows = -(-(n * TOP_K + N_EXPERTS * MOE_ALIGN + MOE_TILE) // row_unit) * row_unit
    d1 = row0[e1] + pos1
    d2 = row0[e2] + pos2
    n_tiles = (n * TOP_K) // MOE_TILE + N_EXPERTS
    tiles_e = (cnt + MOE_TILE - 1) // MOE_TILE
    tend = jnp.cumsum(tiles_e)
    tbeg = tend - tiles_e
    tid = jnp.arange(n_tiles, dtype=jnp.int32)
    te = jnp.minimum(jnp.searchsorted(tend, tid, side='right'), N_EXPERTS - 1).astype(jnp.int32)
    used = tid < tend[-1]
    first = (tid - tbeg[te]) * MOE_TILE
    rows = jnp.where(used, jnp.clip(cnt[te] - first, 0, MOE_TILE), 0).astype(jnp.int32)
    tstart = jnp.where(used, row0[te] + first, 0).astype(jnp.int32)
    last_e = te[jnp.maximum(tend[-1] - 1, 0)]
    te = jnp.where(used, te, last_e).astype(jnp.int32)
    return d1, d2, n_rows, te, tstart, rows


def odd_layer(x, seq, p, experts, final_g):
    n = x.shape[0]
    b = n // seq
    (norm_mix, w_in, ck_pe, ck_w1, ck_b1, ck_w2, cv_pe, cv_w1, cv_b1, cv_w2, norm_ffn,
     router_w, router_b) = p
    w_out, ew1, ew3, ew2, layer = experts
    d = D_MODEL
    wg = w_in[:, OD_G:].reshape(d, NSA_GROUPS, NSA_HPG * 3)
    wg = jnp.pad(wg, ((0, 0), (0, 0), (0, LANES - NSA_HPG * 3))).reshape(d, NSA_GROUPS * LANES)
    w_pack = jnp.concatenate([w_in[:, :OD_G], wg], axis=1).astype(BF16)
    u = norm_matmul(x, norm_mix, w_pack, tm=1024, tn=OD_PACKED // 4)
    u3 = u.reshape(b, seq, OD_PACKED)

    k_cmp = nsa_compress(u3, OD_KC, ck_pe, ck_w1.astype(BF16), ck_b1, ck_w2.astype(BF16))
    v_cmp = nsa_compress(u3, OD_VC, cv_pe, cv_w1.astype(BF16), cv_b1, cv_w2.astype(BF16))
    rc, rs = _nsa_rope_tables(seq)
    ovl, expand = _selection_constants(seq)
    o = nsa_attention(u3, k_cmp, v_cmp, rc, rs, ovl, expand, tq=512, tk=512)
    x = matmul_residual([o.reshape(n, -1)], cast_bf16(w_out, layer), x, tm=512, tn=D_MODEL)

    wr = jnp.pad(router_w, ((0, 0), (0, LANES - N_EXPERTS)))
    br = jnp.pad(router_b, (0, LANES - N_EXPERTS)).reshape(1, LANES)
    xp, meta, counts = moe_router(x, norm_ffn, wr, br, tm=512)
    d1, d2, n_rows, te, tstart, rows = _moe_dispatch(meta, counts, n)
    yr = moe_experts(sc_scatter_rows(xp, [d1, d2], n_rows), te, tstart, rows, ew1, ew3, ew2, layer, tf=256)
    yg = sc_gather_rows(yr, jnp.concatenate([d1, d2]))
    g = final_g if final_g is not None else norm_ffn
    return moe_combine(x, yg, meta, g, final_norm=final_g is not None, tm=512)


def kernel(x, ev_norm_mix, ev_w_in, ev_q_norm, ev_w_q_up, ev_kv_norm, ev_w_kv_up, ev_conv_w, ev_conv_b, ev_gate_a_w, ev_gate_a_b, ev_gate_x_w, ev_gate_x_b, ev_lru_lambda, ev_w_out, ev_norm_ffn, ev_ffn_w1, ev_ffn_w3, ev_ffn_w2, od_norm_mix, od_w_in, od_cmp_k_pe, od_cmp_k_w1, od_cmp_k_b1, od_cmp_k_w2, od_cmp_v_pe, od_cmp_v_w1, od_cmp_v_b1, od_cmp_v_w2, od_w_out, od_norm_ffn, od_router_w, od_router_b, od_exp_w1, od_exp_w3, od_exp_w2, final_norm):
    bsz, seq, d = x.shape
    ev = (ev_norm_mix, ev_w_in, ev_q_norm, ev_w_q_up, ev_kv_norm, ev_w_kv_up, ev_conv_w, ev_conv_b,
          ev_gate_a_w, ev_gate_a_b, ev_gate_x_w, ev_gate_x_b, ev_lru_lambda, ev_norm_ffn)
    od = (od_norm_mix, od_w_in, od_cmp_k_pe, od_cmp_k_w1, od_cmp_k_b1, od_cmp_k_w2, od_cmp_v_pe,
          od_cmp_v_w1, od_cmp_v_b1, od_cmp_v_w2, od_norm_ffn, od_router_w, od_router_b)
    h = x.reshape(bsz * seq, d)
    for layer in range(DEPTH):
        i = layer // 2
        if layer % 2 == 0:
            h = even_layer(h, seq, tuple(a[i] for a in ev), (ev_w_out, ev_ffn_w1, ev_ffn_w3, ev_ffn_w2, i))
        else:
            h = odd_layer(h, seq, tuple(a[i] for a in od), (od_w_out, od_exp_w1, od_exp_w3, od_exp_w2, i),
                          final_norm if layer == DEPTH - 1 else None)
    return h.reshape(bsz, seq, d)
```

```python
import functools
import math

import numpy as np
import jax
import jax.numpy as jnp
from jax import lax
from jax.experimental import pallas as pl
from jax.experimental.pallas import tpu as pltpu
from jax.experimental.pallas import tpu_sc as plsc

F32 = jnp.float32
BF16 = jnp.bfloat16

D_MODEL = 2048
DEPTH = 4
RMS_EPS = 1e-6
ROPE_THETA = 10000.0
NEG_INF = -1e30
LOG2E = math.log2(math.e)

MLA_HEADS = 8
MLA_Q_LORA = 768
MLA_KV_LORA = 512
MLA_NOPE = 128
MLA_ROPE = 64
MLA_V = 128
MLA_QK_PAD = 256
MLA_ROW_GROUPS = 2
MLA_HEADS_PER_STEP = 8

LRU_WIDTH = D_MODEL // 2
LRU_BLOCKS = 8
LRU_BLOCK_W = LRU_WIDTH // LRU_BLOCKS
LRU_C = 8.0
CONV_WIDTH = 4

NSA_HEADS = 16
NSA_GROUPS = 4
NSA_HPG = NSA_HEADS // NSA_GROUPS
NSA_DH = D_MODEL // NSA_HEADS
CMP_BLOCK = 32
CMP_STRIDE = 16
SEL_BLOCK = 64
SEL_COUNT = 16
SEL_FORCE = 1e4
WINDOW = 512
NSA_WIN_ROWS = 256
NSA_SEL_ROWS = 512
NSA_Q_DIM = NSA_HEADS * NSA_DH
NSA_KV_DIM = NSA_GROUPS * NSA_DH

DENSE_FF = 5632
N_EXPERTS = 8
TOP_K = 2
EXPERT_FF = 7168

LANES = 128
SC_CORES = 2
SC_SUBCORES = 16
SC_WORKERS = SC_CORES * SC_SUBCORES
SC_GATHER_BYTES = 256 * 1024
VMEM_LIMIT = 60 * 1024 * 1024
CAST_BLOCK_BYTES = 4 * 1024 * 1024

EV_CQ = 0
EV_CKV = MLA_Q_LORA
EV_REC = MLA_Q_LORA + MLA_KV_LORA
EV_GATE = EV_REC + LRU_WIDTH
EV_PE = EV_GATE + LRU_WIDTH
EV_PACKED = EV_PE + LANES

OD_Q = 0
OD_KC = NSA_Q_DIM
OD_VC = OD_KC + NSA_KV_DIM
OD_KS = OD_VC + NSA_KV_DIM
OD_VS = OD_KS + NSA_KV_DIM
OD_KW = OD_VS + NSA_KV_DIM
OD_VW = OD_KW + NSA_KV_DIM
OD_G = OD_VW + NSA_KV_DIM
OD_PACKED = OD_G + NSA_GROUPS * LANES

MOE_TILE = 2688
MOE_ALIGN = 16
MOE_COPY = 384
MOE_PIECES = (1024, 1024, 512, 256, 128)
MOE_CHAIN = 512
MOE_WSPLIT = 1
MOE_WEIGHT_BUFFERS = 2


def _cparams(sem):
    return pltpu.CompilerParams(dimension_semantics=sem, vmem_limit_bytes=VMEM_LIMIT)


def _rms(x, g):
    ms = jnp.mean(x * x, axis=-1, keepdims=True)
    return x * lax.rsqrt(ms + RMS_EPS) * g


def _sigmoid(x):
    return 1.0 / (1.0 + jnp.exp(-x))


def _gelu_tanh(x):
    return 0.5 * x * (1.0 + jnp.tanh(math.sqrt(2.0 / math.pi) * (x + 0.044715 * (x * x * x))))


def _dot(a, b):
    return jnp.dot(a, b, preferred_element_type=F32)


def _dot_nt(a, b):
    return lax.dot_general(a, b, (((1,), (1,)), ((), ())), preferred_element_type=F32)


def _norm_mm_kernel(x_ref, g_ref, w_ref, o_ref, xn_ref):
    @pl.when(pl.program_id(1) == 0)
    def _():
        xn_ref[...] = _rms(x_ref[...], g_ref[...]).astype(BF16)

    o_ref[...] = _dot(xn_ref[...], w_ref[...]).astype(o_ref.dtype)


def norm_matmul(x, g, w, *, tm, tn, out_dtype=F32):
    n, k = x.shape
    m = w.shape[1]
    assert n % tm == 0 and m % tn == 0
    return pl.pallas_call(
        _norm_mm_kernel,
        grid=(n // tm, m // tn),
        in_specs=[pl.BlockSpec((tm, k), lambda i, j: (i, 0)),
                  pl.BlockSpec((1, k), lambda i, j: (0, 0)),
                  pl.BlockSpec((k, tn), lambda i, j: (0, j))],
        out_specs=pl.BlockSpec((tm, tn), lambda i, j: (i, j)),
        out_shape=jax.ShapeDtypeStruct((n, m), out_dtype),
        scratch_shapes=[pltpu.VMEM((tm, k), BF16)],
        compiler_params=_cparams(("parallel", "arbitrary")),
    )(x, g.reshape(1, k), w)


def _cast_kernel(w_ref, o_ref):
    o_ref[...] = w_ref[0].astype(o_ref.dtype)


def cast_bf16(ws, layer):
    _, r, c = ws.shape
    rb = max(rb for rb in range(16, r + 1, 16) if r % rb == 0 and rb * c * 4 <= CAST_BLOCK_BYTES)
    return pl.pallas_call(
        _cast_kernel,
        grid=(r // rb,),
        in_specs=[pl.BlockSpec((1, rb, c), lambda i: (layer, i, 0))],
        out_specs=pl.BlockSpec((rb, c), lambda i: (i, 0)),
        out_shape=jax.ShapeDtypeStruct((r, c), BF16),
        compiler_params=_cparams(("parallel",)),
    )(ws)


def _mm_res_kernel(*refs, n_in):
    xs = refs[:n_in]
    ws = refs[n_in:2 * n_in]
    res_ref = refs[2 * n_in]
    o_ref = refs[2 * n_in + 1]
    acc = res_ref[...]
    for x_ref, w_ref in zip(xs, ws):
        acc = acc + _dot(x_ref[...], w_ref[...])
    o_ref[...] = acc


def matmul_residual(xs, w, res, *, tm, tn):
    n = res.shape[0]
    m = w.shape[1]
    n_in = len(xs)
    in_specs = [pl.BlockSpec((tm, x.shape[1]), lambda i, j: (i, 0)) for x in xs]
    row = 0
    for x in xs:
        kx = x.shape[1]
        assert row % kx == 0
        in_specs.append(pl.BlockSpec((kx, tn), lambda i, j, rb=row // kx: (rb, j)))
        row += kx
    assert row == w.shape[0]
    in_specs.append(pl.BlockSpec((tm, tn), lambda i, j: (i, j)))
    return pl.pallas_call(
        functools.partial(_mm_res_kernel, n_in=n_in),
        grid=(n // tm, m // tn),
        in_specs=in_specs,
        out_specs=pl.BlockSpec((tm, tn), lambda i, j: (i, j)),
        out_shape=jax.ShapeDtypeStruct((n, m), F32),
        compiler_params=_cparams(("parallel", "arbitrary")),
    )(*xs, *([w] * n_in), res)


def _rope64(x, c, s_lo, s_hi):
    return x * c + pltpu.roll(x, 96, 1) * s_lo + pltpu.roll(x, 32, 1) * s_hi


def _mla_up_kernel(u_ref, pe_ref, qg_ref, kvg_ref, wq_ref, wk_ref, wv_ref,
                   c_ref, slo_ref, shi_ref, q_ref, k_ref, v_ref):
    u = u_ref[...]
    c, s_lo, s_hi = c_ref[...], slo_ref[...], shi_ref[...]
    qn = _rms(u[:, EV_CQ:EV_CQ + MLA_Q_LORA], qg_ref[...]).astype(BF16)
    kvn = _rms(u[:, EV_CKV:EV_CKV + MLA_KV_LORA], kvg_ref[...]).astype(BF16)
    q = _dot(qn, wq_ref[...]) * ((MLA_NOPE + MLA_ROPE) ** -0.5 * LOG2E)
    kn = _dot(kvn, wk_ref[...])
    v_ref[...] = _dot(kvn, wv_ref[...]).astype(v_ref.dtype)
    kpe = _rope64(pe_ref[...], c, s_lo, s_hi).astype(k_ref.dtype)
    for h in range(MLA_HEADS):
        a = h * MLA_QK_PAD
        q_ref[:, a:a + LANES] = q[:, a:a + LANES].astype(q_ref.dtype)
        q_ref[:, a + LANES:a + 2 * LANES] = _rope64(q[:, a + LANES:a + 2 * LANES], c, s_lo, s_hi).astype(q_ref.dtype)
        k_ref[:, a:a + LANES] = kn[:, h * LANES:(h + 1) * LANES].astype(k_ref.dtype)
        k_ref[:, a + LANES:a + 2 * LANES] = kpe


def mla_up(u, q_norm, kv_norm, wq, wk, wv, rope_c, rope_slo, rope_shi, *, seq, tm):
    n = u.shape[0]
    hq = MLA_HEADS * MLA_QK_PAD
    hv = MLA_HEADS * MLA_V
    ab = EV_REC
    assert seq % tm == 0 and EV_PE % LANES == 0
    nsb = seq // tm
    row_spec = pl.BlockSpec((tm, LANES), lambda i: (i % nsb, 0))
    full = lambda a: pl.BlockSpec(a.shape, lambda i: (0,) * a.ndim)
    qg = q_norm.reshape(1, -1)
    kvg = kv_norm.reshape(1, -1)
    return pl.pallas_call(
        _mla_up_kernel,
        grid=(n // tm,),
        in_specs=[pl.BlockSpec((tm, ab), lambda i: (i, 0)),
                  pl.BlockSpec((tm, LANES), lambda i: (i, EV_PE // LANES)),
                  full(qg), full(kvg), full(wq), full(wk), full(wv),
                  row_spec, row_spec, row_spec],
        out_specs=[pl.BlockSpec((tm, hq), lambda i: (i, 0)),
                   pl.BlockSpec((tm, hq), lambda i: (i, 0)),
                   pl.BlockSpec((tm, hv), lambda i: (i, 0))],
        out_shape=[jax.ShapeDtypeStruct((n, hq), BF16),
                   jax.ShapeDtypeStruct((n, hq), BF16),
                   jax.ShapeDtypeStruct((n, hv), BF16)],
        compiler_params=_cparams(("parallel",)),
    )(u, u, qg, kvg, wq, wk, wv, rope_c, rope_slo, rope_shi)


def _values_and_ones(v):
    return jnp.concatenate([v.astype(BF16), jnp.ones(v.shape, BF16)], axis=1)


def _lane_tile(x, n):
    return jnp.concatenate([x] * n, axis=1)


def _softmax_step(s, v1, m_ref, acc_ref):
    m_prev = m_ref[...]
    m_new = jnp.maximum(m_prev, jnp.max(s, axis=-1, keepdims=True))
    alpha = jnp.exp2(m_prev - m_new)
    p = jnp.exp2(s - _lane_tile(m_new, s.shape[1] // LANES))
    acc_ref[...] = _lane_tile(alpha, 2) * acc_ref[...] + _dot(p.astype(BF16), v1)
    m_ref[...] = m_new


def _softmax_result(acc):
    return acc[:, :LANES] / acc[:, LANES:]


def _mla_attn_kernel(q_ref, k_ref, v_ref, cb_ref, o_ref, v1, m_ref, acc_ref, *, t, n_chunks):
    qi = pl.program_id(2)

    hs = MLA_HEADS_PER_STEP

    @pl.when(qi == 0)
    def _():
        for h in range(hs):
            v1[h] = _values_and_ones(v_ref[0, :, h * MLA_V:(h + 1) * MLA_V])

    m_ref[...] = jnp.full(m_ref.shape, NEG_INF, F32)
    acc_ref[...] = jnp.zeros(acc_ref.shape, F32)
    rg = t // MLA_ROW_GROUPS

    def chunk(c, masked):
        for h in range(hs):
            for i in range(MLA_ROW_GROUPS):
                rows = pl.ds(i * rg, rg)
                qh = q_ref[0, i * rg:(i + 1) * rg, h * MLA_QK_PAD:(h + 1) * MLA_QK_PAD]
                s = _dot_nt(qh, k_ref[0, c * t:(c + 1) * t, h * MLA_QK_PAD:(h + 1) * MLA_QK_PAD])
                if masked:
                    s = s + cb_ref[i * rg:(i + 1) * rg, :]
                _softmax_step(s, v1[h, c * t:(c + 1) * t, :], m_ref.at[h, rows], acc_ref.at[h, rows])

    for c in range(n_chunks):
        pl.when(c < qi)(functools.partial(chunk, c, False))
        pl.when(c == qi)(functools.partial(chunk, c, True))
    for h in range(hs):
        o_ref[0, :, h * MLA_V:(h + 1) * MLA_V] = _softmax_result(acc_ref[h]).astype(o_ref.dtype)


def mla_attention(q, k, v, *, t):
    b, s, _ = q.shape
    causal = jnp.asarray(np.where(np.arange(t)[None, :] <= np.arange(t)[:, None], 0.0, NEG_INF), F32)
    kern = functools.partial(_mla_attn_kernel, t=t, n_chunks=s // t)
    hs = MLA_HEADS_PER_STEP
    return pl.pallas_call(
        kern,
        grid=(b, MLA_HEADS // hs, s // t),
        in_specs=[pl.BlockSpec((1, t, hs * MLA_QK_PAD), lambda b_, h, i: (b_, i, h)),
                  pl.BlockSpec((1, s, hs * MLA_QK_PAD), lambda b_, h, i: (b_, 0, h)),
                  pl.BlockSpec((1, s, hs * MLA_V), lambda b_, h, i: (b_, 0, h)),
                  pl.BlockSpec((t, t), lambda b_, h, i: (0, 0))],
        out_specs=pl.BlockSpec((1, t, hs * MLA_V), lambda b_, h, i: (b_, i, h)),
        out_shape=jax.ShapeDtypeStruct((b, s, MLA_HEADS * MLA_V), BF16),
        scratch_shapes=[pltpu.VMEM((hs, s, 2 * LANES), BF16), pltpu.VMEM((hs, t, LANES), F32),
                        pltpu.VMEM((hs, t, 2 * LANES), F32)],
        compiler_params=_cparams(("parallel", "parallel", "arbitrary")),
    )(q, k, v, causal)


def _rglru_kernel(x_ref, y_ref, cw_ref, cb_ref, gaw_ref, gab_ref, gxw_ref, gxb_ref, lam_ref,
                  o_ref, xbuf, h_ref, *, ts):
    t = pl.program_id(2)

    @pl.when(t == 0)
    def _():
        xbuf[0:8, :] = jnp.zeros((8, LANES), F32)
        h_ref[...] = jnp.zeros(h_ref.shape, F32)

    x = x_ref[0]
    xbuf[8:, :] = x
    cw = cw_ref[...]
    xc = cb_ref[...] + cw[3:4] * x
    for kk in range(CONV_WIDTH - 1):
        back = CONV_WIDTH - 1 - kk
        xc = xc + cw[kk:kk + 1] * xbuf[8 - back:8 - back + ts, :]
    xbuf[0:8, :] = x[ts - 8:, :]

    xcb = xc.astype(BF16)
    r = _sigmoid(_dot(xcb, gaw_ref[0]) + gab_ref[0])
    gi = _sigmoid(_dot(xcb, gxw_ref[0]) + gxb_ref[0])
    z = -lam_ref[...]
    softplus = jnp.maximum(z, 0.0) + jnp.log1p(jnp.exp(-jnp.abs(z)))
    log_a = (-LRU_C) * r * softplus
    a = jnp.exp(log_a)
    mult = jnp.sqrt(-jnp.tanh(log_a) * (a * a + 1.0))
    row = lax.broadcasted_iota(jnp.int32, (ts, 1), 0)
    mult = jnp.where(row + t * ts == 0, 1.0, mult)
    bv = mult * gi * xc

    d = 1
    while d < ts:
        keep = row >= d
        a_sh = jnp.where(keep, pltpu.roll(a, d, 0), 1.0)
        b_sh = jnp.where(keep, pltpu.roll(bv, d, 0), 0.0)
        bv = a * b_sh + bv
        a = a * a_sh
        d *= 2
    h = bv + a * h_ref[...]
    h_ref[...] = h[ts - 1:ts, :]
    o_ref[0] = (h * _gelu_tanh(y_ref[0])).astype(o_ref.dtype)


def rglru(u3, conv_w, conv_b, ga_w, ga_b, gx_w, gx_b, lam, *, ts):
    b, s, _ = u3.shape
    rec0 = EV_REC // LANES
    gate0 = EV_GATE // LANES
    cb = conv_b.reshape(1, LRU_WIDTH)
    gab = ga_b.reshape(LRU_BLOCKS, 1, LRU_BLOCK_W)
    gxb = gx_b.reshape(LRU_BLOCKS, 1, LRU_BLOCK_W)
    lam2 = lam.reshape(1, LRU_WIDTH)
    blk_w = pl.BlockSpec((1, LRU_BLOCK_W, LRU_BLOCK_W), lambda b_, n, t: (n, 0, 0))
    blk_b = pl.BlockSpec((1, 1, LRU_BLOCK_W), lambda b_, n, t: (n, 0, 0))
    vec = pl.BlockSpec((1, LANES), lambda b_, n, t: (0, n))
    return pl.pallas_call(
        functools.partial(_rglru_kernel, ts=ts),
        grid=(b, LRU_BLOCKS, s // ts),
        in_specs=[pl.BlockSpec((1, ts, LANES), lambda b_, n, t: (b_, t, rec0 + n)),
                  pl.BlockSpec((1, ts, LANES), lambda b_, n, t: (b_, t, gate0 + n)),
                  pl.BlockSpec((CONV_WIDTH, LANES), lambda b_, n, t: (0, n)),
                  vec, blk_w, blk_b, blk_w, blk_b, vec],
        out_specs=pl.BlockSpec((1, ts, LANES), lambda b_, n, t: (b_, t, n)),
        out_shape=jax.ShapeDtypeStruct((b, s, LRU_WIDTH), BF16),
        scratch_shapes=[pltpu.VMEM((ts + 8, LANES), F32), pltpu.VMEM((1, LANES), F32)],
        compiler_params=_cparams(("parallel", "parallel", "arbitrary")),
    )(u3, u3, conv_w, cb, ga_w, gab, gx_w, gxb, lam2)


def _ffn_kernel(x_ref, g_ref, w1_ref, w3_ref, w2_ref, o_ref, xn_ref):
    f = pl.program_id(1)

    @pl.when(f == 0)
    def _():
        x = x_ref[...]
        xn_ref[...] = _rms(x, g_ref[...]).astype(BF16)
        o_ref[...] = x

    xn = xn_ref[...]
    h1 = _dot(xn, w1_ref[...])
    h3 = _dot(xn, w3_ref[...])
    hh = (h1 * _sigmoid(h1) * h3).astype(BF16)
    o_ref[...] += _dot(hh, w2_ref[...])


def ffn_dense(x, g, w1, w3, w2, *, tm, tf):
    n, d = x.shape
    ff = w1.shape[1]
    assert n % tm == 0 and ff % tf == 0
    return pl.pallas_call(
        _ffn_kernel,
        grid=(n // tm, ff // tf),
        in_specs=[pl.BlockSpec((tm, d), lambda i, f: (i, 0)),
                  pl.BlockSpec((1, d), lambda i, f: (0, 0)),
                  pl.BlockSpec((d, tf), lambda i, f: (0, f)),
                  pl.BlockSpec((d, tf), lambda i, f: (0, f)),
                  pl.BlockSpec((tf, d), lambda i, f: (f, 0))],
        out_specs=pl.BlockSpec((tm, d), lambda i, f: (i, 0)),
        out_shape=jax.ShapeDtypeStruct((n, d), F32),
        scratch_shapes=[pltpu.VMEM((tm, d), BF16)],
        compiler_params=_cparams(("parallel", "arbitrary")),
    )(x, g.reshape(1, d), w1, w3, w2)


def _compress_kernel(*refs, n_half):
    kc_refs = refs[:NSA_GROUPS]
    pe_ref, w1_ref, b1_ref, w2_ref, o_ref = refs[NSA_GROUPS:]
    pe = pe_ref[...]
    half_k = n_half * NSA_DH
    nchunk = o_ref.shape[1]
    for g in range(NSA_GROUPS):
        lo, hi = [], []
        for l in range(n_half):
            piece = kc_refs[g][0, pl.ds(l, nchunk, stride=n_half), :]
            lo.append((piece + pe[l:l + 1]).astype(BF16))
            hi.append((piece + pe[n_half + l:n_half + l + 1]).astype(BF16))
        z0 = _dot(jnp.concatenate(lo, axis=1), w1_ref[0:half_k, :])
        z1 = _dot(jnp.concatenate(hi, axis=1), w1_ref[half_k:2 * half_k, :])
        rows = z1.shape[0]
        pre = z0 + pltpu.roll(z1, rows - 1, 0) + b1_ref[...]
        o_ref[0, :, g * NSA_DH:(g + 1) * NSA_DH] = _dot(_gelu_tanh(pre).astype(BF16), w2_ref[...]).astype(o_ref.dtype)


def nsa_compress(u3, col, pe, w1, b1, w2):
    b, s, _ = u3.shape
    nchunk = s // CMP_STRIDE
    assert CMP_BLOCK == 2 * CMP_STRIDE and col % NSA_KV_DIM == 0
    full = lambda a: pl.BlockSpec(a.shape, lambda i: (0,) * a.ndim)
    b1r = b1.reshape(1, NSA_DH)
    return pl.pallas_call(
        functools.partial(_compress_kernel, n_half=CMP_STRIDE),
        grid=(b,),
        in_specs=[pl.BlockSpec((1, s, NSA_DH), lambda i, g=g: (i, 0, col // NSA_DH + g)) for g in range(NSA_GROUPS)]
        + [full(pe), full(w1), full(b1r), full(w2)],
        out_specs=pl.BlockSpec((1, nchunk, NSA_KV_DIM), lambda i: (i, 0, 0)),
        out_shape=jax.ShapeDtypeStruct((b, nchunk, NSA_KV_DIM), BF16),
        compiler_params=_cparams(("parallel",)),
    )(*([u3] * NSA_GROUPS), pe, w1, b1r, w2)


def _rope128(x, c, s):
    return x * c + pltpu.roll(x, NSA_DH // 2, 1) * s


def _pack_bf16_pairs(x):
    w = x.shape[1] // 2
    bits = pltpu.bitcast(x.astype(BF16).astype(F32), jnp.uint32)
    word = bits[:, w:] | lax.shift_right_logical(bits[:, :w], jnp.uint32(16))
    return pltpu.bitcast(word, F32)


def _unpack_bf16_pairs(word):
    bits = pltpu.bitcast(word, jnp.uint32)
    lo = pltpu.bitcast(lax.shift_left(bits, jnp.uint32(16)), F32).astype(BF16)
    hi = pltpu.bitcast(bits & jnp.uint32(0xFFFF0000), F32).astype(BF16)
    return lo, hi


def _split3(x):
    hi = x.astype(BF16)
    r1 = x - hi.astype(F32)
    mid = r1.astype(BF16)
    lo = (r1 - mid.astype(F32)).astype(BF16)
    return hi, mid, lo


def _nsa_attn_kernel(q_ref, ks_ref, vs_ref, kw_ref, vw_ref, gt_ref, kc_ref, vc_ref,
                     cq_ref, sq_ref, ck_ref, sk_ref, ovl_ref, exp_ref, wb_ref,
                     o_ref, ksr, vsb, kwr, vwb, selb, m_ref, acc_ref,
                     *, tq, tk, seq, scale):
    qi = pl.program_id(2)
    hp = NSA_HPG
    n_sel = seq // SEL_BLOCK

    @pl.when(qi == 0)
    def _():
        ck, sk = ck_ref[...], sk_ref[...]
        ksr[...] = _rope128(ks_ref[0], ck, sk).astype(BF16)
        kwr[...] = _rope128(kw_ref[0], ck, sk).astype(BF16)
        vsb[...] = _values_and_ones(vs_ref[0])
        vwb[...] = _values_and_ones(vw_ref[0])

    q = q_ref[0] * (scale * LOG2E)
    cq, sq = cq_ref[...], sq_ref[...]
    heads = [q[:, p * NSA_DH:(p + 1) * NSA_DH] for p in range(hp)]
    qu = jnp.concatenate(heads, axis=0).astype(BF16)
    qr = jnp.concatenate([_rope128(h, cq, sq) for h in heads], axis=0).astype(BF16)
    t_row = qi * tq + lax.broadcasted_iota(jnp.int32, (tq, 1), 0)
    lane = lax.broadcasted_iota(jnp.int32, (1, LANES), 1)

    sc = _dot_nt(qu, kc_ref[0]).reshape(hp, tq, LANES)
    valid = (lane * CMP_STRIDE + (CMP_BLOCK - 1) <= t_row)[None]
    sm = jnp.where(valid, sc, NEG_INF)
    e = jnp.exp2(sm - jnp.max(sm, axis=-1, keepdims=True))
    p = jnp.where(valid, e / jnp.sum(e, axis=-1, keepdims=True), 0.0)
    o_cmp = _dot(p.reshape(hp * tq, LANES).astype(BF16), vc_ref[0])

    psum = p[0]
    for i in range(1, hp):
        psum = psum + p[i]
    ovl = ovl_ref[...]
    imp = sum(_dot(part, ovl) for part in _split3(psum))
    cur = jnp.right_shift(t_row, SEL_BLOCK.bit_length() - 1)
    future = lane > cur
    forced = (lane == 0) | (lane == cur) | (lane == cur - 1)
    score = jnp.where(future, -1.0, jnp.where(forced, SEL_FORCE, imp))
    sc_t = score.T[0:n_sel, :]
    blk = lax.broadcasted_iota(jnp.int32, (n_sel, 1), 0)
    cnt = jnp.zeros((n_sel, tq), F32)
    for j in range(n_sel):
        other = sc_t[j:j + 1, :]
        beats = (other > sc_t) | ((other == sc_t) & (blk > j))
        cnt = cnt + jnp.where(beats, 1.0, 0.0)
    sel_t = jnp.where(cnt < SEL_COUNT, 1.0, 0.0)
    sel = jnp.concatenate([sel_t, jnp.zeros((LANES - n_sel, tq), F32)], axis=0).T.astype(BF16)
    picked = _dot(sel, exp_ref[...])
    kall = lax.broadcasted_iota(jnp.int32, (1, seq), 1)
    selb[...] = jnp.where((picked > 0.5) & (kall <= t_row), 0.0, NEG_INF)

    m_ref[...] = jnp.full(m_ref.shape, NEG_INF, F32)
    acc_ref[...] = jnp.zeros(acc_ref.shape, F32)
    for c in range(seq // tk):
        @pl.when(c * tk <= qi * tq + (tq - 1))
        def _():
            for i in range(hp):
                for r0 in range(0, tq, NSA_SEL_ROWS):
                    rows = pl.ds(i * tq + r0, NSA_SEL_ROWS)
                    s = _dot_nt(qr[i * tq + r0:i * tq + r0 + NSA_SEL_ROWS], ksr[c * tk:(c + 1) * tk, :])
                    _softmax_step(s + selb[r0:r0 + NSA_SEL_ROWS, c * tk:(c + 1) * tk], vsb[c * tk:(c + 1) * tk, :],
                                  m_ref.at[rows], acc_ref.at[rows])
    o_sel = _softmax_result(acc_ref[...])

    span = WINDOW + NSA_WIN_ROWS
    wins = []
    for i in range(hp):
        for r in range(tq // NSA_WIN_ROWS):
            group = qi * (tq // NSA_WIN_ROWS) + r
            start = pl.multiple_of(jnp.maximum(group * NSA_WIN_ROWS - WINDOW, 0), NSA_WIN_ROWS)
            bias = wb_ref[jnp.minimum(group, WINDOW // NSA_WIN_ROWS)]
            r0 = i * tq + r * NSA_WIN_ROWS
            sw = _dot_nt(qr[r0:r0 + NSA_WIN_ROWS], kwr[pl.ds(start, span), :]) + bias
            ew = jnp.exp2(sw - jnp.max(sw, axis=-1, keepdims=True)).astype(BF16)
            wins.append(_softmax_result(_dot(ew, vwb[pl.ds(start, span), :])))
    o_win = jnp.concatenate(wins, axis=0)

    gates = _sigmoid(gt_ref[0])
    for i in range(hp):
        rows = slice(i * tq, (i + 1) * tq)
        o = (gates[:, 3 * i:3 * i + 1] * o_cmp[rows] + gates[:, 3 * i + 1:3 * i + 2] * o_sel[rows]
             + gates[:, 3 * i + 2:3 * i + 3] * o_win[rows])
        o_ref[0, :, i * NSA_DH:(i + 1) * NSA_DH] = o.astype(o_ref.dtype)


def _window_bias(tq):
    span = WINDOW + tq
    out = []
    for qi in range(WINDOW // tq + 1):
        start = max(qi * tq - WINDOW, 0)
        t = qi * tq + np.arange(tq)[:, None]
        kpos = start + np.arange(span)[None, :]
        out.append(np.where((kpos <= t) & (kpos > t - WINDOW), 0.0, NEG_INF))
    return jnp.asarray(np.stack(out), F32)


def nsa_attention(u3, k_cmp, v_cmp, rope_c, rope_s, ovl, expand, *, tq, tk):
    b, s, _ = u3.shape
    assert s // CMP_STRIDE == LANES and tq % SEL_BLOCK == 0 and tq % NSA_WIN_ROWS == 0 and WINDOW % NSA_WIN_ROWS == 0
    hp = NSA_HPG
    wbias = _window_bias(NSA_WIN_ROWS)
    col = lambda off: (lambda b_, g, i: (b_, 0, off // NSA_DH + g))
    seq_blk = lambda off: pl.BlockSpec((1, s, NSA_DH), col(off))
    full = lambda a: pl.BlockSpec(a.shape, lambda b_, g, i: (0,) * a.ndim)
    cmp_blk = pl.BlockSpec((1, LANES, NSA_DH), lambda b_, g, i: (b_, 0, g))
    rope_q = pl.BlockSpec((tq, NSA_DH), lambda b_, g, i: (i, 0))
    kern = functools.partial(_nsa_attn_kernel, tq=tq, tk=tk, seq=s, scale=NSA_DH ** -0.5)
    return pl.pallas_call(
        kern,
        grid=(b, NSA_GROUPS, s // tq),
        in_specs=[pl.BlockSpec((1, tq, hp * NSA_DH), lambda b_, g, i: (b_, i, g)),
                  seq_blk(OD_KS), seq_blk(OD_VS), seq_blk(OD_KW), seq_blk(OD_VW),
                  pl.BlockSpec((1, tq, LANES), lambda b_, g, i: (b_, i, OD_G // LANES + g)),
                  cmp_blk, cmp_blk, rope_q, rope_q, full(rope_c), full(rope_s), full(ovl), full(expand),
                  full(wbias)],
        out_specs=pl.BlockSpec((1, tq, hp * NSA_DH), lambda b_, g, i: (b_, i, g)),
        out_shape=jax.ShapeDtypeStruct((b, s, NSA_Q_DIM), BF16),
        scratch_shapes=[pltpu.VMEM((s, NSA_DH), BF16), pltpu.VMEM((s, 2 * LANES), BF16)] * 2 + [
            pltpu.VMEM((tq, s), F32),
            pltpu.VMEM((hp * tq, LANES), F32), pltpu.VMEM((hp * tq, 2 * LANES), F32)],
        compiler_params=_cparams(("parallel", "parallel", "arbitrary")),
    )(u3, u3, u3, u3, u3, u3, k_cmp, v_cmp, rope_c, rope_s, rope_c, rope_s, ovl, expand, wbias)


def _router_kernel(x_ref, g_ref, wr_ref, br_ref, xn_ref, meta_ref, cnt_ref, *, tm):
    @pl.when(pl.program_id(0) == 0)
    def _():
        cnt_ref[...] = jnp.zeros(cnt_ref.shape, F32)

    xn = _rms(x_ref[...], g_ref[...])
    xn_ref[...] = _pack_bf16_pairs(xn)
    xh, xm, _ = _split3(xn)
    wh, wm, _ = _split3(wr_ref[...])
    logits = _dot(xh, wh) + _dot(xh, wm) + _dot(xm, wh) + br_ref[...]
    lane = lax.broadcasted_iota(jnp.int32, (1, LANES), 1).astype(F32)
    lg = jnp.where(lane < N_EXPERTS, logits, NEG_INF)
    m1 = jnp.max(lg, axis=-1, keepdims=True)
    e1 = jnp.min(jnp.where(lg == m1, lane, float(LANES)), axis=-1, keepdims=True)
    lg2 = jnp.where(lane == e1, NEG_INF, lg)
    m2 = jnp.max(lg2, axis=-1, keepdims=True)
    e2 = jnp.min(jnp.where(lg2 == m2, lane, float(LANES)), axis=-1, keepdims=True)
    ex = jnp.exp(m2 - m1)
    den = 1.0 + ex
    g1 = 1.0 / den
    g2 = ex / den
    oh = jnp.where((lane == e1) | (lane == e2), 1.0, 0.0)
    r = lax.broadcasted_iota(jnp.int32, (tm, tm), 0)
    c = lax.broadcasted_iota(jnp.int32, (tm, tm), 1)
    tri = jnp.where(r > c, 1.0, 0.0).astype(BF16)
    cum = _dot(tri, oh.astype(BF16)) + cnt_ref[0:1, :]
    pos1 = jnp.sum(jnp.where(lane == e1, cum, 0.0), axis=-1, keepdims=True)
    pos2 = jnp.sum(jnp.where(lane == e2, cum, 0.0), axis=-1, keepdims=True)
    cnt_ref[...] = cnt_ref[...] + jnp.sum(oh, axis=0, keepdims=True)
    meta = jnp.where(lane == 0, e1, 0.0)
    meta = jnp.where(lane == 1, e2, meta)
    meta = jnp.where(lane == 2, g1, meta)
    meta = jnp.where(lane == 3, g2, meta)
    meta = jnp.where(lane == 4, pos1, meta)
    meta = jnp.where(lane == 5, pos2, meta)
    meta_ref[...] = meta


def moe_router(x, g, wr, br, *, tm):
    n, d = x.shape
    return pl.pallas_call(
        functools.partial(_router_kernel, tm=tm),
        grid=(n // tm,),
        in_specs=[pl.BlockSpec((tm, d), lambda i: (i, 0)),
                  pl.BlockSpec((1, d), lambda i: (0, 0)),
                  pl.BlockSpec((d, LANES), lambda i: (0, 0)),
                  pl.BlockSpec((1, LANES), lambda i: (0, 0))],
        out_specs=[pl.BlockSpec((tm, d // 2), lambda i: (i, 0)),
                   pl.BlockSpec((tm, LANES), lambda i: (i, 0)),
                   pl.BlockSpec((8, LANES), lambda i: (0, 0))],
        out_shape=[jax.ShapeDtypeStruct((n, d // 2), F32),
                   jax.ShapeDtypeStruct((n, LANES), F32),
                   jax.ShapeDtypeStruct((8, LANES), F32)],
        compiler_params=_cparams(("arbitrary",)),
    )(x, g.reshape(1, d), wr, br)


def _gather_chunk(per_worker, row_bytes):
    best = 0
    for c in range(8, per_worker + 1, 8):
        if per_worker % c == 0 and c * row_bytes <= SC_GATHER_BYTES and c <= LANES:
            best = c
    assert best > 0, (per_worker, row_bytes)
    return best


def sc_gather_rows(table, idx):
    _, d = table.shape
    b = idx.shape[0]
    assert b % (8 * SC_WORKERS) == 0 and table.dtype.itemsize == 4
    per_w = b // SC_WORKERS
    chunk = _gather_chunk(per_w, d * 4)
    mesh = plsc.VectorSubcoreMesh(core_axis_name="c", subcore_axis_name="s",
                                  num_cores=SC_CORES, num_subcores=SC_SUBCORES)

    @functools.partial(
        pl.kernel, mesh=mesh,
        out_type=jax.ShapeDtypeStruct((b, d), table.dtype),
        scratch_types=[pltpu.VMEM((chunk,), jnp.int32), pltpu.VMEM((chunk, d), table.dtype),
                       pltpu.SemaphoreType.DMA])
    def gather(table_hbm, idx_hbm, out_hbm, idx_v, rows_v, sem):
        wid = lax.axis_index("s") * SC_CORES + lax.axis_index("c")
        base = wid * per_w

        @pl.loop(0, per_w // chunk)
        def _(c):
            off = pl.multiple_of(base + c * chunk, 8)
            pltpu.sync_copy(idx_hbm.at[pl.ds(off, chunk)], idx_v)
            pltpu.async_copy(table_hbm.at[idx_v], rows_v, sem).wait()
            pltpu.sync_copy(rows_v, out_hbm.at[pl.ds(off, chunk)])

    return gather(table, idx)


def sc_scatter_rows(src, dests, n_out):
    n, d = src.shape
    assert n % (8 * SC_WORKERS) == 0 and src.dtype.itemsize == 4
    per_w = n // SC_WORKERS
    chunk = _gather_chunk(per_w, d * 4)
    n_dest = len(dests)
    mesh = plsc.VectorSubcoreMesh(core_axis_name="c", subcore_axis_name="s",
                                  num_cores=SC_CORES, num_subcores=SC_SUBCORES)

    @functools.partial(
        pl.kernel, mesh=mesh,
        out_type=jax.ShapeDtypeStruct((n_out, d), src.dtype),
        scratch_types=[pltpu.VMEM((chunk,), jnp.int32), pltpu.VMEM((chunk, d), src.dtype),
                       pltpu.SemaphoreType.DMA])
    def scatter(src_hbm, *rest):
        dest_hbms = rest[:n_dest]
        out_hbm, idx_v, rows_v, sem = rest[n_dest:]
        wid = lax.axis_index("s") * SC_CORES + lax.axis_index("c")
        base = wid * per_w

        @pl.loop(0, per_w // chunk)
        def _(c):
            off = pl.multiple_of(base + c * chunk, 8)
            pltpu.sync_copy(src_hbm.at[pl.ds(off, chunk)], rows_v)
            for dest_hbm in dest_hbms:
                pltpu.sync_copy(dest_hbm.at[pl.ds(off, chunk)], idx_v)
                pltpu.async_copy(rows_v, out_hbm.at[idx_v], sem).wait()

    return scatter(src, *dests)


def _expert_kernel(te_ref, ts_ref, tr_ref, xp_hbm, *rest, nf):
    ns = MOE_WSPLIT
    w1_refs, w3_refs, w2_refs = rest[:ns], rest[ns:2 * ns], rest[2 * ns:3 * ns]
    yr_hbm, xb, acc, stage, in_sem, out_sem = rest[3 * ns:]
    t = pl.program_id(0)
    f = pl.program_id(1)
    rows = tr_ref[t]
    start = pl.multiple_of(ts_ref[t], MOE_ALIGN)
    n_chunks = MOE_TILE // MOE_COPY
    half = xb.shape[1] // 2

    def in_copy(ci):
        return pltpu.make_async_copy(xp_hbm.at[pl.ds(start + ci * MOE_COPY, MOE_COPY)],
                                     stage.at[ci % 2], in_sem.at[ci % 2])

    def out_copy(ci, first_row):
        return pltpu.make_async_copy(acc.at[pl.ds(ci * MOE_COPY, MOE_COPY)],
                                     yr_hbm.at[pl.ds(first_row + ci * MOE_COPY, MOE_COPY)], out_sem.at[0])

    def when_chunk_live(ci, fn):
        pl.when(ci * MOE_COPY < rows)(fn)

    def wait_outputs(tile):
        tile_rows = tr_ref[tile]
        first_row = pl.multiple_of(ts_ref[tile], MOE_ALIGN)
        for ci in range(n_chunks):
            pl.when(ci * MOE_COPY < tile_rows)(lambda ci=ci: out_copy(ci, first_row).wait())

    @pl.when((f == 0) & (rows > 0))
    def _():
        def unpack(ci):
            in_copy(ci).wait()
            lo, hi = _unpack_bf16_pairs(stage[ci % 2])
            xb[ci * MOE_COPY:(ci + 1) * MOE_COPY, :half] = lo
            xb[ci * MOE_COPY:(ci + 1) * MOE_COPY, half:] = hi

        when_chunk_live(0, lambda: in_copy(0).start())
        for ci in range(n_chunks):
            if ci + 1 < n_chunks:
                when_chunk_live(ci + 1, lambda ci=ci: in_copy(ci + 1).start())
            when_chunk_live(ci, lambda ci=ci: unpack(ci))

    @pl.when((f == 0) & (t > 0))
    def _():
        wait_outputs(jnp.maximum(t - 1, 0))

    @pl.when((f == 0) & (rows > 0))
    def _():
        acc[...] = jnp.zeros(acc.shape, F32)

    @pl.when(rows > 0)
    def _():
        w1 = jnp.concatenate([r[0, 0].astype(BF16) for r in w1_refs], axis=0)
        w3 = jnp.concatenate([r[0, 0].astype(BF16) for r in w3_refs], axis=0)
        w2 = jnp.concatenate([r[0, 0].astype(BF16) for r in w2_refs], axis=1)

        def chain(r0, size):
            rs = pl.ds(pl.multiple_of(r0, MOE_PIECES[-1]), size)
            xc = xb[rs, :]
            h1 = _dot(xc, w1)
            h3 = _dot(xc, w3)
            hh = (h1 * _sigmoid(h1) * h3).astype(BF16)
            acc[rs, :] += _dot(hh, w2)

        todo = (rows + (MOE_PIECES[-1] - 1)) // MOE_PIECES[-1] * MOE_PIECES[-1]
        off = jnp.int32(0)
        for size in MOE_PIECES:
            take = todo - off >= size

            @pl.when(take)
            def _(off=off, size=size):
                for c0 in range(0, size, MOE_CHAIN):
                    chain(off + c0, min(MOE_CHAIN, size))

            off = off + jnp.where(take, size, 0)

    @pl.when((f == nf - 1) & (rows > 0))
    def _():
        for ci in range(n_chunks):
            when_chunk_live(ci, lambda ci=ci: out_copy(ci, start).start())

    @pl.when((f == nf - 1) & (t == pl.num_programs(0) - 1))
    def _():
        wait_outputs(t)


def moe_experts(xp, tile_e, tile_start, tile_rows, w1, w3, w2, layer, *, tf):
    n_rows, half = xp.shape
    d = 2 * half
    n_tiles = tile_e.shape[0]
    ff = w1.shape[3]
    nf = ff // tf
    assert MOE_TILE % MOE_COPY == 0 and sum(MOE_PIECES) >= MOE_TILE

    def f_eff(t, f, tr):
        return jnp.where(tr[t] > 0, f, nf - 1)

    ns = MOE_WSPLIT
    deep = pl.Buffered(MOE_WEIGHT_BUFFERS)
    up_specs = [pl.BlockSpec((1, 1, d // ns, tf), lambda t, f, te, ts, tr, k=k: (layer, te[t], k, f_eff(t, f, tr)),
                             pipeline_mode=deep) for k in range(ns)]
    down_specs = [pl.BlockSpec((1, 1, tf, d // ns), lambda t, f, te, ts, tr, k=k: (layer, te[t], f_eff(t, f, tr), k),
                               pipeline_mode=deep) for k in range(ns)]
    grid_spec = pltpu.PrefetchScalarGridSpec(
        num_scalar_prefetch=3,
        grid=(n_tiles, nf),
        in_specs=[pl.BlockSpec(memory_space=pl.ANY)] + 2 * up_specs + down_specs,
        out_specs=pl.BlockSpec(memory_space=pl.ANY),
        scratch_shapes=[pltpu.VMEM((MOE_TILE, d), BF16), pltpu.VMEM((MOE_TILE, d), F32),
                        pltpu.VMEM((2, MOE_COPY, half), F32),
                        pltpu.SemaphoreType.DMA((2,)), pltpu.SemaphoreType.DMA((1,))],
    )
    return pl.pallas_call(
        functools.partial(_expert_kernel, nf=nf),
        grid_spec=grid_spec,
        out_shape=jax.ShapeDtypeStruct((n_rows, d), F32),
        compiler_params=_cparams(("arbitrary", "arbitrary")),
    )(tile_e, tile_start, tile_rows, xp, *([w1] * ns), *([w3] * ns), *([w2] * ns))


def _combine_kernel(x_ref, y1_ref, y2_ref, meta_ref, g_ref, o_ref, *, final_norm):
    meta = meta_ref[...]
    y = x_ref[...] + (meta[:, 2:3] * y1_ref[...] + meta[:, 3:4] * y2_ref[...])
    if final_norm:
        y = _rms(y, g_ref[...])
    o_ref[...] = y


def moe_combine(x, yg, meta, g, *, final_norm, tm):
    n, d = x.shape
    nb = n // tm
    row = pl.BlockSpec((tm, d), lambda i: (i, 0))
    return pl.pallas_call(
        functools.partial(_combine_kernel, final_norm=final_norm),
        grid=(nb,),
        in_specs=[row, row, pl.BlockSpec((tm, d), lambda i: (i + nb, 0)),
                  pl.BlockSpec((tm, LANES), lambda i: (i, 0)),
                  pl.BlockSpec((1, d), lambda i: (0, 0))],
        out_specs=row,
        out_shape=jax.ShapeDtypeStruct((n, d), F32),
        compiler_params=_cparams(("parallel",)),
    )(x, yg, yg, meta, g.reshape(1, d))


def _mla_rope_tables(seq):
    half = MLA_ROPE // 2
    pos = jnp.arange(seq, dtype=F32)
    inv = ROPE_THETA ** (-jnp.arange(0, MLA_ROPE, 2, dtype=F32) / MLA_ROPE)
    ang = pos[:, None] * inv[None, :]
    cos, sin = jnp.cos(ang), jnp.sin(ang)
    z = jnp.zeros((seq, half), F32)
    pad = jnp.zeros((seq, LANES - MLA_ROPE), F32)
    c = jnp.concatenate([cos, cos, pad], axis=1)
    s_lo = jnp.concatenate([-sin, z, pad], axis=1)
    s_hi = jnp.concatenate([z, sin, pad], axis=1)
    return c, s_lo, s_hi


def _nsa_rope_tables(seq):
    pos = jnp.arange(seq, dtype=F32)
    inv = ROPE_THETA ** (-jnp.arange(0, NSA_DH, 2, dtype=F32) / NSA_DH)
    ang = pos[:, None] * inv[None, :]
    cos, sin = jnp.cos(ang), jnp.sin(ang)
    return jnp.concatenate([cos, cos], axis=1), jnp.concatenate([-sin, sin], axis=1)


def _selection_constants(seq):
    n_sel = seq // SEL_BLOCK
    nc = (seq - CMP_BLOCK) // CMP_STRIDE + 1
    cmp_start = np.arange(LANES) * CMP_STRIDE
    sel_start = np.arange(LANES) * SEL_BLOCK
    ovl = ((cmp_start[:, None] < sel_start[None, :] + SEL_BLOCK) &
           (cmp_start[:, None] + CMP_BLOCK > sel_start[None, :]))
    ovl &= (np.arange(LANES)[:, None] < nc) & (np.arange(LANES)[None, :] < n_sel)
    expand = (np.arange(seq)[None, :] // SEL_BLOCK == np.arange(LANES)[:, None])
    return jnp.asarray(ovl, BF16), jnp.asarray(expand, BF16)


def even_layer(x, seq, p, stacked):
    n = x.shape[0]
    b = n // seq
    (norm_mix, w_in, q_norm, w_q_up, kv_norm, w_kv_up, conv_w, conv_b, ga_w, ga_b, gx_w, gx_b,
     lam, norm_ffn) = p
    w_out, w1, w3, w2, layer = stacked
    d = D_MODEL
    o1 = MLA_Q_LORA + MLA_KV_LORA
    o2 = o1 + MLA_ROPE
    w_pack = jnp.concatenate(
        [w_in[:, :o1], w_in[:, o2:], w_in[:, o1:o2], jnp.zeros((d, LANES - MLA_ROPE), F32)], axis=1).astype(BF16)
    u = norm_matmul(x, norm_mix, w_pack, tm=1024, tn=EV_PACKED // 3)

    wq = w_q_up.reshape(MLA_Q_LORA, MLA_HEADS, MLA_NOPE + MLA_ROPE)
    wq = jnp.pad(wq, ((0, 0), (0, 0), (0, MLA_QK_PAD - MLA_NOPE - MLA_ROPE)))
    wq = wq.reshape(MLA_Q_LORA, MLA_HEADS * MLA_QK_PAD).astype(BF16)
    wkv = w_kv_up.reshape(MLA_KV_LORA, MLA_HEADS, MLA_NOPE + MLA_V)
    wk = wkv[:, :, :MLA_NOPE].reshape(MLA_KV_LORA, MLA_HEADS * MLA_NOPE).astype(BF16)
    wv = wkv[:, :, MLA_NOPE:].reshape(MLA_KV_LORA, MLA_HEADS * MLA_V).astype(BF16)
    rc, rlo, rhi = _mla_rope_tables(seq)
    q, k, v = mla_up(u, q_norm, kv_norm, wq, wk, wv, rc, rlo, rhi, seq=seq, tm=512)
    o_mla = mla_attention(q.reshape(b, seq, -1), k.reshape(b, seq, -1), v.reshape(b, seq, -1), t=512)

    o_rec = rglru(u.reshape(b, seq, EV_PACKED), conv_w, conv_b, ga_w.astype(BF16), ga_b,
                  gx_w.astype(BF16), gx_b, lam, ts=1024)
    x = matmul_residual([o_mla.reshape(n, -1), o_rec.reshape(n, -1)], cast_bf16(w_out, layer), x, tm=512, tn=D_MODEL)
    return ffn_dense(x, norm_ffn, cast_bf16(w1, layer), cast_bf16(w3, layer), cast_bf16(w2, layer), tm=1024, tf=512)


def _moe_dispatch(meta, counts, n):
    e1 = meta[:, 0].astype(jnp.int32)
    e2 = meta[:, 1].astype(jnp.int32)
    pos1 = meta[:, 4].astype(jnp.int32)
    pos2 = meta[:, 5].astype(jnp.int32)
    cnt = counts[0, :N_EXPERTS].astype(jnp.int32)
    span = (cnt + MOE_ALIGN - 1) // MOE_ALIGN * MOE_ALIGN
    row0 = jnp.cumsum(span) - span
    row_unit = 64 * SC_WORKERS
    n_rows = -(-(n * TOP_K + N_EXPERTS * MOE_ALIGN + MOE_TILE) // row_unit) * row_unit
    d1 = row0[e1] + pos1
    d2 = row0[e2] + pos2
    n_tiles = (n * TOP_K) // MOE_TILE + N_EXPERTS
    tiles_e = (cnt + MOE_TILE - 1) // MOE_TILE
    tend = jnp.cumsum(tiles_e)
    tbeg = tend - tiles_e
    tid = jnp.arange(n_tiles, dtype=jnp.int32)
    te = jnp.minimum(jnp.searchsorted(tend, tid, side='right'), N_EXPERTS - 1).astype(jnp.int32)
    used = tid < tend[-1]
    first = (tid - tbeg[te]) * MOE_TILE
    rows = jnp.where(used, jnp.clip(cnt[te] - first, 0, MOE_TILE), 0).astype(jnp.int32)
    tstart = jnp.where(used, row0[te] + first, 0).astype(jnp.int32)
    last_e = te[jnp.maximum(tend[-1] - 1, 0)]
    te = jnp.where(used, te, last_e).astype(jnp.int32)
    return d1, d2, n_rows, te, tstart, rows


def odd_layer(x, seq, p, experts, final_g):
    n = x.shape[0]
    b = n // seq
    (norm_mix, w_in, ck_pe, ck_w1, ck_b1, ck_w2, cv_pe, cv_w1, cv_b1, cv_w2, norm_ffn,
     router_w, router_b) = p
    w_out, ew1, ew3, ew2, layer = experts
    d = D_MODEL
    wg = w_in[:, OD_G:].reshape(d, NSA_GROUPS, NSA_HPG * 3)
    wg = jnp.pad(wg, ((0, 0), (0, 0), (0, LANES - NSA_HPG * 3))).reshape(d, NSA_GROUPS * LANES)
    w_pack = jnp.concatenate([w_in[:, :OD_G], wg], axis=1).astype(BF16)
    u = norm_matmul(x, norm_mix, w_pack, tm=1024, tn=OD_PACKED // 4)
    u3 = u.reshape(b, seq, OD_PACKED)

    k_cmp = nsa_compress(u3, OD_KC, ck_pe, ck_w1.astype(BF16), ck_b1, ck_w2.astype(BF16))
    v_cmp = nsa_compress(u3, OD_VC, cv_pe, cv_w1.astype(BF16), cv_b1, cv_w2.astype(BF16))
    rc, rs = _nsa_rope_tables(seq)
    ovl, expand = _selection_constants(seq)
    o = nsa_attention(u3, k_cmp, v_cmp, rc, rs, ovl, expand, tq=512, tk=512)
    x = matmul_residual([o.reshape(n, -1)], cast_bf16(w_out, layer), x, tm=512, tn=D_MODEL)

    wr = jnp.pad(router_w, ((0, 0), (0, LANES - N_EXPERTS)))
    br = jnp.pad(router_b, (0, LANES - N_EXPERTS)).reshape(1, LANES)
    xp, meta, counts = moe_router(x, norm_ffn, wr, br, tm=512)
    d1, d2, n_rows, te, tstart, rows = _moe_dispatch(meta, counts, n)
    yr = moe_experts(sc_scatter_rows(xp, [d1, d2], n_rows), te, tstart, rows, ew1, ew3, ew2, layer, tf=256)
    yg = sc_gather_rows(yr, jnp.concatenate([d1, d2]))
    g = final_g if final_g is not None else norm_ffn
    return moe_combine(x, yg, meta, g, final_norm=final_g is not None, tm=512)


def kernel(x, ev_norm_mix, ev_w_in, ev_q_norm, ev_w_q_up, ev_kv_norm, ev_w_kv_up, ev_conv_w, ev_conv_b, ev_gate_a_w, ev_gate_a_b, ev_gate_x_w, ev_gate_x_b, ev_lru_lambda, ev_w_out, ev_norm_ffn, ev_ffn_w1, ev_ffn_w3, ev_ffn_w2, od_norm_mix, od_w_in, od_cmp_k_pe, od_cmp_k_w1, od_cmp_k_b1, od_cmp_k_w2, od_cmp_v_pe, od_cmp_v_w1, od_cmp_v_b1, od_cmp_v_w2, od_w_out, od_norm_ffn, od_router_w, od_router_b, od_exp_w1, od_exp_w3, od_exp_w2, final_norm):
    bsz, seq, d = x.shape
    ev = (ev_norm_mix, ev_w_in, ev_q_norm, ev_w_q_up, ev_kv_norm, ev_w_kv_up, ev_conv_w, ev_conv_b,
          ev_gate_a_w, ev_gate_a_b, ev_gate_x_w, ev_gate_x_b, ev_lru_lambda, ev_norm_ffn)
    od = (od_norm_mix, od_w_in, od_cmp_k_pe, od_cmp_k_w1, od_cmp_k_b1, od_cmp_k_w2, od_cmp_v_pe,
          od_cmp_v_w1, od_cmp_v_b1, od_cmp_v_w2, od_norm_ffn, od_router_w, od_router_b)
    h = x.reshape(bsz * seq, d)
    for layer in range(DEPTH):
        i = layer // 2
        if layer % 2 == 0:
            h = even_layer(h, seq, tuple(a[i] for a in ev), (ev_w_out, ev_ffn_w1, ev_ffn_w3, ev_ffn_w2, i))
        else:
            h = odd_layer(h, seq, tuple(a[i] for a in od), (od_w_out, od_exp_w1, od_exp_w3, od_exp_w2, i),
                          final_norm if layer == DEPTH - 1 else None)
    return h.reshape(bsz, seq, d)
```
